```python
import math
import jax
import jax.numpy as jnp
from jax import lax
import numpy as np

D_MODEL = 1024
BATCH = 32
SEQ = 256
DEPTH = 4
DEC_BATCH = 4
DEC_SEQ = 4096
PAST_LEN = 512

GRID_W = 64
HEAD_DIM = 64
N_Q_HEADS = D_MODEL // HEAD_DIM
N_KV_HEADS = N_Q_HEADS // 4
GQA_GROUP = N_Q_HEADS // N_KV_HEADS
ROPE_THETA = 10000.0
Q_BLOCK = 128
N_DIFF_HEADS = D_MODEL // (2 * HEAD_DIM)
DIFF_V_DIM = 2 * HEAD_DIM
D_RNN = 5 * D_MODEL // 4
N_RNN_BLOCKS = 16
RNN_BLOCK = D_RNN // N_RNN_BLOCKS
CONV_W = 4
CONV_LEFT = 2
RGLRU_C = 8.0
D_FF = 2816
N_EXPERTS = 8
TOP_K = 2
D_FF_EXPERT = 1408
N_MIXERS = 3
N_ATTN_LAYERS = (DEPTH + 2) // 3
N_RNN_LAYERS = (DEPTH + 1) // 3
N_DIFF_LAYERS = DEPTH // 3
N_DENSE_LAYERS = (DEPTH + 1) // 2
N_MOE_LAYERS = DEPTH // 2
DEEPNORM_ALPHA = (2.0 * DEPTH) ** 0.25
DEEPNORM_BETA = (8.0 * DEPTH) ** -0.25
LN_EPS = 1e-6
RMS_EPS = 1e-6

kernel_name = 'hybrid_diffusion_gqa_rglru_diffattn_step'

F32 = jnp.float32


def layer_norm(x, g, b):
    xf = x.astype(F32)
    mu = jnp.mean(xf, axis=-1, keepdims=True)
    var = jnp.mean(jnp.square(xf - mu), axis=-1, keepdims=True)
    return ((xf - mu) * lax.rsqrt(var + LN_EPS) * g + b).astype(x.dtype)


def rms_norm(x, g):
    xf = x.astype(F32)
    return (xf * lax.rsqrt(jnp.mean(xf * xf, axis=-1, keepdims=True) + RMS_EPS) * g).astype(x.dtype)


def axial_rope(n_tokens):
    n_rows = n_tokens // GRID_W
    row = jnp.repeat(jnp.arange(n_rows, dtype=F32), GRID_W)
    col = jnp.tile(jnp.arange(GRID_W, dtype=F32), n_rows)
    n_axis = HEAD_DIM // 4
    inv = ROPE_THETA ** (-jnp.arange(n_axis, dtype=F32) / n_axis)
    ang = jnp.concatenate([row[:, None] * inv, col[:, None] * inv], axis=-1)
    return jnp.cos(ang), jnp.sin(ang)


def apply_rope(x, rope):
    cos, sin = rope
    shape = (1, x.shape[1]) + (1,) * (x.ndim - 3) + (cos.shape[-1],)
    c = cos.reshape(shape)
    s = sin.reshape(shape)
    x1, x2 = jnp.split(x.astype(F32), 2, axis=-1)
    return jnp.concatenate([x1 * c - x2 * s, x2 * c + x1 * s], axis=-1).astype(x.dtype)


def blocked_queries(fn, q):
    B, T = q.shape[0], q.shape[1]
    nb = T // Q_BLOCK
    qb = jnp.moveaxis(q.reshape((B, nb, Q_BLOCK) + q.shape[2:]), 1, 0)
    ob = lax.map(fn, qb)
    return jnp.moveaxis(ob, 0, 1).reshape((B, T) + ob.shape[3:])


def gqa_mixer(h, w_qkv, g_q, g_k, w_o, rope, ctx_k, ctx_v):
    B, T, _ = h.shape
    q, k, v = jnp.split(h @ w_qkv, [N_Q_HEADS * HEAD_DIM, (N_Q_HEADS + N_KV_HEADS) * HEAD_DIM], axis=-1)
    q = rms_norm(q.reshape(B, T, N_Q_HEADS, HEAD_DIM), g_q)
    k = rms_norm(k.reshape(B, T, N_KV_HEADS, HEAD_DIM), g_k)
    v = v.reshape(B, T, N_KV_HEADS, HEAD_DIM)
    if rope is not None:
        q = apply_rope(q, rope)
        k = apply_rope(k, rope)
    if ctx_k is None:
        keys, vals = k, v
    else:
        keys = jnp.concatenate([k, ctx_k], axis=1)
        vals = jnp.concatenate([v, ctx_v], axis=1)
    q = q.reshape(B, T, N_KV_HEADS, GQA_GROUP, HEAD_DIM)
    scale = HEAD_DIM ** -0.5

    def block(qb):
        s = jnp.einsum('bqkgd,blkd->bkgql', qb, keys).astype(F32) * scale
        p = jax.nn.softmax(s, axis=-1).astype(vals.dtype)
        return jnp.einsum('bkgql,blkd->bqkgd', p, vals)

    o = blocked_queries(block, q)
    return o.reshape(B, T, D_MODEL) @ w_o, (k, v)


def diff_mixer(h, w_qkv, lam_vec, g_sub, w_o, lam_init, rope, ctx_k, ctx_v):
    B, T, _ = h.shape
    nqk = N_DIFF_HEADS * 2 * HEAD_DIM
    q, k, v = jnp.split(h @ w_qkv, [nqk, 2 * nqk], axis=-1)
    q = q.reshape(B, T, N_DIFF_HEADS, 2, HEAD_DIM)
    k = k.reshape(B, T, N_DIFF_HEADS, 2, HEAD_DIM)
    v = v.reshape(B, T, N_DIFF_HEADS, DIFF_V_DIM)
    if rope is not None:
        q = apply_rope(q, rope)
        k = apply_rope(k, rope)
    if ctx_k is None:
        keys, vals = k, v
    else:
        keys = jnp.concatenate([k, ctx_k], axis=1)
        vals = jnp.concatenate([v, ctx_v], axis=1)
    lv = lam_vec.astype(F32)
    lam = jnp.exp(jnp.sum(lv[0] * lv[1])) - jnp.exp(jnp.sum(lv[2] * lv[3])) + lam_init
    scale = HEAD_DIM ** -0.5

    def block(qb):
        s = jnp.einsum('bqhmd,blhmd->bhmql', qb, keys).astype(F32) * scale
        p = jax.nn.softmax(s, axis=-1)
        pd = (p[:, :, 0] - lam * p[:, :, 1]).astype(vals.dtype)
        return jnp.einsum('bhql,blhe->bqhe', pd, vals)

    o = blocked_queries(block, q)
    o = rms_norm(o, g_sub) * (1.0 - lam_init)
    return o.reshape(B, T, D_MODEL) @ w_o, (k, v)


def _lin_comb(left, right):
    a_l, b_l = left
    a_r, b_r = right
    return a_l * a_r, a_r * b_l + b_r


def rglru_mixer(h, w_in, conv_w, conv_b, w_a, b_a, w_x, b_x, lam, w_out, h0):
    B, T, _ = h.shape
    branch, xb = jnp.split(h @ w_in, 2, axis=-1)
    xp = jnp.pad(xb, ((0, 0), (CONV_LEFT, CONV_W - 1 - CONV_LEFT), (0, 0)))
    xc = sum(xp[:, j:j + T] * conv_w[j] for j in range(CONV_W)) + conv_b
    xblk = xc.reshape(B, T, N_RNN_BLOCKS, RNN_BLOCK)
    r = jax.nn.sigmoid((jnp.einsum('btnd,znde->zbtne', xblk, w_a).reshape(2, B, T, D_RNN)
                        + b_a[:, None, None]).astype(F32))
    i = jax.nn.sigmoid((jnp.einsum('btnd,znde->zbtne', xblk, w_x).reshape(2, B, T, D_RNN)
                        + b_x[:, None, None]).astype(F32))
    log_a = -RGLRU_C * r * jax.nn.softplus(-lam.astype(F32))[:, None, None]
    u = jnp.sqrt(-jnp.expm1(2.0 * log_a)) * (i * xc[None].astype(F32))
    a = jnp.exp(log_a)
    a = jnp.stack([a[0], jnp.flip(a[1], axis=1)])
    u = jnp.stack([u[0], jnp.flip(u[1], axis=1)])
    u = u.at[:, :, 0].add(a[:, :, 0] * jnp.moveaxis(h0, 1, 0).astype(F32))
    _, hs = lax.associative_scan(_lin_comb, (a, u), axis=2)
    y = hs[0] + jnp.flip(hs[1], axis=1)
    final = jnp.moveaxis(hs[:, :, -1], 0, 1).astype(h.dtype)
    out = (y.astype(h.dtype) * jax.nn.gelu(branch)) @ w_out
    return out, final


def swiglu(h, w_gu, w_down):
    g, u = jnp.split(h @ w_gu, 2, axis=-1)
    return (jax.nn.silu(g) * u) @ w_down


def moe_swiglu(h, w_router, w_gu, w_down):
    B, T, D = h.shape
    xf = h.reshape(B * T, D)
    logits = (xf @ w_router).astype(F32)
    top_v, top_i = lax.top_k(logits, TOP_K)
    probs = jax.nn.softmax(top_v, axis=-1)
    combine = jnp.sum(jax.nn.one_hot(top_i, N_EXPERTS, dtype=F32) * probs[..., None], axis=1)
    y = jnp.zeros_like(xf)
    for e in range(N_EXPERTS):
        g, u = jnp.split(xf @ w_gu[e], 2, axis=-1)
        y = y + combine[:, e:e + 1].astype(xf.dtype) * ((jax.nn.silu(g) * u) @ w_down[e])
    return y.reshape(B, T, D)


def trunk(x, cond, rope, ck_attn, cv_attn, s_rnn, ck_diff, cv_diff,
          w_ada, b_ada, ln_g, ln_b,
          attn_w_qkv, attn_g_q, attn_g_k, attn_w_o,
          rnn_w_in, rnn_conv_w, rnn_conv_b, rnn_w_a, rnn_b_a, rnn_w_x, rnn_b_x, rnn_lambda, rnn_w_out,
          diff_w_qkv, diff_lambda, diff_g_sub, diff_w_o,
          ffn_w_gu, ffn_w_down, moe_w_router, moe_w_gu, moe_w_down):
    is_ctx = ck_attn is None
    B = x.shape[0]
    cond_act = jax.nn.silu(cond)
    attn_k, attn_v, rnn_s, diff_k, diff_v = [], [], [], [], []
    for li in range(DEPTH):
        mod = (cond_act @ w_ada[li] + b_ada[li]).reshape(-1, 1, 6 * D_MODEL)
        sh1, sc1, g1, sh2, sc2, g2 = jnp.split(mod, 6, axis=-1)
        h = x * (1 + sc1) + sh1
        j = li // N_MIXERS
        kind = li % N_MIXERS
        if kind == 0:
            out, (k, v) = gqa_mixer(h, attn_w_qkv[j], attn_g_q[j], attn_g_k[j], attn_w_o[j], rope,
                                    None if is_ctx else ck_attn[:, j], None if is_ctx else cv_attn[:, j])
            attn_k.append(k)
            attn_v.append(v)
        elif kind == 1:
            h0 = jnp.zeros((B, 2, D_RNN), x.dtype) if is_ctx else s_rnn[:, j]
            out, s = rglru_mixer(h, rnn_w_in[j], rnn_conv_w[j], rnn_conv_b[j], rnn_w_a[j], rnn_b_a[j],
                                 rnn_w_x[j], rnn_b_x[j], rnn_lambda[j], rnn_w_out[j], h0)
            rnn_s.append(s)
        else:
            lam_init = 0.8 - 0.6 * math.exp(-0.3 * li)
            out, (k, v) = diff_mixer(h, diff_w_qkv[j], diff_lambda[j], diff_g_sub[j], diff_w_o[j], lam_init,
                                     rope, None if is_ctx else ck_diff[:, j], None if is_ctx else cv_diff[:, j])
            diff_k.append(k)
            diff_v.append(v)
        x = layer_norm(DEEPNORM_ALPHA * x + g1 * out, ln_g[li, 0], ln_b[li, 0])
        h = x * (1 + sc2) + sh2
        if li % 2 == 0:
            f = swiglu(h, ffn_w_gu[li // 2], ffn_w_down[li // 2])
        else:
            f = moe_swiglu(h, moe_w_router[li // 2], moe_w_gu[li // 2], moe_w_down[li // 2])
        x = layer_norm(DEEPNORM_ALPHA * x + g2 * f, ln_g[li, 1], ln_b[li, 1])
    return x, attn_k, attn_v, rnn_s, diff_k, diff_v


def setup_inputs(seed: int = 0) -> dict:
    key = jax.random.key(seed)
    ks = iter(jax.random.split(key, 48))

    def nrm(shape, scale=1.0):
        return jax.random.normal(next(ks), shape, F32) * scale

    d = D_MODEL
    inv_d = d ** -0.5
    a_init = jax.random.uniform(next(ks), (N_RNN_LAYERS, 2, D_RNN), F32, 0.9, 0.999)
    return {
        'x_prompt': nrm((BATCH, SEQ, d)),
        'x_sample': nrm((DEC_BATCH, DEC_SEQ, d)),
        'cache_attn_k': nrm((DEC_BATCH, N_ATTN_LAYERS, PAST_LEN, N_KV_HEADS, HEAD_DIM)),
        'cache_attn_v': nrm((DEC_BATCH, N_ATTN_LAYERS, PAST_LEN, N_KV_HEADS, HEAD_DIM)),
        'state_rglru': nrm((DEC_BATCH, N_RNN_LAYERS, 2, D_RNN), 0.5),
        'cache_diff_k': nrm((DEC_BATCH, N_DIFF_LAYERS, PAST_LEN, N_DIFF_HEADS, 2, HEAD_DIM)),
        'cache_diff_v': nrm((DEC_BATCH, N_DIFF_LAYERS, PAST_LEN, N_DIFF_HEADS, DIFF_V_DIM)),
        'c': nrm((DEC_BATCH, d)),
        'c_ctx': nrm((d,)),
        'w_ada': nrm((DEPTH, d, 6 * d), 0.5 * inv_d),
        'b_ada': nrm((DEPTH, 6 * d), 0.02),
        'ln_g': 1.0 + nrm((DEPTH, 2, d), 0.02),
        'ln_b': nrm((DEPTH, 2, d), 0.02),
        'attn_w_qkv': nrm((N_ATTN_LAYERS, d, (N_Q_HEADS + 2 * N_KV_HEADS) * HEAD_DIM), inv_d),
        'attn_g_q': 1.0 + nrm((N_ATTN_LAYERS, HEAD_DIM), 0.02),
        'attn_g_k': 1.0 + nrm((N_ATTN_LAYERS, HEAD_DIM), 0.02),
        'attn_w_o': nrm((N_ATTN_LAYERS, d, d), inv_d * DEEPNORM_BETA),
        'rnn_w_in': nrm((N_RNN_LAYERS, d, 2 * D_RNN), inv_d),
        'rnn_conv_w': nrm((N_RNN_LAYERS, CONV_W, D_RNN), CONV_W ** -0.5),
        'rnn_conv_b': nrm((N_RNN_LAYERS, D_RNN), 0.02),
        'rnn_w_a': nrm((N_RNN_LAYERS, 2, N_RNN_BLOCKS, RNN_BLOCK, RNN_BLOCK), RNN_BLOCK ** -0.5),
        'rnn_b_a': nrm((N_RNN_LAYERS, 2, D_RNN), 0.02),
        'rnn_w_x': nrm((N_RNN_LAYERS, 2, N_RNN_BLOCKS, RNN_BLOCK, RNN_BLOCK), RNN_BLOCK ** -0.5),
        'rnn_b_x': nrm((N_RNN_LAYERS, 2, D_RNN), 0.02),
        'rnn_lambda': jnp.log(a_init) - jnp.log1p(-a_init),
        'rnn_w_out': nrm((N_RNN_LAYERS, D_RNN, d), D_RNN ** -0.5 * DEEPNORM_BETA),
        'diff_w_qkv': nrm((N_DIFF_LAYERS, d, 3 * N_DIFF_HEADS * 2 * HEAD_DIM), inv_d),
        'diff_lambda': nrm((N_DIFF_LAYERS, 4, HEAD_DIM), 0.1),
        'diff_g_sub': 1.0 + nrm((N_DIFF_LAYERS, DIFF_V_DIM), 0.02),
        'diff_w_o': nrm((N_DIFF_LAYERS, d, d), inv_d * DEEPNORM_BETA),
        'ffn_w_gu': nrm((N_DENSE_LAYERS, d, 2 * D_FF), inv_d),
        'ffn_w_down': nrm((N_DENSE_LAYERS, D_FF, d), D_FF ** -0.5 * DEEPNORM_BETA),
        'moe_w_router': nrm((N_MOE_LAYERS, d, N_EXPERTS), inv_d),
        'moe_w_gu': nrm((N_MOE_LAYERS, N_EXPERTS, d, 2 * D_FF_EXPERT), inv_d),
        'moe_w_down': nrm((N_MOE_LAYERS, N_EXPERTS, D_FF_EXPERT, d), D_FF_EXPERT ** -0.5 * DEEPNORM_BETA),
    }


def reference(x_prompt, x_sample, cache_attn_k, cache_attn_v, state_rglru, cache_diff_k, cache_diff_v,
              c, c_ctx, w_ada, b_ada, ln_g, ln_b,
              attn_w_qkv, attn_g_q, attn_g_k, attn_w_o,
              rnn_w_in, rnn_conv_w, rnn_conv_b, rnn_w_a, rnn_b_a, rnn_w_x, rnn_b_x, rnn_lambda, rnn_w_out,
              diff_w_qkv, diff_lambda, diff_g_sub, diff_w_o,
              ffn_w_gu, ffn_w_down, moe_w_router, moe_w_gu, moe_w_down):
    weights = (w_ada, b_ada, ln_g, ln_b,
               attn_w_qkv, attn_g_q, attn_g_k, attn_w_o,
               rnn_w_in, rnn_conv_w, rnn_conv_b, rnn_w_a, rnn_b_a, rnn_w_x, rnn_b_x, rnn_lambda, rnn_w_out,
               diff_w_qkv, diff_lambda, diff_g_sub, diff_w_o,
               ffn_w_gu, ffn_w_down, moe_w_router, moe_w_gu, moe_w_down)
    y_prompt, ak, av, rs, dk, dv = trunk(x_prompt, c_ctx, None, None, None, None, None, None, *weights)
    rope = axial_rope(x_sample.shape[1])
    y_sample = trunk(x_sample, c, rope, cache_attn_k, cache_attn_v, state_rglru,
                     cache_diff_k, cache_diff_v, *weights)[0]
    new_attn_k = jnp.stack(ak, axis=1)
    new_attn_v = jnp.stack(av, axis=1)
    new_state_rglru = jnp.stack(rs, axis=1)
    new_diff_k = jnp.stack(dk, axis=1)
    new_diff_v = jnp.stack(dv, axis=1)
    return (y_prompt, y_sample, new_attn_k, new_attn_v, new_state_rglru, new_diff_k, new_diff_v)
```

```python
import functools
import math

import jax
import jax.numpy as jnp
import numpy as np
from jax import lax
from jax.experimental import pallas as pl
from jax.experimental.pallas import tpu as pltpu

F32 = jnp.float32
BF16 = jnp.bfloat16

D_MODEL = 1024
DEPTH = 4
GRID_W = 64
HEAD_DIM = 64
N_Q_HEADS = 16
N_KV_HEADS = 4
ROPE_THETA = 10000.0
N_DIFF_HEADS = 8
D_RNN = 1280
N_RNN_BLOCKS = 16
RNN_BLOCK = 80
CONV_W = 4
CONV_LEFT = 2
RGLRU_C = 8.0
D_FF = 2816
N_EXPERTS = 8
D_FF_EXPERT = 1408
N_MIXERS = 3
DEEPNORM_ALPHA = (2.0 * DEPTH) ** 0.25
LN_EPS = 1e-6
RMS_EPS = 1e-6

LANES = 128
SUBLANES = 8
VMEM_LIMIT = 56 * 1024 * 1024
COND_ROWS = 8
SCAN_CHUNK = 256


def _cparams(*sem):
    return pltpu.CompilerParams(dimension_semantics=sem, vmem_limit_bytes=VMEM_LIMIT)


def _layer_norm_rows(z, g, b):
    mu = jnp.mean(z, axis=-1, keepdims=True)
    zc = z - mu
    var = jnp.mean(zc * zc, axis=-1, keepdims=True)
    return zc * lax.rsqrt(var + LN_EPS) * g + b


def _seg_of_block(i, tm, n_ctx, t_s):
    r0 = i * tm
    return jnp.where(r0 < n_ctx, 0, 1 + (r0 - n_ctx) // t_s)


def _mod_spec(col, tm, n_ctx, t_s):
    return pl.BlockSpec((None, 1, D_MODEL), lambda i, *_: (_seg_of_block(i, tm, n_ctx, t_s), 0, col))


def _ada_kernel(c_ref, w_ref, b_ref, o_ref):
    c = c_ref[...]
    a = (c * jax.nn.sigmoid(c)).astype(BF16)
    o_ref[...] = jnp.dot(a, w_ref[...].astype(BF16), preferred_element_type=F32) + b_ref[...]


def _ada_table(cond, w_ada, b_ada):
    n_l, d, n = w_ada.shape
    tn = 1536
    return pl.pallas_call(
        _ada_kernel,
        out_shape=jax.ShapeDtypeStruct((n_l, COND_ROWS, n), F32),
        grid=(n_l, n // tn),
        in_specs=[pl.BlockSpec((COND_ROWS, d), lambda l, j: (0, 0)),
                  pl.BlockSpec((None, d, tn), lambda l, j: (l, 0, j)),
                  pl.BlockSpec((None, 1, tn), lambda l, j: (l, 0, j))],
        out_specs=pl.BlockSpec((None, COND_ROWS, tn), lambda l, j: (l, 0, j)),
        compiler_params=_cparams("parallel", "parallel"),
        name="ada_table",
    )(cond, w_ada, b_ada.reshape(n_l, 1, n))


def _head_masks(tm):
    lane = lax.broadcasted_iota(jnp.int32, (tm, LANES), 1)
    return lane < HEAD_DIM, (lane & (HEAD_DIM - 1)) < HEAD_DIM // 2


def _rope_tile(t, first_half, cos, sin):
    partner = jnp.where(first_half, pltpu.roll(t, LANES - HEAD_DIM // 2, 1),
                        pltpu.roll(t, HEAD_DIM // 2, 1))
    return t * cos + partner * sin


def _gqa_qkv_kernel(x_ref, sc_ref, sh_ref, w_ref, gq_ref, gk_ref, cos_ref, sin_ref,
                    q_ref, kb_ref, vb_ref, kf_ref, vf_ref, *, tm, n_ctx):
    i = pl.program_id(0)
    h = (x_ref[...] * (1.0 + sc_ref[...]) + sh_ref[...]).astype(BF16)
    y = jnp.dot(h, w_ref[...], preferred_element_type=F32)
    use_rope = i * tm >= n_ctx
    cos = jnp.where(use_rope, cos_ref[...], 1.0)
    sin = jnp.where(use_rope, sin_ref[...], 0.0)
    low_head, first_half = _head_masks(tm)

    def norm_rope(t, g):
        t2 = t * t
        s_lo = jnp.sum(jnp.where(low_head, t2, 0.0), axis=-1, keepdims=True)
        s_hi = jnp.sum(jnp.where(low_head, 0.0, t2), axis=-1, keepdims=True)
        inv = jnp.where(low_head, lax.rsqrt(s_lo * (1.0 / HEAD_DIM) + RMS_EPS),
                        lax.rsqrt(s_hi * (1.0 / HEAD_DIM) + RMS_EPS))
        return _rope_tile(t * inv * g, first_half, cos, sin)

    nq = N_Q_HEADS * HEAD_DIM
    nkv = N_KV_HEADS * HEAD_DIM
    gq = gq_ref[...]
    gk = gk_ref[...]
    for t in range(nq // LANES):
        sl = slice(t * LANES, (t + 1) * LANES)
        q_ref[:, sl] = (norm_rope(y[:, sl], gq) * (HEAD_DIM ** -0.5)).astype(BF16)
    for t in range(nkv // LANES):
        sl = slice(t * LANES, (t + 1) * LANES)
        k = norm_rope(y[:, nq + t * LANES: nq + (t + 1) * LANES], gk)
        kf_ref[:, sl] = k
        kb_ref[:, sl] = k.astype(BF16)
    v = y[:, nq + nkv:]
    vf_ref[...] = v
    vb_ref[...] = v.astype(BF16)


def _gqa_qkv(x, mod, w, gq, gk, cos, sin, *, n_ctx, t_s):
    m = x.shape[0]
    tm = 512
    nq = N_Q_HEADS * HEAD_DIM
    nkv = N_KV_HEADS * HEAD_DIM
    n_rope_blocks = t_s // tm
    rope_spec = pl.BlockSpec(
        (tm, LANES), lambda i: (jnp.where(i * tm >= n_ctx, ((i * tm - n_ctx) % t_s) // tm, 0) % n_rope_blocks, 0))
    row = lambda n: pl.BlockSpec((tm, n), lambda i: (i, 0))
    const = lambda a: pl.BlockSpec(a.shape, lambda i: (0,) * a.ndim)
    return pl.pallas_call(
        functools.partial(_gqa_qkv_kernel, tm=tm, n_ctx=n_ctx),
        out_shape=(jax.ShapeDtypeStruct((m, nq), BF16), jax.ShapeDtypeStruct((m, nkv), BF16),
                   jax.ShapeDtypeStruct((m, nkv), BF16), jax.ShapeDtypeStruct((m, nkv), F32),
                   jax.ShapeDtypeStruct((m, nkv), F32)),
        grid=(m // tm,),
        in_specs=[row(D_MODEL), _mod_spec(1, tm, n_ctx, t_s), _mod_spec(0, tm, n_ctx, t_s),
                  const(w), const(gq), const(gk), rope_spec, rope_spec],
        out_specs=(row(nq), row(nkv), row(nkv), row(nkv), row(nkv)),
        compiler_params=_cparams("parallel"),
        name="gqa_qkv",
    )(x, mod, mod, w, gq, gk, cos, sin)


def _diff_qkv_kernel(x_ref, sc_ref, sh_ref, w_ref, cos_ref, sin_ref,
                     q_ref, kb_ref, vb_ref, kf_ref, vf_ref, *, tm, n_ctx):
    i = pl.program_id(0)
    h = (x_ref[...] * (1.0 + sc_ref[...]) + sh_ref[...]).astype(BF16)
    y = jnp.dot(h, w_ref[...], preferred_element_type=F32)
    use_rope = i * tm >= n_ctx
    cos = jnp.where(use_rope, cos_ref[...], 1.0)
    sin = jnp.where(use_rope, sin_ref[...], 0.0)
    _, first_half = _head_masks(tm)
    for t in range(D_MODEL // LANES):
        sl = slice(t * LANES, (t + 1) * LANES)
        q = _rope_tile(y[:, sl], first_half, cos, sin)
        q_ref[:, sl] = (q * (HEAD_DIM ** -0.5)).astype(BF16)
        k = _rope_tile(y[:, D_MODEL + t * LANES: D_MODEL + (t + 1) * LANES], first_half, cos, sin)
        kf_ref[:, sl] = k
        kb_ref[:, sl] = k.astype(BF16)
    v = y[:, 2 * D_MODEL:]
    vf_ref[...] = v
    vb_ref[...] = v.astype(BF16)


def _diff_qkv(x, mod, w, cos, sin, *, n_ctx, t_s):
    m = x.shape[0]
    tm = 256
    n_rope_blocks = t_s // tm
    rope_spec = pl.BlockSpec(
        (tm, LANES), lambda i: (jnp.where(i * tm >= n_ctx, ((i * tm - n_ctx) % t_s) // tm, 0) % n_rope_blocks, 0))
    row = pl.BlockSpec((tm, D_MODEL), lambda i: (i, 0))
    return pl.pallas_call(
        functools.partial(_diff_qkv_kernel, tm=tm, n_ctx=n_ctx),
        out_shape=(jax.ShapeDtypeStruct((m, D_MODEL), BF16), jax.ShapeDtypeStruct((m, D_MODEL), BF16),
                   jax.ShapeDtypeStruct((m, D_MODEL), BF16), jax.ShapeDtypeStruct((m, D_MODEL), F32),
                   jax.ShapeDtypeStruct((m, D_MODEL), F32)),
        grid=(m // tm,),
        in_specs=[row, _mod_spec(1, tm, n_ctx, t_s), _mod_spec(0, tm, n_ctx, t_s),
                  pl.BlockSpec(w.shape, lambda i: (0, 0)), rope_spec, rope_spec],
        out_specs=(row, row, row, row, row),
        compiler_params=_cparams("parallel"),
        name="diff_qkv",
    )(x, mod, mod, w, cos, sin)


_NT = (((1,), (1,)), ((), ()))


def _gqa_attn_kernel(*refs, has_cache, tq):
    if has_cache:
        q_ref, k_ref, v_ref, ck_ref, cv_ref, o_ref = refs
    else:
        q_ref, k_ref, v_ref, o_ref = refs
    low_head = lax.broadcasted_iota(jnp.int32, (tq, LANES), 1) < HEAD_DIM
    k = k_ref[...]
    v = v_ref[...]
    for j in range(q_ref.shape[1] // LANES):
        sl = slice(j * LANES, (j + 1) * LANES)
        qt = q_ref[:, sl]
        halves = []
        for keep_low in (True, False):
            qh = jnp.where(low_head == keep_low, qt, jnp.zeros_like(qt))
            s = lax.dot_general(qh, k, _NT, preferred_element_type=F32)
            mx = jnp.max(s, axis=-1, keepdims=True)
            if has_cache:
                s_c = lax.dot_general(qh, ck_ref[...], _NT, preferred_element_type=F32)
                mx = jnp.maximum(mx, jnp.max(s_c, axis=-1, keepdims=True))
            e = jnp.exp(s - mx)
            den = jnp.sum(e, axis=-1, keepdims=True)
            acc = jnp.dot(e.astype(BF16), v, preferred_element_type=F32)
            if has_cache:
                e_c = jnp.exp(s_c - mx)
                den = den + jnp.sum(e_c, axis=-1, keepdims=True)
                acc = acc + jnp.dot(e_c.astype(BF16), cv_ref[...], preferred_element_type=F32)
            halves.append(acc / den)
        o_ref[:, sl] = jnp.where(low_head, halves[0], halves[1]).astype(BF16)


def _gqa_attention(q, k, v, o_prev, cache_k, cache_v, *, row0, n_b, t, tq):
    m = q.shape[0]
    n_pairs = N_KV_HEADS // 2
    qw = D_MODEL // n_pairs
    nq_blocks = t // tq
    has_cache = cache_k is not None
    q_spec = pl.BlockSpec((tq, qw), lambda b, p, iq: (row0 // tq + b * nq_blocks + iq, p))
    kv_spec = pl.BlockSpec((t, LANES), lambda b, p, iq: (row0 // t + b, p))
    in_specs = [q_spec, kv_spec, kv_spec]
    args = [q, k, v]
    if has_cache:
        c_spec = pl.BlockSpec((None, cache_k.shape[1], LANES), lambda b, p, iq: (b, 0, p))
        in_specs += [c_spec, c_spec]
        args += [cache_k, cache_v]
    aliases = {}
    if o_prev is not None:
        in_specs.append(pl.BlockSpec(memory_space=pl.ANY))
        args.append(o_prev)
        aliases = {len(args) - 1: 0}
        kern = functools.partial(_with_ignored_alias, functools.partial(_gqa_attn_kernel, has_cache=has_cache, tq=tq),
                                 n_in=len(args))
    else:
        kern = functools.partial(_gqa_attn_kernel, has_cache=has_cache, tq=tq)
    return pl.pallas_call(
        kern,
        out_shape=jax.ShapeDtypeStruct((m, D_MODEL), BF16),
        grid=(n_b, n_pairs, nq_blocks),
        in_specs=in_specs,
        out_specs=q_spec,
        input_output_aliases=aliases,
        compiler_params=_cparams("parallel", "parallel", "arbitrary"),
        name="gqa_attn_cache" if has_cache else "gqa_attn",
    )(*args)


def _with_ignored_alias(kern, *refs, n_in):
    kern(*refs[:n_in - 1], *refs[n_in:])


def _diff_attn_kernel(*refs, has_cache, tq, lam_init):
    if has_cache:
        lam_ref, g_ref, q_ref, k_ref, v_ref, ck_ref, cv_ref, o_ref = refs
    else:
        lam_ref, g_ref, q_ref, k_ref, v_ref, o_ref = refs
    lv = lam_ref[...]
    lam = (jnp.exp(jnp.sum(lv[0:1] * lv[1:2], axis=-1, keepdims=True))
           - jnp.exp(jnp.sum(lv[2:3] * lv[3:4], axis=-1, keepdims=True)) + lam_init)
    low_head = lax.broadcasted_iota(jnp.int32, (tq, LANES), 1) < HEAD_DIM
    qt = q_ref[...]
    k = k_ref[...]
    probs = []
    for keep_low in (True, False):
        qh = jnp.where(low_head == keep_low, qt, jnp.zeros_like(qt))
        s = lax.dot_general(qh, k, _NT, preferred_element_type=F32)
        mx = jnp.max(s, axis=-1, keepdims=True)
        if has_cache:
            s_c = lax.dot_general(qh, ck_ref[...], _NT, preferred_element_type=F32)
            mx = jnp.maximum(mx, jnp.max(s_c, axis=-1, keepdims=True))
        e = jnp.exp(s - mx)
        den = jnp.sum(e, axis=-1, keepdims=True)
        if has_cache:
            e_c = jnp.exp(s_c - mx)
            den = den + jnp.sum(e_c, axis=-1, keepdims=True)
            inv = 1.0 / den
            probs.append((e * inv, e_c * inv))
        else:
            probs.append((e * (1.0 / den),))
    pd = (probs[0][0] - lam * probs[1][0]).astype(BF16)
    o = jnp.dot(pd, v_ref[...], preferred_element_type=F32)
    if has_cache:
        pd_c = (probs[0][1] - lam * probs[1][1]).astype(BF16)
        o = o + jnp.dot(pd_c, cv_ref[...], preferred_element_type=F32)
    inv = lax.rsqrt(jnp.mean(o * o, axis=-1, keepdims=True) + RMS_EPS)
    o_ref[...] = ((o * inv * g_ref[...]) * (1.0 - lam_init)).astype(BF16)


def _diff_attention(lam_vec, g_sub, q, k, v, o_prev, cache_k, cache_v, *, row0, n_b, t, tq, lam_init):
    m = q.shape[0]
    nq_blocks = t // tq
    has_cache = cache_k is not None
    q_spec = pl.BlockSpec((tq, LANES), lambda b, h, iq: (row0 // tq + b * nq_blocks + iq, h))
    kv_spec = pl.BlockSpec((t, LANES), lambda b, h, iq: (row0 // t + b, h))
    const = lambda a: pl.BlockSpec(a.shape, lambda b, h, iq: (0,) * a.ndim)
    in_specs = [const(lam_vec), const(g_sub), q_spec, kv_spec, kv_spec]
    args = [lam_vec, g_sub, q, k, v]
    if has_cache:
        c_spec = pl.BlockSpec((None, cache_k.shape[1], LANES), lambda b, h, iq: (b, 0, h))
        in_specs += [c_spec, c_spec]
        args += [cache_k, cache_v]
    base = functools.partial(_diff_attn_kernel, has_cache=has_cache, tq=tq, lam_init=lam_init)
    aliases = {}
    kern = base
    if o_prev is not None:
        in_specs.append(pl.BlockSpec(memory_space=pl.ANY))
        args.append(o_prev)
        aliases = {len(args) - 1: 0}
        kern = functools.partial(_with_ignored_alias, base, n_in=len(args))
    return pl.pallas_call(
        kern,
        out_shape=jax.ShapeDtypeStruct((m, D_MODEL), BF16),
        grid=(n_b, N_DIFF_HEADS, nq_blocks),
        in_specs=in_specs,
        out_specs=q_spec,
        input_output_aliases=aliases,
        compiler_params=_cparams("parallel", "parallel", "arbitrary"),
        name="diff_attn_cache" if has_cache else "diff_attn",
    )(*args)


def _oproj_kernel(a_ref, w_ref, x_ref, gate_ref, lng_ref, lnb_ref, o_ref):
    out = jnp.dot(a_ref[...], w_ref[...], preferred_element_type=F32)
    z = DEEPNORM_ALPHA * x_ref[...] + gate_ref[...] * out
    o_ref[...] = _layer_norm_rows(z, lng_ref[...], lnb_ref[...])


def _oproj(a, w, x, mod, lng, lnb, *, n_ctx, t_s):
    m, kdim = a.shape
    tm = 512
    const = lambda arr: pl.BlockSpec(arr.shape, lambda i: (0,) * arr.ndim)
    row = pl.BlockSpec((tm, D_MODEL), lambda i: (i, 0))
    return pl.pallas_call(
        _oproj_kernel,
        out_shape=jax.ShapeDtypeStruct((m, D_MODEL), F32),
        grid=(m // tm,),
        in_specs=[pl.BlockSpec((tm, kdim), lambda i: (i, 0)), const(w), row,
                  _mod_spec(2, tm, n_ctx, t_s), const(lng), const(lnb)],
        out_specs=row,
        compiler_params=_cparams("parallel"),
        name="oproj_ln",
    )(a, w, x, mod, lng, lnb)


def _gelu_tanh(x):
    return 0.5 * x * (1.0 + jnp.tanh(math.sqrt(2.0 / math.pi) * (x + 0.044715 * (x * x * x))))


def _rnn_oproj_kernel(hf_ref, hb_ref, br_ref, w_ref, x_ref, gate_ref, lng_ref, lnb_ref, o_ref):
    y = (hf_ref[...] + hb_ref[...]) * _gelu_tanh(br_ref[...])
    out = jnp.dot(y.astype(BF16), w_ref[...], preferred_element_type=F32)
    z = DEEPNORM_ALPHA * x_ref[...] + gate_ref[...] * out
    o_ref[...] = _layer_norm_rows(z, lng_ref[...], lnb_ref[...])


def _rnn_oproj(hs_f, hs_b, bx, w, x, mod, lng, lnb, *, n_ctx, t_s):
    m = x.shape[0]
    tm = 512
    const = lambda arr: pl.BlockSpec(arr.shape, lambda i: (0,) * arr.ndim)
    row = pl.BlockSpec((tm, D_MODEL), lambda i: (i, 0))
    rnn_row = pl.BlockSpec((tm, D_RNN), lambda i: (i, 0))
    return pl.pallas_call(
        _rnn_oproj_kernel,
        out_shape=jax.ShapeDtypeStruct((m, D_MODEL), F32),
        grid=(m // tm,),
        in_specs=[rnn_row, rnn_row, rnn_row, const(w), row,
                  _mod_spec(2, tm, n_ctx, t_s), const(lng), const(lnb)],
        out_specs=row,
        compiler_params=_cparams("parallel"),
        name="rnn_oproj_ln",
    )(hs_f, hs_b, bx, w, x, mod, lng, lnb)


def _mod_mm_kernel(x_ref, sc_ref, sh_ref, w_ref, o_ref):
    h = (x_ref[...] * (1.0 + sc_ref[...]) + sh_ref[...]).astype(BF16)
    o_ref[...] = jnp.dot(h, w_ref[...], preferred_element_type=F32)


def _rnn_in(x, mod, w, *, n_ctx, t_s):
    m = x.shape[0]
    n = w.shape[1]
    tm = 512
    return pl.pallas_call(
        _mod_mm_kernel,
        out_shape=jax.ShapeDtypeStruct((m, n), F32),
        grid=(m // tm,),
        in_specs=[pl.BlockSpec((tm, D_MODEL), lambda i: (i, 0)),
                  _mod_spec(1, tm, n_ctx, t_s), _mod_spec(0, tm, n_ctx, t_s),
                  pl.BlockSpec(w.shape, lambda i: (0, 0))],
        out_specs=pl.BlockSpec((tm, n), lambda i: (i, 0)),
        compiler_params=_cparams("parallel"),
        name="rnn_in",
    )(x, mod, mod, w)


def _rnn_gate_kernel(xb_ref, prev_ref, next_ref, cw_ref, cb_ref, wg_ref, bg_ref, lam_ref,
                     af_ref, uf_ref, ab_ref, ub_ref, ext_ref, *, tm, n_ctx, t_c, t_s):
    i = pl.program_id(0)
    r0 = i * tm
    pos = jnp.where(r0 < n_ctx, r0 % t_c, (r0 - n_ctx) % t_s)
    t_seq = jnp.where(r0 < n_ctx, t_c, t_s)
    at_start = pos == 0
    at_end = pos + tm == t_seq
    ext_ref[0:SUBLANES, :] = jnp.where(at_start, 0.0, prev_ref[...])
    ext_ref[SUBLANES:SUBLANES + tm, :] = xb_ref[...]
    ext_ref[SUBLANES + tm:, :] = jnp.where(at_end, 0.0, next_ref[...])
    xc = 0.0
    for j in range(CONV_W):
        off = SUBLANES - CONV_LEFT + j
        xc = xc + ext_ref[off:off + tm, :] * cw_ref[j:j + 1, :]
    xc = xc + cb_ref[...]
    gates = jnp.dot(xc.astype(BF16), wg_ref[...], preferred_element_type=F32) + bg_ref[...]
    neg_lam = -lam_ref[...]
    sp = jnp.maximum(neg_lam, 0.0) + jnp.log1p(jnp.exp(-jnp.abs(neg_lam)))
    for z, (a_ref, u_ref) in enumerate(((af_ref, uf_ref), (ab_ref, ub_ref))):
        base = 2 * z * D_RNN
        r = jax.nn.sigmoid(gates[:, base:base + D_RNN])
        g_in = jax.nn.sigmoid(gates[:, base + D_RNN:base + 2 * D_RNN])
        log_a = -RGLRU_C * r * sp[z:z + 1, :]
        a = jnp.exp(log_a)
        a_ref[...] = a
        u_ref[...] = jnp.sqrt(-jnp.tanh(log_a) * (a * a + 1.0)) * (g_in * xc)


def _rnn_gates(bx, conv_w, conv_b, wg, bg, lam, *, n_ctx, t_c, t_s):
    m = bx.shape[0]
    tm = 256
    n8 = tm // SUBLANES
    last8 = m // SUBLANES - 1
    const = lambda arr: pl.BlockSpec(arr.shape, lambda i: (0,) * arr.ndim)
    row = pl.BlockSpec((tm, D_RNN), lambda i: (i, 0))
    out = jax.ShapeDtypeStruct((m, D_RNN), F32)
    return pl.pallas_call(
        functools.partial(_rnn_gate_kernel, tm=tm, n_ctx=n_ctx, t_c=t_c, t_s=t_s),
        out_shape=(out, out, out, out),
        grid=(m // tm,),
        in_specs=[pl.BlockSpec((tm, D_RNN), lambda i: (i, 1)),
                  pl.BlockSpec((SUBLANES, D_RNN), lambda i: (jnp.maximum(i * n8 - 1, 0), 1)),
                  pl.BlockSpec((SUBLANES, D_RNN), lambda i: (jnp.minimum((i + 1) * n8, last8), 1)),
                  const(conv_w), const(conv_b), const(wg), const(bg), const(lam)],
        out_specs=(row, row, row, row),
        scratch_shapes=[pltpu.VMEM((tm + 2 * SUBLANES, D_RNN), F32)],
        compiler_params=_cparams("parallel"),
        name="rnn_gates",
    )(bx, bx, bx, conv_w, conv_b, wg, bg, lam)


def _rnn_scan_kernel(fblk_ref, bblk_ref, seq_ref, first_ref,
                     af_ref, uf_ref, ab_ref, ub_ref, h0_ref, hf_ref, hb_ref, fin_ref, carry_ref, *, tc):
    s = pl.program_id(0)

    @pl.when(first_ref[s] == 1)
    def _():
        carry_ref[...] = h0_ref[...]

    def step(t, carry):
        h_f, h_b = carry
        tb = tc - 1 - t
        h_f = af_ref[pl.ds(t, 1), :] * h_f + uf_ref[pl.ds(t, 1), :]
        h_b = ab_ref[pl.ds(tb, 1), :] * h_b + ub_ref[pl.ds(tb, 1), :]
        hf_ref[pl.ds(t, 1), :] = h_f
        hb_ref[pl.ds(tb, 1), :] = h_b
        return h_f, h_b

    h_f, h_b = lax.fori_loop(0, tc, step, (carry_ref[0:1, :], carry_ref[1:2, :]), unroll=8)
    carry_ref[0:1, :] = h_f
    carry_ref[1:2, :] = h_b
    fin_ref[...] = carry_ref[...]


def _rnn_scan(a_f, u_f, a_b, u_b, h0, *, seq_lens):
    m = a_f.shape[0]
    tc = SCAN_CHUNK
    fblk, bblk, seq, first = [], [], [], []
    blk0 = 0
    for si, t in enumerate(seq_lens):
        nch = t // tc
        for c in range(nch):
            fblk.append(blk0 + c)
            bblk.append(blk0 + nch - 1 - c)
            seq.append(si)
            first.append(1 if c == 0 else 0)
        blk0 += nch
    n_steps = len(fblk)
    tables = [jnp.asarray(np.array(v, np.int32)) for v in (fblk, bblk, seq, first)]
    f_spec = pl.BlockSpec((tc, D_RNN), lambda s, fb, bb, sq, fr: (fb[s], 0))
    b_spec = pl.BlockSpec((tc, D_RNN), lambda s, fb, bb, sq, fr: (bb[s], 0))
    st_spec = pl.BlockSpec((None, 2, D_RNN), lambda s, fb, bb, sq, fr: (sq[s], 0, 0))
    out = jax.ShapeDtypeStruct((m, D_RNN), F32)
    return pl.pallas_call(
        functools.partial(_rnn_scan_kernel, tc=tc),
        out_shape=(out, out, jax.ShapeDtypeStruct(h0.shape, F32)),
        grid_spec=pltpu.PrefetchScalarGridSpec(
            num_scalar_prefetch=4, grid=(n_steps,),
            in_specs=[f_spec, f_spec, b_spec, b_spec, st_spec],
            out_specs=(f_spec, b_spec, st_spec),
            scratch_shapes=[pltpu.VMEM((2, D_RNN), F32)]),
        compiler_params=_cparams("arbitrary"),
        name="rnn_scan",
    )(*tables, a_f, u_f, a_b, u_b, h0)


def _ffn_kernel(x_ref, sc_ref, sh_ref, gate_ref, wg_ref, wu_ref, wd_ref, lng_ref, lnb_ref,
                o_ref, h_ref, acc_ref):
    j = pl.program_id(1)

    @pl.when(j == 0)
    def _():
        h_ref[...] = (x_ref[...] * (1.0 + sc_ref[...]) + sh_ref[...]).astype(BF16)
        acc_ref[...] = jnp.zeros_like(acc_ref)

    h = h_ref[...]
    g = jnp.dot(h, wg_ref[...], preferred_element_type=F32)
    u = jnp.dot(h, wu_ref[...], preferred_element_type=F32)
    act = ((g * jax.nn.sigmoid(g)) * u).astype(BF16)
    acc_ref[...] += jnp.dot(act, wd_ref[...], preferred_element_type=F32)

    @pl.when(j == pl.num_programs(1) - 1)
    def _():
        z = DEEPNORM_ALPHA * x_ref[...] + gate_ref[...] * acc_ref[...]
        o_ref[...] = _layer_norm_rows(z, lng_ref[...], lnb_ref[...])


def _ffn(x, mod, w_gu, w_down, lng, lnb, *, n_ctx, t_s):
    m = x.shape[0]
    tm = 1024
    tf = 256
    nf = D_FF // tf
    const = lambda arr: pl.BlockSpec(arr.shape, lambda i, j: (0,) * arr.ndim)
    row = pl.BlockSpec((tm, D_MODEL), lambda i, j: (i, 0))
    return pl.pallas_call(
        _ffn_kernel,
        out_shape=jax.ShapeDtypeStruct((m, D_MODEL), F32),
        grid=(m // tm, nf),
        in_specs=[row, _mod_spec(4, tm, n_ctx, t_s), _mod_spec(3, tm, n_ctx, t_s), _mod_spec(5, tm, n_ctx, t_s),
                  pl.BlockSpec((D_MODEL, tf), lambda i, j: (0, j)),
                  pl.BlockSpec((D_MODEL, tf), lambda i, j: (0, nf + j)),
                  pl.BlockSpec((tf, D_MODEL), lambda i, j: (j, 0)),
                  const(lng), const(lnb)],
        out_specs=row,
        scratch_shapes=[pltpu.VMEM((tm, D_MODEL), BF16), pltpu.VMEM((tm, D_MODEL), F32)],
        compiler_params=_cparams("parallel", "arbitrary"),
        name="ffn",
    )(x, mod, mod, mod, w_gu, w_gu, w_down, lng, lnb)


def _moe_kernel(x_ref, sc_ref, sh_ref, gate_ref, wr_ref, wg_ref, wu_ref, wd_ref, lng_ref, lnb_ref,
                o_ref, h_ref, comb_ref, acc_ref, *, tm):
    e = pl.program_id(1)
    lane = lax.broadcasted_iota(jnp.int32, (tm, LANES), 1)

    @pl.when(e == 0)
    def _():
        h = (x_ref[...] * (1.0 + sc_ref[...]) + sh_ref[...]).astype(BF16)
        h_ref[...] = h
        acc_ref[...] = jnp.zeros_like(acc_ref)
        logits = jnp.dot(h, wr_ref[...], preferred_element_type=F32)
        s1 = jnp.where(lane < N_EXPERTS, logits, -jnp.inf)
        m1 = jnp.max(s1, axis=-1, keepdims=True)
        i1 = jnp.min(jnp.where(s1 == m1, lane, LANES), axis=-1, keepdims=True)
        s2 = jnp.where(lane == i1, -jnp.inf, s1)
        m2 = jnp.max(s2, axis=-1, keepdims=True)
        i2 = jnp.min(jnp.where(s2 == m2, lane, LANES), axis=-1, keepdims=True)
        e2 = jnp.exp(m2 - m1)
        den = 1.0 + e2
        comb_ref[...] = jnp.where(lane == i1, 1.0 / den, 0.0) + jnp.where(lane == i2, e2 / den, 0.0)

    h = h_ref[...]
    g = jnp.dot(h, wg_ref[...], preferred_element_type=F32)
    u = jnp.dot(h, wu_ref[...], preferred_element_type=F32)
    act = ((g * jax.nn.sigmoid(g)) * u).astype(BF16)
    c_e = jnp.sum(jnp.where(lane == e, comb_ref[...], 0.0), axis=-1, keepdims=True)
    acc_ref[...] += c_e * jnp.dot(act, wd_ref[...], preferred_element_type=F32)

    @pl.when(e == pl.num_programs(1) - 1)
    def _():
        z = DEEPNORM_ALPHA * x_ref[...] + gate_ref[...] * acc_ref[...]
        o_ref[...] = _layer_norm_rows(z, lng_ref[...], lnb_ref[...])


def _moe(x, mod, w_router, w_gu, w_down, lng, lnb, *, n_ctx, t_s):
    m = x.shape[0]
    tm = 512
    const = lambda arr: pl.BlockSpec(arr.shape, lambda i, e: (0,) * arr.ndim)
    row = pl.BlockSpec((tm, D_MODEL), lambda i, e: (i, 0))
    return pl.pallas_call(
        functools.partial(_moe_kernel, tm=tm),
        out_shape=jax.ShapeDtypeStruct((m, D_MODEL), F32),
        grid=(m // tm, N_EXPERTS),
        in_specs=[row, _mod_spec(4, tm, n_ctx, t_s), _mod_spec(3, tm, n_ctx, t_s), _mod_spec(5, tm, n_ctx, t_s),
                  const(w_router),
                  pl.BlockSpec((None, D_MODEL, D_FF_EXPERT), lambda i, e: (e, 0, 0)),
                  pl.BlockSpec((None, D_MODEL, D_FF_EXPERT), lambda i, e: (e, 0, 1)),
                  pl.BlockSpec((None, D_FF_EXPERT, D_MODEL), lambda i, e: (e, 0, 0)),
                  const(lng), const(lnb)],
        out_specs=row,
        scratch_shapes=[pltpu.VMEM((tm, D_MODEL), BF16), pltpu.VMEM((tm, LANES), F32),
                        pltpu.VMEM((tm, D_MODEL), F32)],
        compiler_params=_cparams("parallel", "arbitrary"),
        name="moe",
    )(x, mod, mod, mod, w_router, w_gu, w_gu, w_down, lng, lnb)


def _rope_tables(n_tokens):
    n_rows = n_tokens // GRID_W
    row = jnp.repeat(jnp.arange(n_rows, dtype=F32), GRID_W)
    col = jnp.tile(jnp.arange(GRID_W, dtype=F32), n_rows)
    n_axis = HEAD_DIM // 4
    inv = ROPE_THETA ** (-jnp.arange(n_axis, dtype=F32) / n_axis)
    ang = jnp.concatenate([row[:, None] * inv, col[:, None] * inv], axis=-1)
    cos, sin = jnp.cos(ang), jnp.sin(ang)
    reps = LANES // HEAD_DIM
    cos_t = jnp.tile(jnp.concatenate([cos, cos], axis=-1), (1, reps))
    sin_t = jnp.tile(jnp.concatenate([-sin, sin], axis=-1), (1, reps))
    return cos_t, sin_t


def _gqa_head_perm():
    cols = []
    per_pair = 2 * (N_Q_HEADS // N_KV_HEADS)
    for p in range(N_KV_HEADS // 2):
        for j in range(N_Q_HEADS // N_KV_HEADS):
            for hd in (per_pair * p + j, per_pair * p + per_pair // 2 + j):
                cols.append(np.arange(hd * HEAD_DIM, (hd + 1) * HEAD_DIM))
    return np.concatenate(cols)


def _block_diag_dense(w):
    eye = jnp.eye(N_RNN_BLOCKS, dtype=bool)[:, None, :, None]
    return jnp.where(eye, w[:, :, None, :], 0.0).reshape(D_RNN, D_RNN)


def kernel(x_prompt, x_sample, cache_attn_k, cache_attn_v, state_rglru, cache_diff_k, cache_diff_v, c, c_ctx, w_ada, b_ada, ln_g, ln_b, attn_w_qkv, attn_g_q, attn_g_k, attn_w_o, rnn_w_in, rnn_conv_w, rnn_conv_b, rnn_w_a, rnn_b_a, rnn_w_x, rnn_b_x, rnn_lambda, rnn_w_out, diff_w_qkv, diff_lambda, diff_g_sub, diff_w_o, ffn_w_gu, ffn_w_down, moe_w_router, moe_w_gu, moe_w_down):
    b_c, t_c, d = x_prompt.shape
    b_s, t_s, _ = x_sample.shape
    n_ctx = b_c * t_c
    n_s = b_s * t_s
    past = cache_attn_k.shape[2]
    assert d == D_MODEL and n_ctx % t_s == 0 and t_c == SCAN_CHUNK and t_s % SCAN_CHUNK == 0
    assert 1 + b_s <= COND_ROWS

    x = jnp.concatenate([x_prompt.reshape(n_ctx, d), x_sample.reshape(n_s, d)], axis=0)
    cond = jnp.zeros((COND_ROWS, d), F32).at[0].set(c_ctx).at[1:1 + b_s].set(c)
    mods = _ada_table(cond, w_ada, b_ada)[:, :1 + b_s].reshape(DEPTH, 1 + b_s, 1, 6 * d)

    cos_t, sin_t = _rope_tables(t_s)
    perm = _gqa_head_perm()
    nq = N_Q_HEADS * HEAD_DIM
    tile2 = lambda g: jnp.tile(g, LANES // g.shape[-1]).reshape(1, LANES)
    kw = dict(n_ctx=n_ctx, t_s=t_s)

    attn_k, attn_v, rnn_s, diff_k, diff_v = [], [], [], [], []
    for li in range(DEPTH):
        mod = mods[li]
        lng = ln_g[li].reshape(2, 1, d)
        lnb = ln_b[li].reshape(2, 1, d)
        j = li // N_MIXERS
        kind = li % N_MIXERS
        if kind == 0:
            w = attn_w_qkv[j]
            w = jnp.concatenate([w[:, :nq][:, perm], w[:, nq:]], axis=1).astype(BF16)
            q, kb, vb, kf, vf = _gqa_qkv(x, mod, w, tile2(attn_g_q[j]), tile2(attn_g_k[j]), cos_t, sin_t, **kw)
            ck = cache_attn_k[:, j].reshape(b_s, past, N_KV_HEADS * HEAD_DIM).astype(BF16)
            cv = cache_attn_v[:, j].reshape(b_s, past, N_KV_HEADS * HEAD_DIM).astype(BF16)
            o = _gqa_attention(q, kb, vb, None, None, None, row0=0, n_b=b_c, t=t_c, tq=t_c)
            o = _gqa_attention(q, kb, vb, o, ck, cv, row0=n_ctx, n_b=b_s, t=t_s, tq=256)
            x = _oproj(o, attn_w_o[j][perm, :].astype(BF16), x, mod, lng[0], lnb[0], **kw)
            attn_k.append(kf[:n_ctx])
            attn_v.append(vf[:n_ctx])
        elif kind == 1:
            bx = _rnn_in(x, mod, rnn_w_in[j].astype(BF16), **kw)
            wg = jnp.concatenate([_block_diag_dense(rnn_w_a[j, 0]), _block_diag_dense(rnn_w_x[j, 0]),
                                  _block_diag_dense(rnn_w_a[j, 1]), _block_diag_dense(rnn_w_x[j, 1])],
                                 axis=1).astype(BF16)
            bg = jnp.concatenate([rnn_b_a[j, 0], rnn_b_x[j, 0], rnn_b_a[j, 1], rnn_b_x[j, 1]]).reshape(1, -1)
            a_f, u_f, a_b, u_b = _rnn_gates(bx, rnn_conv_w[j], rnn_conv_b[j].reshape(1, -1), wg, bg,
                                            rnn_lambda[j], n_ctx=n_ctx, t_c=t_c, t_s=t_s)
            h0 = jnp.concatenate([jnp.zeros((b_c, 2, D_RNN), F32), state_rglru[:, j]], axis=0)
            hs_f, hs_b, fin = _rnn_scan(a_f, u_f, a_b, u_b, h0, seq_lens=[t_c] * b_c + [t_s] * b_s)
            x = _rnn_oproj(hs_f, hs_b, bx, rnn_w_out[j].astype(BF16), x, mod, lng[0], lnb[0], **kw)
            rnn_s.append(fin[:b_c])
        else:
            lam_init = 0.8 - 0.6 * math.exp(-0.3 * li)
            q, kb, vb, kf, vf = _diff_qkv(x, mod, diff_w_qkv[j].astype(BF16), cos_t, sin_t, **kw)
            ck = cache_diff_k[:, j].reshape(b_s, past, D_MODEL).astype(BF16)
            cv = cache_diff_v[:, j].reshape(b_s, past, D_MODEL).astype(BF16)
            g_sub = diff_g_sub[j].reshape(1, LANES)
            o = _diff_attention(diff_lambda[j], g_sub, q, kb, vb, None, None, None,
                                row0=0, n_b=b_c, t=t_c, tq=t_c, lam_init=lam_init)
            o = _diff_attention(diff_lambda[j], g_sub, q, kb, vb, o, ck, cv,
                                row0=n_ctx, n_b=b_s, t=t_s, tq=256, lam_init=lam_init)
            x = _oproj(o, diff_w_o[j].astype(BF16), x, mod, lng[0], lnb[0], **kw)
            diff_k.append(kf[:n_ctx])
            diff_v.append(vf[:n_ctx])
        if li % 2 == 0:
            x = _ffn(x, mod, ffn_w_gu[li // 2].astype(BF16), ffn_w_down[li // 2].astype(BF16), lng[1], lnb[1], **kw)
        else:
            w_r = jnp.zeros((d, LANES), F32).at[:, :N_EXPERTS].set(moe_w_router[li // 2]).astype(BF16)
            x = _moe(x, mod, w_r, moe_w_gu[li // 2].astype(BF16), moe_w_down[li // 2].astype(BF16),
                     lng[1], lnb[1], **kw)

    y_prompt = x[:n_ctx].reshape(b_c, t_c, d)
    y_sample = x[n_ctx:].reshape(b_s, t_s, d)
    new_attn_k = jnp.stack([k.reshape(b_c, t_c, N_KV_HEADS, HEAD_DIM) for k in attn_k], axis=1)
    new_attn_v = jnp.stack([v.reshape(b_c, t_c, N_KV_HEADS, HEAD_DIM) for v in attn_v], axis=1)
    new_state = jnp.stack(rnn_s, axis=1)
    new_diff_k = jnp.stack([k.reshape(b_c, t_c, N_DIFF_HEADS, 2, HEAD_DIM) for k in diff_k], axis=1)
    new_diff_v = jnp.stack([v.reshape(b_c, t_c, N_DIFF_HEADS, 2 * HEAD_DIM) for v in diff_v], axis=1)
    return (y_prompt, y_sample, new_attn_k, new_attn_v, new_state, new_diff_k, new_diff_v)
```

```python
import functools
import math

import jax
import jax.numpy as jnp
import numpy as np
from jax import lax
from jax.experimental import pallas as pl
from jax.experimental.pallas import tpu as pltpu

F32 = jnp.float32
BF16 = jnp.bfloat16

D_MODEL = 1024
DEPTH = 4
GRID_W = 64
HEAD_DIM = 64
N_Q_HEADS = 16
N_KV_HEADS = 4
ROPE_THETA = 10000.0
N_DIFF_HEADS = 8
D_RNN = 1280
N_RNN_BLOCKS = 16
RNN_BLOCK = 80
CONV_W = 4
CONV_LEFT = 2
RGLRU_C = 8.0
D_FF = 2816
N_EXPERTS = 8
D_FF_EXPERT = 1408
N_MIXERS = 3
DEEPNORM_ALPHA = (2.0 * DEPTH) ** 0.25
LN_EPS = 1e-6
RMS_EPS = 1e-6

LANES = 128
SUBLANES = 8
VMEM_LIMIT = 56 * 1024 * 1024
COND_ROWS = 8
SCAN_CHUNK = 256
GATE_N = 256
GATE_K = 512
GATE_K0 = (0, 128, 384, 640, 768)
Q_SCALE = HEAD_DIM ** -0.5 * math.log2(math.e)


def _cparams(*sem):
    return pltpu.CompilerParams(dimension_semantics=sem, vmem_limit_bytes=VMEM_LIMIT)


def _layer_norm_rows(z, g, b):
    mu = jnp.mean(z, axis=-1, keepdims=True)
    zc = z - mu
    var = jnp.mean(zc * zc, axis=-1, keepdims=True)
    return zc * lax.rsqrt(var + LN_EPS) * g + b


def _seg_of_block(i, tm, n_ctx, t_s):
    r0 = i * tm
    return jnp.where(r0 < n_ctx, 0, 1 + (r0 - n_ctx) // t_s)


def _mod_spec(col, tm, n_ctx, t_s):
    return pl.BlockSpec((None, 1, D_MODEL), lambda i, *_: (_seg_of_block(i, tm, n_ctx, t_s), 0, col))


def _ada_kernel(c_ref, w_ref, b_ref, o_ref):
    c = c_ref[...]
    a = (c * jax.nn.sigmoid(c)).astype(BF16)
    o_ref[...] = jnp.dot(a, w_ref[...].astype(BF16), preferred_element_type=F32) + b_ref[...]


def _ada_table(cond, w_ada, b_ada):
    n_l, d, n = w_ada.shape
    tn = 1536
    return pl.pallas_call(
        _ada_kernel,
        out_shape=jax.ShapeDtypeStruct((n_l, COND_ROWS, n), F32),
        grid=(n_l, n // tn),
        in_specs=[pl.BlockSpec((COND_ROWS, d), lambda l, j: (0, 0)),
                  pl.BlockSpec((None, d, tn), lambda l, j: (l, 0, j)),
                  pl.BlockSpec((None, 1, tn), lambda l, j: (l, 0, j))],
        out_specs=pl.BlockSpec((None, COND_ROWS, tn), lambda l, j: (l, 0, j)),
        compiler_params=_cparams("parallel", "parallel"),
        name="ada_table",
    )(cond, w_ada, b_ada.reshape(n_l, 1, n))


def _head_masks(tm):
    lane = lax.broadcasted_iota(jnp.int32, (tm, LANES), 1)
    return lane < HEAD_DIM, (lane & (HEAD_DIM - 1)) < HEAD_DIM // 2


def _rope_tile(t, first_half, cos, sin):
    partner = jnp.where(first_half, pltpu.roll(t, LANES - HEAD_DIM // 2, 1),
                        pltpu.roll(t, HEAD_DIM // 2, 1))
    return t * cos + partner * sin


def _gqa_qkv_kernel(x_ref, sc_ref, sh_ref, w_ref, gq_ref, gk_ref, cos_ref, sin_ref,
                    q_ref, kb_ref, vb_ref, kf_ref, vf_ref, *, tm, n_ctx):
    i = pl.program_id(0)
    h = (x_ref[...] * (1.0 + sc_ref[...]) + sh_ref[...]).astype(BF16)
    y = jnp.dot(h, w_ref[...], preferred_element_type=F32)
    use_rope = i * tm >= n_ctx
    cos = jnp.where(use_rope, cos_ref[...], 1.0)
    sin = jnp.where(use_rope, sin_ref[...], 0.0)
    low_head, first_half = _head_masks(tm)

    def norm_rope(t, g):
        t2 = t * t
        s_lo = jnp.sum(jnp.where(low_head, t2, 0.0), axis=-1, keepdims=True)
        s_hi = jnp.sum(jnp.where(low_head, 0.0, t2), axis=-1, keepdims=True)
        inv = jnp.where(low_head, lax.rsqrt(s_lo * (1.0 / HEAD_DIM) + RMS_EPS),
                        lax.rsqrt(s_hi * (1.0 / HEAD_DIM) + RMS_EPS))
        return _rope_tile(t * inv * g, first_half, cos, sin)

    nq = N_Q_HEADS * HEAD_DIM
    nkv = N_KV_HEADS * HEAD_DIM
    gq = gq_ref[...]
    gk = gk_ref[...]
    for t in range(nq // LANES):
        sl = slice(t * LANES, (t + 1) * LANES)
        q_ref[:, sl] = (norm_rope(y[:, sl], gq) * Q_SCALE).astype(BF16)
    for t in range(nkv // LANES):
        sl = slice(t * LANES, (t + 1) * LANES)
        k = norm_rope(y[:, nq + t * LANES: nq + (t + 1) * LANES], gk)
        kf_ref[:, sl] = k
        kb_ref[:, sl] = k.astype(BF16)
    v = y[:, nq + nkv:]
    vf_ref[...] = v
    vb_ref[...] = v.astype(BF16)


def _gqa_qkv(x, mod, w, gq, gk, cos, sin, *, n_ctx, t_s):
    m = x.shape[0]
    tm = 512
    nq = N_Q_HEADS * HEAD_DIM
    nkv = N_KV_HEADS * HEAD_DIM
    n_rope_blocks = t_s // tm
    rope_spec = pl.BlockSpec(
        (tm, LANES), lambda i: (jnp.where(i * tm >= n_ctx, ((i * tm - n_ctx) % t_s) // tm, 0) % n_rope_blocks, 0))
    row = lambda n: pl.BlockSpec((tm, n), lambda i: (i, 0))
    const = lambda a: pl.BlockSpec(a.shape, lambda i: (0,) * a.ndim)
    return pl.pallas_call(
        functools.partial(_gqa_qkv_kernel, tm=tm, n_ctx=n_ctx),
        out_shape=(jax.ShapeDtypeStruct((m, nq), BF16), jax.ShapeDtypeStruct((m, nkv), BF16),
                   jax.ShapeDtypeStruct((m, nkv), BF16), jax.ShapeDtypeStruct((m, nkv), F32),
                   jax.ShapeDtypeStruct((m, nkv), F32)),
        grid=(m // tm,),
        in_specs=[row(D_MODEL), _mod_spec(1, tm, n_ctx, t_s), _mod_spec(0, tm, n_ctx, t_s),
                  const(w), const(gq), const(gk), rope_spec, rope_spec],
        out_specs=(row(nq), row(nkv), row(nkv), row(nkv), row(nkv)),
        compiler_params=_cparams("parallel"),
        name="gqa_qkv",
    )(x, mod, mod, w, gq, gk, cos, sin)


def _diff_qkv_kernel(x_ref, sc_ref, sh_ref, w_ref, cos_ref, sin_ref,
                     q_ref, kb_ref, vb_ref, kf_ref, vf_ref, *, tm, n_ctx):
    i = pl.program_id(0)
    h = (x_ref[...] * (1.0 + sc_ref[...]) + sh_ref[...]).astype(BF16)
    y = jnp.dot(h, w_ref[...], preferred_element_type=F32)
    use_rope = i * tm >= n_ctx
    cos = jnp.where(use_rope, cos_ref[...], 1.0)
    sin = jnp.where(use_rope, sin_ref[...], 0.0)
    _, first_half = _head_masks(tm)
    for t in range(D_MODEL // LANES):
        sl = slice(t * LANES, (t + 1) * LANES)
        q = _rope_tile(y[:, sl], first_half, cos, sin)
        q_ref[:, sl] = (q * Q_SCALE).astype(BF16)
        k = _rope_tile(y[:, D_MODEL + t * LANES: D_MODEL + (t + 1) * LANES], first_half, cos, sin)
        kf_ref[:, sl] = k
        kb_ref[:, sl] = k.astype(BF16)
    v = y[:, 2 * D_MODEL:]
    vf_ref[...] = v
    vb_ref[...] = v.astype(BF16)


def _diff_qkv(x, mod, w, cos, sin, *, n_ctx, t_s):
    m = x.shape[0]
    tm = 256
    n_rope_blocks = t_s // tm
    rope_spec = pl.BlockSpec(
        (tm, LANES), lambda i: (jnp.where(i * tm >= n_ctx, ((i * tm - n_ctx) % t_s) // tm, 0) % n_rope_blocks, 0))
    row = pl.BlockSpec((tm, D_MODEL), lambda i: (i, 0))
    return pl.pallas_call(
        functools.partial(_diff_qkv_kernel, tm=tm, n_ctx=n_ctx),
        out_shape=(jax.ShapeDtypeStruct((m, D_MODEL), BF16), jax.ShapeDtypeStruct((m, D_MODEL), BF16),
                   jax.ShapeDtypeStruct((m, D_MODEL), BF16), jax.ShapeDtypeStruct((m, D_MODEL), F32),
                   jax.ShapeDtypeStruct((m, D_MODEL), F32)),
        grid=(m // tm,),
        in_specs=[row, _mod_spec(1, tm, n_ctx, t_s), _mod_spec(0, tm, n_ctx, t_s),
                  pl.BlockSpec(w.shape, lambda i: (0, 0)), rope_spec, rope_spec],
        out_specs=(row, row, row, row, row),
        compiler_params=_cparams("parallel"),
        name="diff_qkv",
    )(x, mod, mod, w, cos, sin)


_NT = (((1,), (1,)), ((), ()))


def _with_ones_column(v, low_head, lane, keep_low):
    ones_lane = HEAD_DIM if keep_low else 0
    other = jnp.where(lane == ones_lane, 1.0, 0.0).astype(v.dtype)
    return jnp.where(low_head == keep_low, v, other)


def _gqa_attn_kernel(*refs, has_cache, tq):
    if has_cache:
        q_ref, k_ref, v_ref, ck_ref, cv_ref, o_ref = refs
    else:
        q_ref, k_ref, v_ref, o_ref = refs
    low_q = lax.broadcasted_iota(jnp.int32, (tq, LANES), 1) < HEAD_DIM
    k = k_ref[...]
    lane_v = lax.broadcasted_iota(jnp.int32, v_ref.shape, 1)
    v_aug = [_with_ones_column(v_ref[...], lane_v < HEAD_DIM, lane_v, keep) for keep in (True, False)]
    if has_cache:
        ck = ck_ref[...]
        lane_c = lax.broadcasted_iota(jnp.int32, cv_ref.shape, 1)
        cv_aug = [_with_ones_column(cv_ref[...], lane_c < HEAD_DIM, lane_c, keep) for keep in (True, False)]
    n_tiles = q_ref.shape[1] // LANES
    heads = [(j, keep) for j in range(n_tiles) for keep in (True, False)]

    def scores(j, keep_low):
        qt = q_ref[:, j * LANES:(j + 1) * LANES]
        qh = jnp.where(low_q == keep_low, qt, jnp.zeros_like(qt))
        s = lax.dot_general(qh, k, _NT, preferred_element_type=F32)
        s_c = lax.dot_general(qh, ck, _NT, preferred_element_type=F32) if has_cache else None
        return s, s_c

    nxt = scores(*heads[0])
    halves = []
    for idx, (j, keep_low) in enumerate(heads):
        s, s_c = nxt
        if idx + 1 < len(heads):
            nxt = scores(*heads[idx + 1])
        mx = jnp.max(s, axis=-1, keepdims=True)
        if has_cache:
            mx = jnp.maximum(mx, jnp.max(s_c, axis=-1, keepdims=True))
        side = 0 if keep_low else 1
        acc = jnp.dot(jnp.exp2(s - mx).astype(BF16), v_aug[side], preferred_element_type=F32)
        if has_cache:
            acc = acc + jnp.dot(jnp.exp2(s_c - mx).astype(BF16), cv_aug[side], preferred_element_type=F32)
        den = acc[:, HEAD_DIM:HEAD_DIM + 1] if keep_low else acc[:, 0:1]
        halves.append(acc / den)
        if not keep_low:
            o_ref[:, j * LANES:(j + 1) * LANES] = jnp.where(low_q, halves[0], halves[1]).astype(BF16)
            halves = []


def _gqa_attention(q, k, v, o_prev, cache_k, cache_v, *, row0, n_b, t, tq):
    m = q.shape[0]
    n_pairs = N_KV_HEADS // 2
    qw = D_MODEL // n_pairs
    nq_blocks = t // tq
    has_cache = cache_k is not None
    q_spec = pl.BlockSpec((tq, qw), lambda b, p, iq: (row0 // tq + b * nq_blocks + iq, p))
    kv_spec = pl.BlockSpec((t, LANES), lambda b, p, iq: (row0 // t + b, p))
    in_specs = [q_spec, kv_spec, kv_spec]
    args = [q, k, v]
    if has_cache:
        c_spec = pl.BlockSpec((None, cache_k.shape[1], LANES), lambda b, p, iq: (b, 0, p))
        in_specs += [c_spec, c_spec]
        args += [cache_k, cache_v]
    aliases = {}
    if o_prev is not None:
        in_specs.append(pl.BlockSpec(memory_space=pl.ANY))
        args.append(o_prev)
        aliases = {len(args) - 1: 0}
        kern = functools.partial(_with_ignored_alias, functools.partial(_gqa_attn_kernel, has_cache=has_cache, tq=tq),
                                 n_in=len(args))
    else:
        kern = functools.partial(_gqa_attn_kernel, has_cache=has_cache, tq=tq)
    return pl.pallas_call(
        kern,
        out_shape=jax.ShapeDtypeStruct((m, D_MODEL), BF16),
        grid=(n_b, n_pairs, nq_blocks),
        in_specs=in_specs,
        out_specs=q_spec,
        input_output_aliases=aliases,
        compiler_params=_cparams("parallel", "parallel", "arbitrary"),
        name="gqa_attn_cache" if has_cache else "gqa_attn",
    )(*args)


def _with_ignored_alias(kern, *refs, n_in):
    kern(*refs[:n_in - 1], *refs[n_in:])


def _diff_attn_kernel(*refs, has_cache, tq, lam_init):
    if has_cache:
        lam_ref, g_ref, q_ref, k_ref, v_ref, ck_ref, cv_ref, o_ref = refs
    else:
        lam_ref, g_ref, q_ref, k_ref, v_ref, o_ref = refs
    lv = lam_ref[...]
    lam = (jnp.exp(jnp.sum(lv[0:1] * lv[1:2], axis=-1, keepdims=True))
           - jnp.exp(jnp.sum(lv[2:3] * lv[3:4], axis=-1, keepdims=True)) + lam_init)
    low_head = lax.broadcasted_iota(jnp.int32, (tq, LANES), 1) < HEAD_DIM
    qt = q_ref[...]
    k = k_ref[...]
    probs = []
    for keep_low in (True, False):
        qh = jnp.where(low_head == keep_low, qt, jnp.zeros_like(qt))
        s = lax.dot_general(qh, k, _NT, preferred_element_type=F32)
        mx = jnp.max(s, axis=-1, keepdims=True)
        if has_cache:
            s_c = lax.dot_general(qh, ck_ref[...], _NT, preferred_element_type=F32)
            mx = jnp.maximum(mx, jnp.max(s_c, axis=-1, keepdims=True))
        e = jnp.exp2(s - mx)
        den = jnp.sum(e, axis=-1, keepdims=True)
        if has_cache:
            e_c = jnp.exp2(s_c - mx)
            den = den + jnp.sum(e_c, axis=-1, keepdims=True)
            inv = 1.0 / den
            probs.append((e * inv, e_c * inv))
        else:
            probs.append((e * (1.0 / den),))
    pd = (probs[0][0] - lam * probs[1][0]).astype(BF16)
    o = jnp.dot(pd, v_ref[...], preferred_element_type=F32)
    if has_cache:
        pd_c = (probs[0][1] - lam * probs[1][1]).astype(BF16)
        o = o + jnp.dot(pd_c, cv_ref[...], preferred_element_type=F32)
    inv = lax.rsqrt(jnp.mean(o * o, axis=-1, keepdims=True) + RMS_EPS)
    o_ref[...] = ((o * inv * g_ref[...]) * (1.0 - lam_init)).astype(BF16)


def _diff_attention(lam_vec, g_sub, q, k, v, o_prev, cache_k, cache_v, *, row0, n_b, t, tq, lam_init):
    m = q.shape[0]
    nq_blocks = t // tq
    has_cache = cache_k is not None
    q_spec = pl.BlockSpec((tq, LANES), lambda b, h, iq: (row0 // tq + b * nq_blocks + iq, h))
    kv_spec = pl.BlockSpec((t, LANES), lambda b, h, iq: (row0 // t + b, h))
    const = lambda a: pl.BlockSpec(a.shape, lambda b, h, iq: (0,) * a.ndim)
    in_specs = [const(lam_vec), const(g_sub), q_spec, kv_spec, kv_spec]
    args = [lam_vec, g_sub, q, k, v]
    if has_cache:
        c_spec = pl.BlockSpec((None, cache_k.shape[1], LANES), lambda b, h, iq: (b, 0, h))
        in_specs += [c_spec, c_spec]
        args += [cache_k, cache_v]
    base = functools.partial(_diff_attn_kernel, has_cache=has_cache, tq=tq, lam_init=lam_init)
    aliases = {}
    kern = base
    if o_prev is not None:
        in_specs.append(pl.BlockSpec(memory_space=pl.ANY))
        args.append(o_prev)
        aliases = {len(args) - 1: 0}
        kern = functools.partial(_with_ignored_alias, base, n_in=len(args))
    return pl.pallas_call(
        kern,
        out_shape=jax.ShapeDtypeStruct((m, D_MODEL), BF16),
        grid=(n_b, N_DIFF_HEADS, nq_blocks),
        in_specs=in_specs,
        out_specs=q_spec,
        input_output_aliases=aliases,
        compiler_params=_cparams("parallel", "parallel", "arbitrary"),
        name="diff_attn_cache" if has_cache else "diff_attn",
    )(*args)


def _oproj_kernel(a_ref, w_ref, x_ref, gate_ref, lng_ref, lnb_ref, o_ref):
    out = jnp.dot(a_ref[...], w_ref[...], preferred_element_type=F32)
    z = DEEPNORM_ALPHA * x_ref[...] + gate_ref[...] * out
    o_ref[...] = _layer_norm_rows(z, lng_ref[...], lnb_ref[...])


def _oproj(a, w, x, mod, lng, lnb, *, n_ctx, t_s):
    m, kdim = a.shape
    tm = 512
    const = lambda arr: pl.BlockSpec(arr.shape, lambda i: (0,) * arr.ndim)
    row = pl.BlockSpec((tm, D_MODEL), lambda i: (i, 0))
    return pl.pallas_call(
        _oproj_kernel,
        out_shape=jax.ShapeDtypeStruct((m, D_MODEL), F32),
        grid=(m // tm,),
        in_specs=[pl.BlockSpec((tm, kdim), lambda i: (i, 0)), const(w), row,
                  _mod_spec(2, tm, n_ctx, t_s), const(lng), const(lnb)],
        out_specs=row,
        compiler_params=_cparams("parallel"),
        name="oproj_ln",
    )(a, w, x, mod, lng, lnb)


def _gelu_tanh(x):
    return 0.5 * x * (1.0 + jnp.tanh(math.sqrt(2.0 / math.pi) * (x + 0.044715 * (x * x * x))))


def _rnn_oproj_kernel(hf_ref, hb_ref, br_ref, w_ref, x_ref, gate_ref, lng_ref, lnb_ref, o_ref):
    y = (hf_ref[...] + hb_ref[...]) * _gelu_tanh(br_ref[...])
    out = jnp.dot(y.astype(BF16), w_ref[...], preferred_element_type=F32)
    z = DEEPNORM_ALPHA * x_ref[...] + gate_ref[...] * out
    o_ref[...] = _layer_norm_rows(z, lng_ref[...], lnb_ref[...])


def _rnn_oproj(hs_f, hs_b, bx, w, x, mod, lng, lnb, *, n_ctx, t_s):
    m = x.shape[0]
    tm = 512
    const = lambda arr: pl.BlockSpec(arr.shape, lambda i: (0,) * arr.ndim)
    row = pl.BlockSpec((tm, D_MODEL), lambda i: (i, 0))
    rnn_row = pl.BlockSpec((tm, D_RNN), lambda i: (i, 0))
    return pl.pallas_call(
        _rnn_oproj_kernel,
        out_shape=jax.ShapeDtypeStruct((m, D_MODEL), F32),
        grid=(m // tm,),
        in_specs=[rnn_row, rnn_row, rnn_row, const(w), row,
                  _mod_spec(2, tm, n_ctx, t_s), const(lng), const(lnb)],
        out_specs=row,
        compiler_params=_cparams("parallel"),
        name="rnn_oproj_ln",
    )(hs_f, hs_b, bx, w, x, mod, lng, lnb)


def _mod_mm_kernel(x_ref, sc_ref, sh_ref, w_ref, o_ref):
    h = (x_ref[...] * (1.0 + sc_ref[...]) + sh_ref[...]).astype(BF16)
    o_ref[...] = jnp.dot(h, w_ref[...], preferred_element_type=F32)


def _rnn_in(x, mod, w, *, n_ctx, t_s):
    m = x.shape[0]
    n = w.shape[1]
    tm = 512
    return pl.pallas_call(
        _mod_mm_kernel,
        out_shape=jax.ShapeDtypeStruct((m, n), F32),
        grid=(m // tm,),
        in_specs=[pl.BlockSpec((tm, D_MODEL), lambda i: (i, 0)),
                  _mod_spec(1, tm, n_ctx, t_s), _mod_spec(0, tm, n_ctx, t_s),
                  pl.BlockSpec(w.shape, lambda i: (0, 0))],
        out_specs=pl.BlockSpec((tm, n), lambda i: (i, 0)),
        compiler_params=_cparams("parallel"),
        name="rnn_in",
    )(x, mod, mod, w)


def _rnn_gate_kernel(xb_ref, prev_ref, next_ref, cw_ref, cb_ref, wg_ref, bg_ref, lam_ref,
                     af_ref, uf_ref, ab_ref, ub_ref, ext_ref, *, tm, n_ctx, t_c, t_s):
    i = pl.program_id(0)
    r0 = i * tm
    pos = jnp.where(r0 < n_ctx, r0 % t_c, (r0 - n_ctx) % t_s)
    t_seq = jnp.where(r0 < n_ctx, t_c, t_s)
    at_start = pos == 0
    at_end = pos + tm == t_seq
    ext_ref[0:SUBLANES, :] = jnp.where(at_start, 0.0, prev_ref[...])
    ext_ref[SUBLANES:SUBLANES + tm, :] = xb_ref[...]
    ext_ref[SUBLANES + tm:, :] = jnp.where(at_end, 0.0, next_ref[...])
    xc = 0.0
    for j in range(CONV_W):
        off = SUBLANES - CONV_LEFT + j
        xc = xc + ext_ref[off:off + tm, :] * cw_ref[j:j + 1, :]
    xc = xc + cb_ref[...]
    xcb = xc.astype(BF16)
    neg_lam = -lam_ref[...]
    sp = jnp.maximum(neg_lam, 0.0) + jnp.log1p(jnp.exp(-jnp.abs(neg_lam)))

    def gate(g, ct, cols):
        k0 = GATE_K0[ct]
        pre = jnp.dot(xcb[:, k0:k0 + GATE_K], wg_ref[g, ct], preferred_element_type=F32)
        return jax.nn.sigmoid(pre + bg_ref[:, g * D_RNN + cols.start:g * D_RNN + cols.stop])

    for z, (a_ref, u_ref) in enumerate(((af_ref, uf_ref), (ab_ref, ub_ref))):
        for ct in range(D_RNN // GATE_N):
            cols = slice(ct * GATE_N, (ct + 1) * GATE_N)
            r = gate(2 * z, ct, cols)
            g_in = gate(2 * z + 1, ct, cols)
            log_a = -RGLRU_C * r * sp[z:z + 1, cols]
            a = jnp.exp(log_a)
            a_ref[:, cols] = a
            u_ref[:, cols] = jnp.sqrt(-jnp.tanh(log_a) * (a * a + 1.0)) * (g_in * xc[:, cols])


def _rnn_gates(bx, conv_w, conv_b, wg, bg, lam, *, n_ctx, t_c, t_s):
    m = bx.shape[0]
    tm = 256
    n8 = tm // SUBLANES
    last8 = m // SUBLANES - 1
    const = lambda arr: pl.BlockSpec(arr.shape, lambda i: (0,) * arr.ndim)
    row = pl.BlockSpec((tm, D_RNN), lambda i: (i, 0))
    out = jax.ShapeDtypeStruct((m, D_RNN), F32)
    return pl.pallas_call(
        functools.partial(_rnn_gate_kernel, tm=tm, n_ctx=n_ctx, t_c=t_c, t_s=t_s),
        out_shape=(out, out, out, out),
        grid=(m // tm,),
        in_specs=[pl.BlockSpec((tm, D_RNN), lambda i: (i, 1)),
                  pl.BlockSpec((SUBLANES, D_RNN), lambda i: (jnp.maximum(i * n8 - 1, 0), 1)),
                  pl.BlockSpec((SUBLANES, D_RNN), lambda i: (jnp.minimum((i + 1) * n8, last8), 1)),
                  const(conv_w), const(conv_b), const(wg), const(bg), const(lam)],
        out_specs=(row, row, row, row),
        scratch_shapes=[pltpu.VMEM((tm + 2 * SUBLANES, D_RNN), F32)],
        compiler_params=_cparams("parallel"),
        name="rnn_gates",
    )(bx, bx, bx, conv_w, conv_b, wg, bg, lam)


def _rnn_scan_kernel(fblk_ref, bblk_ref, seq_ref, first_ref,
                     af_ref, uf_ref, ab_ref, ub_ref, h0_ref, hf_ref, hb_ref, fin_ref, carry_ref, *, tc):
    s = pl.program_id(0)

    @pl.when(first_ref[s] == 1)
    def _():
        carry_ref[...] = h0_ref[...]

    def step(t, carry):
        h_f, h_b = carry
        tb = tc - 1 - t
        h_f = af_ref[pl.ds(t, 1), :] * h_f + uf_ref[pl.ds(t, 1), :]
        h_b = ab_ref[pl.ds(tb, 1), :] * h_b + ub_ref[pl.ds(tb, 1), :]
        hf_ref[pl.ds(t, 1), :] = h_f
        hb_ref[pl.ds(tb, 1), :] = h_b
        return h_f, h_b

    h_f, h_b = lax.fori_loop(0, tc, step, (carry_ref[0:1, :], carry_ref[1:2, :]), unroll=8)
    carry_ref[0:1, :] = h_f
    carry_ref[1:2, :] = h_b
    fin_ref[...] = carry_ref[...]


def _rnn_scan(a_f, u_f, a_b, u_b, h0, *, seq_lens):
    m = a_f.shape[0]
    tc = SCAN_CHUNK
    fblk, bblk, seq, first = [], [], [], []
    blk0 = 0
    for si, t in enumerate(seq_lens):
        nch = t // tc
        for c in range(nch):
            fblk.append(blk0 + c)
            bblk.append(blk0 + nch - 1 - c)
            seq.append(si)
            first.append(1 if c == 0 else 0)
        blk0 += nch
    n_steps = len(fblk)
    tables = [jnp.asarray(np.array(v, np.int32)) for v in (fblk, bblk, seq, first)]
    f_spec = pl.BlockSpec((tc, D_RNN), lambda s, fb, bb, sq, fr: (fb[s], 0))
    b_spec = pl.BlockSpec((tc, D_RNN), lambda s, fb, bb, sq, fr: (bb[s], 0))
    st_spec = pl.BlockSpec((None, 2, D_RNN), lambda s, fb, bb, sq, fr: (sq[s], 0, 0))
    out = jax.ShapeDtypeStruct((m, D_RNN), F32)
    return pl.pallas_call(
        functools.partial(_rnn_scan_kernel, tc=tc),
        out_shape=(out, out, jax.ShapeDtypeStruct(h0.shape, F32)),
        grid_spec=pltpu.PrefetchScalarGridSpec(
            num_scalar_prefetch=4, grid=(n_steps,),
            in_specs=[f_spec, f_spec, b_spec, b_spec, st_spec],
            out_specs=(f_spec, b_spec, st_spec),
            scratch_shapes=[pltpu.VMEM((2, D_RNN), F32)]),
        compiler_params=_cparams("arbitrary"),
        name="rnn_scan",
    )(*tables, a_f, u_f, a_b, u_b, h0)


def _ffn_kernel(x_ref, sc_ref, sh_ref, gate_ref, wg_ref, wu_ref, wd_ref, lng_ref, lnb_ref,
                o_ref, h_ref, acc_ref):
    j = pl.program_id(1)

    @pl.when(j == 0)
    def _():
        h_ref[...] = (x_ref[...] * (1.0 + sc_ref[...]) + sh_ref[...]).astype(BF16)
        acc_ref[...] = jnp.zeros_like(acc_ref)

    h = h_ref[...]
    g = jnp.dot(h, wg_ref[...], preferred_element_type=F32)
    u = jnp.dot(h, wu_ref[...], preferred_element_type=F32)
    act = ((g * jax.nn.sigmoid(g)) * u).astype(BF16)
    acc_ref[...] += jnp.dot(act, wd_ref[...], preferred_element_type=F32)

    @pl.when(j == pl.num_programs(1) - 1)
    def _():
        z = DEEPNORM_ALPHA * x_ref[...] + gate_ref[...] * acc_ref[...]
        o_ref[...] = _layer_norm_rows(z, lng_ref[...], lnb_ref[...])


def _ffn(x, mod, w_gu, w_down, lng, lnb, *, n_ctx, t_s):
    m = x.shape[0]
    tm = 1024
    tf = 256
    nf = D_FF // tf
    const = lambda arr: pl.BlockSpec(arr.shape, lambda i, j: (0,) * arr.ndim)
    row = pl.BlockSpec((tm, D_MODEL), lambda i, j: (i, 0))
    return pl.pallas_call(
        _ffn_kernel,
        out_shape=jax.ShapeDtypeStruct((m, D_MODEL), F32),
        grid=(m // tm, nf),
        in_specs=[row, _mod_spec(4, tm, n_ctx, t_s), _mod_spec(3, tm, n_ctx, t_s), _mod_spec(5, tm, n_ctx, t_s),
                  pl.BlockSpec((D_MODEL, tf), lambda i, j: (0, j)),
                  pl.BlockSpec((D_MODEL, tf), lambda i, j: (0, nf + j)),
                  pl.BlockSpec((tf, D_MODEL), lambda i, j: (j, 0)),
                  const(lng), const(lnb)],
        out_specs=row,
        scratch_shapes=[pltpu.VMEM((tm, D_MODEL), BF16), pltpu.VMEM((tm, D_MODEL), F32)],
        compiler_params=_cparams("parallel", "arbitrary"),
        name="ffn",
    )(x, mod, mod, mod, w_gu, w_gu, w_down, lng, lnb)


def _moe_kernel(x_ref, sc_ref, sh_ref, gate_ref, wr_ref, wg_ref, wu_ref, wd_ref, lng_ref, lnb_ref,
                o_ref, h_ref, comb_ref, acc_ref, *, tm):
    e = pl.program_id(1)
    lane = lax.broadcasted_iota(jnp.int32, (tm, LANES), 1)

    @pl.when(e == 0)
    def _():
        h = (x_ref[...] * (1.0 + sc_ref[...]) + sh_ref[...]).astype(BF16)
        h_ref[...] = h
        acc_ref[...] = jnp.zeros_like(acc_ref)
        logits = jnp.dot(h, wr_ref[...], preferred_element_type=F32)
        s1 = jnp.where(lane < N_EXPERTS, logits, -jnp.inf)
        m1 = jnp.max(s1, axis=-1, keepdims=True)
        i1 = jnp.min(jnp.where(s1 == m1, lane, LANES), axis=-1, keepdims=True)
        s2 = jnp.where(lane == i1, -jnp.inf, s1)
        m2 = jnp.max(s2, axis=-1, keepdims=True)
        i2 = jnp.min(jnp.where(s2 == m2, lane, LANES), axis=-1, keepdims=True)
        e2 = jnp.exp(m2 - m1)
        den = 1.0 + e2
        comb_ref[...] = jnp.where(lane == i1, 1.0 / den, 0.0) + jnp.where(lane == i2, e2 / den, 0.0)

    h = h_ref[...]
    g = jnp.dot(h, wg_ref[...], preferred_element_type=F32)
    u = jnp.dot(h, wu_ref[...], preferred_element_type=F32)
    act = ((g * jax.nn.sigmoid(g)) * u).astype(BF16)
    c_e = jnp.sum(jnp.where(lane == e, comb_ref[...], 0.0), axis=-1, keepdims=True)
    acc_ref[...] += c_e * jnp.dot(act, wd_ref[...], preferred_element_type=F32)

    @pl.when(e == pl.num_programs(1) - 1)
    def _():
        z = DEEPNORM_ALPHA * x_ref[...] + gate_ref[...] * acc_ref[...]
        o_ref[...] = _layer_norm_rows(z, lng_ref[...], lnb_ref[...])


def _moe(x, mod, w_router, w_gu, w_down, lng, lnb, *, n_ctx, t_s):
    m = x.shape[0]
    tm = 512
    const = lambda arr: pl.BlockSpec(arr.shape, lambda i, e: (0,) * arr.ndim)
    row = pl.BlockSpec((tm, D_MODEL), lambda i, e: (i, 0))
    return pl.pallas_call(
        functools.partial(_moe_kernel, tm=tm),
        out_shape=jax.ShapeDtypeStruct((m, D_MODEL), F32),
        grid=(m // tm, N_EXPERTS),
        in_specs=[row, _mod_spec(4, tm, n_ctx, t_s), _mod_spec(3, tm, n_ctx, t_s), _mod_spec(5, tm, n_ctx, t_s),
                  const(w_router),
                  pl.BlockSpec((None, D_MODEL, D_FF_EXPERT), lambda i, e: (e, 0, 0)),
                  pl.BlockSpec((None, D_MODEL, D_FF_EXPERT), lambda i, e: (e, 0, 1)),
                  pl.BlockSpec((None, D_FF_EXPERT, D_MODEL), lambda i, e: (e, 0, 0)),
                  const(lng), const(lnb)],
        out_specs=row,
        scratch_shapes=[pltpu.VMEM((tm, D_MODEL), BF16), pltpu.VMEM((tm, LANES), F32),
                        pltpu.VMEM((tm, D_MODEL), F32)],
        compiler_params=_cparams("parallel", "arbitrary"),
        name="moe",
    )(x, mod, mod, mod, w_router, w_gu, w_gu, w_down, lng, lnb)


def _rope_tables(n_tokens):
    n_rows = n_tokens // GRID_W
    row = jnp.repeat(jnp.arange(n_rows, dtype=F32), GRID_W)
    col = jnp.tile(jnp.arange(GRID_W, dtype=F32), n_rows)
    n_axis = HEAD_DIM // 4
    inv = ROPE_THETA ** (-jnp.arange(n_axis, dtype=F32) / n_axis)
    ang = jnp.concatenate([row[:, None] * inv, col[:, None] * inv], axis=-1)
    cos, sin = jnp.cos(ang), jnp.sin(ang)
    reps = LANES // HEAD_DIM
    cos_t = jnp.tile(jnp.concatenate([cos, cos], axis=-1), (1, reps))
    sin_t = jnp.tile(jnp.concatenate([-sin, sin], axis=-1), (1, reps))
    return cos_t, sin_t


def _gqa_head_perm():
    cols = []
    per_pair = 2 * (N_Q_HEADS // N_KV_HEADS)
    for p in range(N_KV_HEADS // 2):
        for j in range(N_Q_HEADS // N_KV_HEADS):
            for hd in (per_pair * p + j, per_pair * p + per_pair // 2 + j):
                cols.append(np.arange(hd * HEAD_DIM, (hd + 1) * HEAD_DIM))
    return np.concatenate(cols)


def _block_diag_windows(w):
    eye = jnp.eye(N_RNN_BLOCKS, dtype=bool)[:, None, :, None]
    dense = jnp.where(eye, w[:, :, None, :], 0.0).reshape(D_RNN, D_RNN)
    for ct, k0 in enumerate(GATE_K0):
        lo, hi = ct * GATE_N, (ct + 1) * GATE_N
        assert k0 <= RNN_BLOCK * (lo // RNN_BLOCK) and RNN_BLOCK * -(-hi // RNN_BLOCK) <= k0 + GATE_K <= D_RNN
    return jnp.stack([dense[k0:k0 + GATE_K, ct * GATE_N:(ct + 1) * GATE_N] for ct, k0 in enumerate(GATE_K0)])


def kernel(x_prompt, x_sample, cache_attn_k, cache_attn_v, state_rglru, cache_diff_k, cache_diff_v, c, c_ctx, w_ada, b_ada, ln_g, ln_b, attn_w_qkv, attn_g_q, attn_g_k, attn_w_o, rnn_w_in, rnn_conv_w, rnn_conv_b, rnn_w_a, rnn_b_a, rnn_w_x, rnn_b_x, rnn_lambda, rnn_w_out, diff_w_qkv, diff_lambda, diff_g_sub, diff_w_o, ffn_w_gu, ffn_w_down, moe_w_router, moe_w_gu, moe_w_down):
    b_c, t_c, d = x_prompt.shape
    b_s, t_s, _ = x_sample.shape
    n_ctx = b_c * t_c
    n_s = b_s * t_s
    past = cache_attn_k.shape[2]
    assert d == D_MODEL and n_ctx % t_s == 0 and t_c == SCAN_CHUNK and t_s % SCAN_CHUNK == 0
    assert 1 + b_s <= COND_ROWS

    x = jnp.concatenate([x_prompt.reshape(n_ctx, d), x_sample.reshape(n_s, d)], axis=0)
    cond = jnp.zeros((COND_ROWS, d), F32).at[0].set(c_ctx).at[1:1 + b_s].set(c)
    mods = _ada_table(cond, w_ada, b_ada)[:, :1 + b_s].reshape(DEPTH, 1 + b_s, 1, 6 * d)

    cos_t, sin_t = _rope_tables(t_s)
    perm = _gqa_head_perm()
    nq = N_Q_HEADS * HEAD_DIM
    tile2 = lambda g: jnp.tile(g, LANES // g.shape[-1]).reshape(1, LANES)
    kw = dict(n_ctx=n_ctx, t_s=t_s)

    attn_k, attn_v, rnn_s, diff_k, diff_v = [], [], [], [], []
    for li in range(DEPTH):
        mod = mods[li]
        lng = ln_g[li].reshape(2, 1, d)
        lnb = ln_b[li].reshape(2, 1, d)
        j = li // N_MIXERS
        kind = li % N_MIXERS
        if kind == 0:
            w = attn_w_qkv[j]
            w = jnp.concatenate([w[:, :nq][:, perm], w[:, nq:]], axis=1).astype(BF16)
            q, kb, vb, kf, vf = _gqa_qkv(x, mod, w, tile2(attn_g_q[j]), tile2(attn_g_k[j]), cos_t, sin_t, **kw)
            ck = cache_attn_k[:, j].reshape(b_s, past, N_KV_HEADS * HEAD_DIM).astype(BF16)
            cv = cache_attn_v[:, j].reshape(b_s, past, N_KV_HEADS * HEAD_DIM).astype(BF16)
            o = _gqa_attention(q, kb, vb, None, None, None, row0=0, n_b=b_c, t=t_c, tq=t_c)
            o = _gqa_attention(q, kb, vb, o, ck, cv, row0=n_ctx, n_b=b_s, t=t_s, tq=256)
            x = _oproj(o, attn_w_o[j][perm, :].astype(BF16), x, mod, lng[0], lnb[0], **kw)
            attn_k.append(kf[:n_ctx])
            attn_v.append(vf[:n_ctx])
        elif kind == 1:
            bx = _rnn_in(x, mod, rnn_w_in[j].astype(BF16), **kw)
            wg = jnp.stack([_block_diag_windows(rnn_w_a[j, 0]), _block_diag_windows(rnn_w_x[j, 0]),
                            _block_diag_windows(rnn_w_a[j, 1]), _block_diag_windows(rnn_w_x[j, 1])]).astype(BF16)
            bg = jnp.concatenate([rnn_b_a[j, 0], rnn_b_x[j, 0], rnn_b_a[j, 1], rnn_b_x[j, 1]]).reshape(1, -1)
            a_f, u_f, a_b, u_b = _rnn_gates(bx, rnn_conv_w[j], rnn_conv_b[j].reshape(1, -1), wg, bg,
                                            rnn_lambda[j], n_ctx=n_ctx, t_c=t_c, t_s=t_s)
            h0 = jnp.concatenate([jnp.zeros((b_c, 2, D_RNN), F32), state_rglru[:, j]], axis=0)
            hs_f, hs_b, fin = _rnn_scan(a_f, u_f, a_b, u_b, h0, seq_lens=[t_c] * b_c + [t_s] * b_s)
            x = _rnn_oproj(hs_f, hs_b, bx, rnn_w_out[j].astype(BF16), x, mod, lng[0], lnb[0], **kw)
            rnn_s.append(fin[:b_c])
        else:
            lam_init = 0.8 - 0.6 * math.exp(-0.3 * li)
            q, kb, vb, kf, vf = _diff_qkv(x, mod, diff_w_qkv[j].astype(BF16), cos_t, sin_t, **kw)
            ck = cache_diff_k[:, j].reshape(b_s, past, D_MODEL).astype(BF16)
            cv = cache_diff_v[:, j].reshape(b_s, past, D_MODEL).astype(BF16)
            g_sub = diff_g_sub[j].reshape(1, LANES)
            o = _diff_attention(diff_lambda[j], g_sub, q, kb, vb, None, None, None,
                                row0=0, n_b=b_c, t=t_c, tq=t_c, lam_init=lam_init)
            o = _diff_attention(diff_lambda[j], g_sub, q, kb, vb, o, ck, cv,
                                row0=n_ctx, n_b=b_s, t=t_s, tq=256, lam_init=lam_init)
            x = _oproj(o, diff_w_o[j].astype(BF16), x, mod, lng[0], lnb[0], **kw)
            diff_k.append(kf[:n_ctx])
            diff_v.append(vf[:n_ctx])
        if li % 2 == 0:
            x = _ffn(x, mod, ffn_w_gu[li // 2].astype(BF16), ffn_w_down[li // 2].astype(BF16), lng[1], lnb[1], **kw)
        else:
            w_r = jnp.zeros((d, LANES), F32).at[:, :N_EXPERTS].set(moe_w_router[li // 2]).astype(BF16)
            x = _moe(x, mod, w_r, moe_w_gu[li // 2].astype(BF16), moe_w_down[li // 2].astype(BF16),
                     lng[1], lnb[1], **kw)

    y_prompt = x[:n_ctx].reshape(b_c, t_c, d)
    y_sample = x[n_ctx:].reshape(b_s, t_s, d)
    new_attn_k = jnp.stack([k.reshape(b_c, t_c, N_KV_HEADS, HEAD_DIM) for k in attn_k], axis=1)
    new_attn_v = jnp.stack([v.reshape(b_c, t_c, N_KV_HEADS, HEAD_DIM) for v in attn_v], axis=1)
    new_state = jnp.stack(rnn_s, axis=1)
    new_diff_k = jnp.stack([k.reshape(b_c, t_c, N_DIFF_HEADS, 2, HEAD_DIM) for k in diff_k], axis=1)
    new_diff_v = jnp.stack([v.reshape(b_c, t_c, N_DIFF_HEADS, 2 * HEAD_DIM) for v in diff_v], axis=1)
    return (y_prompt, y_sample, new_attn_k, new_attn_v, new_state, new_diff_k, new_diff_v)
```

```python
import functools
import math

import jax
import jax.numpy as jnp
import numpy as np
from jax import lax
from jax.experimental import pallas as pl
from jax.experimental.pallas import tpu as pltpu

F32 = jnp.float32
BF16 = jnp.bfloat16

D_MODEL = 1024
DEPTH = 4
GRID_W = 64
HEAD_DIM = 64
N_Q_HEADS = 16
N_KV_HEADS = 4
ROPE_THETA = 10000.0
N_DIFF_HEADS = 8
D_RNN = 1280
N_RNN_BLOCKS = 16
RNN_BLOCK = 80
CONV_W = 4
CONV_LEFT = 2
RGLRU_C = 8.0
D_FF = 2816
N_EXPERTS = 8
D_FF_EXPERT = 1408
N_MIXERS = 3
DEEPNORM_ALPHA = (2.0 * DEPTH) ** 0.25
LN_EPS = 1e-6
RMS_EPS = 1e-6

LANES = 128
SUBLANES = 8
VMEM_LIMIT = 56 * 1024 * 1024
COND_ROWS = 8
SCAN_CHUNK = 256
MOE_TILE = 512
ROUTE_COLS = 8
GATE_N = 256
GATE_K = 512
GATE_K0 = (0, 128, 384, 640, 768)
Q_SCALE = HEAD_DIM ** -0.5 * math.log2(math.e)


def _cparams(*sem):
    return pltpu.CompilerParams(dimension_semantics=sem, vmem_limit_bytes=VMEM_LIMIT)


def _layer_norm_rows(z, g, b):
    mu = jnp.mean(z, axis=-1, keepdims=True)
    zc = z - mu
    var = jnp.mean(zc * zc, axis=-1, keepdims=True)
    return zc * lax.rsqrt(var + LN_EPS) * g + b


def _seg_of_block(i, tm, n_ctx, t_s):
    r0 = i * tm
    return jnp.where(r0 < n_ctx, 0, 1 + (r0 - n_ctx) // t_s)


def _mod_spec(col, tm, n_ctx, t_s):
    return pl.BlockSpec((None, 1, D_MODEL), lambda i, *_: (_seg_of_block(i, tm, n_ctx, t_s), 0, col))


def _ada_kernel(c_ref, w_ref, b_ref, o_ref):
    c = c_ref[...]
    a = (c * jax.nn.sigmoid(c)).astype(BF16)
    o_ref[...] = jnp.dot(a, w_ref[...].astype(BF16), preferred_element_type=F32) + b_ref[...]


def _ada_table(cond, w_ada, b_ada):
    n_l, d, n = w_ada.shape
    tn = 1536
    return pl.pallas_call(
        _ada_kernel,
        out_shape=jax.ShapeDtypeStruct((n_l, COND_ROWS, n), F32),
        grid=(n_l, n // tn),
        in_specs=[pl.BlockSpec((COND_ROWS, d), lambda l, j: (0, 0)),
                  pl.BlockSpec((None, d, tn), lambda l, j: (l, 0, j)),
                  pl.BlockSpec((None, 1, tn), lambda l, j: (l, 0, j))],
        out_specs=pl.BlockSpec((None, COND_ROWS, tn), lambda l, j: (l, 0, j)),
        compiler_params=_cparams("parallel", "parallel"),
        name="ada_table",
    )(cond, w_ada, b_ada.reshape(n_l, 1, n))


def _head_masks(tm):
    lane = lax.broadcasted_iota(jnp.int32, (tm, LANES), 1)
    return lane < HEAD_DIM, (lane & (HEAD_DIM - 1)) < HEAD_DIM // 2


def _rope_tile(t, first_half, cos, sin):
    partner = jnp.where(first_half, pltpu.roll(t, LANES - HEAD_DIM // 2, 1),
                        pltpu.roll(t, HEAD_DIM // 2, 1))
    return t * cos + partner * sin


def _gqa_qkv_kernel(x_ref, sc_ref, sh_ref, w_ref, gq_ref, gk_ref, cos_ref, sin_ref,
                    q_ref, kb_ref, vb_ref, kf_ref, vf_ref, *, tm, n_ctx):
    i = pl.program_id(0)
    h = (x_ref[...] * (1.0 + sc_ref[...]) + sh_ref[...]).astype(BF16)
    y = jnp.dot(h, w_ref[...], preferred_element_type=F32)
    use_rope = i * tm >= n_ctx
    cos = jnp.where(use_rope, cos_ref[...], 1.0)
    sin = jnp.where(use_rope, sin_ref[...], 0.0)
    low_head, first_half = _head_masks(tm)

    def norm_rope(t, g):
        t2 = t * t
        s_lo = jnp.sum(jnp.where(low_head, t2, 0.0), axis=-1, keepdims=True)
        s_hi = jnp.sum(jnp.where(low_head, 0.0, t2), axis=-1, keepdims=True)
        inv = jnp.where(low_head, lax.rsqrt(s_lo * (1.0 / HEAD_DIM) + RMS_EPS),
                        lax.rsqrt(s_hi * (1.0 / HEAD_DIM) + RMS_EPS))
        return _rope_tile(t * inv * g, first_half, cos, sin)

    nq = N_Q_HEADS * HEAD_DIM
    nkv = N_KV_HEADS * HEAD_DIM
    gq = gq_ref[...]
    gk = gk_ref[...]
    for t in range(nq // LANES):
        sl = slice(t * LANES, (t + 1) * LANES)
        q_ref[:, sl] = (norm_rope(y[:, sl], gq) * Q_SCALE).astype(BF16)
    for t in range(nkv // LANES):
        sl = slice(t * LANES, (t + 1) * LANES)
        k = norm_rope(y[:, nq + t * LANES: nq + (t + 1) * LANES], gk)
        kf_ref[:, sl] = k
        kb_ref[:, sl] = k.astype(BF16)
    v = y[:, nq + nkv:]
    vf_ref[...] = v
    vb_ref[...] = v.astype(BF16)


def _gqa_qkv(x, mod, w, gq, gk, cos, sin, *, n_ctx, t_s):
    m = x.shape[0]
    tm = 512
    nq = N_Q_HEADS * HEAD_DIM
    nkv = N_KV_HEADS * HEAD_DIM
    n_rope_blocks = t_s // tm
    rope_spec = pl.BlockSpec(
        (tm, LANES), lambda i: (jnp.where(i * tm >= n_ctx, ((i * tm - n_ctx) % t_s) // tm, 0) % n_rope_blocks, 0))
    row = lambda n: pl.BlockSpec((tm, n), lambda i: (i, 0))
    const = lambda a: pl.BlockSpec(a.shape, lambda i: (0,) * a.ndim)
    return pl.pallas_call(
        functools.partial(_gqa_qkv_kernel, tm=tm, n_ctx=n_ctx),
        out_shape=(jax.ShapeDtypeStruct((m, nq), BF16), jax.ShapeDtypeStruct((m, nkv), BF16),
                   jax.ShapeDtypeStruct((m, nkv), BF16), jax.ShapeDtypeStruct((m, nkv), F32),
                   jax.ShapeDtypeStruct((m, nkv), F32)),
        grid=(m // tm,),
        in_specs=[row(D_MODEL), _mod_spec(1, tm, n_ctx, t_s), _mod_spec(0, tm, n_ctx, t_s),
                  const(w), const(gq), const(gk), rope_spec, rope_spec],
        out_specs=(row(nq), row(nkv), row(nkv), row(nkv), row(nkv)),
        compiler_params=_cparams("parallel"),
        name="gqa_qkv",
    )(x, mod, mod, w, gq, gk, cos, sin)


def _diff_qkv_kernel(x_ref, sc_ref, sh_ref, w_ref, cos_ref, sin_ref,
                     q_ref, kb_ref, vb_ref, kf_ref, vf_ref, *, tm, n_ctx):
    i = pl.program_id(0)
    h = (x_ref[...] * (1.0 + sc_ref[...]) + sh_ref[...]).astype(BF16)
    y = jnp.dot(h, w_ref[...], preferred_element_type=F32)
    use_rope = i * tm >= n_ctx
    cos = jnp.where(use_rope, cos_ref[...], 1.0)
    sin = jnp.where(use_rope, sin_ref[...], 0.0)
    _, first_half = _head_masks(tm)
    for t in range(D_MODEL // LANES):
        sl = slice(t * LANES, (t + 1) * LANES)
        q = _rope_tile(y[:, sl], first_half, cos, sin)
        q_ref[:, sl] = (q * Q_SCALE).astype(BF16)
        k = _rope_tile(y[:, D_MODEL + t * LANES: D_MODEL + (t + 1) * LANES], first_half, cos, sin)
        kf_ref[:, sl] = k
        kb_ref[:, sl] = k.astype(BF16)
    v = y[:, 2 * D_MODEL:]
    vf_ref[...] = v
    vb_ref[...] = v.astype(BF16)


def _diff_qkv(x, mod, w, cos, sin, *, n_ctx, t_s):
    m = x.shape[0]
    tm = 256
    n_rope_blocks = t_s // tm
    rope_spec = pl.BlockSpec(
        (tm, LANES), lambda i: (jnp.where(i * tm >= n_ctx, ((i * tm - n_ctx) % t_s) // tm, 0) % n_rope_blocks, 0))
    row = pl.BlockSpec((tm, D_MODEL), lambda i: (i, 0))
    return pl.pallas_call(
        functools.partial(_diff_qkv_kernel, tm=tm, n_ctx=n_ctx),
        out_shape=(jax.ShapeDtypeStruct((m, D_MODEL), BF16), jax.ShapeDtypeStruct((m, D_MODEL), BF16),
                   jax.ShapeDtypeStruct((m, D_MODEL), BF16), jax.ShapeDtypeStruct((m, D_MODEL), F32),
                   jax.ShapeDtypeStruct((m, D_MODEL), F32)),
        grid=(m // tm,),
        in_specs=[row, _mod_spec(1, tm, n_ctx, t_s), _mod_spec(0, tm, n_ctx, t_s),
                  pl.BlockSpec(w.shape, lambda i: (0, 0)), rope_spec, rope_spec],
        out_specs=(row, row, row, row, row),
        compiler_params=_cparams("parallel"),
        name="diff_qkv",
    )(x, mod, mod, w, cos, sin)


_NT = (((1,), (1,)), ((), ()))


def _with_ones_column(v, low_head, lane, keep_low):
    ones_lane = HEAD_DIM if keep_low else 0
    other = jnp.where(lane == ones_lane, 1.0, 0.0).astype(v.dtype)
    return jnp.where(low_head == keep_low, v, other)


def _gqa_attn_kernel(*refs, has_cache, tq):
    if has_cache:
        q_ref, k_ref, v_ref, ck_ref, cv_ref, o_ref = refs
    else:
        q_ref, k_ref, v_ref, o_ref = refs
    low_q = lax.broadcasted_iota(jnp.int32, (tq, LANES), 1) < HEAD_DIM
    k = k_ref[...]
    lane_v = lax.broadcasted_iota(jnp.int32, v_ref.shape, 1)
    v_aug = [_with_ones_column(v_ref[...], lane_v < HEAD_DIM, lane_v, keep) for keep in (True, False)]
    if has_cache:
        ck = ck_ref[...]
        lane_c = lax.broadcasted_iota(jnp.int32, cv_ref.shape, 1)
        cv_aug = [_with_ones_column(cv_ref[...], lane_c < HEAD_DIM, lane_c, keep) for keep in (True, False)]
    n_tiles = q_ref.shape[1] // LANES
    heads = [(j, keep) for j in range(n_tiles) for keep in (True, False)]

    def scores(j, keep_low):
        qt = q_ref[:, j * LANES:(j + 1) * LANES]
        qh = jnp.where(low_q == keep_low, qt, jnp.zeros_like(qt))
        s = lax.dot_general(qh, k, _NT, preferred_element_type=F32)
        s_c = lax.dot_general(qh, ck, _NT, preferred_element_type=F32) if has_cache else None
        return s, s_c

    nxt = scores(*heads[0])
    halves = []
    for idx, (j, keep_low) in enumerate(heads):
        s, s_c = nxt
        if idx + 1 < len(heads):
            nxt = scores(*heads[idx + 1])
        mx = jnp.max(s, axis=-1, keepdims=True)
        if has_cache:
            mx = jnp.maximum(mx, jnp.max(s_c, axis=-1, keepdims=True))
        side = 0 if keep_low else 1
        acc = jnp.dot(jnp.exp2(s - mx).astype(BF16), v_aug[side], preferred_element_type=F32)
        if has_cache:
            acc = acc + jnp.dot(jnp.exp2(s_c - mx).astype(BF16), cv_aug[side], preferred_element_type=F32)
        den = acc[:, HEAD_DIM:HEAD_DIM + 1] if keep_low else acc[:, 0:1]
        halves.append(acc / den)
        if not keep_low:
            o_ref[:, j * LANES:(j + 1) * LANES] = jnp.where(low_q, halves[0], halves[1]).astype(BF16)
            halves = []


def _gqa_attention(q, k, v, cache_k, cache_v, *, row0, n_b, t, tq):
    n_pairs = N_KV_HEADS // 2
    qw = D_MODEL // n_pairs
    nq_blocks = t // tq
    has_cache = cache_k is not None
    q_spec = pl.BlockSpec((tq, qw), lambda b, p, iq: (row0 // tq + b * nq_blocks + iq, p))
    kv_spec = pl.BlockSpec((t, LANES), lambda b, p, iq: (row0 // t + b, p))
    in_specs = [q_spec, kv_spec, kv_spec]
    args = [q, k, v]
    if has_cache:
        c_spec = pl.BlockSpec((None, cache_k.shape[1], LANES), lambda b, p, iq: (b, 0, p))
        in_specs += [c_spec, c_spec]
        args += [cache_k, cache_v]
    return pl.pallas_call(
        functools.partial(_gqa_attn_kernel, has_cache=has_cache, tq=tq),
        out_shape=jax.ShapeDtypeStruct((n_b * t, D_MODEL), BF16),
        grid=(n_b, n_pairs, nq_blocks),
        in_specs=in_specs,
        out_specs=pl.BlockSpec((tq, qw), lambda b, p, iq: (b * nq_blocks + iq, p)),
        compiler_params=_cparams("parallel", "parallel", "arbitrary"),
        name="gqa_attn_cache" if has_cache else "gqa_attn",
    )(*args)


def _diff_attn_kernel(*refs, has_cache, tq, lam_init):
    if has_cache:
        lam_ref, g_ref, q_ref, k_ref, v_ref, ck_ref, cv_ref, o_ref = refs
    else:
        lam_ref, g_ref, q_ref, k_ref, v_ref, o_ref = refs
    lv = lam_ref[...]
    lam = (jnp.exp(jnp.sum(lv[0:1] * lv[1:2], axis=-1, keepdims=True))
           - jnp.exp(jnp.sum(lv[2:3] * lv[3:4], axis=-1, keepdims=True)) + lam_init)
    low_head = lax.broadcasted_iota(jnp.int32, (tq, LANES), 1) < HEAD_DIM
    qt = q_ref[...]
    k = k_ref[...]
    probs = []
    for keep_low in (True, False):
        qh = jnp.where(low_head == keep_low, qt, jnp.zeros_like(qt))
        s = lax.dot_general(qh, k, _NT, preferred_element_type=F32)
        mx = jnp.max(s, axis=-1, keepdims=True)
        if has_cache:
            s_c = lax.dot_general(qh, ck_ref[...], _NT, preferred_element_type=F32)
            mx = jnp.maximum(mx, jnp.max(s_c, axis=-1, keepdims=True))
        e = jnp.exp2(s - mx)
        den = jnp.sum(e, axis=-1, keepdims=True)
        if has_cache:
            e_c = jnp.exp2(s_c - mx)
            den = den + jnp.sum(e_c, axis=-1, keepdims=True)
            inv = 1.0 / den
            probs.append((e * inv, e_c * inv))
        else:
            probs.append((e * (1.0 / den),))
    pd = (probs[0][0] - lam * probs[1][0]).astype(BF16)
    o = jnp.dot(pd, v_ref[...], preferred_element_type=F32)
    if has_cache:
        pd_c = (probs[0][1] - lam * probs[1][1]).astype(BF16)
        o = o + jnp.dot(pd_c, cv_ref[...], preferred_element_type=F32)
    inv = lax.rsqrt(jnp.mean(o * o, axis=-1, keepdims=True) + RMS_EPS)
    o_ref[...] = ((o * inv * g_ref[...]) * (1.0 - lam_init)).astype(BF16)


def _diff_attention(lam_vec, g_sub, q, k, v, cache_k, cache_v, *, row0, n_b, t, tq, lam_init):
    nq_blocks = t // tq
    has_cache = cache_k is not None
    q_spec = pl.BlockSpec((tq, LANES), lambda b, h, iq: (row0 // tq + b * nq_blocks + iq, h))
    kv_spec = pl.BlockSpec((t, LANES), lambda b, h, iq: (row0 // t + b, h))
    const = lambda a: pl.BlockSpec(a.shape, lambda b, h, iq: (0,) * a.ndim)
    in_specs = [const(lam_vec), const(g_sub), q_spec, kv_spec, kv_spec]
    args = [lam_vec, g_sub, q, k, v]
    if has_cache:
        c_spec = pl.BlockSpec((None, cache_k.shape[1], LANES), lambda b, h, iq: (b, 0, h))
        in_specs += [c_spec, c_spec]
        args += [cache_k, cache_v]
    return pl.pallas_call(
        functools.partial(_diff_attn_kernel, has_cache=has_cache, tq=tq, lam_init=lam_init),
        out_shape=jax.ShapeDtypeStruct((n_b * t, D_MODEL), BF16),
        grid=(n_b, N_DIFF_HEADS, nq_blocks),
        in_specs=in_specs,
        out_specs=pl.BlockSpec((tq, LANES), lambda b, h, iq: (b * nq_blocks + iq, h)),
        compiler_params=_cparams("parallel", "parallel", "arbitrary"),
        name="diff_attn_cache" if has_cache else "diff_attn",
    )(*args)


def _oproj_kernel(ac_ref, as_ref, w_ref, x_ref, gate_ref, lng_ref, lnb_ref, o_ref, *, n_ctx_blocks):
    is_ctx = pl.program_id(0) < n_ctx_blocks
    a = jnp.where(is_ctx, ac_ref[...], as_ref[...])
    out = jnp.dot(a, w_ref[...], preferred_element_type=F32)
    z = DEEPNORM_ALPHA * x_ref[...] + gate_ref[...] * out
    o_ref[...] = _layer_norm_rows(z, lng_ref[...], lnb_ref[...])


def _oproj(a_ctx, a_smp, w, x, mod, lng, lnb, *, n_ctx, t_s):
    m = x.shape[0]
    kdim = a_ctx.shape[1]
    tm = 512
    ncb = n_ctx // tm
    const = lambda arr: pl.BlockSpec(arr.shape, lambda i: (0,) * arr.ndim)
    row = pl.BlockSpec((tm, D_MODEL), lambda i: (i, 0))
    return pl.pallas_call(
        functools.partial(_oproj_kernel, n_ctx_blocks=ncb),
        out_shape=jax.ShapeDtypeStruct((m, D_MODEL), F32),
        grid=(m // tm,),
        in_specs=[pl.BlockSpec((tm, kdim), lambda i: (jnp.minimum(i, ncb - 1), 0)),
                  pl.BlockSpec((tm, kdim), lambda i: (jnp.maximum(i - ncb, 0), 0)),
                  const(w), row, _mod_spec(2, tm, n_ctx, t_s), const(lng), const(lnb)],
        out_specs=row,
        compiler_params=_cparams("parallel"),
        name="oproj_ln",
    )(a_ctx, a_smp, w, x, mod, lng, lnb)


def _gelu_tanh(x):
    return 0.5 * x * (1.0 + jnp.tanh(math.sqrt(2.0 / math.pi) * (x + 0.044715 * (x * x * x))))


def _rnn_oproj_kernel(hf_ref, hb_ref, br_ref, w_ref, x_ref, gate_ref, lng_ref, lnb_ref, o_ref):
    y = (hf_ref[...] + hb_ref[...]) * _gelu_tanh(br_ref[...])
    out = jnp.dot(y.astype(BF16), w_ref[...], preferred_element_type=F32)
    z = DEEPNORM_ALPHA * x_ref[...] + gate_ref[...] * out
    o_ref[...] = _layer_norm_rows(z, lng_ref[...], lnb_ref[...])


def _rnn_oproj(hs_f, hs_b, bx, w, x, mod, lng, lnb, *, n_ctx, t_s):
    m = x.shape[0]
    tm = 512
    const = lambda arr: pl.BlockSpec(arr.shape, lambda i: (0,) * arr.ndim)
    row = pl.BlockSpec((tm, D_MODEL), lambda i: (i, 0))
    rnn_row = pl.BlockSpec((tm, D_RNN), lambda i: (i, 0))
    return pl.pallas_call(
        _rnn_oproj_kernel,
        out_shape=jax.ShapeDtypeStruct((m, D_MODEL), F32),
        grid=(m // tm,),
        in_specs=[rnn_row, rnn_row, rnn_row, const(w), row,
                  _mod_spec(2, tm, n_ctx, t_s), const(lng), const(lnb)],
        out_specs=row,
        compiler_params=_cparams("parallel"),
        name="rnn_oproj_ln",
    )(hs_f, hs_b, bx, w, x, mod, lng, lnb)


def _mod_mm_kernel(x_ref, sc_ref, sh_ref, w_ref, o_ref):
    h = (x_ref[...] * (1.0 + sc_ref[...]) + sh_ref[...]).astype(BF16)
    o_ref[...] = jnp.dot(h, w_ref[...], preferred_element_type=F32)


def _rnn_in(x, mod, w, *, n_ctx, t_s):
    m = x.shape[0]
    n = w.shape[1]
    tm = 512
    return pl.pallas_call(
        _mod_mm_kernel,
        out_shape=jax.ShapeDtypeStruct((m, n), F32),
        grid=(m // tm,),
        in_specs=[pl.BlockSpec((tm, D_MODEL), lambda i: (i, 0)),
                  _mod_spec(1, tm, n_ctx, t_s), _mod_spec(0, tm, n_ctx, t_s),
                  pl.BlockSpec(w.shape, lambda i: (0, 0))],
        out_specs=pl.BlockSpec((tm, n), lambda i: (i, 0)),
        compiler_params=_cparams("parallel"),
        name="rnn_in",
    )(x, mod, mod, w)


def _rnn_gate_kernel(xb_ref, prev_ref, next_ref, cw_ref, cb_ref, wg_ref, bg_ref, lam_ref,
                     af_ref, uf_ref, ab_ref, ub_ref, ext_ref, *, tm, n_ctx, t_c, t_s):
    i = pl.program_id(0)
    r0 = i * tm
    pos = jnp.where(r0 < n_ctx, r0 % t_c, (r0 - n_ctx) % t_s)
    t_seq = jnp.where(r0 < n_ctx, t_c, t_s)
    at_start = pos == 0
    at_end = pos + tm == t_seq
    ext_ref[0:SUBLANES, :] = jnp.where(at_start, 0.0, prev_ref[...])
    ext_ref[SUBLANES:SUBLANES + tm, :] = xb_ref[...]
    ext_ref[SUBLANES + tm:, :] = jnp.where(at_end, 0.0, next_ref[...])
    xc = 0.0
    for j in range(CONV_W):
        off = SUBLANES - CONV_LEFT + j
        xc = xc + ext_ref[off:off + tm, :] * cw_ref[j:j + 1, :]
    xc = xc + cb_ref[...]
    xcb = xc.astype(BF16)
    neg_lam = -lam_ref[...]
    sp = jnp.maximum(neg_lam, 0.0) + jnp.log1p(jnp.exp(-jnp.abs(neg_lam)))

    def gate(g, ct, cols):
        k0 = GATE_K0[ct]
        pre = jnp.dot(xcb[:, k0:k0 + GATE_K], wg_ref[g, ct], preferred_element_type=F32)
        return jax.nn.sigmoid(pre + bg_ref[:, g * D_RNN + cols.start:g * D_RNN + cols.stop])

    for z, (a_ref, u_ref) in enumerate(((af_ref, uf_ref), (ab_ref, ub_ref))):
        for ct in range(D_RNN // GATE_N):
            cols = slice(ct * GATE_N, (ct + 1) * GATE_N)
            r = gate(2 * z, ct, cols)
            g_in = gate(2 * z + 1, ct, cols)
            log_a = -RGLRU_C * r * sp[z:z + 1, cols]
            a = jnp.exp(log_a)
            a_ref[:, cols] = a
            u_ref[:, cols] = jnp.sqrt(-jnp.tanh(log_a) * (a * a + 1.0)) * (g_in * xc[:, cols])


def _rnn_gates(bx, conv_w, conv_b, wg, bg, lam, *, n_ctx, t_c, t_s):
    m = bx.shape[0]
    tm = 256
    n8 = tm // SUBLANES
    last8 = m // SUBLANES - 1
    const = lambda arr: pl.BlockSpec(arr.shape, lambda i: (0,) * arr.ndim)
    row = pl.BlockSpec((tm, D_RNN), lambda i: (i, 0))
    out = jax.ShapeDtypeStruct((m, D_RNN), F32)
    return pl.pallas_call(
        functools.partial(_rnn_gate_kernel, tm=tm, n_ctx=n_ctx, t_c=t_c, t_s=t_s),
        out_shape=(out, out, out, out),
        grid=(m // tm,),
        in_specs=[pl.BlockSpec((tm, D_RNN), lambda i: (i, 1)),
                  pl.BlockSpec((SUBLANES, D_RNN), lambda i: (jnp.maximum(i * n8 - 1, 0), 1)),
                  pl.BlockSpec((SUBLANES, D_RNN), lambda i: (jnp.minimum((i + 1) * n8, last8), 1)),
                  const(conv_w), const(conv_b), const(wg), const(bg), const(lam)],
        out_specs=(row, row, row, row),
        scratch_shapes=[pltpu.VMEM((tm + 2 * SUBLANES, D_RNN), F32)],
        compiler_params=_cparams("parallel"),
        name="rnn_gates",
    )(bx, bx, bx, conv_w, conv_b, wg, bg, lam)


def _rnn_scan_kernel(fblk_ref, bblk_ref, seq_ref, first_ref,
                     af_ref, uf_ref, ab_ref, ub_ref, h0_ref, hf_ref, hb_ref, fin_ref, carry_ref, *, tc):
    s = pl.program_id(0)

    @pl.when(first_ref[s] == 1)
    def _():
        carry_ref[...] = h0_ref[...]

    def step(t, carry):
        h_f, h_b = carry
        tb = tc - 1 - t
        h_f = af_ref[pl.ds(t, 1), :] * h_f + uf_ref[pl.ds(t, 1), :]
        h_b = ab_ref[pl.ds(tb, 1), :] * h_b + ub_ref[pl.ds(tb, 1), :]
        hf_ref[pl.ds(t, 1), :] = h_f
        hb_ref[pl.ds(tb, 1), :] = h_b
        return h_f, h_b

    h_f, h_b = lax.fori_loop(0, tc, step, (carry_ref[0:1, :], carry_ref[1:2, :]), unroll=8)
    carry_ref[0:1, :] = h_f
    carry_ref[1:2, :] = h_b
    fin_ref[...] = carry_ref[...]


def _rnn_scan(a_f, u_f, a_b, u_b, h0, *, seq_lens):
    m = a_f.shape[0]
    tc = SCAN_CHUNK
    fblk, bblk, seq, first = [], [], [], []
    blk0 = 0
    for si, t in enumerate(seq_lens):
        nch = t // tc
        for c in range(nch):
            fblk.append(blk0 + c)
            bblk.append(blk0 + nch - 1 - c)
            seq.append(si)
            first.append(1 if c == 0 else 0)
        blk0 += nch
    n_steps = len(fblk)
    tables = [jnp.asarray(np.array(v, np.int32)) for v in (fblk, bblk, seq, first)]
    f_spec = pl.BlockSpec((tc, D_RNN), lambda s, fb, bb, sq, fr: (fb[s], 0))
    b_spec = pl.BlockSpec((tc, D_RNN), lambda s, fb, bb, sq, fr: (bb[s], 0))
    st_spec = pl.BlockSpec((None, 2, D_RNN), lambda s, fb, bb, sq, fr: (sq[s], 0, 0))
    out = jax.ShapeDtypeStruct((m, D_RNN), F32)
    return pl.pallas_call(
        functools.partial(_rnn_scan_kernel, tc=tc),
        out_shape=(out, out, jax.ShapeDtypeStruct(h0.shape, F32)),
        grid_spec=pltpu.PrefetchScalarGridSpec(
            num_scalar_prefetch=4, grid=(n_steps,),
            in_specs=[f_spec, f_spec, b_spec, b_spec, st_spec],
            out_specs=(f_spec, b_spec, st_spec),
            scratch_shapes=[pltpu.VMEM((2, D_RNN), F32)]),
        compiler_params=_cparams("arbitrary"),
        name="rnn_scan",
    )(*tables, a_f, u_f, a_b, u_b, h0)


def _ffn_kernel(x_ref, sc_ref, sh_ref, gate_ref, wg_ref, wu_ref, wd_ref, lng_ref, lnb_ref,
                o_ref, h_ref, acc_ref):
    j = pl.program_id(1)

    @pl.when(j == 0)
    def _():
        h_ref[...] = (x_ref[...] * (1.0 + sc_ref[...]) + sh_ref[...]).astype(BF16)
        acc_ref[...] = jnp.zeros_like(acc_ref)

    h = h_ref[...]
    g = jnp.dot(h, wg_ref[...], preferred_element_type=F32)
    u = jnp.dot(h, wu_ref[...], preferred_element_type=F32)
    act = ((g * jax.nn.sigmoid(g)) * u).astype(BF16)
    acc_ref[...] += jnp.dot(act, wd_ref[...], preferred_element_type=F32)

    @pl.when(j == pl.num_programs(1) - 1)
    def _():
        z = DEEPNORM_ALPHA * x_ref[...] + gate_ref[...] * acc_ref[...]
        o_ref[...] = _layer_norm_rows(z, lng_ref[...], lnb_ref[...])


def _ffn(x, mod, w_gu, w_down, lng, lnb, *, n_ctx, t_s):
    m = x.shape[0]
    tm = 1024
    tf = 256
    nf = D_FF // tf
    const = lambda arr: pl.BlockSpec(arr.shape, lambda i, j: (0,) * arr.ndim)
    row = pl.BlockSpec((tm, D_MODEL), lambda i, j: (i, 0))
    return pl.pallas_call(
        _ffn_kernel,
        out_shape=jax.ShapeDtypeStruct((m, D_MODEL), F32),
        grid=(m // tm, nf),
        in_specs=[row, _mod_spec(4, tm, n_ctx, t_s), _mod_spec(3, tm, n_ctx, t_s), _mod_spec(5, tm, n_ctx, t_s),
                  pl.BlockSpec((D_MODEL, tf), lambda i, j: (0, j)),
                  pl.BlockSpec((D_MODEL, tf), lambda i, j: (0, nf + j)),
                  pl.BlockSpec((tf, D_MODEL), lambda i, j: (j, 0)),
                  const(lng), const(lnb)],
        out_specs=row,
        scratch_shapes=[pltpu.VMEM((tm, D_MODEL), BF16), pltpu.VMEM((tm, D_MODEL), F32)],
        compiler_params=_cparams("parallel", "arbitrary"),
        name="ffn",
    )(x, mod, mod, mod, w_gu, w_gu, w_down, lng, lnb)


def _route_kernel(x_ref, sc_ref, sh_ref, wr_ref, h_ref, route_ref, gates_ref, cnt_ref, carry_ref, *, tm):
    @pl.when(pl.program_id(0) == 0)
    def _():
        carry_ref[...] = jnp.zeros_like(carry_ref)

    lane = lax.broadcasted_iota(jnp.int32, (tm, LANES), 1)
    h = x_ref[...] * (1.0 + sc_ref[...]) + sh_ref[...]
    h_ref[...] = h
    logits = jnp.dot(h.astype(BF16), wr_ref[...], preferred_element_type=F32)
    s1 = jnp.where(lane < N_EXPERTS, logits, -jnp.inf)
    m1 = jnp.max(s1, axis=-1, keepdims=True)
    i1 = jnp.min(jnp.where(s1 == m1, lane, LANES), axis=-1, keepdims=True)
    s2 = jnp.where(lane == i1, -jnp.inf, s1)
    m2 = jnp.max(s2, axis=-1, keepdims=True)
    i2 = jnp.min(jnp.where(s2 == m2, lane, LANES), axis=-1, keepdims=True)
    e2 = jnp.exp(m2 - m1)
    den = 1.0 + e2
    hit = jnp.where((lane == i1) | (lane == i2), 1.0, 0.0)
    earlier = lax.broadcasted_iota(jnp.int32, (tm, tm), 1) < lax.broadcasted_iota(jnp.int32, (tm, tm), 0)
    before = carry_ref[...] + jnp.dot(jnp.where(earlier, 1.0, 0.0).astype(BF16), hit.astype(BF16),
                                      preferred_element_type=F32)
    r1 = jnp.sum(jnp.where(lane == i1, before, 0.0), axis=-1, keepdims=True).astype(jnp.int32)
    r2 = jnp.sum(jnp.where(lane == i2, before, 0.0), axis=-1, keepdims=True).astype(jnp.int32)
    carry_ref[...] += jnp.sum(hit, axis=0, keepdims=True)
    route = jnp.where(lane == 0, i1, jnp.where(lane == 1, i2, jnp.where(lane == 2, r1, jnp.where(lane == 3, r2, 0))))
    route_ref[...] = route[:, :ROUTE_COLS]
    gates_ref[...] = jnp.where(lane == 0, 1.0 / den, jnp.where(lane == 1, e2 / den, 0.0))[:, :ROUTE_COLS]
    cnt_ref[...] = jnp.broadcast_to(carry_ref[...], cnt_ref.shape)


def _moe_route(x, mod, w_router, *, n_ctx, t_s):
    m = x.shape[0]
    tm = 512
    row = pl.BlockSpec((tm, D_MODEL), lambda i: (i, 0))
    small = pl.BlockSpec((tm, ROUTE_COLS), lambda i: (i, 0))
    return pl.pallas_call(
        functools.partial(_route_kernel, tm=tm),
        out_shape=(jax.ShapeDtypeStruct((m, D_MODEL), F32), jax.ShapeDtypeStruct((m, ROUTE_COLS), jnp.int32),
                   jax.ShapeDtypeStruct((m, ROUTE_COLS), F32), jax.ShapeDtypeStruct((SUBLANES, LANES), F32)),
        grid=(m // tm,),
        in_specs=[row, _mod_spec(4, tm, n_ctx, t_s), _mod_spec(3, tm, n_ctx, t_s),
                  pl.BlockSpec(w_router.shape, lambda i: (0, 0))],
        out_specs=(row, small, small, pl.BlockSpec((SUBLANES, LANES), lambda i: (0, 0))),
        scratch_shapes=[pltpu.VMEM((1, LANES), F32)],
        compiler_params=_cparams("arbitrary"),
        name="moe_route",
    )(x, mod, mod, w_router)


def _row_copy(src_hbm, row, buf, slot, dst_row, sem):
    return pltpu.make_async_copy(src_hbm.at[pl.ds(row, 1)], buf.at[slot, pl.ds(dst_row, 1)], sem.at[slot])


def _gather_rows_kernel(idx_ref, nxt_ref, src_hbm, o_ref, buf, sem, *, rows):
    i = pl.program_id(0)
    slot = i % 2

    def issue(ref, s):
        def body(r, c):
            _row_copy(src_hbm, ref[r], buf, s, r, sem).start()
            return c
        lax.fori_loop(0, rows, body, 0, unroll=8)

    @pl.when(i == 0)
    def _():
        issue(idx_ref, 0)

    @pl.when(i + 1 < pl.num_programs(0))
    def _():
        issue(nxt_ref, 1 - slot)

    def drain(r, c):
        _row_copy(src_hbm, 0, buf, slot, r, sem).wait()
        return c
    lax.fori_loop(0, rows, drain, 0, unroll=8)
    o_ref[...] = buf[slot].astype(o_ref.dtype)


def _moe_gather(h, src_rows, *, n_tiles):
    rows = MOE_TILE
    idx = lambda f: pl.BlockSpec((rows,), f, memory_space=pltpu.SMEM)
    return pl.pallas_call(
        functools.partial(_gather_rows_kernel, rows=rows),
        out_shape=jax.ShapeDtypeStruct((n_tiles * rows, D_MODEL), BF16),
        grid=(n_tiles,),
        in_specs=[idx(lambda i: (i,)), idx(lambda i: (jnp.minimum(i + 1, n_tiles - 1),)),
                  pl.BlockSpec(memory_space=pl.ANY)],
        out_specs=pl.BlockSpec((rows, D_MODEL), lambda i: (i, 0)),
        scratch_shapes=[pltpu.VMEM((2, rows, D_MODEL), F32), pltpu.SemaphoreType.DMA((2,))],
        compiler_params=_cparams("arbitrary"),
        name="moe_gather",
    )(src_rows, src_rows, h)


def _experts_kernel(te_ref, nu_ref, xs_ref, wg_ref, wu_ref, wd_ref, o_ref):
    del te_ref

    @pl.when(pl.program_id(0) < nu_ref[0])
    def _():
        xs = xs_ref[...]
        g = jnp.dot(xs, wg_ref[...], preferred_element_type=F32)
        u = jnp.dot(xs, wu_ref[...], preferred_element_type=F32)
        act = ((g * jax.nn.sigmoid(g)) * u).astype(BF16)
        o_ref[...] = jnp.dot(act, wd_ref[...], preferred_element_type=F32)

    @pl.when(pl.program_id(0) >= nu_ref[0])
    def _():
        o_ref[...] = jnp.zeros_like(o_ref)


def _moe_experts(tile_expert, n_used, xs, w_gu, w_down):
    rows = MOE_TILE
    n_tiles = xs.shape[0] // rows
    return pl.pallas_call(
        _experts_kernel,
        out_shape=jax.ShapeDtypeStruct(xs.shape, F32),
        grid_spec=pltpu.PrefetchScalarGridSpec(
            num_scalar_prefetch=2, grid=(n_tiles,),
            in_specs=[pl.BlockSpec((rows, D_MODEL), lambda i, te, nu: (i, 0)),
                      pl.BlockSpec((None, D_MODEL, D_FF_EXPERT), lambda i, te, nu: (te[i], 0, 0)),
                      pl.BlockSpec((None, D_MODEL, D_FF_EXPERT), lambda i, te, nu: (te[i], 0, 1)),
                      pl.BlockSpec((None, D_FF_EXPERT, D_MODEL), lambda i, te, nu: (te[i], 0, 0))],
            out_specs=pl.BlockSpec((rows, D_MODEL), lambda i, te, nu: (i, 0))),
        compiler_params=_cparams("arbitrary"),
        name="moe_experts",
    )(tile_expert, n_used, xs, w_gu, w_gu, w_down)


def _combine_kernel(idx_ref, nxt_ref, ys_hbm, gates_ref, x_ref, gate_ref, lng_ref, lnb_ref, o_ref, buf, sem, *, tm):
    i = pl.program_id(0)
    slot = i % 2

    def issue(ref, s):
        def body(t, c):
            _row_copy(ys_hbm, ref[2 * t], buf, s, t, sem).start()
            _row_copy(ys_hbm, ref[2 * t + 1], buf, s, tm + t, sem).start()
            return c
        lax.fori_loop(0, tm, body, 0, unroll=4)

    @pl.when(i == 0)
    def _():
        issue(idx_ref, 0)

    @pl.when(i + 1 < pl.num_programs(0))
    def _():
        issue(nxt_ref, 1 - slot)

    def drain(r, c):
        _row_copy(ys_hbm, 0, buf, slot, r, sem).wait()
        return c
    lax.fori_loop(0, 2 * tm, drain, 0, unroll=8)
    y = gates_ref[:, 0:1] * buf[slot, 0:tm, :] + gates_ref[:, 1:2] * buf[slot, tm:2 * tm, :]
    z = DEEPNORM_ALPHA * x_ref[...] + gate_ref[...] * y
    o_ref[...] = _layer_norm_rows(z, lng_ref[...], lnb_ref[...])


def _moe_combine(slots, ys, gates, x, mod, lng, lnb, *, n_ctx, t_s):
    m = x.shape[0]
    tm = 256
    n_blocks = m // tm
    idx = lambda f: pl.BlockSpec((2 * tm,), f, memory_space=pltpu.SMEM)
    const = lambda arr: pl.BlockSpec(arr.shape, lambda i: (0,) * arr.ndim)
    row = pl.BlockSpec((tm, D_MODEL), lambda i: (i, 0))
    return pl.pallas_call(
        functools.partial(_combine_kernel, tm=tm),
        out_shape=jax.ShapeDtypeStruct((m, D_MODEL), F32),
        grid=(n_blocks,),
        in_specs=[idx(lambda i: (i,)), idx(lambda i: (jnp.minimum(i + 1, n_blocks - 1),)),
                  pl.BlockSpec(memory_space=pl.ANY), pl.BlockSpec((tm, ROUTE_COLS), lambda i: (i, 0)),
                  row, _mod_spec(5, tm, n_ctx, t_s), const(lng), const(lnb)],
        out_specs=row,
        scratch_shapes=[pltpu.VMEM((2, 2 * tm, D_MODEL), F32), pltpu.SemaphoreType.DMA((2,))],
        compiler_params=_cparams("arbitrary"),
        name="moe_combine",
    )(slots, slots, ys, gates, x, mod, lng, lnb)


def _moe(x, mod, w_router, w_gu, w_down, lng, lnb, *, n_ctx, t_s):
    m = x.shape[0]
    h, route, gates, cnt = _moe_route(x, mod, w_router, n_ctx=n_ctx, t_s=t_s)
    counts = cnt[0, :N_EXPERTS].astype(jnp.int32)
    padded = (counts + MOE_TILE - 1) // MOE_TILE * MOE_TILE
    g_end = jnp.cumsum(padded)
    g_start = g_end - padded
    slots = (g_start[route[:, 0:2]] + route[:, 2:4]).reshape(2 * m)
    n_tiles = 2 * m // MOE_TILE + N_EXPERTS
    src_rows = jnp.zeros((n_tiles * MOE_TILE,), jnp.int32).at[slots].set(jnp.arange(2 * m, dtype=jnp.int32) // 2)
    tile_row0 = jnp.arange(n_tiles, dtype=jnp.int32) * MOE_TILE
    tile_expert = jnp.minimum(jnp.sum(tile_row0[:, None] >= g_end[None, :], axis=1), N_EXPERTS - 1).astype(jnp.int32)
    n_used = (g_end[-1:] // MOE_TILE).astype(jnp.int32)
    xs = _moe_gather(h, src_rows, n_tiles=n_tiles)
    ys = _moe_experts(tile_expert, n_used, xs, w_gu, w_down)
    return _moe_combine(slots, ys, gates, x, mod, lng, lnb, n_ctx=n_ctx, t_s=t_s)


def _rope_tables(n_tokens):
    n_rows = n_tokens // GRID_W
    row = jnp.repeat(jnp.arange(n_rows, dtype=F32), GRID_W)
    col = jnp.tile(jnp.arange(GRID_W, dtype=F32), n_rows)
    n_axis = HEAD_DIM // 4
    inv = ROPE_THETA ** (-jnp.arange(n_axis, dtype=F32) / n_axis)
    ang = jnp.concatenate([row[:, None] * inv, col[:, None] * inv], axis=-1)
    cos, sin = jnp.cos(ang), jnp.sin(ang)
    reps = LANES // HEAD_DIM
    cos_t = jnp.tile(jnp.concatenate([cos, cos], axis=-1), (1, reps))
    sin_t = jnp.tile(jnp.concatenate([-sin, sin], axis=-1), (1, reps))
    return cos_t, sin_t


def _gqa_head_perm():
    cols = []
    per_pair = 2 * (N_Q_HEADS // N_KV_HEADS)
    for p in range(N_KV_HEADS // 2):
        for j in range(N_Q_HEADS // N_KV_HEADS):
            for hd in (per_pair * p + j, per_pair * p + per_pair // 2 + j):
                cols.append(np.arange(hd * HEAD_DIM, (hd + 1) * HEAD_DIM))
    return np.concatenate(cols)


def _block_diag_windows(w):
    eye = jnp.eye(N_RNN_BLOCKS, dtype=bool)[:, None, :, None]
    dense = jnp.where(eye, w[:, :, None, :], 0.0).reshape(D_RNN, D_RNN)
    for ct, k0 in enumerate(GATE_K0):
        lo, hi = ct * GATE_N, (ct + 1) * GATE_N
        assert k0 <= RNN_BLOCK * (lo // RNN_BLOCK) and RNN_BLOCK * -(-hi // RNN_BLOCK) <= k0 + GATE_K <= D_RNN
    return jnp.stack([dense[k0:k0 + GATE_K, ct * GATE_N:(ct + 1) * GATE_N] for ct, k0 in enumerate(GATE_K0)])


def kernel(x_prompt, x_sample, cache_attn_k, cache_attn_v, state_rglru, cache_diff_k, cache_diff_v, c, c_ctx, w_ada, b_ada, ln_g, ln_b, attn_w_qkv, attn_g_q, attn_g_k, attn_w_o, rnn_w_in, rnn_conv_w, rnn_conv_b, rnn_w_a, rnn_b_a, rnn_w_x, rnn_b_x, rnn_lambda, rnn_w_out, diff_w_qkv, diff_lambda, diff_g_sub, diff_w_o, ffn_w_gu, ffn_w_down, moe_w_router, moe_w_gu, moe_w_down):
    b_c, t_c, d = x_prompt.shape
    b_s, t_s, _ = x_sample.shape
    n_ctx = b_c * t_c
    n_s = b_s * t_s
    past = cache_attn_k.shape[2]
    assert d == D_MODEL and n_ctx % t_s == 0 and t_c == SCAN_CHUNK and t_s % SCAN_CHUNK == 0
    assert 1 + b_s <= COND_ROWS

    x = jnp.concatenate([x_prompt.reshape(n_ctx, d), x_sample.reshape(n_s, d)], axis=0)
    cond = jnp.zeros((COND_ROWS, d), F32).at[0].set(c_ctx).at[1:1 + b_s].set(c)
    mods = _ada_table(cond, w_ada, b_ada)[:, :1 + b_s].reshape(DEPTH, 1 + b_s, 1, 6 * d)

    cos_t, sin_t = _rope_tables(t_s)
    perm = _gqa_head_perm()
    nq = N_Q_HEADS * HEAD_DIM
    tile2 = lambda g: jnp.tile(g, LANES // g.shape[-1]).reshape(1, LANES)
    kw = dict(n_ctx=n_ctx, t_s=t_s)

    attn_k, attn_v, rnn_s, diff_k, diff_v = [], [], [], [], []
    for li in range(DEPTH):
        mod = mods[li]
        lng = ln_g[li].reshape(2, 1, d)
        lnb = ln_b[li].reshape(2, 1, d)
        j = li // N_MIXERS
        kind = li % N_MIXERS
        if kind == 0:
            w = attn_w_qkv[j]
            w = jnp.concatenate([w[:, :nq][:, perm], w[:, nq:]], axis=1).astype(BF16)
            q, kb, vb, kf, vf = _gqa_qkv(x, mod, w, tile2(attn_g_q[j]), tile2(attn_g_k[j]), cos_t, sin_t, **kw)
            ck = cache_attn_k[:, j].reshape(b_s, past, N_KV_HEADS * HEAD_DIM).astype(BF16)
            cv = cache_attn_v[:, j].reshape(b_s, past, N_KV_HEADS * HEAD_DIM).astype(BF16)
            o_c = _gqa_attention(q, kb, vb, None, None, row0=0, n_b=b_c, t=t_c, tq=t_c)
            o_s = _gqa_attention(q, kb, vb, ck, cv, row0=n_ctx, n_b=b_s, t=t_s, tq=256)
            x = _oproj(o_c, o_s, attn_w_o[j][perm, :].astype(BF16), x, mod, lng[0], lnb[0], **kw)
            attn_k.append(kf[:n_ctx])
            attn_v.append(vf[:n_ctx])
        elif kind == 1:
            bx = _rnn_in(x, mod, rnn_w_in[j].astype(BF16), **kw)
            wg = jnp.stack([_block_diag_windows(rnn_w_a[j, 0]), _block_diag_windows(rnn_w_x[j, 0]),
                            _block_diag_windows(rnn_w_a[j, 1]), _block_diag_windows(rnn_w_x[j, 1])]).astype(BF16)
            bg = jnp.concatenate([rnn_b_a[j, 0], rnn_b_x[j, 0], rnn_b_a[j, 1], rnn_b_x[j, 1]]).reshape(1, -1)
            a_f, u_f, a_b, u_b = _rnn_gates(bx, rnn_conv_w[j], rnn_conv_b[j].reshape(1, -1), wg, bg,
                                            rnn_lambda[j], n_ctx=n_ctx, t_c=t_c, t_s=t_s)
            h0 = jnp.concatenate([jnp.zeros((b_c, 2, D_RNN), F32), state_rglru[:, j]], axis=0)
            hs_f, hs_b, fin = _rnn_scan(a_f, u_f, a_b, u_b, h0, seq_lens=[t_c] * b_c + [t_s] * b_s)
            x = _rnn_oproj(hs_f, hs_b, bx, rnn_w_out[j].astype(BF16), x, mod, lng[0], lnb[0], **kw)
            rnn_s.append(fin[:b_c])
        else:
            lam_init = 0.8 - 0.6 * math.exp(-0.3 * li)
            q, kb, vb, kf, vf = _diff_qkv(x, mod, diff_w_qkv[j].astype(BF16), cos_t, sin_t, **kw)
            ck = cache_diff_k[:, j].reshape(b_s, past, D_MODEL).astype(BF16)
            cv = cache_diff_v[:, j].reshape(b_s, past, D_MODEL).astype(BF16)
            g_sub = diff_g_sub[j].reshape(1, LANES)
            o_c = _diff_attention(diff_lambda[j], g_sub, q, kb, vb, None, None,
                                  row0=0, n_b=b_c, t=t_c, tq=t_c, lam_init=lam_init)
            o_s = _diff_attention(diff_lambda[j], g_sub, q, kb, vb, ck, cv,
                                  row0=n_ctx, n_b=b_s, t=t_s, tq=256, lam_init=lam_init)
            x = _oproj(o_c, o_s, diff_w_o[j].astype(BF16), x, mod, lng[0], lnb[0], **kw)
            diff_k.append(kf[:n_ctx])
            diff_v.append(vf[:n_ctx])
        if li % 2 == 0:
            x = _ffn(x, mod, ffn_w_gu[li // 2].astype(BF16), ffn_w_down[li // 2].astype(BF16), lng[1], lnb[1], **kw)
        else:
            w_r = jnp.zeros((d, LANES), F32).at[:, :N_EXPERTS].set(moe_w_router[li // 2]).astype(BF16)
            x = _moe(x, mod, w_r, moe_w_gu[li // 2].astype(BF16), moe_w_down[li // 2].astype(BF16),
                     lng[1], lnb[1], **kw)

    y_prompt = x[:n_ctx].reshape(b_c, t_c, d)
    y_sample = x[n_ctx:].reshape(b_s, t_s, d)
    new_attn_k = jnp.stack([k.reshape(b_c, t_c, N_KV_HEADS, HEAD_DIM) for k in attn_k], axis=1)
    new_attn_v = jnp.stack([v.reshape(b_c, t_c, N_KV_HEADS, HEAD_DIM) for v in attn_v], axis=1)
    new_state = jnp.stack(rnn_s, axis=1)
    new_diff_k = jnp.stack([k.reshape(b_c, t_c, N_DIFF_HEADS, 2, HEAD_DIM) for k in diff_k], axis=1)
    new_diff_v = jnp.stack([v.reshape(b_c, t_c, N_DIFF_HEADS, 2 * HEAD_DIM) for v in diff_v], axis=1)
    return (y_prompt, y_sample, new_attn_k, new_attn_v, new_state, new_diff_k, new_diff_v)
```

```python
import functools
import math

import jax
import jax.numpy as jnp
import numpy as np
from jax import lax
from jax.experimental import pallas as pl
from jax.experimental.pallas import tpu as pltpu

F32 = jnp.float32
BF16 = jnp.bfloat16

D_MODEL = 1024
DEPTH = 4
GRID_W = 64
HEAD_DIM = 64
N_Q_HEADS = 16
N_KV_HEADS = 4
ROPE_THETA = 10000.0
N_DIFF_HEADS = 8
D_RNN = 1280
N_RNN_BLOCKS = 16
RNN_BLOCK = 80
CONV_W = 4
CONV_LEFT = 2
RGLRU_C = 8.0
D_FF = 2816
N_EXPERTS = 8
D_FF_EXPERT = 1408
N_MIXERS = 3
DEEPNORM_ALPHA = (2.0 * DEPTH) ** 0.25
LN_EPS = 1e-6
RMS_EPS = 1e-6

LANES = 128
SUBLANES = 8
VMEM_LIMIT = 56 * 1024 * 1024
COND_ROWS = 8
SCAN_CHUNK = 256
MOE_TILE = 512
ROUTE_COLS = 8
GATE_N = 256
GATE_K = 512
GATE_K0 = (0, 128, 384, 640, 768)
Q_SCALE = HEAD_DIM ** -0.5 * math.log2(math.e)


def _cparams(*sem):
    return pltpu.CompilerParams(dimension_semantics=sem, vmem_limit_bytes=VMEM_LIMIT)


def _layer_norm_rows(z, g, b):
    mu = jnp.mean(z, axis=-1, keepdims=True)
    zc = z - mu
    var = jnp.mean(zc * zc, axis=-1, keepdims=True)
    return zc * lax.rsqrt(var + LN_EPS) * g + b


def _seg_of_block(i, tm, n_ctx, t_s):
    r0 = i * tm
    return jnp.where(r0 < n_ctx, 0, 1 + (r0 - n_ctx) // t_s)


def _mod_spec(col, tm, n_ctx, t_s):
    return pl.BlockSpec((None, 1, D_MODEL), lambda i, *_: (_seg_of_block(i, tm, n_ctx, t_s), 0, col))


def _ada_kernel(c_ref, w_ref, b_ref, o_ref):
    c = c_ref[...]
    a = (c * jax.nn.sigmoid(c)).astype(BF16)
    o_ref[...] = jnp.dot(a, w_ref[...].astype(BF16), preferred_element_type=F32) + b_ref[...]


def _ada_table(cond, w_ada, b_ada):
    n_l, d, n = w_ada.shape
    tn = 1536
    return pl.pallas_call(
        _ada_kernel,
        out_shape=jax.ShapeDtypeStruct((n_l, COND_ROWS, n), F32),
        grid=(n_l, n // tn),
        in_specs=[pl.BlockSpec((COND_ROWS, d), lambda l, j: (0, 0)),
                  pl.BlockSpec((None, d, tn), lambda l, j: (l, 0, j)),
                  pl.BlockSpec((None, 1, tn), lambda l, j: (l, 0, j))],
        out_specs=pl.BlockSpec((None, COND_ROWS, tn), lambda l, j: (l, 0, j)),
        compiler_params=_cparams("parallel", "parallel"),
        name="ada_table",
    )(cond, w_ada, b_ada.reshape(n_l, 1, n))


def _head_masks(tm):
    lane = lax.broadcasted_iota(jnp.int32, (tm, LANES), 1)
    return lane < HEAD_DIM, (lane & (HEAD_DIM - 1)) < HEAD_DIM // 2


def _rope_tile(t, first_half, cos, sin):
    partner = jnp.where(first_half, pltpu.roll(t, LANES - HEAD_DIM // 2, 1),
                        pltpu.roll(t, HEAD_DIM // 2, 1))
    return t * cos + partner * sin


def _gqa_qkv_kernel(x_ref, sc_ref, sh_ref, w_ref, gq_ref, gk_ref, cos_ref, sin_ref,
                    q_ref, kb_ref, vb_ref, kf_ref, vf_ref, *, tm, n_ctx):
    i = pl.program_id(0)
    h = (x_ref[...] * (1.0 + sc_ref[...]) + sh_ref[...]).astype(BF16)
    y = jnp.dot(h, w_ref[...], preferred_element_type=F32)
    use_rope = i * tm >= n_ctx
    cos = jnp.where(use_rope, cos_ref[...], 1.0)
    sin = jnp.where(use_rope, sin_ref[...], 0.0)
    low_head, first_half = _head_masks(tm)

    def norm_rope(t, g):
        t2 = t * t
        s_lo = jnp.sum(jnp.where(low_head, t2, 0.0), axis=-1, keepdims=True)
        s_hi = jnp.sum(jnp.where(low_head, 0.0, t2), axis=-1, keepdims=True)
        inv = jnp.where(low_head, lax.rsqrt(s_lo * (1.0 / HEAD_DIM) + RMS_EPS),
                        lax.rsqrt(s_hi * (1.0 / HEAD_DIM) + RMS_EPS))
        return _rope_tile(t * inv * g, first_half, cos, sin)

    nq = N_Q_HEADS * HEAD_DIM
    nkv = N_KV_HEADS * HEAD_DIM
    gq = gq_ref[...]
    gk = gk_ref[...]
    for t in range(nq // LANES):
        sl = slice(t * LANES, (t + 1) * LANES)
        q_ref[:, sl] = (norm_rope(y[:, sl], gq) * Q_SCALE).astype(BF16)
    for t in range(nkv // LANES):
        sl = slice(t * LANES, (t + 1) * LANES)
        k = norm_rope(y[:, nq + t * LANES: nq + (t + 1) * LANES], gk)
        kf_ref[:, sl] = k
        kb_ref[:, sl] = k.astype(BF16)
    v = y[:, nq + nkv:]
    vf_ref[...] = v
    vb_ref[...] = v.astype(BF16)


def _gqa_qkv(x, mod, w, gq, gk, cos, sin, *, n_ctx, t_s):
    m = x.shape[0]
    tm = 512
    nq = N_Q_HEADS * HEAD_DIM
    nkv = N_KV_HEADS * HEAD_DIM
    n_rope_blocks = t_s // tm
    rope_spec = pl.BlockSpec(
        (tm, LANES), lambda i: (jnp.where(i * tm >= n_ctx, ((i * tm - n_ctx) % t_s) // tm, 0) % n_rope_blocks, 0))
    row = lambda n: pl.BlockSpec((tm, n), lambda i: (i, 0))
    const = lambda a: pl.BlockSpec(a.shape, lambda i: (0,) * a.ndim)
    return pl.pallas_call(
        functools.partial(_gqa_qkv_kernel, tm=tm, n_ctx=n_ctx),
        out_shape=(jax.ShapeDtypeStruct((m, nq), BF16), jax.ShapeDtypeStruct((m, nkv), BF16),
                   jax.ShapeDtypeStruct((m, nkv), BF16), jax.ShapeDtypeStruct((m, nkv), F32),
                   jax.ShapeDtypeStruct((m, nkv), F32)),
        grid=(m // tm,),
        in_specs=[row(D_MODEL), _mod_spec(1, tm, n_ctx, t_s), _mod_spec(0, tm, n_ctx, t_s),
                  const(w), const(gq), const(gk), rope_spec, rope_spec],
        out_specs=(row(nq), row(nkv), row(nkv), row(nkv), row(nkv)),
        compiler_params=_cparams("parallel"),
        name="gqa_qkv",
    )(x, mod, mod, w, gq, gk, cos, sin)


def _diff_qkv_kernel(x_ref, sc_ref, sh_ref, w_ref, cos_ref, sin_ref,
                     q_ref, kb_ref, vb_ref, kf_ref, vf_ref, *, tm, n_ctx):
    i = pl.program_id(0)
    h = (x_ref[...] * (1.0 + sc_ref[...]) + sh_ref[...]).astype(BF16)
    y = jnp.dot(h, w_ref[...], preferred_element_type=F32)
    use_rope = i * tm >= n_ctx
    cos = jnp.where(use_rope, cos_ref[...], 1.0)
    sin = jnp.where(use_rope, sin_ref[...], 0.0)
    _, first_half = _head_masks(tm)
    for t in range(D_MODEL // LANES):
        sl = slice(t * LANES, (t + 1) * LANES)
        q = _rope_tile(y[:, sl], first_half, cos, sin)
        q_ref[:, sl] = (q * Q_SCALE).astype(BF16)
        k = _rope_tile(y[:, D_MODEL + t * LANES: D_MODEL + (t + 1) * LANES], first_half, cos, sin)
        kf_ref[:, sl] = k
        kb_ref[:, sl] = k.astype(BF16)
    v = y[:, 2 * D_MODEL:]
    vf_ref[...] = v
    vb_ref[...] = v.astype(BF16)


def _diff_qkv(x, mod, w, cos, sin, *, n_ctx, t_s):
    m = x.shape[0]
    tm = 256
    n_rope_blocks = t_s // tm
    rope_spec = pl.BlockSpec(
        (tm, LANES), lambda i: (jnp.where(i * tm >= n_ctx, ((i * tm - n_ctx) % t_s) // tm, 0) % n_rope_blocks, 0))
    row = pl.BlockSpec((tm, D_MODEL), lambda i: (i, 0))
    return pl.pallas_call(
        functools.partial(_diff_qkv_kernel, tm=tm, n_ctx=n_ctx),
        out_shape=(jax.ShapeDtypeStruct((m, D_MODEL), BF16), jax.ShapeDtypeStruct((m, D_MODEL), BF16),
                   jax.ShapeDtypeStruct((m, D_MODEL), BF16), jax.ShapeDtypeStruct((m, D_MODEL), F32),
                   jax.ShapeDtypeStruct((m, D_MODEL), F32)),
        grid=(m // tm,),
        in_specs=[row, _mod_spec(1, tm, n_ctx, t_s), _mod_spec(0, tm, n_ctx, t_s),
                  pl.BlockSpec(w.shape, lambda i: (0, 0)), rope_spec, rope_spec],
        out_specs=(row, row, row, row, row),
        compiler_params=_cparams("parallel"),
        name="diff_qkv",
    )(x, mod, mod, w, cos, sin)


_NT = (((1,), (1,)), ((), ()))


def _with_ones_column(v, low_head, lane, keep_low):
    ones_lane = HEAD_DIM if keep_low else 0
    other = jnp.where(lane == ones_lane, 1.0, 0.0).astype(v.dtype)
    return jnp.where(low_head == keep_low, v, other)


def _gqa_attn_kernel(*refs, has_cache, tq):
    if has_cache:
        q_ref, k_ref, v_ref, ck_ref, cv_ref, o_ref = refs
    else:
        q_ref, k_ref, v_ref, o_ref = refs
    low_q = lax.broadcasted_iota(jnp.int32, (tq, LANES), 1) < HEAD_DIM
    k = k_ref[...]
    lane_v = lax.broadcasted_iota(jnp.int32, v_ref.shape, 1)
    v_aug = [_with_ones_column(v_ref[...], lane_v < HEAD_DIM, lane_v, keep) for keep in (True, False)]
    if has_cache:
        ck = ck_ref[...]
        lane_c = lax.broadcasted_iota(jnp.int32, cv_ref.shape, 1)
        cv_aug = [_with_ones_column(cv_ref[...], lane_c < HEAD_DIM, lane_c, keep) for keep in (True, False)]
    n_tiles = q_ref.shape[1] // LANES
    heads = [(j, keep) for j in range(n_tiles) for keep in (True, False)]

    def scores(j, keep_low):
        qt = q_ref[:, j * LANES:(j + 1) * LANES]
        qh = jnp.where(low_q == keep_low, qt, jnp.zeros_like(qt))
        s = lax.dot_general(qh, k, _NT, preferred_element_type=F32)
        s_c = lax.dot_general(qh, ck, _NT, preferred_element_type=F32) if has_cache else None
        return s, s_c

    nxt = scores(*heads[0])
    halves = []
    for idx, (j, keep_low) in enumerate(heads):
        s, s_c = nxt
        if idx + 1 < len(heads):
            nxt = scores(*heads[idx + 1])
        mx = jnp.max(s, axis=-1, keepdims=True)
        if has_cache:
            mx = jnp.maximum(mx, jnp.max(s_c, axis=-1, keepdims=True))
        side = 0 if keep_low else 1
        acc = jnp.dot(jnp.exp2(s - mx).astype(BF16), v_aug[side], preferred_element_type=F32)
        if has_cache:
            acc = acc + jnp.dot(jnp.exp2(s_c - mx).astype(BF16), cv_aug[side], preferred_element_type=F32)
        den = acc[:, HEAD_DIM:HEAD_DIM + 1] if keep_low else acc[:, 0:1]
        halves.append(acc / den)
        if not keep_low:
            o_ref[:, j * LANES:(j + 1) * LANES] = jnp.where(low_q, halves[0], halves[1]).astype(BF16)
            halves = []


def _gqa_attention(q, k, v, cache_k, cache_v, *, row0, n_b, t, tq):
    n_pairs = N_KV_HEADS // 2
    qw = D_MODEL // n_pairs
    nq_blocks = t // tq
    has_cache = cache_k is not None
    q_spec = pl.BlockSpec((tq, qw), lambda b, p, iq: (row0 // tq + b * nq_blocks + iq, p))
    kv_spec = pl.BlockSpec((t, LANES), lambda b, p, iq: (row0 // t + b, p))
    in_specs = [q_spec, kv_spec, kv_spec]
    args = [q, k, v]
    if has_cache:
        c_spec = pl.BlockSpec((None, cache_k.shape[1], LANES), lambda b, p, iq: (b, 0, p))
        in_specs += [c_spec, c_spec]
        args += [cache_k, cache_v]
    return pl.pallas_call(
        functools.partial(_gqa_attn_kernel, has_cache=has_cache, tq=tq),
        out_shape=jax.ShapeDtypeStruct((n_b * t, D_MODEL), BF16),
        grid=(n_b, n_pairs, nq_blocks),
        in_specs=in_specs,
        out_specs=pl.BlockSpec((tq, qw), lambda b, p, iq: (b * nq_blocks + iq, p)),
        compiler_params=_cparams("parallel", "parallel", "arbitrary"),
        name="gqa_attn_cache" if has_cache else "gqa_attn",
    )(*args)


def _diff_attn_kernel(*refs, has_cache, tq, lam_init):
    if has_cache:
        lam_ref, g_ref, q_ref, k_ref, v_ref, ck_ref, cv_ref, o_ref = refs
    else:
        lam_ref, g_ref, q_ref, k_ref, v_ref, o_ref = refs
    lv = lam_ref[...]
    lam = (jnp.exp(jnp.sum(lv[0:1] * lv[1:2], axis=-1, keepdims=True))
           - jnp.exp(jnp.sum(lv[2:3] * lv[3:4], axis=-1, keepdims=True)) + lam_init)
    low_head = lax.broadcasted_iota(jnp.int32, (tq, LANES), 1) < HEAD_DIM
    qt = q_ref[...]
    k = k_ref[...]
    probs = []
    for keep_low in (True, False):
        qh = jnp.where(low_head == keep_low, qt, jnp.zeros_like(qt))
        s = lax.dot_general(qh, k, _NT, preferred_element_type=F32)
        mx = jnp.max(s, axis=-1, keepdims=True)
        if has_cache:
            s_c = lax.dot_general(qh, ck_ref[...], _NT, preferred_element_type=F32)
            mx = jnp.maximum(mx, jnp.max(s_c, axis=-1, keepdims=True))
        e = jnp.exp2(s - mx)
        den = jnp.sum(e, axis=-1, keepdims=True)
        if has_cache:
            e_c = jnp.exp2(s_c - mx)
            den = den + jnp.sum(e_c, axis=-1, keepdims=True)
            inv = 1.0 / den
            probs.append((e * inv, e_c * inv))
        else:
            probs.append((e * (1.0 / den),))
    pd = (probs[0][0] - lam * probs[1][0]).astype(BF16)
    o = jnp.dot(pd, v_ref[...], preferred_element_type=F32)
    if has_cache:
        pd_c = (probs[0][1] - lam * probs[1][1]).astype(BF16)
        o = o + jnp.dot(pd_c, cv_ref[...], preferred_element_type=F32)
    inv = lax.rsqrt(jnp.mean(o * o, axis=-1, keepdims=True) + RMS_EPS)
    o_ref[...] = ((o * inv * g_ref[...]) * (1.0 - lam_init)).astype(BF16)


def _diff_attention(lam_vec, g_sub, q, k, v, cache_k, cache_v, *, row0, n_b, t, tq, lam_init):
    nq_blocks = t // tq
    has_cache = cache_k is not None
    q_spec = pl.BlockSpec((tq, LANES), lambda b, h, iq: (row0 // tq + b * nq_blocks + iq, h))
    kv_spec = pl.BlockSpec((t, LANES), lambda b, h, iq: (row0 // t + b, h))
    const = lambda a: pl.BlockSpec(a.shape, lambda b, h, iq: (0,) * a.ndim)
    in_specs = [const(lam_vec), const(g_sub), q_spec, kv_spec, kv_spec]
    args = [lam_vec, g_sub, q, k, v]
    if has_cache:
        c_spec = pl.BlockSpec((None, cache_k.shape[1], LANES), lambda b, h, iq: (b, 0, h))
        in_specs += [c_spec, c_spec]
        args += [cache_k, cache_v]
    return pl.pallas_call(
        functools.partial(_diff_attn_kernel, has_cache=has_cache, tq=tq, lam_init=lam_init),
        out_shape=jax.ShapeDtypeStruct((n_b * t, D_MODEL), BF16),
        grid=(n_b, N_DIFF_HEADS, nq_blocks),
        in_specs=in_specs,
        out_specs=pl.BlockSpec((tq, LANES), lambda b, h, iq: (b * nq_blocks + iq, h)),
        compiler_params=_cparams("parallel", "parallel", "arbitrary"),
        name="diff_attn_cache" if has_cache else "diff_attn",
    )(*args)


def _oproj_kernel(ac_ref, as_ref, w_ref, x_ref, gate_ref, lng_ref, lnb_ref, o_ref, *, n_ctx_blocks):
    is_ctx = pl.program_id(0) < n_ctx_blocks
    a = jnp.where(is_ctx, ac_ref[...], as_ref[...])
    out = jnp.dot(a, w_ref[...], preferred_element_type=F32)
    z = DEEPNORM_ALPHA * x_ref[...] + gate_ref[...] * out
    o_ref[...] = _layer_norm_rows(z, lng_ref[...], lnb_ref[...])


def _oproj(a_ctx, a_smp, w, x, mod, lng, lnb, *, n_ctx, t_s):
    m = x.shape[0]
    kdim = a_ctx.shape[1]
    tm = 512
    ncb = n_ctx // tm
    const = lambda arr: pl.BlockSpec(arr.shape, lambda i: (0,) * arr.ndim)
    row = pl.BlockSpec((tm, D_MODEL), lambda i: (i, 0))
    return pl.pallas_call(
        functools.partial(_oproj_kernel, n_ctx_blocks=ncb),
        out_shape=jax.ShapeDtypeStruct((m, D_MODEL), F32),
        grid=(m // tm,),
        in_specs=[pl.BlockSpec((tm, kdim), lambda i: (jnp.minimum(i, ncb - 1), 0)),
                  pl.BlockSpec((tm, kdim), lambda i: (jnp.maximum(i - ncb, 0), 0)),
                  const(w), row, _mod_spec(2, tm, n_ctx, t_s), const(lng), const(lnb)],
        out_specs=row,
        compiler_params=_cparams("parallel"),
        name="oproj_ln",
    )(a_ctx, a_smp, w, x, mod, lng, lnb)


def _gelu_tanh(x):
    return 0.5 * x * (1.0 + jnp.tanh(math.sqrt(2.0 / math.pi) * (x + 0.044715 * (x * x * x))))


def _rnn_oproj_kernel(hf_ref, hb_ref, br_ref, w_ref, x_ref, gate_ref, lng_ref, lnb_ref, o_ref):
    y = (hf_ref[...] + hb_ref[...]) * _gelu_tanh(br_ref[...])
    out = jnp.dot(y.astype(BF16), w_ref[...], preferred_element_type=F32)
    z = DEEPNORM_ALPHA * x_ref[...] + gate_ref[...] * out
    o_ref[...] = _layer_norm_rows(z, lng_ref[...], lnb_ref[...])


def _rnn_oproj(hs_f, hs_b, bx, w, x, mod, lng, lnb, *, n_ctx, t_s):
    m = x.shape[0]
    tm = 512
    const = lambda arr: pl.BlockSpec(arr.shape, lambda i: (0,) * arr.ndim)
    row = pl.BlockSpec((tm, D_MODEL), lambda i: (i, 0))
    rnn_row = pl.BlockSpec((tm, D_RNN), lambda i: (i, 0))
    return pl.pallas_call(
        _rnn_oproj_kernel,
        out_shape=jax.ShapeDtypeStruct((m, D_MODEL), F32),
        grid=(m // tm,),
        in_specs=[rnn_row, rnn_row, rnn_row, const(w), row,
                  _mod_spec(2, tm, n_ctx, t_s), const(lng), const(lnb)],
        out_specs=row,
        compiler_params=_cparams("parallel"),
        name="rnn_oproj_ln",
    )(hs_f, hs_b, bx, w, x, mod, lng, lnb)


def _mod_mm_kernel(x_ref, sc_ref, sh_ref, w_ref, o_ref):
    h = (x_ref[...] * (1.0 + sc_ref[...]) + sh_ref[...]).astype(BF16)
    o_ref[...] = jnp.dot(h, w_ref[...], preferred_element_type=F32)


def _rnn_in(x, mod, w, *, n_ctx, t_s):
    m = x.shape[0]
    n = w.shape[1]
    tm = 512
    return pl.pallas_call(
        _mod_mm_kernel,
        out_shape=jax.ShapeDtypeStruct((m, n), F32),
        grid=(m // tm,),
        in_specs=[pl.BlockSpec((tm, D_MODEL), lambda i: (i, 0)),
                  _mod_spec(1, tm, n_ctx, t_s), _mod_spec(0, tm, n_ctx, t_s),
                  pl.BlockSpec(w.shape, lambda i: (0, 0))],
        out_specs=pl.BlockSpec((tm, n), lambda i: (i, 0)),
        compiler_params=_cparams("parallel"),
        name="rnn_in",
    )(x, mod, mod, w)


def _rnn_gate_kernel(xb_ref, prev_ref, next_ref, cw_ref, cb_ref, wg_ref, bg_ref, lam_ref,
                     af_ref, uf_ref, ab_ref, ub_ref, ext_ref, *, tm, n_ctx, t_c, t_s):
    i = pl.program_id(0)
    r0 = i * tm
    pos = jnp.where(r0 < n_ctx, r0 % t_c, (r0 - n_ctx) % t_s)
    t_seq = jnp.where(r0 < n_ctx, t_c, t_s)
    at_start = pos == 0
    at_end = pos + tm == t_seq
    ext_ref[0:SUBLANES, :] = jnp.where(at_start, 0.0, prev_ref[...])
    ext_ref[SUBLANES:SUBLANES + tm, :] = xb_ref[...]
    ext_ref[SUBLANES + tm:, :] = jnp.where(at_end, 0.0, next_ref[...])
    xc = 0.0
    for j in range(CONV_W):
        off = SUBLANES - CONV_LEFT + j
        xc = xc + ext_ref[off:off + tm, :] * cw_ref[j:j + 1, :]
    xc = xc + cb_ref[...]
    xcb = xc.astype(BF16)
    neg_lam = -lam_ref[...]
    sp = jnp.maximum(neg_lam, 0.0) + jnp.log1p(jnp.exp(-jnp.abs(neg_lam)))

    def gate(g, ct, cols):
        k0 = GATE_K0[ct]
        pre = jnp.dot(xcb[:, k0:k0 + GATE_K], wg_ref[g, ct], preferred_element_type=F32)
        return jax.nn.sigmoid(pre + bg_ref[:, g * D_RNN + cols.start:g * D_RNN + cols.stop])

    for z, (a_ref, u_ref) in enumerate(((af_ref, uf_ref), (ab_ref, ub_ref))):
        for ct in range(D_RNN // GATE_N):
            cols = slice(ct * GATE_N, (ct + 1) * GATE_N)
            r = gate(2 * z, ct, cols)
            g_in = gate(2 * z + 1, ct, cols)
            log_a = -RGLRU_C * r * sp[z:z + 1, cols]
            a = jnp.exp(log_a)
            a_ref[:, cols] = a
            u_ref[:, cols] = jnp.sqrt(-jnp.tanh(log_a) * (a * a + 1.0)) * (g_in * xc[:, cols])


def _rnn_gates(bx, conv_w, conv_b, wg, bg, lam, *, n_ctx, t_c, t_s):
    m = bx.shape[0]
    tm = 256
    n8 = tm // SUBLANES
    last8 = m // SUBLANES - 1
    const = lambda arr: pl.BlockSpec(arr.shape, lambda i: (0,) * arr.ndim)
    row = pl.BlockSpec((tm, D_RNN), lambda i: (i, 0))
    out = jax.ShapeDtypeStruct((m, D_RNN), F32)
    return pl.pallas_call(
        functools.partial(_rnn_gate_kernel, tm=tm, n_ctx=n_ctx, t_c=t_c, t_s=t_s),
        out_shape=(out, out, out, out),
        grid=(m // tm,),
        in_specs=[pl.BlockSpec((tm, D_RNN), lambda i: (i, 1)),
                  pl.BlockSpec((SUBLANES, D_RNN), lambda i: (jnp.maximum(i * n8 - 1, 0), 1)),
                  pl.BlockSpec((SUBLANES, D_RNN), lambda i: (jnp.minimum((i + 1) * n8, last8), 1)),
                  const(conv_w), const(conv_b), const(wg), const(bg), const(lam)],
        out_specs=(row, row, row, row),
        scratch_shapes=[pltpu.VMEM((tm + 2 * SUBLANES, D_RNN), F32)],
        compiler_params=_cparams("parallel"),
        name="rnn_gates",
    )(bx, bx, bx, conv_w, conv_b, wg, bg, lam)


def _rnn_scan_kernel(fblk_ref, bblk_ref, seq_ref, first_ref,
                     af_ref, uf_ref, ab_ref, ub_ref, h0_ref, hf_ref, hb_ref, fin_ref, carry_ref, *, tc):
    s = pl.program_id(0)

    @pl.when(first_ref[s] == 1)
    def _():
        carry_ref[...] = h0_ref[...]

    def step(t, carry):
        h_f, h_b = carry
        tb = tc - 1 - t
        h_f = af_ref[pl.ds(t, 1), :] * h_f + uf_ref[pl.ds(t, 1), :]
        h_b = ab_ref[pl.ds(tb, 1), :] * h_b + ub_ref[pl.ds(tb, 1), :]
        hf_ref[pl.ds(t, 1), :] = h_f
        hb_ref[pl.ds(tb, 1), :] = h_b
        return h_f, h_b

    h_f, h_b = lax.fori_loop(0, tc, step, (carry_ref[0:1, :], carry_ref[1:2, :]), unroll=8)
    carry_ref[0:1, :] = h_f
    carry_ref[1:2, :] = h_b
    fin_ref[...] = carry_ref[...]


def _rnn_scan(a_f, u_f, a_b, u_b, h0, *, seq_lens):
    m = a_f.shape[0]
    tc = SCAN_CHUNK
    fblk, bblk, seq, first = [], [], [], []
    blk0 = 0
    for si, t in enumerate(seq_lens):
        nch = t // tc
        for c in range(nch):
            fblk.append(blk0 + c)
            bblk.append(blk0 + nch - 1 - c)
            seq.append(si)
            first.append(1 if c == 0 else 0)
        blk0 += nch
    n_steps = len(fblk)
    tables = [jnp.asarray(np.array(v, np.int32)) for v in (fblk, bblk, seq, first)]
    f_spec = pl.BlockSpec((tc, D_RNN), lambda s, fb, bb, sq, fr: (fb[s], 0))
    b_spec = pl.BlockSpec((tc, D_RNN), lambda s, fb, bb, sq, fr: (bb[s], 0))
    st_spec = pl.BlockSpec((None, 2, D_RNN), lambda s, fb, bb, sq, fr: (sq[s], 0, 0))
    out = jax.ShapeDtypeStruct((m, D_RNN), F32)
    return pl.pallas_call(
        functools.partial(_rnn_scan_kernel, tc=tc),
        out_shape=(out, out, jax.ShapeDtypeStruct(h0.shape, F32)),
        grid_spec=pltpu.PrefetchScalarGridSpec(
            num_scalar_prefetch=4, grid=(n_steps,),
            in_specs=[f_spec, f_spec, b_spec, b_spec, st_spec],
            out_specs=(f_spec, b_spec, st_spec),
            scratch_shapes=[pltpu.VMEM((2, D_RNN), F32)]),
        compiler_params=_cparams("arbitrary"),
        name="rnn_scan",
    )(*tables, a_f, u_f, a_b, u_b, h0)


def _ffn_kernel(x_ref, sc_ref, sh_ref, gate_ref, wg_ref, wu_ref, wd_ref, lng_ref, lnb_ref,
                o_ref, h_ref, acc_ref):
    j = pl.program_id(1)

    @pl.when(j == 0)
    def _():
        h_ref[...] = (x_ref[...] * (1.0 + sc_ref[...]) + sh_ref[...]).astype(BF16)
        acc_ref[...] = jnp.zeros_like(acc_ref)

    h = h_ref[...]
    g = jnp.dot(h, wg_ref[...], preferred_element_type=F32)
    u = jnp.dot(h, wu_ref[...], preferred_element_type=F32)
    act = ((g * jax.nn.sigmoid(g)) * u).astype(BF16)
    acc_ref[...] += jnp.dot(act, wd_ref[...], preferred_element_type=F32)

    @pl.when(j == pl.num_programs(1) - 1)
    def _():
        z = DEEPNORM_ALPHA * x_ref[...] + gate_ref[...] * acc_ref[...]
        o_ref[...] = _layer_norm_rows(z, lng_ref[...], lnb_ref[...])


def _ffn(x, mod, w_gu, w_down, lng, lnb, *, n_ctx, t_s):
    m = x.shape[0]
    tm = 1024
    tf = 256
    nf = D_FF // tf
    const = lambda arr: pl.BlockSpec(arr.shape, lambda i, j: (0,) * arr.ndim)
    row = pl.BlockSpec((tm, D_MODEL), lambda i, j: (i, 0))
    return pl.pallas_call(
        _ffn_kernel,
        out_shape=jax.ShapeDtypeStruct((m, D_MODEL), F32),
        grid=(m // tm, nf),
        in_specs=[row, _mod_spec(4, tm, n_ctx, t_s), _mod_spec(3, tm, n_ctx, t_s), _mod_spec(5, tm, n_ctx, t_s),
                  pl.BlockSpec((D_MODEL, tf), lambda i, j: (0, j)),
                  pl.BlockSpec((D_MODEL, tf), lambda i, j: (0, nf + j)),
                  pl.BlockSpec((tf, D_MODEL), lambda i, j: (j, 0)),
                  const(lng), const(lnb)],
        out_specs=row,
        scratch_shapes=[pltpu.VMEM((tm, D_MODEL), BF16), pltpu.VMEM((tm, D_MODEL), F32)],
        compiler_params=_cparams("parallel", "arbitrary"),
        name="ffn",
    )(x, mod, mod, mod, w_gu, w_gu, w_down, lng, lnb)


ROW_TILES = D_MODEL // LANES


def _to_token_rows(ref, base, value):
    n = value.shape[0]
    for j in range(ROW_TILES):
        ref[pl.ds(base + j, n, stride=ROW_TILES), :] = value[:, j * LANES:(j + 1) * LANES]


def _from_token_rows(ref, base, n):
    return jnp.concatenate([ref[pl.ds(base + j, n, stride=ROW_TILES), :] for j in range(ROW_TILES)], axis=1)


def _token_row(ref, r):
    start = r * ROW_TILES if isinstance(r, int) else pl.multiple_of(r * ROW_TILES, ROW_TILES)
    return ref.at[pl.ds(start, ROW_TILES)]


def _route_kernel(x_ref, sc_ref, sh_ref, wr_ref, h_ref, route_ref, gates_ref, cnt_ref, carry_ref, *, tm):
    @pl.when(pl.program_id(0) == 0)
    def _():
        carry_ref[...] = jnp.zeros_like(carry_ref)

    lane = lax.broadcasted_iota(jnp.int32, (tm, LANES), 1)
    h = x_ref[...] * (1.0 + sc_ref[...]) + sh_ref[...]
    _to_token_rows(h_ref, 0, h)
    logits = jnp.dot(h.astype(BF16), wr_ref[...], preferred_element_type=F32)
    s1 = jnp.where(lane < N_EXPERTS, logits, -jnp.inf)
    m1 = jnp.max(s1, axis=-1, keepdims=True)
    i1 = jnp.min(jnp.where(s1 == m1, lane, LANES), axis=-1, keepdims=True)
    s2 = jnp.where(lane == i1, -jnp.inf, s1)
    m2 = jnp.max(s2, axis=-1, keepdims=True)
    i2 = jnp.min(jnp.where(s2 == m2, lane, LANES), axis=-1, keepdims=True)
    e2 = jnp.exp(m2 - m1)
    den = 1.0 + e2
    hit = jnp.where((lane == i1) | (lane == i2), 1.0, 0.0)
    earlier = lax.broadcasted_iota(jnp.int32, (tm, tm), 1) < lax.broadcasted_iota(jnp.int32, (tm, tm), 0)
    before = carry_ref[...] + jnp.dot(jnp.where(earlier, 1.0, 0.0).astype(BF16), hit.astype(BF16),
                                      preferred_element_type=F32)
    r1 = jnp.sum(jnp.where(lane == i1, before, 0.0), axis=-1, keepdims=True).astype(jnp.int32)
    r2 = jnp.sum(jnp.where(lane == i2, before, 0.0), axis=-1, keepdims=True).astype(jnp.int32)
    carry_ref[...] += jnp.sum(hit, axis=0, keepdims=True)
    route = jnp.where(lane == 0, i1, jnp.where(lane == 1, i2, jnp.where(lane == 2, r1, jnp.where(lane == 3, r2, 0))))
    route_ref[...] = route[:, :ROUTE_COLS]
    gates_ref[...] = jnp.where(lane == 0, 1.0 / den, jnp.where(lane == 1, e2 / den, 0.0))[:, :ROUTE_COLS]
    cnt_ref[...] = jnp.broadcast_to(carry_ref[...], cnt_ref.shape)


def _moe_route(x, mod, w_router, *, n_ctx, t_s):
    m = x.shape[0]
    tm = 512
    small = pl.BlockSpec((tm, ROUTE_COLS), lambda i: (i, 0))
    return pl.pallas_call(
        functools.partial(_route_kernel, tm=tm),
        out_shape=(jax.ShapeDtypeStruct((m * ROW_TILES, LANES), F32), jax.ShapeDtypeStruct((m, ROUTE_COLS), jnp.int32),
                   jax.ShapeDtypeStruct((m, ROUTE_COLS), F32), jax.ShapeDtypeStruct((SUBLANES, LANES), F32)),
        grid=(m // tm,),
        in_specs=[pl.BlockSpec((tm, D_MODEL), lambda i: (i, 0)), _mod_spec(4, tm, n_ctx, t_s),
                  _mod_spec(3, tm, n_ctx, t_s), pl.BlockSpec(w_router.shape, lambda i: (0, 0))],
        out_specs=(pl.BlockSpec((tm * ROW_TILES, LANES), lambda i: (i, 0)), small, small,
                   pl.BlockSpec((SUBLANES, LANES), lambda i: (0, 0))),
        scratch_shapes=[pltpu.VMEM((1, LANES), F32)],
        compiler_params=_cparams("arbitrary"),
        name="moe_route",
    )(x, mod, mod, w_router)


def _dispatch_kernel(slot_ref, h_hbm, xs_hbm, sem, *, tok):
    i = pl.program_id(0)
    par = i % 2

    def issue(t, c):
        src = _token_row(h_hbm, i * tok + t)
        pltpu.make_async_copy(src, _token_row(xs_hbm, slot_ref[2 * t]), sem.at[par]).start()
        pltpu.make_async_copy(src, _token_row(xs_hbm, slot_ref[2 * t + 1]), sem.at[par]).start()
        return c
    lax.fori_loop(0, tok, issue, 0, unroll=4)

    def drain(which):
        def body(t, c):
            pltpu.make_async_copy(_token_row(h_hbm, 0), _token_row(xs_hbm, 0), sem.at[which]).wait()
            return c
        lax.fori_loop(0, 2 * tok, body, 0, unroll=8)

    @pl.when(i > 0)
    def _():
        drain(1 - par)

    @pl.when(i == pl.num_programs(0) - 1)
    def _():
        drain(par)


def _moe_dispatch(slots, h):
    m = h.shape[0] // ROW_TILES
    tok = 512
    return pl.pallas_call(
        functools.partial(_dispatch_kernel, tok=tok),
        out_shape=jax.ShapeDtypeStruct((2 * m * ROW_TILES, LANES), F32),
        grid=(m // tok,),
        in_specs=[pl.BlockSpec((2 * tok,), lambda i: (i,), memory_space=pltpu.SMEM),
                  pl.BlockSpec(memory_space=pl.ANY)],
        out_specs=pl.BlockSpec(memory_space=pl.ANY),
        scratch_shapes=[pltpu.SemaphoreType.DMA((2,))],
        compiler_params=_cparams("arbitrary"),
        name="moe_dispatch",
    )(slots, h)


def _experts_kernel(wt_ref, we_ref, lo_ref, hi_ref, nw_ref, xs_ref, wg_ref, wu_ref, wd_ref, o_ref, *, rows):
    del we_ref
    w = pl.program_id(0)

    @pl.when(w < nw_ref[0])
    def _():
        xs = _from_token_rows(xs_ref, 0, rows).astype(BF16)
        g = jnp.dot(xs, wg_ref[...], preferred_element_type=F32)
        u = jnp.dot(xs, wu_ref[...], preferred_element_type=F32)
        act = ((g * jax.nn.sigmoid(g)) * u).astype(BF16)
        y = jnp.dot(act, wd_ref[...], preferred_element_type=F32)
        row = lax.broadcasted_iota(jnp.int32, (rows, D_MODEL), 0)
        mine = (row >= lo_ref[w]) & (row < hi_ref[w])
        first = jnp.logical_or(w == 0, wt_ref[jnp.maximum(w - 1, 0)] != wt_ref[w])

        @pl.when(first)
        def _():
            _to_token_rows(o_ref, 0, jnp.where(mine, y, 0.0))

        @pl.when(jnp.logical_not(first))
        def _():
            _to_token_rows(o_ref, 0, jnp.where(mine, y, _from_token_rows(o_ref, 0, rows)))


def _moe_experts(items, xs, w_gu, w_down):
    rows = MOE_TILE
    n_items = items[0].shape[0]
    tile = lambda w, wt, we, lo, hi, nw: (wt[w], 0)
    return pl.pallas_call(
        functools.partial(_experts_kernel, rows=rows),
        out_shape=jax.ShapeDtypeStruct(xs.shape, F32),
        grid_spec=pltpu.PrefetchScalarGridSpec(
            num_scalar_prefetch=5, grid=(n_items,),
            in_specs=[pl.BlockSpec((rows * ROW_TILES, LANES), tile),
                      pl.BlockSpec((None, D_MODEL, D_FF_EXPERT), lambda w, wt, we, lo, hi, nw: (we[w], 0, 0)),
                      pl.BlockSpec((None, D_MODEL, D_FF_EXPERT), lambda w, wt, we, lo, hi, nw: (we[w], 0, 1)),
                      pl.BlockSpec((None, D_FF_EXPERT, D_MODEL), lambda w, wt, we, lo, hi, nw: (we[w], 0, 0))],
            out_specs=pl.BlockSpec((rows * ROW_TILES, LANES), tile)),
        compiler_params=_cparams("arbitrary"),
        name="moe_experts",
    )(*items, xs, w_gu, w_gu, w_down)


def _combine_kernel(idx_ref, nxt_ref, ys_hbm, gates_ref, x_ref, gate_ref, lng_ref, lnb_ref, o_ref, buf, sem, *, tm):
    i = pl.program_id(0)
    slot = i % 2

    def issue(ref, s):
        def body(t, c):
            pltpu.make_async_copy(_token_row(ys_hbm, ref[2 * t]), _token_row(buf.at[s], t), sem.at[s]).start()
            pltpu.make_async_copy(_token_row(ys_hbm, ref[2 * t + 1]), _token_row(buf.at[s], tm + t), sem.at[s]).start()
            return c
        lax.fori_loop(0, tm, body, 0, unroll=4)

    @pl.when(i == 0)
    def _():
        issue(idx_ref, 0)

    @pl.when(i + 1 < pl.num_programs(0))
    def _():
        issue(nxt_ref, 1 - slot)

    def drain(r, c):
        pltpu.make_async_copy(_token_row(ys_hbm, 0), _token_row(buf.at[slot], 0), sem.at[slot]).wait()
        return c
    lax.fori_loop(0, 2 * tm, drain, 0, unroll=8)
    cur = buf.at[slot]
    y = (gates_ref[:, 0:1] * _from_token_rows(cur, 0, tm)
         + gates_ref[:, 1:2] * _from_token_rows(cur, tm * ROW_TILES, tm))
    z = DEEPNORM_ALPHA * x_ref[...] + gate_ref[...] * y
    o_ref[...] = _layer_norm_rows(z, lng_ref[...], lnb_ref[...])


def _moe_combine(slots, ys, gates, x, mod, lng, lnb, *, n_ctx, t_s):
    m = x.shape[0]
    tm = 256
    n_blocks = m // tm
    idx = lambda f: pl.BlockSpec((2 * tm,), f, memory_space=pltpu.SMEM)
    const = lambda arr: pl.BlockSpec(arr.shape, lambda i: (0,) * arr.ndim)
    row = pl.BlockSpec((tm, D_MODEL), lambda i: (i, 0))
    return pl.pallas_call(
        functools.partial(_combine_kernel, tm=tm),
        out_shape=jax.ShapeDtypeStruct((m, D_MODEL), F32),
        grid=(n_blocks,),
        in_specs=[idx(lambda i: (i,)), idx(lambda i: (jnp.minimum(i + 1, n_blocks - 1),)),
                  pl.BlockSpec(memory_space=pl.ANY), pl.BlockSpec((tm, ROUTE_COLS), lambda i: (i, 0)),
                  row, _mod_spec(5, tm, n_ctx, t_s), const(lng), const(lnb)],
        out_specs=row,
        scratch_shapes=[pltpu.VMEM((2, 2 * tm * ROW_TILES, LANES), F32), pltpu.SemaphoreType.DMA((2,))],
        compiler_params=_cparams("arbitrary"),
        name="moe_combine",
    )(slots, slots, ys, gates, x, mod, lng, lnb)


def _moe_work_items(counts, n_rows):
    n_tiles = n_rows // MOE_TILE
    n_items = n_tiles + N_EXPERTS - 1
    g_end = jnp.cumsum(counts)
    g_start = g_end - counts
    row0 = jnp.arange(n_tiles, dtype=jnp.int32) * MOE_TILE
    first_e = jnp.sum(row0[:, None] >= g_end[None, :], axis=1)
    last_e = jnp.sum((row0 + MOE_TILE - 1)[:, None] >= g_end[None, :], axis=1)
    per_tile = last_e - first_e + 1
    item0 = jnp.cumsum(per_tile) - per_tile
    w = jnp.arange(n_items, dtype=jnp.int32)
    tile = jnp.clip(jnp.sum(w[:, None] >= item0[None, :], axis=1) - 1, 0, n_tiles - 1)
    expert = jnp.clip(first_e[tile] + w - item0[tile], 0, N_EXPERTS - 1)
    lo = jnp.clip(g_start[expert] - row0[tile], 0, MOE_TILE)
    hi = jnp.clip(g_end[expert] - row0[tile], 0, MOE_TILE)
    n_used = jnp.sum(per_tile).reshape(1)
    i32 = lambda a: a.astype(jnp.int32)
    return i32(tile), i32(expert), i32(lo), i32(hi), i32(n_used), g_start


def _moe(x, mod, w_router, w_gu, w_down, lng, lnb, *, n_ctx, t_s):
    m = x.shape[0]
    assert (2 * m) % MOE_TILE == 0
    h, route, gates, cnt = _moe_route(x, mod, w_router, n_ctx=n_ctx, t_s=t_s)
    counts = cnt[0, :N_EXPERTS].astype(jnp.int32)
    *items, g_start = _moe_work_items(counts, 2 * m)
    slots = (g_start[route[:, 0:2]] + route[:, 2:4]).reshape(2 * m).astype(jnp.int32)
    xs = _moe_dispatch(slots, h)
    ys = _moe_experts(items, xs, w_gu, w_down)
    return _moe_combine(slots, ys, gates, x, mod, lng, lnb, n_ctx=n_ctx, t_s=t_s)


def _rope_tables(n_tokens):
    n_rows = n_tokens // GRID_W
    row = jnp.repeat(jnp.arange(n_rows, dtype=F32), GRID_W)
    col = jnp.tile(jnp.arange(GRID_W, dtype=F32), n_rows)
    n_axis = HEAD_DIM // 4
    inv = ROPE_THETA ** (-jnp.arange(n_axis, dtype=F32) / n_axis)
    ang = jnp.concatenate([row[:, None] * inv, col[:, None] * inv], axis=-1)
    cos, sin = jnp.cos(ang), jnp.sin(ang)
    reps = LANES // HEAD_DIM
    cos_t = jnp.tile(jnp.concatenate([cos, cos], axis=-1), (1, reps))
    sin_t = jnp.tile(jnp.concatenate([-sin, sin], axis=-1), (1, reps))
    return cos_t, sin_t


def _gqa_head_perm():
    cols = []
    per_pair = 2 * (N_Q_HEADS // N_KV_HEADS)
    for p in range(N_KV_HEADS // 2):
        for j in range(N_Q_HEADS // N_KV_HEADS):
            for hd in (per_pair * p + j, per_pair * p + per_pair // 2 + j):
                cols.append(np.arange(hd * HEAD_DIM, (hd + 1) * HEAD_DIM))
    return np.concatenate(cols)


def _block_diag_windows(w):
    eye = jnp.eye(N_RNN_BLOCKS, dtype=bool)[:, None, :, None]
    dense = jnp.where(eye, w[:, :, None, :], 0.0).reshape(D_RNN, D_RNN)
    for ct, k0 in enumerate(GATE_K0):
        lo, hi = ct * GATE_N, (ct + 1) * GATE_N
        assert k0 <= RNN_BLOCK * (lo // RNN_BLOCK) and RNN_BLOCK * -(-hi // RNN_BLOCK) <= k0 + GATE_K <= D_RNN
    return jnp.stack([dense[k0:k0 + GATE_K, ct * GATE_N:(ct + 1) * GATE_N] for ct, k0 in enumerate(GATE_K0)])


def kernel(x_prompt, x_sample, cache_attn_k, cache_attn_v, state_rglru, cache_diff_k, cache_diff_v, c, c_ctx, w_ada, b_ada, ln_g, ln_b, attn_w_qkv, attn_g_q, attn_g_k, attn_w_o, rnn_w_in, rnn_conv_w, rnn_conv_b, rnn_w_a, rnn_b_a, rnn_w_x, rnn_b_x, rnn_lambda, rnn_w_out, diff_w_qkv, diff_lambda, diff_g_sub, diff_w_o, ffn_w_gu, ffn_w_down, moe_w_router, moe_w_gu, moe_w_down):
    b_c, t_c, d = x_prompt.shape
    b_s, t_s, _ = x_sample.shape
    n_ctx = b_c * t_c
    n_s = b_s * t_s
    past = cache_attn_k.shape[2]
    assert d == D_MODEL and n_ctx % t_s == 0 and t_c == SCAN_CHUNK and t_s % SCAN_CHUNK == 0
    assert 1 + b_s <= COND_ROWS

    x = jnp.concatenate([x_prompt.reshape(n_ctx, d), x_sample.reshape(n_s, d)], axis=0)
    cond = jnp.zeros((COND_ROWS, d), F32).at[0].set(c_ctx).at[1:1 + b_s].set(c)
    mods = _ada_table(cond, w_ada, b_ada)[:, :1 + b_s].reshape(DEPTH, 1 + b_s, 1, 6 * d)

    cos_t, sin_t = _rope_tables(t_s)
    perm = _gqa_head_perm()
    nq = N_Q_HEADS * HEAD_DIM
    tile2 = lambda g: jnp.tile(g, LANES // g.shape[-1]).reshape(1, LANES)
    kw = dict(n_ctx=n_ctx, t_s=t_s)

    attn_k, attn_v, rnn_s, diff_k, diff_v = [], [], [], [], []
    for li in range(DEPTH):
        mod = mods[li]
        lng = ln_g[li].reshape(2, 1, d)
        lnb = ln_b[li].reshape(2, 1, d)
        j = li // N_MIXERS
        kind = li % N_MIXERS
        if kind == 0:
            w = attn_w_qkv[j]
            w = jnp.concatenate([w[:, :nq][:, perm], w[:, nq:]], axis=1).astype(BF16)
            q, kb, vb, kf, vf = _gqa_qkv(x, mod, w, tile2(attn_g_q[j]), tile2(attn_g_k[j]), cos_t, sin_t, **kw)
            ck = cache_attn_k[:, j].reshape(b_s, past, N_KV_HEADS * HEAD_DIM).astype(BF16)
            cv = cache_attn_v[:, j].reshape(b_s, past, N_KV_HEADS * HEAD_DIM).astype(BF16)
            o_c = _gqa_attention(q, kb, vb, None, None, row0=0, n_b=b_c, t=t_c, tq=t_c)
            o_s = _gqa_attention(q, kb, vb, ck, cv, row0=n_ctx, n_b=b_s, t=t_s, tq=256)
            x = _oproj(o_c, o_s, attn_w_o[j][perm, :].astype(BF16), x, mod, lng[0], lnb[0], **kw)
            attn_k.append(kf[:n_ctx])
            attn_v.append(vf[:n_ctx])
        elif kind == 1:
            bx = _rnn_in(x, mod, rnn_w_in[j].astype(BF16), **kw)
            wg = jnp.stack([_block_diag_windows(rnn_w_a[j, 0]), _block_diag_windows(rnn_w_x[j, 0]),
                            _block_diag_windows(rnn_w_a[j, 1]), _block_diag_windows(rnn_w_x[j, 1])]).astype(BF16)
            bg = jnp.concatenate([rnn_b_a[j, 0], rnn_b_x[j, 0], rnn_b_a[j, 1], rnn_b_x[j, 1]]).reshape(1, -1)
            a_f, u_f, a_b, u_b = _rnn_gates(bx, rnn_conv_w[j], rnn_conv_b[j].reshape(1, -1), wg, bg,
                                            rnn_lambda[j], n_ctx=n_ctx, t_c=t_c, t_s=t_s)
            h0 = jnp.concatenate([jnp.zeros((b_c, 2, D_RNN), F32), state_rglru[:, j]], axis=0)
            hs_f, hs_b, fin = _rnn_scan(a_f, u_f, a_b, u_b, h0, seq_lens=[t_c] * b_c + [t_s] * b_s)
            x = _rnn_oproj(hs_f, hs_b, bx, rnn_w_out[j].astype(BF16), x, mod, lng[0], lnb[0], **kw)
            rnn_s.append(fin[:b_c])
        else:
            lam_init = 0.8 - 0.6 * math.exp(-0.3 * li)
            q, kb, vb, kf, vf = _diff_qkv(x, mod, diff_w_qkv[j].astype(BF16), cos_t, sin_t, **kw)
            ck = cache_diff_k[:, j].reshape(b_s, past, D_MODEL).astype(BF16)
            cv = cache_diff_v[:, j].reshape(b_s, past, D_MODEL).astype(BF16)
            g_sub = diff_g_sub[j].reshape(1, LANES)
            o_c = _diff_attention(diff_lambda[j], g_sub, q, kb, vb, None, None,
                                  row0=0, n_b=b_c, t=t_c, tq=t_c, lam_init=lam_init)
            o_s = _diff_attention(diff_lambda[j], g_sub, q, kb, vb, ck, cv,
                                  row0=n_ctx, n_b=b_s, t=t_s, tq=256, lam_init=lam_init)
            x = _oproj(o_c, o_s, diff_w_o[j].astype(BF16), x, mod, lng[0], lnb[0], **kw)
            diff_k.append(kf[:n_ctx])
            diff_v.append(vf[:n_ctx])
        if li % 2 == 0:
            x = _ffn(x, mod, ffn_w_gu[li // 2].astype(BF16), ffn_w_down[li // 2].astype(BF16), lng[1], lnb[1], **kw)
        else:
            w_r = jnp.zeros((d, LANES), F32).at[:, :N_EXPERTS].set(moe_w_router[li // 2]).astype(BF16)
            x = _moe(x, mod, w_r, moe_w_gu[li // 2].astype(BF16), moe_w_down[li // 2].astype(BF16),
                     lng[1], lnb[1], **kw)

    y_prompt = x[:n_ctx].reshape(b_c, t_c, d)
    y_sample = x[n_ctx:].reshape(b_s, t_s, d)
    new_attn_k = jnp.stack([k.reshape(b_c, t_c, N_KV_HEADS, HEAD_DIM) for k in attn_k], axis=1)
    new_attn_v = jnp.stack([v.reshape(b_c, t_c, N_KV_HEADS, HEAD_DIM) for v in attn_v], axis=1)
    new_state = jnp.stack(rnn_s, axis=1)
    new_diff_k = jnp.stack([k.reshape(b_c, t_c, N_DIFF_HEADS, 2, HEAD_DIM) for k in diff_k], axis=1)
    new_diff_v = jnp.stack([v.reshape(b_c, t_c, N_DIFF_HEADS, 2 * HEAD_DIM) for v in diff_v], axis=1)
    return (y_prompt, y_sample, new_attn_k, new_attn_v, new_state, new_diff_k, new_diff_v)
```

```python
import functools
import math

import jax
import jax.numpy as jnp
import numpy as np
from jax import lax
from jax.experimental import pallas as pl
from jax.experimental.pallas import tpu as pltpu

F32 = jnp.float32
BF16 = jnp.bfloat16

D_MODEL = 1024
DEPTH = 4
GRID_W = 64
HEAD_DIM = 64
N_Q_HEADS = 16
N_KV_HEADS = 4
ROPE_THETA = 10000.0
N_DIFF_HEADS = 8
D_RNN = 1280
N_RNN_BLOCKS = 16
RNN_BLOCK = 80
CONV_W = 4
CONV_LEFT = 2
RGLRU_C = 8.0
D_FF = 2816
N_EXPERTS = 8
D_FF_EXPERT = 1408
N_MIXERS = 3
DEEPNORM_ALPHA = (2.0 * DEPTH) ** 0.25
LN_EPS = 1e-6
RMS_EPS = 1e-6

LANES = 128
SUBLANES = 8
VMEM_LIMIT = 56 * 1024 * 1024
COND_ROWS = 8
SCAN_CHUNK = 256
MOE_TILE = 512
ROUTE_COLS = 8
GATE_N = 256
GATE_K = 512
GATE_K0 = (0, 128, 384, 640, 768)
Q_SCALE = HEAD_DIM ** -0.5 * math.log2(math.e)


def _cparams(*sem):
    return pltpu.CompilerParams(dimension_semantics=sem, vmem_limit_bytes=VMEM_LIMIT)


def _layer_norm_rows(z, g, b):
    mu = jnp.mean(z, axis=-1, keepdims=True)
    zc = z - mu
    var = jnp.mean(zc * zc, axis=-1, keepdims=True)
    return zc * lax.rsqrt(var + LN_EPS) * g + b


def _seg_of_block(i, tm, n_ctx, t_s):
    r0 = i * tm
    return jnp.where(r0 < n_ctx, 0, 1 + (r0 - n_ctx) // t_s)


def _mod_spec(col, tm, n_ctx, t_s):
    return pl.BlockSpec((None, 1, D_MODEL), lambda i, *_: (_seg_of_block(i, tm, n_ctx, t_s), 0, col))


def _ada_kernel(c_ref, w_ref, b_ref, o_ref):
    c = c_ref[...]
    a = (c * jax.nn.sigmoid(c)).astype(BF16)
    o_ref[...] = jnp.dot(a, w_ref[...].astype(BF16), preferred_element_type=F32) + b_ref[...]


def _ada_table(cond, w_ada, b_ada):
    n_l, d, n = w_ada.shape
    tn = 1536
    return pl.pallas_call(
        _ada_kernel,
        out_shape=jax.ShapeDtypeStruct((n_l, COND_ROWS, n), F32),
        grid=(n_l, n // tn),
        in_specs=[pl.BlockSpec((COND_ROWS, d), lambda l, j: (0, 0)),
                  pl.BlockSpec((None, d, tn), lambda l, j: (l, 0, j)),
                  pl.BlockSpec((None, 1, tn), lambda l, j: (l, 0, j))],
        out_specs=pl.BlockSpec((None, COND_ROWS, tn), lambda l, j: (l, 0, j)),
        compiler_params=_cparams("parallel", "parallel"),
        name="ada_table",
    )(cond, w_ada, b_ada.reshape(n_l, 1, n))


def _head_masks(tm):
    lane = lax.broadcasted_iota(jnp.int32, (tm, LANES), 1)
    return lane < HEAD_DIM, (lane & (HEAD_DIM - 1)) < HEAD_DIM // 2


def _rope_tile(t, first_half, cos, sin):
    partner = jnp.where(first_half, pltpu.roll(t, LANES - HEAD_DIM // 2, 1),
                        pltpu.roll(t, HEAD_DIM // 2, 1))
    return t * cos + partner * sin


def _gqa_qkv_kernel(x_ref, sc_ref, sh_ref, w_ref, gq_ref, gk_ref, cos_ref, sin_ref,
                    q_ref, kb_ref, vb_ref, kf_ref, vf_ref, *, tm, n_ctx):
    i = pl.program_id(0)
    h = (x_ref[...] * (1.0 + sc_ref[...]) + sh_ref[...]).astype(BF16)
    y = jnp.dot(h, w_ref[...], preferred_element_type=F32)
    use_rope = i * tm >= n_ctx
    cos = jnp.where(use_rope, cos_ref[...], 1.0)
    sin = jnp.where(use_rope, sin_ref[...], 0.0)
    low_head, first_half = _head_masks(tm)

    def norm_rope(t, g):
        t2 = t * t
        s_lo = jnp.sum(jnp.where(low_head, t2, 0.0), axis=-1, keepdims=True)
        s_hi = jnp.sum(jnp.where(low_head, 0.0, t2), axis=-1, keepdims=True)
        inv = jnp.where(low_head, lax.rsqrt(s_lo * (1.0 / HEAD_DIM) + RMS_EPS),
                        lax.rsqrt(s_hi * (1.0 / HEAD_DIM) + RMS_EPS))
        return _rope_tile(t * inv * g, first_half, cos, sin)

    nq = N_Q_HEADS * HEAD_DIM
    nkv = N_KV_HEADS * HEAD_DIM
    gq = gq_ref[...]
    gk = gk_ref[...]
    for t in range(nq // LANES):
        sl = slice(t * LANES, (t + 1) * LANES)
        q_ref[:, sl] = (norm_rope(y[:, sl], gq) * Q_SCALE).astype(BF16)
    for t in range(nkv // LANES):
        sl = slice(t * LANES, (t + 1) * LANES)
        k = norm_rope(y[:, nq + t * LANES: nq + (t + 1) * LANES], gk)
        kf_ref[:, sl] = k
        kb_ref[:, sl] = k.astype(BF16)
    v = y[:, nq + nkv:]
    vf_ref[...] = v
    vb_ref[...] = v.astype(BF16)


def _gqa_qkv(x, mod, w, gq, gk, cos, sin, *, n_ctx, t_s):
    m = x.shape[0]
    tm = 512
    nq = N_Q_HEADS * HEAD_DIM
    nkv = N_KV_HEADS * HEAD_DIM
    n_rope_blocks = t_s // tm
    rope_spec = pl.BlockSpec(
        (tm, LANES), lambda i: (jnp.where(i * tm >= n_ctx, ((i * tm - n_ctx) % t_s) // tm, 0) % n_rope_blocks, 0))
    row = lambda n: pl.BlockSpec((tm, n), lambda i: (i, 0))
    const = lambda a: pl.BlockSpec(a.shape, lambda i: (0,) * a.ndim)
    return pl.pallas_call(
        functools.partial(_gqa_qkv_kernel, tm=tm, n_ctx=n_ctx),
        out_shape=(jax.ShapeDtypeStruct((m, nq), BF16), jax.ShapeDtypeStruct((m, nkv), BF16),
                   jax.ShapeDtypeStruct((m, nkv), BF16), jax.ShapeDtypeStruct((m, nkv), F32),
                   jax.ShapeDtypeStruct((m, nkv), F32)),
        grid=(m // tm,),
        in_specs=[row(D_MODEL), _mod_spec(1, tm, n_ctx, t_s), _mod_spec(0, tm, n_ctx, t_s),
                  const(w), const(gq), const(gk), rope_spec, rope_spec],
        out_specs=(row(nq), row(nkv), row(nkv), row(nkv), row(nkv)),
        compiler_params=_cparams("parallel"),
        name="gqa_qkv",
    )(x, mod, mod, w, gq, gk, cos, sin)


def _diff_qkv_kernel(x_ref, sc_ref, sh_ref, w_ref, cos_ref, sin_ref,
                     q_ref, kb_ref, vb_ref, kf_ref, vf_ref, *, tm, n_ctx):
    i = pl.program_id(0)
    h = (x_ref[...] * (1.0 + sc_ref[...]) + sh_ref[...]).astype(BF16)
    y = jnp.dot(h, w_ref[...], preferred_element_type=F32)
    use_rope = i * tm >= n_ctx
    cos = jnp.where(use_rope, cos_ref[...], 1.0)
    sin = jnp.where(use_rope, sin_ref[...], 0.0)
    _, first_half = _head_masks(tm)
    for t in range(D_MODEL // LANES):
        sl = slice(t * LANES, (t + 1) * LANES)
        q = _rope_tile(y[:, sl], first_half, cos, sin)
        q_ref[:, sl] = (q * Q_SCALE).astype(BF16)
        k = _rope_tile(y[:, D_MODEL + t * LANES: D_MODEL + (t + 1) * LANES], first_half, cos, sin)
        kf_ref[:, sl] = k
        kb_ref[:, sl] = k.astype(BF16)
    v = y[:, 2 * D_MODEL:]
    vf_ref[...] = v
    vb_ref[...] = v.astype(BF16)


def _diff_qkv(x, mod, w, cos, sin, *, n_ctx, t_s):
    m = x.shape[0]
    tm = 256
    n_rope_blocks = t_s // tm
    rope_spec = pl.BlockSpec(
        (tm, LANES), lambda i: (jnp.where(i * tm >= n_ctx, ((i * tm - n_ctx) % t_s) // tm, 0) % n_rope_blocks, 0))
    row = pl.BlockSpec((tm, D_MODEL), lambda i: (i, 0))
    return pl.pallas_call(
        functools.partial(_diff_qkv_kernel, tm=tm, n_ctx=n_ctx),
        out_shape=(jax.ShapeDtypeStruct((m, D_MODEL), BF16), jax.ShapeDtypeStruct((m, D_MODEL), BF16),
                   jax.ShapeDtypeStruct((m, D_MODEL), BF16), jax.ShapeDtypeStruct((m, D_MODEL), F32),
                   jax.ShapeDtypeStruct((m, D_MODEL), F32)),
        grid=(m // tm,),
        in_specs=[row, _mod_spec(1, tm, n_ctx, t_s), _mod_spec(0, tm, n_ctx, t_s),
                  pl.BlockSpec(w.shape, lambda i: (0, 0)), rope_spec, rope_spec],
        out_specs=(row, row, row, row, row),
        compiler_params=_cparams("parallel"),
        name="diff_qkv",
    )(x, mod, mod, w, cos, sin)


_NT = (((1,), (1,)), ((), ()))


def _with_ones_column(v, low_head, lane, keep_low):
    ones_lane = HEAD_DIM if keep_low else 0
    other = jnp.where(lane == ones_lane, 1.0, 0.0).astype(v.dtype)
    return jnp.where(low_head == keep_low, v, other)


def _gqa_attn_kernel(*refs, has_cache, tq):
    if has_cache:
        q_ref, k_ref, v_ref, ck_ref, cv_ref, o_ref = refs
    else:
        q_ref, k_ref, v_ref, o_ref = refs
    low_q = lax.broadcasted_iota(jnp.int32, (tq, LANES), 1) < HEAD_DIM
    k = k_ref[...]
    lane_v = lax.broadcasted_iota(jnp.int32, v_ref.shape, 1)
    v_aug = [_with_ones_column(v_ref[...], lane_v < HEAD_DIM, lane_v, keep) for keep in (True, False)]
    if has_cache:
        ck = ck_ref[...]
        lane_c = lax.broadcasted_iota(jnp.int32, cv_ref.shape, 1)
        cv_aug = [_with_ones_column(cv_ref[...], lane_c < HEAD_DIM, lane_c, keep) for keep in (True, False)]
    n_tiles = q_ref.shape[1] // LANES
    heads = [(j, keep) for j in range(n_tiles) for keep in (True, False)]

    def scores(j, keep_low):
        qt = q_ref[:, j * LANES:(j + 1) * LANES]
        qh = jnp.where(low_q == keep_low, qt, jnp.zeros_like(qt))
        s = lax.dot_general(qh, k, _NT, preferred_element_type=F32)
        s_c = lax.dot_general(qh, ck, _NT, preferred_element_type=F32) if has_cache else None
        return s, s_c

    nxt = scores(*heads[0])
    halves = []
    for idx, (j, keep_low) in enumerate(heads):
        s, s_c = nxt
        if idx + 1 < len(heads):
            nxt = scores(*heads[idx + 1])
        mx = jnp.max(s, axis=-1, keepdims=True)
        if has_cache:
            mx = jnp.maximum(mx, jnp.max(s_c, axis=-1, keepdims=True))
        side = 0 if keep_low else 1
        acc = jnp.dot(jnp.exp2(s - mx).astype(BF16), v_aug[side], preferred_element_type=F32)
        if has_cache:
            acc = acc + jnp.dot(jnp.exp2(s_c - mx).astype(BF16), cv_aug[side], preferred_element_type=F32)
        den = acc[:, HEAD_DIM:HEAD_DIM + 1] if keep_low else acc[:, 0:1]
        halves.append(acc / den)
        if not keep_low:
            o_ref[:, j * LANES:(j + 1) * LANES] = jnp.where(low_q, halves[0], halves[1]).astype(BF16)
            halves = []


def _gqa_attention(q, k, v, cache_k, cache_v, *, row0, n_b, t, tq):
    n_pairs = N_KV_HEADS // 2
    qw = D_MODEL // n_pairs
    nq_blocks = t // tq
    has_cache = cache_k is not None
    q_spec = pl.BlockSpec((tq, qw), lambda b, p, iq: (row0 // tq + b * nq_blocks + iq, p))
    kv_spec = pl.BlockSpec((t, LANES), lambda b, p, iq: (row0 // t + b, p))
    in_specs = [q_spec, kv_spec, kv_spec]
    args = [q, k, v]
    if has_cache:
        c_spec = pl.BlockSpec((None, cache_k.shape[1], LANES), lambda b, p, iq: (b, 0, p))
        in_specs += [c_spec, c_spec]
        args += [cache_k, cache_v]
    return pl.pallas_call(
        functools.partial(_gqa_attn_kernel, has_cache=has_cache, tq=tq),
        out_shape=jax.ShapeDtypeStruct((n_b * t, D_MODEL), BF16),
        grid=(n_b, n_pairs, nq_blocks),
        in_specs=in_specs,
        out_specs=pl.BlockSpec((tq, qw), lambda b, p, iq: (b * nq_blocks + iq, p)),
        compiler_params=_cparams("parallel", "parallel", "arbitrary"),
        name="gqa_attn_cache" if has_cache else "gqa_attn",
    )(*args)


def _diff_attn_kernel(*refs, has_cache, tq, lam_init):
    if has_cache:
        lam_ref, g_ref, q_ref, k_ref, v_ref, ck_ref, cv_ref, o_ref = refs
    else:
        lam_ref, g_ref, q_ref, k_ref, v_ref, o_ref = refs
    lv = lam_ref[...]
    lam = (jnp.exp(jnp.sum(lv[0:1] * lv[1:2], axis=-1, keepdims=True))
           - jnp.exp(jnp.sum(lv[2:3] * lv[3:4], axis=-1, keepdims=True)) + lam_init)
    low_head = lax.broadcasted_iota(jnp.int32, (tq, LANES), 1) < HEAD_DIM
    qt = q_ref[...]
    k = k_ref[...]
    probs = []
    for keep_low in (True, False):
        qh = jnp.where(low_head == keep_low, qt, jnp.zeros_like(qt))
        s = lax.dot_general(qh, k, _NT, preferred_element_type=F32)
        mx = jnp.max(s, axis=-1, keepdims=True)
        if has_cache:
            s_c = lax.dot_general(qh, ck_ref[...], _NT, preferred_element_type=F32)
            mx = jnp.maximum(mx, jnp.max(s_c, axis=-1, keepdims=True))
        e = jnp.exp2(s - mx)
        den = jnp.sum(e, axis=-1, keepdims=True)
        if has_cache:
            e_c = jnp.exp2(s_c - mx)
            den = den + jnp.sum(e_c, axis=-1, keepdims=True)
            inv = 1.0 / den
            probs.append((e * inv, e_c * inv))
        else:
            probs.append((e * (1.0 / den),))
    pd = (probs[0][0] - lam * probs[1][0]).astype(BF16)
    o = jnp.dot(pd, v_ref[...], preferred_element_type=F32)
    if has_cache:
        pd_c = (probs[0][1] - lam * probs[1][1]).astype(BF16)
        o = o + jnp.dot(pd_c, cv_ref[...], preferred_element_type=F32)
    inv = lax.rsqrt(jnp.mean(o * o, axis=-1, keepdims=True) + RMS_EPS)
    o_ref[...] = ((o * inv * g_ref[...]) * (1.0 - lam_init)).astype(BF16)


def _diff_attention(lam_vec, g_sub, q, k, v, cache_k, cache_v, *, row0, n_b, t, tq, lam_init):
    nq_blocks = t // tq
    has_cache = cache_k is not None
    q_spec = pl.BlockSpec((tq, LANES), lambda b, h, iq: (row0 // tq + b * nq_blocks + iq, h))
    kv_spec = pl.BlockSpec((t, LANES), lambda b, h, iq: (row0 // t + b, h))
    const = lambda a: pl.BlockSpec(a.shape, lambda b, h, iq: (0,) * a.ndim)
    in_specs = [const(lam_vec), const(g_sub), q_spec, kv_spec, kv_spec]
    args = [lam_vec, g_sub, q, k, v]
    if has_cache:
        c_spec = pl.BlockSpec((None, cache_k.shape[1], LANES), lambda b, h, iq: (b, 0, h))
        in_specs += [c_spec, c_spec]
        args += [cache_k, cache_v]
    return pl.pallas_call(
        functools.partial(_diff_attn_kernel, has_cache=has_cache, tq=tq, lam_init=lam_init),
        out_shape=jax.ShapeDtypeStruct((n_b * t, D_MODEL), BF16),
        grid=(n_b, N_DIFF_HEADS, nq_blocks),
        in_specs=in_specs,
        out_specs=pl.BlockSpec((tq, LANES), lambda b, h, iq: (b * nq_blocks + iq, h)),
        compiler_params=_cparams("parallel", "parallel", "arbitrary"),
        name="diff_attn_cache" if has_cache else "diff_attn",
    )(*args)


def _oproj_kernel(ac_ref, as_ref, w_ref, x_ref, gate_ref, lng_ref, lnb_ref, o_ref, *, n_ctx_blocks):
    is_ctx = pl.program_id(0) < n_ctx_blocks
    a = jnp.where(is_ctx, ac_ref[...], as_ref[...])
    out = jnp.dot(a, w_ref[...], preferred_element_type=F32)
    z = DEEPNORM_ALPHA * x_ref[...] + gate_ref[...] * out
    o_ref[...] = _layer_norm_rows(z, lng_ref[...], lnb_ref[...])


def _oproj(a_ctx, a_smp, w, x, mod, lng, lnb, *, n_ctx, t_s):
    m = x.shape[0]
    kdim = a_ctx.shape[1]
    tm = 512
    ncb = n_ctx // tm
    const = lambda arr: pl.BlockSpec(arr.shape, lambda i: (0,) * arr.ndim)
    row = pl.BlockSpec((tm, D_MODEL), lambda i: (i, 0))
    return pl.pallas_call(
        functools.partial(_oproj_kernel, n_ctx_blocks=ncb),
        out_shape=jax.ShapeDtypeStruct((m, D_MODEL), F32),
        grid=(m // tm,),
        in_specs=[pl.BlockSpec((tm, kdim), lambda i: (jnp.minimum(i, ncb - 1), 0)),
                  pl.BlockSpec((tm, kdim), lambda i: (jnp.maximum(i - ncb, 0), 0)),
                  const(w), row, _mod_spec(2, tm, n_ctx, t_s), const(lng), const(lnb)],
        out_specs=row,
        compiler_params=_cparams("parallel"),
        name="oproj_ln",
    )(a_ctx, a_smp, w, x, mod, lng, lnb)


def _gelu_tanh(x):
    return 0.5 * x * (1.0 + jnp.tanh(math.sqrt(2.0 / math.pi) * (x + 0.044715 * (x * x * x))))


def _rnn_oproj_kernel(hf_ref, hb_ref, br_ref, w_ref, x_ref, gate_ref, lng_ref, lnb_ref, o_ref):
    y = (hf_ref[...] + hb_ref[...]) * _gelu_tanh(br_ref[...])
    out = jnp.dot(y.astype(BF16), w_ref[...], preferred_element_type=F32)
    z = DEEPNORM_ALPHA * x_ref[...] + gate_ref[...] * out
    o_ref[...] = _layer_norm_rows(z, lng_ref[...], lnb_ref[...])


def _rnn_oproj(hs_f, hs_b, bx, w, x, mod, lng, lnb, *, n_ctx, t_s):
    m = x.shape[0]
    tm = 512
    const = lambda arr: pl.BlockSpec(arr.shape, lambda i: (0,) * arr.ndim)
    row = pl.BlockSpec((tm, D_MODEL), lambda i: (i, 0))
    rnn_row = pl.BlockSpec((tm, D_RNN), lambda i: (i, 0))
    return pl.pallas_call(
        _rnn_oproj_kernel,
        out_shape=jax.ShapeDtypeStruct((m, D_MODEL), F32),
        grid=(m // tm,),
        in_specs=[rnn_row, rnn_row, rnn_row, const(w), row,
                  _mod_spec(2, tm, n_ctx, t_s), const(lng), const(lnb)],
        out_specs=row,
        compiler_params=_cparams("parallel"),
        name="rnn_oproj_ln",
    )(hs_f, hs_b, bx, w, x, mod, lng, lnb)


def _mod_mm_kernel(x_ref, sc_ref, sh_ref, w_ref, o_ref):
    h = (x_ref[...] * (1.0 + sc_ref[...]) + sh_ref[...]).astype(BF16)
    o_ref[...] = jnp.dot(h, w_ref[...], preferred_element_type=F32)


def _rnn_in(x, mod, w, *, n_ctx, t_s):
    m = x.shape[0]
    n = w.shape[1]
    tm = 512
    return pl.pallas_call(
        _mod_mm_kernel,
        out_shape=jax.ShapeDtypeStruct((m, n), F32),
        grid=(m // tm,),
        in_specs=[pl.BlockSpec((tm, D_MODEL), lambda i: (i, 0)),
                  _mod_spec(1, tm, n_ctx, t_s), _mod_spec(0, tm, n_ctx, t_s),
                  pl.BlockSpec(w.shape, lambda i: (0, 0))],
        out_specs=pl.BlockSpec((tm, n), lambda i: (i, 0)),
        compiler_params=_cparams("parallel"),
        name="rnn_in",
    )(x, mod, mod, w)


def _rnn_gate_kernel(xb_ref, prev_ref, next_ref, cw_ref, cb_ref, wg_ref, bg_ref, lam_ref,
                     af_ref, uf_ref, ab_ref, ub_ref, ext_ref, *, tm, n_ctx, t_c, t_s):
    i = pl.program_id(0)
    r0 = i * tm
    pos = jnp.where(r0 < n_ctx, r0 % t_c, (r0 - n_ctx) % t_s)
    t_seq = jnp.where(r0 < n_ctx, t_c, t_s)
    at_start = pos == 0
    at_end = pos + tm == t_seq
    ext_ref[0:SUBLANES, :] = jnp.where(at_start, 0.0, prev_ref[...])
    ext_ref[SUBLANES:SUBLANES + tm, :] = xb_ref[...]
    ext_ref[SUBLANES + tm:, :] = jnp.where(at_end, 0.0, next_ref[...])
    xc = 0.0
    for j in range(CONV_W):
        off = SUBLANES - CONV_LEFT + j
        xc = xc + ext_ref[off:off + tm, :] * cw_ref[j:j + 1, :]
    xc = xc + cb_ref[...]
    xcb = xc.astype(BF16)
    neg_lam = -lam_ref[...]
    sp = jnp.maximum(neg_lam, 0.0) + jnp.log1p(jnp.exp(-jnp.abs(neg_lam)))

    def gate(g, ct, cols):
        k0 = GATE_K0[ct]
        pre = jnp.dot(xcb[:, k0:k0 + GATE_K], wg_ref[g, ct], preferred_element_type=F32)
        return jax.nn.sigmoid(pre + bg_ref[:, g * D_RNN + cols.start:g * D_RNN + cols.stop])

    for z, (a_ref, u_ref) in enumerate(((af_ref, uf_ref), (ab_ref, ub_ref))):
        for ct in range(D_RNN // GATE_N):
            cols = slice(ct * GATE_N, (ct + 1) * GATE_N)
            r = gate(2 * z, ct, cols)
            g_in = gate(2 * z + 1, ct, cols)
            log_a = -RGLRU_C * r * sp[z:z + 1, cols]
            a = jnp.exp(log_a)
            a_ref[:, cols] = a
            u_ref[:, cols] = jnp.sqrt(-jnp.tanh(log_a) * (a * a + 1.0)) * (g_in * xc[:, cols])


def _rnn_gates(bx, conv_w, conv_b, wg, bg, lam, *, n_ctx, t_c, t_s):
    m = bx.shape[0]
    tm = 256
    n8 = tm // SUBLANES
    last8 = m // SUBLANES - 1
    const = lambda arr: pl.BlockSpec(arr.shape, lambda i: (0,) * arr.ndim)
    row = pl.BlockSpec((tm, D_RNN), lambda i: (i, 0))
    out = jax.ShapeDtypeStruct((m, D_RNN), F32)
    return pl.pallas_call(
        functools.partial(_rnn_gate_kernel, tm=tm, n_ctx=n_ctx, t_c=t_c, t_s=t_s),
        out_shape=(out, out, out, out),
        grid=(m // tm,),
        in_specs=[pl.BlockSpec((tm, D_RNN), lambda i: (i, 1)),
                  pl.BlockSpec((SUBLANES, D_RNN), lambda i: (jnp.maximum(i * n8 - 1, 0), 1)),
                  pl.BlockSpec((SUBLANES, D_RNN), lambda i: (jnp.minimum((i + 1) * n8, last8), 1)),
                  const(conv_w), const(conv_b), const(wg), const(bg), const(lam)],
        out_specs=(row, row, row, row),
        scratch_shapes=[pltpu.VMEM((tm + 2 * SUBLANES, D_RNN), F32)],
        compiler_params=_cparams("parallel"),
        name="rnn_gates",
    )(bx, bx, bx, conv_w, conv_b, wg, bg, lam)


def _rnn_scan_kernel(fblk_ref, bblk_ref, seq_ref, first_ref,
                     af_ref, uf_ref, ab_ref, ub_ref, h0_ref, hf_ref, hb_ref, fin_ref, carry_ref, *, tc):
    s = pl.program_id(0)

    @pl.when(first_ref[s] == 1)
    def _():
        carry_ref[...] = h0_ref[...]

    def step(t, carry):
        h_f, h_b = carry
        tb = tc - 1 - t
        h_f = af_ref[pl.ds(t, 1), :] * h_f + uf_ref[pl.ds(t, 1), :]
        h_b = ab_ref[pl.ds(tb, 1), :] * h_b + ub_ref[pl.ds(tb, 1), :]
        hf_ref[pl.ds(t, 1), :] = h_f
        hb_ref[pl.ds(tb, 1), :] = h_b
        return h_f, h_b

    h_f, h_b = lax.fori_loop(0, tc, step, (carry_ref[0:1, :], carry_ref[1:2, :]), unroll=8)
    carry_ref[0:1, :] = h_f
    carry_ref[1:2, :] = h_b
    fin_ref[...] = carry_ref[...]


def _rnn_scan(a_f, u_f, a_b, u_b, h0, *, seq_lens):
    m = a_f.shape[0]
    tc = SCAN_CHUNK
    fblk, bblk, seq, first = [], [], [], []
    blk0 = 0
    for si, t in enumerate(seq_lens):
        nch = t // tc
        for c in range(nch):
            fblk.append(blk0 + c)
            bblk.append(blk0 + nch - 1 - c)
            seq.append(si)
            first.append(1 if c == 0 else 0)
        blk0 += nch
    n_steps = len(fblk)
    tables = [jnp.asarray(np.array(v, np.int32)) for v in (fblk, bblk, seq, first)]
    f_spec = pl.BlockSpec((tc, D_RNN), lambda s, fb, bb, sq, fr: (fb[s], 0))
    b_spec = pl.BlockSpec((tc, D_RNN), lambda s, fb, bb, sq, fr: (bb[s], 0))
    st_spec = pl.BlockSpec((None, 2, D_RNN), lambda s, fb, bb, sq, fr: (sq[s], 0, 0))
    out = jax.ShapeDtypeStruct((m, D_RNN), F32)
    return pl.pallas_call(
        functools.partial(_rnn_scan_kernel, tc=tc),
        out_shape=(out, out, jax.ShapeDtypeStruct(h0.shape, F32)),
        grid_spec=pltpu.PrefetchScalarGridSpec(
            num_scalar_prefetch=4, grid=(n_steps,),
            in_specs=[f_spec, f_spec, b_spec, b_spec, st_spec],
            out_specs=(f_spec, b_spec, st_spec),
            scratch_shapes=[pltpu.VMEM((2, D_RNN), F32)]),
        compiler_params=_cparams("arbitrary"),
        name="rnn_scan",
    )(*tables, a_f, u_f, a_b, u_b, h0)


def _ffn_kernel(x_ref, sc_ref, sh_ref, gate_ref, wg_ref, wu_ref, wd_ref, lng_ref, lnb_ref,
                o_ref, h_ref, acc_ref):
    j = pl.program_id(1)

    @pl.when(j == 0)
    def _():
        h_ref[...] = (x_ref[...] * (1.0 + sc_ref[...]) + sh_ref[...]).astype(BF16)
        acc_ref[...] = jnp.zeros_like(acc_ref)

    h = h_ref[...]
    g = jnp.dot(h, wg_ref[...], preferred_element_type=F32)
    u = jnp.dot(h, wu_ref[...], preferred_element_type=F32)
    act = ((g * jax.nn.sigmoid(g)) * u).astype(BF16)
    acc_ref[...] += jnp.dot(act, wd_ref[...], preferred_element_type=F32)

    @pl.when(j == pl.num_programs(1) - 1)
    def _():
        z = DEEPNORM_ALPHA * x_ref[...] + gate_ref[...] * acc_ref[...]
        o_ref[...] = _layer_norm_rows(z, lng_ref[...], lnb_ref[...])


def _ffn(x, mod, w_gu, w_down, lng, lnb, *, n_ctx, t_s):
    m = x.shape[0]
    tm = 1024
    tf = 256
    nf = D_FF // tf
    const = lambda arr: pl.BlockSpec(arr.shape, lambda i, j: (0,) * arr.ndim)
    row = pl.BlockSpec((tm, D_MODEL), lambda i, j: (i, 0))
    return pl.pallas_call(
        _ffn_kernel,
        out_shape=jax.ShapeDtypeStruct((m, D_MODEL), F32),
        grid=(m // tm, nf),
        in_specs=[row, _mod_spec(4, tm, n_ctx, t_s), _mod_spec(3, tm, n_ctx, t_s), _mod_spec(5, tm, n_ctx, t_s),
                  pl.BlockSpec((D_MODEL, tf), lambda i, j: (0, j)),
                  pl.BlockSpec((D_MODEL, tf), lambda i, j: (0, nf + j)),
                  pl.BlockSpec((tf, D_MODEL), lambda i, j: (j, 0)),
                  const(lng), const(lnb)],
        out_specs=row,
        scratch_shapes=[pltpu.VMEM((tm, D_MODEL), BF16), pltpu.VMEM((tm, D_MODEL), F32)],
        compiler_params=_cparams("parallel", "arbitrary"),
        name="ffn",
    )(x, mod, mod, mod, w_gu, w_gu, w_down, lng, lnb)


ROW_TILES = D_MODEL // LANES


def _to_token_rows(ref, base, value):
    n = value.shape[0]
    for j in range(ROW_TILES):
        ref[pl.ds(base + j, n, stride=ROW_TILES), :] = value[:, j * LANES:(j + 1) * LANES]


def _from_token_rows(ref, base, n):
    return jnp.concatenate([ref[pl.ds(base + j, n, stride=ROW_TILES), :] for j in range(ROW_TILES)], axis=1)


def _token_row(ref, r):
    start = r * ROW_TILES if isinstance(r, int) else pl.multiple_of(r * ROW_TILES, ROW_TILES)
    return ref.at[pl.ds(start, ROW_TILES)]


def _route_kernel(x_ref, sc_ref, sh_ref, wr_ref, route_ref, gates_ref, cnt_ref, carry_ref, *, tm):
    @pl.when(pl.program_id(0) == 0)
    def _():
        carry_ref[...] = jnp.zeros_like(carry_ref)

    lane = lax.broadcasted_iota(jnp.int32, (tm, LANES), 1)
    h = x_ref[...] * (1.0 + sc_ref[...]) + sh_ref[...]
    logits = jnp.dot(h.astype(BF16), wr_ref[...], preferred_element_type=F32)
    s1 = jnp.where(lane < N_EXPERTS, logits, -jnp.inf)
    m1 = jnp.max(s1, axis=-1, keepdims=True)
    i1 = jnp.min(jnp.where(s1 == m1, lane, LANES), axis=-1, keepdims=True)
    s2 = jnp.where(lane == i1, -jnp.inf, s1)
    m2 = jnp.max(s2, axis=-1, keepdims=True)
    i2 = jnp.min(jnp.where(s2 == m2, lane, LANES), axis=-1, keepdims=True)
    e2 = jnp.exp(m2 - m1)
    den = 1.0 + e2
    hit = jnp.where((lane == i1) | (lane == i2), 1.0, 0.0)
    earlier = lax.broadcasted_iota(jnp.int32, (tm, tm), 1) < lax.broadcasted_iota(jnp.int32, (tm, tm), 0)
    before = carry_ref[...] + jnp.dot(jnp.where(earlier, 1.0, 0.0).astype(BF16), hit.astype(BF16),
                                      preferred_element_type=F32)
    r1 = jnp.sum(jnp.where(lane == i1, before, 0.0), axis=-1, keepdims=True).astype(jnp.int32)
    r2 = jnp.sum(jnp.where(lane == i2, before, 0.0), axis=-1, keepdims=True).astype(jnp.int32)
    carry_ref[...] += jnp.sum(hit, axis=0, keepdims=True)
    route = jnp.where(lane == 0, i1, jnp.where(lane == 1, i2, jnp.where(lane == 2, r1, jnp.where(lane == 3, r2, 0))))
    route_ref[...] = route[:, :ROUTE_COLS]
    gates_ref[...] = jnp.where(lane == 0, 1.0 / den, jnp.where(lane == 1, e2 / den, 0.0))[:, :ROUTE_COLS]
    cnt_ref[...] = jnp.broadcast_to(carry_ref[...], cnt_ref.shape)


def _moe_route(x, mod, w_router, *, n_ctx, t_s):
    m = x.shape[0]
    tm = 512
    small = pl.BlockSpec((tm, ROUTE_COLS), lambda i: (i, 0))
    return pl.pallas_call(
        functools.partial(_route_kernel, tm=tm),
        out_shape=(jax.ShapeDtypeStruct((m, ROUTE_COLS), jnp.int32), jax.ShapeDtypeStruct((m, ROUTE_COLS), F32),
                   jax.ShapeDtypeStruct((SUBLANES, LANES), F32)),
        grid=(m // tm,),
        in_specs=[pl.BlockSpec((tm, D_MODEL), lambda i: (i, 0)), _mod_spec(4, tm, n_ctx, t_s),
                  _mod_spec(3, tm, n_ctx, t_s), pl.BlockSpec(w_router.shape, lambda i: (0, 0))],
        out_specs=(small, small, pl.BlockSpec((SUBLANES, LANES), lambda i: (0, 0))),
        scratch_shapes=[pltpu.VMEM((1, LANES), F32)],
        compiler_params=_cparams("arbitrary"),
        name="moe_route",
    )(x, mod, mod, w_router)


def _dispatch_kernel(slot_ref, x_ref, sc_ref, sh_ref, xs_hbm, buf, sem, *, tok):
    i = pl.program_id(0)
    par = i % 2
    cur = buf.at[par]
    _to_token_rows(cur, 0, x_ref[...] * (1.0 + sc_ref[...]) + sh_ref[...])

    def issue(t, c):
        src = _token_row(cur, t)
        pltpu.make_async_copy(src, _token_row(xs_hbm, slot_ref[2 * t]), sem.at[par]).start()
        pltpu.make_async_copy(src, _token_row(xs_hbm, slot_ref[2 * t + 1]), sem.at[par]).start()
        return c
    lax.fori_loop(0, tok, issue, 0, unroll=4)

    def drain(which):
        def body(t, c):
            pltpu.make_async_copy(_token_row(buf.at[which], 0), _token_row(xs_hbm, 0), sem.at[which]).wait()
            return c
        lax.fori_loop(0, 2 * tok, body, 0, unroll=8)

    @pl.when(i > 0)
    def _():
        drain(1 - par)

    @pl.when(i == pl.num_programs(0) - 1)
    def _():
        drain(par)


def _moe_dispatch(slots, x, mod, *, n_ctx, t_s):
    m = x.shape[0]
    tok = 512
    return pl.pallas_call(
        functools.partial(_dispatch_kernel, tok=tok),
        out_shape=jax.ShapeDtypeStruct((2 * m * ROW_TILES, LANES), F32),
        grid=(m // tok,),
        in_specs=[pl.BlockSpec((2 * tok,), lambda i: (i,), memory_space=pltpu.SMEM),
                  pl.BlockSpec((tok, D_MODEL), lambda i: (i, 0)),
                  _mod_spec(4, tok, n_ctx, t_s), _mod_spec(3, tok, n_ctx, t_s)],
        out_specs=pl.BlockSpec(memory_space=pl.ANY),
        scratch_shapes=[pltpu.VMEM((2, tok * ROW_TILES, LANES), F32), pltpu.SemaphoreType.DMA((2,))],
        compiler_params=_cparams("arbitrary"),
        name="moe_dispatch",
    )(slots, x, mod, mod)


def _experts_kernel(wt_ref, we_ref, lo_ref, hi_ref, nw_ref, xs_ref, wg_ref, wu_ref, wd_ref, o_ref, *, rows):
    del we_ref
    w = pl.program_id(0)

    @pl.when(w < nw_ref[0])
    def _():
        xs = _from_token_rows(xs_ref, 0, rows).astype(BF16)
        g = jnp.dot(xs, wg_ref[...], preferred_element_type=F32)
        u = jnp.dot(xs, wu_ref[...], preferred_element_type=F32)
        act = ((g * jax.nn.sigmoid(g)) * u).astype(BF16)
        y = jnp.dot(act, wd_ref[...], preferred_element_type=F32)
        whole = jnp.logical_and(lo_ref[w] == 0, hi_ref[w] == rows)
        first = jnp.logical_or(w == 0, wt_ref[jnp.maximum(w - 1, 0)] != wt_ref[w])

        @pl.when(whole)
        def _():
            _to_token_rows(o_ref, 0, y)

        @pl.when(jnp.logical_not(whole))
        def _():
            row = lax.broadcasted_iota(jnp.int32, (rows, D_MODEL), 0)
            mine = (row >= lo_ref[w]) & (row < hi_ref[w])

            @pl.when(first)
            def _():
                _to_token_rows(o_ref, 0, jnp.where(mine, y, 0.0))

            @pl.when(jnp.logical_not(first))
            def _():
                _to_token_rows(o_ref, 0, jnp.where(mine, y, _from_token_rows(o_ref, 0, rows)))


def _moe_experts(items, xs, w_gu, w_down, layer):
    rows = MOE_TILE
    n_items = items[0].shape[0]
    tile = lambda w, wt, we, lo, hi, nw: (wt[w], 0)
    return pl.pallas_call(
        functools.partial(_experts_kernel, rows=rows),
        out_shape=jax.ShapeDtypeStruct(xs.shape, F32),
        grid_spec=pltpu.PrefetchScalarGridSpec(
            num_scalar_prefetch=5, grid=(n_items,),
            in_specs=[pl.BlockSpec((rows * ROW_TILES, LANES), tile),
                      pl.BlockSpec((None, None, D_MODEL, D_FF_EXPERT),
                                   lambda w, wt, we, lo, hi, nw: (layer, we[w], 0, 0)),
                      pl.BlockSpec((None, None, D_MODEL, D_FF_EXPERT),
                                   lambda w, wt, we, lo, hi, nw: (layer, we[w], 0, 1)),
                      pl.BlockSpec((None, None, D_FF_EXPERT, D_MODEL),
                                   lambda w, wt, we, lo, hi, nw: (layer, we[w], 0, 0))],
            out_specs=pl.BlockSpec((rows * ROW_TILES, LANES), tile)),
        compiler_params=_cparams("arbitrary"),
        name="moe_experts",
    )(*items, xs, w_gu, w_gu, w_down)


def _combine_kernel(idx_ref, nxt_ref, ys_hbm, gates_ref, x_ref, gate_ref, lng_ref, lnb_ref, o_ref, buf, sem, *, tm):
    i = pl.program_id(0)
    slot = i % 2

    def issue(ref, s):
        def body(t, c):
            pltpu.make_async_copy(_token_row(ys_hbm, ref[2 * t]), _token_row(buf.at[s], t), sem.at[s]).start()
            pltpu.make_async_copy(_token_row(ys_hbm, ref[2 * t + 1]), _token_row(buf.at[s], tm + t), sem.at[s]).start()
            return c
        lax.fori_loop(0, tm, body, 0, unroll=4)

    @pl.when(i == 0)
    def _():
        issue(idx_ref, 0)

    @pl.when(i + 1 < pl.num_programs(0))
    def _():
        issue(nxt_ref, 1 - slot)

    def drain(r, c):
        pltpu.make_async_copy(_token_row(ys_hbm, 0), _token_row(buf.at[slot], 0), sem.at[slot]).wait()
        return c
    lax.fori_loop(0, 2 * tm, drain, 0, unroll=8)
    cur = buf.at[slot]
    y = (gates_ref[:, 0:1] * _from_token_rows(cur, 0, tm)
         + gates_ref[:, 1:2] * _from_token_rows(cur, tm * ROW_TILES, tm))
    z = DEEPNORM_ALPHA * x_ref[...] + gate_ref[...] * y
    o_ref[...] = _layer_norm_rows(z, lng_ref[...], lnb_ref[...])


def _moe_combine(slots, ys, gates, x, mod, lng, lnb, *, n_ctx, t_s):
    m = x.shape[0]
    tm = 256
    n_blocks = m // tm
    idx = lambda f: pl.BlockSpec((2 * tm,), f, memory_space=pltpu.SMEM)
    const = lambda arr: pl.BlockSpec(arr.shape, lambda i: (0,) * arr.ndim)
    row = pl.BlockSpec((tm, D_MODEL), lambda i: (i, 0))
    return pl.pallas_call(
        functools.partial(_combine_kernel, tm=tm),
        out_shape=jax.ShapeDtypeStruct((m, D_MODEL), F32),
        grid=(n_blocks,),
        in_specs=[idx(lambda i: (i,)), idx(lambda i: (jnp.minimum(i + 1, n_blocks - 1),)),
                  pl.BlockSpec(memory_space=pl.ANY), pl.BlockSpec((tm, ROUTE_COLS), lambda i: (i, 0)),
                  row, _mod_spec(5, tm, n_ctx, t_s), const(lng), const(lnb)],
        out_specs=row,
        scratch_shapes=[pltpu.VMEM((2, 2 * tm * ROW_TILES, LANES), F32), pltpu.SemaphoreType.DMA((2,))],
        compiler_params=_cparams("arbitrary"),
        name="moe_combine",
    )(slots, slots, ys, gates, x, mod, lng, lnb)


def _moe_work_items(counts, n_rows):
    n_tiles = n_rows // MOE_TILE
    n_items = n_tiles + N_EXPERTS - 1
    g_end = jnp.cumsum(counts)
    g_start = g_end - counts
    row0 = jnp.arange(n_tiles, dtype=jnp.int32) * MOE_TILE
    first_e = jnp.sum(row0[:, None] >= g_end[None, :], axis=1)
    last_e = jnp.sum((row0 + MOE_TILE - 1)[:, None] >= g_end[None, :], axis=1)
    per_tile = last_e - first_e + 1
    item0 = jnp.cumsum(per_tile) - per_tile
    w = jnp.arange(n_items, dtype=jnp.int32)
    tile = jnp.clip(jnp.sum(w[:, None] >= item0[None, :], axis=1) - 1, 0, n_tiles - 1)
    expert = jnp.clip(first_e[tile] + w - item0[tile], 0, N_EXPERTS - 1)
    lo = jnp.clip(g_start[expert] - row0[tile], 0, MOE_TILE)
    hi = jnp.clip(g_end[expert] - row0[tile], 0, MOE_TILE)
    n_used = jnp.sum(per_tile).reshape(1)
    i32 = lambda a: a.astype(jnp.int32)
    return i32(tile), i32(expert), i32(lo), i32(hi), i32(n_used), g_start


def _moe(x, mod, w_router, w_gu, w_down, layer, lng, lnb, *, n_ctx, t_s):
    m = x.shape[0]
    assert (2 * m) % MOE_TILE == 0
    route, gates, cnt = _moe_route(x, mod, w_router, n_ctx=n_ctx, t_s=t_s)
    counts = cnt[0, :N_EXPERTS].astype(jnp.int32)
    *items, g_start = _moe_work_items(counts, 2 * m)
    slots = (g_start[route[:, 0:2]] + route[:, 2:4]).reshape(2 * m).astype(jnp.int32)
    xs = _moe_dispatch(slots, x, mod, n_ctx=n_ctx, t_s=t_s)
    ys = _moe_experts(items, xs, w_gu, w_down, layer)
    return _moe_combine(slots, ys, gates, x, mod, lng, lnb, n_ctx=n_ctx, t_s=t_s)


def _rope_tables(n_tokens):
    n_rows = n_tokens // GRID_W
    row = jnp.repeat(jnp.arange(n_rows, dtype=F32), GRID_W)
    col = jnp.tile(jnp.arange(GRID_W, dtype=F32), n_rows)
    n_axis = HEAD_DIM // 4
    inv = ROPE_THETA ** (-jnp.arange(n_axis, dtype=F32) / n_axis)
    ang = jnp.concatenate([row[:, None] * inv, col[:, None] * inv], axis=-1)
    cos, sin = jnp.cos(ang), jnp.sin(ang)
    reps = LANES // HEAD_DIM
    cos_t = jnp.tile(jnp.concatenate([cos, cos], axis=-1), (1, reps))
    sin_t = jnp.tile(jnp.concatenate([-sin, sin], axis=-1), (1, reps))
    return cos_t, sin_t


def _gqa_head_perm():
    cols = []
    per_pair = 2 * (N_Q_HEADS // N_KV_HEADS)
    for p in range(N_KV_HEADS // 2):
        for j in range(N_Q_HEADS // N_KV_HEADS):
            for hd in (per_pair * p + j, per_pair * p + per_pair // 2 + j):
                cols.append(np.arange(hd * HEAD_DIM, (hd + 1) * HEAD_DIM))
    return np.concatenate(cols)


def _block_diag_windows(w):
    eye = jnp.eye(N_RNN_BLOCKS, dtype=bool)[:, None, :, None]
    dense = jnp.where(eye, w[:, :, None, :], 0.0).reshape(D_RNN, D_RNN)
    for ct, k0 in enumerate(GATE_K0):
        lo, hi = ct * GATE_N, (ct + 1) * GATE_N
        assert k0 <= RNN_BLOCK * (lo // RNN_BLOCK) and RNN_BLOCK * -(-hi // RNN_BLOCK) <= k0 + GATE_K <= D_RNN
    return jnp.stack([dense[k0:k0 + GATE_K, ct * GATE_N:(ct + 1) * GATE_N] for ct, k0 in enumerate(GATE_K0)])


def kernel(x_prompt, x_sample, cache_attn_k, cache_attn_v, state_rglru, cache_diff_k, cache_diff_v, c, c_ctx, w_ada, b_ada, ln_g, ln_b, attn_w_qkv, attn_g_q, attn_g_k, attn_w_o, rnn_w_in, rnn_conv_w, rnn_conv_b, rnn_w_a, rnn_b_a, rnn_w_x, rnn_b_x, rnn_lambda, rnn_w_out, diff_w_qkv, diff_lambda, diff_g_sub, diff_w_o, ffn_w_gu, ffn_w_down, moe_w_router, moe_w_gu, moe_w_down):
    b_c, t_c, d = x_prompt.shape
    b_s, t_s, _ = x_sample.shape
    n_ctx = b_c * t_c
    n_s = b_s * t_s
    past = cache_attn_k.shape[2]
    assert d == D_MODEL and n_ctx % t_s == 0 and t_c == SCAN_CHUNK and t_s % SCAN_CHUNK == 0
    assert 1 + b_s <= COND_ROWS

    x = jnp.concatenate([x_prompt.reshape(n_ctx, d), x_sample.reshape(n_s, d)], axis=0)
    cond = jnp.zeros((COND_ROWS, d), F32).at[0].set(c_ctx).at[1:1 + b_s].set(c)
    mods = _ada_table(cond, w_ada, b_ada)[:, :1 + b_s].reshape(DEPTH, 1 + b_s, 1, 6 * d)

    cos_t, sin_t = _rope_tables(t_s)
    perm = _gqa_head_perm()
    nq = N_Q_HEADS * HEAD_DIM
    tile2 = lambda g: jnp.tile(g, LANES // g.shape[-1]).reshape(1, LANES)
    kw = dict(n_ctx=n_ctx, t_s=t_s)
    moe_gu_b = moe_w_gu.astype(BF16)
    moe_down_b = moe_w_down.astype(BF16)

    attn_k, attn_v, rnn_s, diff_k, diff_v = [], [], [], [], []
    for li in range(DEPTH):
        mod = mods[li]
        lng = ln_g[li].reshape(2, 1, d)
        lnb = ln_b[li].reshape(2, 1, d)
        j = li // N_MIXERS
        kind = li % N_MIXERS
        if kind == 0:
            w = attn_w_qkv[j]
            w = jnp.concatenate([w[:, :nq][:, perm], w[:, nq:]], axis=1).astype(BF16)
            q, kb, vb, kf, vf = _gqa_qkv(x, mod, w, tile2(attn_g_q[j]), tile2(attn_g_k[j]), cos_t, sin_t, **kw)
            ck = cache_attn_k[:, j].reshape(b_s, past, N_KV_HEADS * HEAD_DIM).astype(BF16)
            cv = cache_attn_v[:, j].reshape(b_s, past, N_KV_HEADS * HEAD_DIM).astype(BF16)
            o_c = _gqa_attention(q, kb, vb, None, None, row0=0, n_b=b_c, t=t_c, tq=t_c)
            o_s = _gqa_attention(q, kb, vb, ck, cv, row0=n_ctx, n_b=b_s, t=t_s, tq=256)
            x = _oproj(o_c, o_s, attn_w_o[j][perm, :].astype(BF16), x, mod, lng[0], lnb[0], **kw)
            attn_k.append(kf[:n_ctx])
            attn_v.append(vf[:n_ctx])
        elif kind == 1:
            bx = _rnn_in(x, mod, rnn_w_in[j].astype(BF16), **kw)
            wg = jnp.stack([_block_diag_windows(rnn_w_a[j, 0]), _block_diag_windows(rnn_w_x[j, 0]),
                            _block_diag_windows(rnn_w_a[j, 1]), _block_diag_windows(rnn_w_x[j, 1])]).astype(BF16)
            bg = jnp.concatenate([rnn_b_a[j, 0], rnn_b_x[j, 0], rnn_b_a[j, 1], rnn_b_x[j, 1]]).reshape(1, -1)
            a_f, u_f, a_b, u_b = _rnn_gates(bx, rnn_conv_w[j], rnn_conv_b[j].reshape(1, -1), wg, bg,
                                            rnn_lambda[j], n_ctx=n_ctx, t_c=t_c, t_s=t_s)
            h0 = jnp.concatenate([jnp.zeros((b_c, 2, D_RNN), F32), state_rglru[:, j]], axis=0)
            hs_f, hs_b, fin = _rnn_scan(a_f, u_f, a_b, u_b, h0, seq_lens=[t_c] * b_c + [t_s] * b_s)
            x = _rnn_oproj(hs_f, hs_b, bx, rnn_w_out[j].astype(BF16), x, mod, lng[0], lnb[0], **kw)
            rnn_s.append(fin[:b_c])
        else:
            lam_init = 0.8 - 0.6 * math.exp(-0.3 * li)
            q, kb, vb, kf, vf = _diff_qkv(x, mod, diff_w_qkv[j].astype(BF16), cos_t, sin_t, **kw)
            ck = cache_diff_k[:, j].reshape(b_s, past, D_MODEL).astype(BF16)
            cv = cache_diff_v[:, j].reshape(b_s, past, D_MODEL).astype(BF16)
            g_sub = diff_g_sub[j].reshape(1, LANES)
            o_c = _diff_attention(diff_lambda[j], g_sub, q, kb, vb, None, None,
                                  row0=0, n_b=b_c, t=t_c, tq=t_c, lam_init=lam_init)
            o_s = _diff_attention(diff_lambda[j], g_sub, q, kb, vb, ck, cv,
                                  row0=n_ctx, n_b=b_s, t=t_s, tq=256, lam_init=lam_init)
            x = _oproj(o_c, o_s, diff_w_o[j].astype(BF16), x, mod, lng[0], lnb[0], **kw)
            diff_k.append(kf[:n_ctx])
            diff_v.append(vf[:n_ctx])
        if li % 2 == 0:
            x = _ffn(x, mod, ffn_w_gu[li // 2].astype(BF16), ffn_w_down[li // 2].astype(BF16), lng[1], lnb[1], **kw)
        else:
            w_r = jnp.zeros((d, LANES), F32).at[:, :N_EXPERTS].set(moe_w_router[li // 2]).astype(BF16)
            x = _moe(x, mod, w_r, moe_gu_b, moe_down_b, li // 2, lng[1], lnb[1], **kw)

    y_prompt = x[:n_ctx].reshape(b_c, t_c, d)
    y_sample = x[n_ctx:].reshape(b_s, t_s, d)
    new_attn_k = jnp.stack([k.reshape(b_c, t_c, N_KV_HEADS, HEAD_DIM) for k in attn_k], axis=1)
    new_attn_v = jnp.stack([v.reshape(b_c, t_c, N_KV_HEADS, HEAD_DIM) for v in attn_v], axis=1)
    new_state = jnp.stack(rnn_s, axis=1)
    new_diff_k = jnp.stack([k.reshape(b_c, t_c, N_DIFF_HEADS, 2, HEAD_DIM) for k in diff_k], axis=1)
    new_diff_v = jnp.stack([v.reshape(b_c, t_c, N_DIFF_HEADS, 2 * HEAD_DIM) for v in diff_v], axis=1)
    return (y_prompt, y_sample, new_attn_k, new_attn_v, new_state, new_diff_k, new_diff_v)
```

```python
import functools
import math

import jax
import jax.numpy as jnp
import numpy as np
from jax import lax
from jax.experimental import pallas as pl
from jax.experimental.pallas import tpu as pltpu

F32 = jnp.float32
BF16 = jnp.bfloat16

D_MODEL = 1024
DEPTH = 4
GRID_W = 64
HEAD_DIM = 64
N_Q_HEADS = 16
N_KV_HEADS = 4
ROPE_THETA = 10000.0
N_DIFF_HEADS = 8
D_RNN = 1280
N_RNN_BLOCKS = 16
RNN_BLOCK = 80
CONV_W = 4
CONV_LEFT = 2
RGLRU_C = 8.0
D_FF = 2816
N_EXPERTS = 8
D_FF_EXPERT = 1408
N_MIXERS = 3
DEEPNORM_ALPHA = (2.0 * DEPTH) ** 0.25
LN_EPS = 1e-6
RMS_EPS = 1e-6

LANES = 128
SUBLANES = 8
VMEM_LIMIT = 56 * 1024 * 1024
COND_ROWS = 8
SCAN_CHUNK = 256
MOE_TILE = 512
ROUTE_COLS = 8
GATE_N = 256
GATE_K = 512
GATE_K0 = (0, 128, 384, 640, 768)
Q_SCALE = HEAD_DIM ** -0.5 * math.log2(math.e)


def _cparams(*sem):
    return pltpu.CompilerParams(dimension_semantics=sem, vmem_limit_bytes=VMEM_LIMIT)


def _layer_norm_rows(z, g, b):
    mu = jnp.mean(z, axis=-1, keepdims=True)
    zc = z - mu
    var = jnp.mean(zc * zc, axis=-1, keepdims=True)
    return zc * lax.rsqrt(var + LN_EPS) * g + b


def _seg_of_block(i, tm, n_ctx, t_s):
    r0 = i * tm
    return jnp.where(r0 < n_ctx, 0, 1 + (r0 - n_ctx) // t_s)


def _mod_spec(col, tm, n_ctx, t_s):
    return pl.BlockSpec((None, 1, D_MODEL), lambda i, *_: (_seg_of_block(i, tm, n_ctx, t_s), 0, col))


def _ada_kernel(c_ref, w_ref, b_ref, o_ref):
    c = c_ref[...]
    a = (c * jax.nn.sigmoid(c)).astype(BF16)
    o_ref[...] = jnp.dot(a, w_ref[...].astype(BF16), preferred_element_type=F32) + b_ref[...]


def _ada_table(cond, w_ada, b_ada):
    n_l, d, n = w_ada.shape
    tn = 1536
    return pl.pallas_call(
        _ada_kernel,
        out_shape=jax.ShapeDtypeStruct((n_l, COND_ROWS, n), F32),
        grid=(n_l, n // tn),
        in_specs=[pl.BlockSpec((COND_ROWS, d), lambda l, j: (0, 0)),
                  pl.BlockSpec((None, d, tn), lambda l, j: (l, 0, j)),
                  pl.BlockSpec((None, 1, tn), lambda l, j: (l, 0, j))],
        out_specs=pl.BlockSpec((None, COND_ROWS, tn), lambda l, j: (l, 0, j)),
        compiler_params=_cparams("parallel", "parallel"),
        name="ada_table",
    )(cond, w_ada, b_ada.reshape(n_l, 1, n))


def _head_masks(tm):
    lane = lax.broadcasted_iota(jnp.int32, (tm, LANES), 1)
    return lane < HEAD_DIM, (lane & (HEAD_DIM - 1)) < HEAD_DIM // 2


def _rope_tile(t, first_half, cos, sin):
    partner = jnp.where(first_half, pltpu.roll(t, LANES - HEAD_DIM // 2, 1),
                        pltpu.roll(t, HEAD_DIM // 2, 1))
    return t * cos + partner * sin


def _gqa_qkv_kernel(x_ref, sc_ref, sh_ref, w_ref, gq_ref, gk_ref, cos_ref, sin_ref,
                    q_ref, kb_ref, vb_ref, kf_ref, vf_ref, *, tm, n_ctx):
    i = pl.program_id(0)
    h = (x_ref[...] * (1.0 + sc_ref[...]) + sh_ref[...]).astype(BF16)
    y = jnp.dot(h, w_ref[...], preferred_element_type=F32)
    use_rope = i * tm >= n_ctx
    cos = jnp.where(use_rope, cos_ref[...], 1.0)
    sin = jnp.where(use_rope, sin_ref[...], 0.0)
    low_head, first_half = _head_masks(tm)

    def norm_rope(t, g):
        t2 = t * t
        s_lo = jnp.sum(jnp.where(low_head, t2, 0.0), axis=-1, keepdims=True)
        s_hi = jnp.sum(jnp.where(low_head, 0.0, t2), axis=-1, keepdims=True)
        inv = jnp.where(low_head, lax.rsqrt(s_lo * (1.0 / HEAD_DIM) + RMS_EPS),
                        lax.rsqrt(s_hi * (1.0 / HEAD_DIM) + RMS_EPS))
        return _rope_tile(t * inv * g, first_half, cos, sin)

    nq = N_Q_HEADS * HEAD_DIM
    nkv = N_KV_HEADS * HEAD_DIM
    gq = gq_ref[...]
    gk = gk_ref[...]
    for t in range(nq // LANES):
        sl = slice(t * LANES, (t + 1) * LANES)
        q_ref[:, sl] = (norm_rope(y[:, sl], gq) * Q_SCALE).astype(BF16)
    for t in range(nkv // LANES):
        sl = slice(t * LANES, (t + 1) * LANES)
        k = norm_rope(y[:, nq + t * LANES: nq + (t + 1) * LANES], gk)
        kf_ref[:, sl] = k
        kb_ref[:, sl] = k.astype(BF16)
    v = y[:, nq + nkv:]
    vf_ref[...] = v
    vb_ref[...] = v.astype(BF16)


def _gqa_qkv(x, mod, w, gq, gk, cos, sin, *, n_ctx, t_s):
    m = x.shape[0]
    tm = 512
    nq = N_Q_HEADS * HEAD_DIM
    nkv = N_KV_HEADS * HEAD_DIM
    n_rope_blocks = t_s // tm
    rope_spec = pl.BlockSpec(
        (tm, LANES), lambda i: (jnp.where(i * tm >= n_ctx, ((i * tm - n_ctx) % t_s) // tm, 0) % n_rope_blocks, 0))
    row = lambda n: pl.BlockSpec((tm, n), lambda i: (i, 0))
    const = lambda a: pl.BlockSpec(a.shape, lambda i: (0,) * a.ndim)
    return pl.pallas_call(
        functools.partial(_gqa_qkv_kernel, tm=tm, n_ctx=n_ctx),
        out_shape=(jax.ShapeDtypeStruct((m, nq), BF16), jax.ShapeDtypeStruct((m, nkv), BF16),
                   jax.ShapeDtypeStruct((m, nkv), BF16), jax.ShapeDtypeStruct((m, nkv), F32),
                   jax.ShapeDtypeStruct((m, nkv), F32)),
        grid=(m // tm,),
        in_specs=[row(D_MODEL), _mod_spec(1, tm, n_ctx, t_s), _mod_spec(0, tm, n_ctx, t_s),
                  const(w), const(gq), const(gk), rope_spec, rope_spec],
        out_specs=(row(nq), row(nkv), row(nkv), row(nkv), row(nkv)),
        compiler_params=_cparams("parallel"),
        name="gqa_qkv",
    )(x, mod, mod, w, gq, gk, cos, sin)


def _diff_qkv_kernel(x_ref, sc_ref, sh_ref, w_ref, cos_ref, sin_ref,
                     q_ref, kb_ref, vb_ref, kf_ref, vf_ref, *, tm, n_ctx):
    i = pl.program_id(0)
    h = (x_ref[...] * (1.0 + sc_ref[...]) + sh_ref[...]).astype(BF16)
    y = jnp.dot(h, w_ref[...], preferred_element_type=F32)
    use_rope = i * tm >= n_ctx
    cos = jnp.where(use_rope, cos_ref[...], 1.0)
    sin = jnp.where(use_rope, sin_ref[...], 0.0)
    _, first_half = _head_masks(tm)
    for t in range(D_MODEL // LANES):
        sl = slice(t * LANES, (t + 1) * LANES)
        q = _rope_tile(y[:, sl], first_half, cos, sin)
        q_ref[:, sl] = (q * Q_SCALE).astype(BF16)
        k = _rope_tile(y[:, D_MODEL + t * LANES: D_MODEL + (t + 1) * LANES], first_half, cos, sin)
        kf_ref[:, sl] = k
        kb_ref[:, sl] = k.astype(BF16)
    v = y[:, 2 * D_MODEL:]
    vf_ref[...] = v
    vb_ref[...] = v.astype(BF16)


def _diff_qkv(x, mod, w, cos, sin, *, n_ctx, t_s):
    m = x.shape[0]
    tm = 256
    n_rope_blocks = t_s // tm
    rope_spec = pl.BlockSpec(
        (tm, LANES), lambda i: (jnp.where(i * tm >= n_ctx, ((i * tm - n_ctx) % t_s) // tm, 0) % n_rope_blocks, 0))
    row = pl.BlockSpec((tm, D_MODEL), lambda i: (i, 0))
    return pl.pallas_call(
        functools.partial(_diff_qkv_kernel, tm=tm, n_ctx=n_ctx),
        out_shape=(jax.ShapeDtypeStruct((m, D_MODEL), BF16), jax.ShapeDtypeStruct((m, D_MODEL), BF16),
                   jax.ShapeDtypeStruct((m, D_MODEL), BF16), jax.ShapeDtypeStruct((m, D_MODEL), F32),
                   jax.ShapeDtypeStruct((m, D_MODEL), F32)),
        grid=(m // tm,),
        in_specs=[row, _mod_spec(1, tm, n_ctx, t_s), _mod_spec(0, tm, n_ctx, t_s),
                  pl.BlockSpec(w.shape, lambda i: (0, 0)), rope_spec, rope_spec],
        out_specs=(row, row, row, row, row),
        compiler_params=_cparams("parallel"),
        name="diff_qkv",
    )(x, mod, mod, w, cos, sin)


ATT_GROUP = 4
ATT_KCH = 512
ATT_UNROLL = 3


def _attn_kernel(*refs, diff, has_cache, tq, kch, n_own, n_cache, lam_init):
    refs = list(refs)
    if diff:
        lam_ref, g_ref = refs[:2]
        refs = refs[2:]
    if has_cache:
        q_ref, k_ref, v_ref, ck_ref, cv_ref, o_ref, k_scr, vt_scr, s_scr = refs
    else:
        q_ref, k_ref, v_ref, o_ref, k_scr, vt_scr, s_scr = refs
    nch = n_own + n_cache
    n_kv = k_ref.shape[1] // LANES
    v_rows = LANES if diff else HEAD_DIM
    ones = jnp.where(lax.broadcasted_iota(jnp.int32, (SUBLANES, kch), 0) == 0, 1.0, 0.0).astype(BF16)

    @pl.when(pl.program_id(2) == 0)
    def _():
        for t in range(n_kv):
            lanes = slice(t * LANES, (t + 1) * LANES)
            for c in range(nch):
                if c < n_own:
                    rows = slice(c * kch, (c + 1) * kch)
                    kc, vc = k_ref[rows, lanes], v_ref[rows, lanes]
                else:
                    rows = slice((c - n_own) * kch, (c - n_own + 1) * kch)
                    kc, vc = ck_ref[rows, lanes], cv_ref[rows, lanes]
                k_scr[t, c] = kc
                vt = vc.T
                if diff:
                    vt_scr[t, c, 0] = jnp.concatenate([vt, ones], axis=0)
                else:
                    vt_scr[t, c, 0] = jnp.concatenate([vt[:HEAD_DIM], ones], axis=0)
                    vt_scr[t, c, 1] = jnp.concatenate([vt[HEAD_DIM:], ones], axis=0)

    low_q = lax.broadcasted_iota(jnp.int32, (tq, LANES), 1) < HEAD_DIM
    n_tiles = q_ref.shape[1] // LANES
    kv_of = (lambda j: j) if diff else (lambda j: 0)

    def weights(j):
        qt = q_ref[:, j * LANES:(j + 1) * LANES]
        zero = jnp.zeros_like(qt)
        both = jnp.concatenate([jnp.where(low_q, qt, zero), jnp.where(low_q, zero, qt)], axis=0)
        return both.T

    def stage(j_a, j_b, mx_b):
        w = weights(j_a) if j_a is not None else None

        def body(c, carry):
            m8, accs = carry
            if j_a is not None:
                s = jnp.dot(k_scr[kv_of(j_a), c], w, preferred_element_type=F32)
                s_scr[j_a % 2, c] = s
                m8 = jnp.maximum(m8, jnp.max(s.reshape(kch // SUBLANES, SUBLANES, 2 * tq), axis=0))
            if j_b is not None:
                e = jnp.exp2(s_scr[j_b % 2, c] - mx_b).astype(BF16)
                if diff:
                    accs = (accs[0] + jnp.dot(vt_scr[kv_of(j_b), c, 0], e, preferred_element_type=F32),)
                else:
                    accs = (accs[0] + jnp.dot(vt_scr[0, c, 0], e[:, :tq], preferred_element_type=F32),
                            accs[1] + jnp.dot(vt_scr[0, c, 1], e[:, tq:], preferred_element_type=F32))
            return m8, accs

        acc_shape = (v_rows + SUBLANES, 2 * tq if diff else tq)
        init = (jnp.full((SUBLANES, 2 * tq), -jnp.inf, F32),
                tuple(jnp.zeros(acc_shape, F32) for _ in range(1 if diff else 2)))
        m8, accs = lax.fori_loop(0, nch, body, init, unroll=min(nch, ATT_UNROLL))
        return jnp.max(m8, axis=0, keepdims=True), accs

    if diff:
        lv = lam_ref[...]
        lam = (jnp.exp(jnp.sum(lv[0:1] * lv[1:2], axis=-1, keepdims=True))
               - jnp.exp(jnp.sum(lv[2:3] * lv[3:4], axis=-1, keepdims=True)) + lam_init)

    mx_prev = None
    for st in range(n_tiles + 1):
        j_a = st if st < n_tiles else None
        j_b = st - 1 if st >= 1 else None
        mx_new, accs = stage(j_a, j_b, mx_prev)
        if j_b is not None:
            if diff:
                a = accs[0]
                o_t = (a[:v_rows, :tq] / a[v_rows:v_rows + 1, :tq]
                       - lam * (a[:v_rows, tq:] / a[v_rows:v_rows + 1, tq:]))
                o = o_t.T
                inv = lax.rsqrt(jnp.mean(o * o, axis=-1, keepdims=True) + RMS_EPS)
                o = (o * inv * g_ref[...]) * (1.0 - lam_init)
            else:
                o_t = jnp.concatenate([a[:v_rows] / a[v_rows:v_rows + 1] for a in accs], axis=0)
                o = o_t.T
            o_ref[:, j_b * LANES:(j_b + 1) * LANES] = o.astype(BF16)
        mx_prev = mx_new


def _attention(q, k, v, cache_k, cache_v, *, row0, n_b, t, tq, diff=None):
    is_diff = diff is not None
    qw = ATT_GROUP * LANES
    n_groups = D_MODEL // qw
    kvw = qw if is_diff else LANES
    nq_blocks = t // tq
    has_cache = cache_k is not None
    past = cache_k.shape[1] if has_cache else 0
    kch = min(ATT_KCH, t)
    assert t % kch == 0 and past % kch == 0
    n_own, n_cache = t // kch, past // kch
    nch = n_own + n_cache
    q_spec = pl.BlockSpec((tq, qw), lambda b, p, iq: (row0 // tq + b * nq_blocks + iq, p))
    kv_spec = pl.BlockSpec((t, kvw), lambda b, p, iq: (row0 // t + b, p))
    in_specs, args = [], []
    if is_diff:
        lam_vec, g_sub, lam_init = diff
        const = lambda a: pl.BlockSpec(a.shape, lambda b, p, iq: (0,) * a.ndim)
        in_specs += [const(lam_vec), const(g_sub)]
        args += [lam_vec, g_sub]
    else:
        lam_init = 0.0
    in_specs += [q_spec, kv_spec, kv_spec]
    args += [q, k, v]
    if has_cache:
        c_spec = pl.BlockSpec((None, past, kvw), lambda b, p, iq: (b, 0, p))
        in_specs += [c_spec, c_spec]
        args += [cache_k, cache_v]
    n_kv = kvw // LANES
    v_rows = (LANES if is_diff else HEAD_DIM) + SUBLANES
    return pl.pallas_call(
        functools.partial(_attn_kernel, diff=is_diff, has_cache=has_cache, tq=tq, kch=kch,
                          n_own=n_own, n_cache=n_cache, lam_init=lam_init),
        out_shape=jax.ShapeDtypeStruct((n_b * t, D_MODEL), BF16),
        grid=(n_b, n_groups, nq_blocks),
        in_specs=in_specs,
        out_specs=pl.BlockSpec((tq, qw), lambda b, p, iq: (b * nq_blocks + iq, p)),
        scratch_shapes=[pltpu.VMEM((n_kv, nch, kch, LANES), BF16),
                        pltpu.VMEM((n_kv, nch, 1 if is_diff else 2, v_rows, kch), BF16),
                        pltpu.VMEM((2, nch, kch, 2 * tq), F32)],
        compiler_params=_cparams("arbitrary", "arbitrary", "arbitrary"),
        name=("diff_attn" if is_diff else "gqa_attn") + ("_cache" if has_cache else ""),
    )(*args)


def _oproj_kernel(ac_ref, as_ref, w_ref, x_ref, gate_ref, lng_ref, lnb_ref, o_ref, *, n_ctx_blocks):
    is_ctx = pl.program_id(0) < n_ctx_blocks
    a = jnp.where(is_ctx, ac_ref[...], as_ref[...])
    out = jnp.dot(a, w_ref[...], preferred_element_type=F32)
    z = DEEPNORM_ALPHA * x_ref[...] + gate_ref[...] * out
    o_ref[...] = _layer_norm_rows(z, lng_ref[...], lnb_ref[...])


def _oproj(a_ctx, a_smp, w, x, mod, lng, lnb, *, n_ctx, t_s):
    m = x.shape[0]
    kdim = a_ctx.shape[1]
    tm = 512
    ncb = n_ctx // tm
    const = lambda arr: pl.BlockSpec(arr.shape, lambda i: (0,) * arr.ndim)
    row = pl.BlockSpec((tm, D_MODEL), lambda i: (i, 0))
    return pl.pallas_call(
        functools.partial(_oproj_kernel, n_ctx_blocks=ncb),
        out_shape=jax.ShapeDtypeStruct((m, D_MODEL), F32),
        grid=(m // tm,),
        in_specs=[pl.BlockSpec((tm, kdim), lambda i: (jnp.minimum(i, ncb - 1), 0)),
                  pl.BlockSpec((tm, kdim), lambda i: (jnp.maximum(i - ncb, 0), 0)),
                  const(w), row, _mod_spec(2, tm, n_ctx, t_s), const(lng), const(lnb)],
        out_specs=row,
        compiler_params=_cparams("parallel"),
        name="oproj_ln",
    )(a_ctx, a_smp, w, x, mod, lng, lnb)


def _gelu_tanh(x):
    return 0.5 * x * (1.0 + jnp.tanh(math.sqrt(2.0 / math.pi) * (x + 0.044715 * (x * x * x))))


def _rnn_oproj_kernel(hf_ref, hb_ref, br_ref, w_ref, x_ref, gate_ref, lng_ref, lnb_ref, o_ref):
    y = (hf_ref[...] + hb_ref[...]) * _gelu_tanh(br_ref[...])
    out = jnp.dot(y.astype(BF16), w_ref[...], preferred_element_type=F32)
    z = DEEPNORM_ALPHA * x_ref[...] + gate_ref[...] * out
    o_ref[...] = _layer_norm_rows(z, lng_ref[...], lnb_ref[...])


def _rnn_oproj(hs_f, hs_b, bx, w, x, mod, lng, lnb, *, n_ctx, t_s):
    m = x.shape[0]
    tm = 512
    const = lambda arr: pl.BlockSpec(arr.shape, lambda i: (0,) * arr.ndim)
    row = pl.BlockSpec((tm, D_MODEL), lambda i: (i, 0))
    rnn_row = pl.BlockSpec((tm, D_RNN), lambda i: (i, 0))
    return pl.pallas_call(
        _rnn_oproj_kernel,
        out_shape=jax.ShapeDtypeStruct((m, D_MODEL), F32),
        grid=(m // tm,),
        in_specs=[rnn_row, rnn_row, rnn_row, const(w), row,
                  _mod_spec(2, tm, n_ctx, t_s), const(lng), const(lnb)],
        out_specs=row,
        compiler_params=_cparams("parallel"),
        name="rnn_oproj_ln",
    )(hs_f, hs_b, bx, w, x, mod, lng, lnb)


def _mod_mm_kernel(x_ref, sc_ref, sh_ref, w_ref, o_ref):
    h = (x_ref[...] * (1.0 + sc_ref[...]) + sh_ref[...]).astype(BF16)
    o_ref[...] = jnp.dot(h, w_ref[...], preferred_element_type=F32)


def _rnn_in(x, mod, w, *, n_ctx, t_s):
    m = x.shape[0]
    n = w.shape[1]
    tm = 512
    return pl.pallas_call(
        _mod_mm_kernel,
        out_shape=jax.ShapeDtypeStruct((m, n), F32),
        grid=(m // tm,),
        in_specs=[pl.BlockSpec((tm, D_MODEL), lambda i: (i, 0)),
                  _mod_spec(1, tm, n_ctx, t_s), _mod_spec(0, tm, n_ctx, t_s),
                  pl.BlockSpec(w.shape, lambda i: (0, 0))],
        out_specs=pl.BlockSpec((tm, n), lambda i: (i, 0)),
        compiler_params=_cparams("parallel"),
        name="rnn_in",
    )(x, mod, mod, w)


def _rnn_gate_kernel(xb_ref, prev_ref, next_ref, cw_ref, cb_ref, wg_ref, bg_ref, lam_ref,
                     af_ref, uf_ref, ab_ref, ub_ref, ext_ref, *, tm, n_ctx, t_c, t_s):
    i = pl.program_id(0)
    r0 = i * tm
    pos = jnp.where(r0 < n_ctx, r0 % t_c, (r0 - n_ctx) % t_s)
    t_seq = jnp.where(r0 < n_ctx, t_c, t_s)
    at_start = pos == 0
    at_end = pos + tm == t_seq
    ext_ref[0:SUBLANES, :] = jnp.where(at_start, 0.0, prev_ref[...])
    ext_ref[SUBLANES:SUBLANES + tm, :] = xb_ref[...]
    ext_ref[SUBLANES + tm:, :] = jnp.where(at_end, 0.0, next_ref[...])
    xc = 0.0
    for j in range(CONV_W):
        off = SUBLANES - CONV_LEFT + j
        xc = xc + ext_ref[off:off + tm, :] * cw_ref[j:j + 1, :]
    xc = xc + cb_ref[...]
    xcb = xc.astype(BF16)
    neg_lam = -lam_ref[...]
    sp = jnp.maximum(neg_lam, 0.0) + jnp.log1p(jnp.exp(-jnp.abs(neg_lam)))

    def gate(g, ct, cols):
        k0 = GATE_K0[ct]
        pre = jnp.dot(xcb[:, k0:k0 + GATE_K], wg_ref[g, ct], preferred_element_type=F32)
        return jax.nn.sigmoid(pre + bg_ref[:, g * D_RNN + cols.start:g * D_RNN + cols.stop])

    for z, (a_ref, u_ref) in enumerate(((af_ref, uf_ref), (ab_ref, ub_ref))):
        for ct in range(D_RNN // GATE_N):
            cols = slice(ct * GATE_N, (ct + 1) * GATE_N)
            r = gate(2 * z, ct, cols)
            g_in = gate(2 * z + 1, ct, cols)
            log_a = -RGLRU_C * r * sp[z:z + 1, cols]
            a = jnp.exp(log_a)
            a_ref[:, cols] = a
            u_ref[:, cols] = jnp.sqrt(-jnp.tanh(log_a) * (a * a + 1.0)) * (g_in * xc[:, cols])


def _rnn_gates(bx, conv_w, conv_b, wg, bg, lam, *, n_ctx, t_c, t_s):
    m = bx.shape[0]
    tm = 256
    n8 = tm // SUBLANES
    last8 = m // SUBLANES - 1
    const = lambda arr: pl.BlockSpec(arr.shape, lambda i: (0,) * arr.ndim)
    row = pl.BlockSpec((tm, D_RNN), lambda i: (i, 0))
    out = jax.ShapeDtypeStruct((m, D_RNN), F32)
    return pl.pallas_call(
        functools.partial(_rnn_gate_kernel, tm=tm, n_ctx=n_ctx, t_c=t_c, t_s=t_s),
        out_shape=(out, out, out, out),
        grid=(m // tm,),
        in_specs=[pl.BlockSpec((tm, D_RNN), lambda i: (i, 1)),
                  pl.BlockSpec((SUBLANES, D_RNN), lambda i: (jnp.maximum(i * n8 - 1, 0), 1)),
                  pl.BlockSpec((SUBLANES, D_RNN), lambda i: (jnp.minimum((i + 1) * n8, last8), 1)),
                  const(conv_w), const(conv_b), const(wg), const(bg), const(lam)],
        out_specs=(row, row, row, row),
        scratch_shapes=[pltpu.VMEM((tm + 2 * SUBLANES, D_RNN), F32)],
        compiler_params=_cparams("parallel"),
        name="rnn_gates",
    )(bx, bx, bx, conv_w, conv_b, wg, bg, lam)


def _rnn_scan_kernel(fblk_ref, bblk_ref, seq_ref, first_ref,
                     af_ref, uf_ref, ab_ref, ub_ref, h0_ref, hf_ref, hb_ref, fin_ref, carry_ref, *, tc):
    s = pl.program_id(0)

    @pl.when(first_ref[s] == 1)
    def _():
        carry_ref[...] = h0_ref[...]

    def step(t, carry):
        h_f, h_b = carry
        tb = tc - 1 - t
        h_f = af_ref[pl.ds(t, 1), :] * h_f + uf_ref[pl.ds(t, 1), :]
        h_b = ab_ref[pl.ds(tb, 1), :] * h_b + ub_ref[pl.ds(tb, 1), :]
        hf_ref[pl.ds(t, 1), :] = h_f
        hb_ref[pl.ds(tb, 1), :] = h_b
        return h_f, h_b

    h_f, h_b = lax.fori_loop(0, tc, step, (carry_ref[0:1, :], carry_ref[1:2, :]), unroll=8)
    carry_ref[0:1, :] = h_f
    carry_ref[1:2, :] = h_b
    fin_ref[...] = carry_ref[...]


def _rnn_scan(a_f, u_f, a_b, u_b, h0, *, seq_lens):
    m = a_f.shape[0]
    tc = SCAN_CHUNK
    fblk, bblk, seq, first = [], [], [], []
    blk0 = 0
    for si, t in enumerate(seq_lens):
        nch = t // tc
        for c in range(nch):
            fblk.append(blk0 + c)
            bblk.append(blk0 + nch - 1 - c)
            seq.append(si)
            first.append(1 if c == 0 else 0)
        blk0 += nch
    n_steps = len(fblk)
    tables = [jnp.asarray(np.array(v, np.int32)) for v in (fblk, bblk, seq, first)]
    f_spec = pl.BlockSpec((tc, D_RNN), lambda s, fb, bb, sq, fr: (fb[s], 0))
    b_spec = pl.BlockSpec((tc, D_RNN), lambda s, fb, bb, sq, fr: (bb[s], 0))
    st_spec = pl.BlockSpec((None, 2, D_RNN), lambda s, fb, bb, sq, fr: (sq[s], 0, 0))
    out = jax.ShapeDtypeStruct((m, D_RNN), F32)
    return pl.pallas_call(
        functools.partial(_rnn_scan_kernel, tc=tc),
        out_shape=(out, out, jax.ShapeDtypeStruct(h0.shape, F32)),
        grid_spec=pltpu.PrefetchScalarGridSpec(
            num_scalar_prefetch=4, grid=(n_steps,),
            in_specs=[f_spec, f_spec, b_spec, b_spec, st_spec],
            out_specs=(f_spec, b_spec, st_spec),
            scratch_shapes=[pltpu.VMEM((2, D_RNN), F32)]),
        compiler_params=_cparams("arbitrary"),
        name="rnn_scan",
    )(*tables, a_f, u_f, a_b, u_b, h0)


def _ffn_kernel(x_ref, sc_ref, sh_ref, gate_ref, wg_ref, wu_ref, wd_ref, lng_ref, lnb_ref,
                o_ref, h_ref, acc_ref):
    j = pl.program_id(1)

    @pl.when(j == 0)
    def _():
        h_ref[...] = (x_ref[...] * (1.0 + sc_ref[...]) + sh_ref[...]).astype(BF16)
        acc_ref[...] = jnp.zeros_like(acc_ref)

    h = h_ref[...]
    g = jnp.dot(h, wg_ref[...], preferred_element_type=F32)
    u = jnp.dot(h, wu_ref[...], preferred_element_type=F32)
    act = ((g * jax.nn.sigmoid(g)) * u).astype(BF16)
    acc_ref[...] += jnp.dot(act, wd_ref[...], preferred_element_type=F32)

    @pl.when(j == pl.num_programs(1) - 1)
    def _():
        z = DEEPNORM_ALPHA * x_ref[...] + gate_ref[...] * acc_ref[...]
        o_ref[...] = _layer_norm_rows(z, lng_ref[...], lnb_ref[...])


def _ffn(x, mod, w_gu, w_down, lng, lnb, *, n_ctx, t_s):
    m = x.shape[0]
    tm = 1024
    tf = 256
    nf = D_FF // tf
    const = lambda arr: pl.BlockSpec(arr.shape, lambda i, j: (0,) * arr.ndim)
    row = pl.BlockSpec((tm, D_MODEL), lambda i, j: (i, 0))
    return pl.pallas_call(
        _ffn_kernel,
        out_shape=jax.ShapeDtypeStruct((m, D_MODEL), F32),
        grid=(m // tm, nf),
        in_specs=[row, _mod_spec(4, tm, n_ctx, t_s), _mod_spec(3, tm, n_ctx, t_s), _mod_spec(5, tm, n_ctx, t_s),
                  pl.BlockSpec((D_MODEL, tf), lambda i, j: (0, j)),
                  pl.BlockSpec((D_MODEL, tf), lambda i, j: (0, nf + j)),
                  pl.BlockSpec((tf, D_MODEL), lambda i, j: (j, 0)),
                  const(lng), const(lnb)],
        out_specs=row,
        scratch_shapes=[pltpu.VMEM((tm, D_MODEL), BF16), pltpu.VMEM((tm, D_MODEL), F32)],
        compiler_params=_cparams("parallel", "arbitrary"),
        name="ffn",
    )(x, mod, mod, mod, w_gu, w_gu, w_down, lng, lnb)


ROW_TILES = D_MODEL // LANES


def _to_token_rows(ref, base, value):
    n = value.shape[0]
    for j in range(ROW_TILES):
        ref[pl.ds(base + j, n, stride=ROW_TILES), :] = value[:, j * LANES:(j + 1) * LANES]


def _from_token_rows(ref, base, n):
    return jnp.concatenate([ref[pl.ds(base + j, n, stride=ROW_TILES), :] for j in range(ROW_TILES)], axis=1)


def _token_row(ref, r):
    start = r * ROW_TILES if isinstance(r, int) else pl.multiple_of(r * ROW_TILES, ROW_TILES)
    return ref.at[pl.ds(start, ROW_TILES)]


def _route_kernel(x_ref, sc_ref, sh_ref, wr_ref, route_ref, gates_ref, cnt_ref, carry_ref, *, tm):
    @pl.when(pl.program_id(0) == 0)
    def _():
        carry_ref[...] = jnp.zeros_like(carry_ref)

    lane = lax.broadcasted_iota(jnp.int32, (tm, LANES), 1)
    h = x_ref[...] * (1.0 + sc_ref[...]) + sh_ref[...]
    logits = jnp.dot(h.astype(BF16), wr_ref[...], preferred_element_type=F32)
    s1 = jnp.where(lane < N_EXPERTS, logits, -jnp.inf)
    m1 = jnp.max(s1, axis=-1, keepdims=True)
    i1 = jnp.min(jnp.where(s1 == m1, lane, LANES), axis=-1, keepdims=True)
    s2 = jnp.where(lane == i1, -jnp.inf, s1)
    m2 = jnp.max(s2, axis=-1, keepdims=True)
    i2 = jnp.min(jnp.where(s2 == m2, lane, LANES), axis=-1, keepdims=True)
    e2 = jnp.exp(m2 - m1)
    den = 1.0 + e2
    hit = jnp.where((lane == i1) | (lane == i2), 1.0, 0.0)
    earlier = lax.broadcasted_iota(jnp.int32, (tm, tm), 1) < lax.broadcasted_iota(jnp.int32, (tm, tm), 0)
    before = carry_ref[...] + jnp.dot(jnp.where(earlier, 1.0, 0.0).astype(BF16), hit.astype(BF16),
                                      preferred_element_type=F32)
    r1 = jnp.sum(jnp.where(lane == i1, before, 0.0), axis=-1, keepdims=True).astype(jnp.int32)
    r2 = jnp.sum(jnp.where(lane == i2, before, 0.0), axis=-1, keepdims=True).astype(jnp.int32)
    carry_ref[...] += jnp.sum(hit, axis=0, keepdims=True)
    route = jnp.where(lane == 0, i1, jnp.where(lane == 1, i2, jnp.where(lane == 2, r1, jnp.where(lane == 3, r2, 0))))
    route_ref[...] = route[:, :ROUTE_COLS]
    gates_ref[...] = jnp.where(lane == 0, 1.0 / den, jnp.where(lane == 1, e2 / den, 0.0))[:, :ROUTE_COLS]
    cnt_ref[...] = jnp.broadcast_to(carry_ref[...], cnt_ref.shape)


def _moe_route(x, mod, w_router, *, n_ctx, t_s):
    m = x.shape[0]
    tm = 512
    small = pl.BlockSpec((tm, ROUTE_COLS), lambda i: (i, 0))
    return pl.pallas_call(
        functools.partial(_route_kernel, tm=tm),
        out_shape=(jax.ShapeDtypeStruct((m, ROUTE_COLS), jnp.int32), jax.ShapeDtypeStruct((m, ROUTE_COLS), F32),
                   jax.ShapeDtypeStruct((SUBLANES, LANES), F32)),
        grid=(m // tm,),
        in_specs=[pl.BlockSpec((tm, D_MODEL), lambda i: (i, 0)), _mod_spec(4, tm, n_ctx, t_s),
                  _mod_spec(3, tm, n_ctx, t_s), pl.BlockSpec(w_router.shape, lambda i: (0, 0))],
        out_specs=(small, small, pl.BlockSpec((SUBLANES, LANES), lambda i: (0, 0))),
        scratch_shapes=[pltpu.VMEM((1, LANES), F32)],
        compiler_params=_cparams("arbitrary"),
        name="moe_route",
    )(x, mod, mod, w_router)


def _dispatch_kernel(slot_ref, x_ref, sc_ref, sh_ref, xs_hbm, buf, sem, *, tok):
    i = pl.program_id(0)
    par = i % 2
    cur = buf.at[par]
    _to_token_rows(cur, 0, x_ref[...] * (1.0 + sc_ref[...]) + sh_ref[...])

    def issue(t, c):
        src = _token_row(cur, t)
        pltpu.make_async_copy(src, _token_row(xs_hbm, slot_ref[2 * t]), sem.at[par]).start()
        pltpu.make_async_copy(src, _token_row(xs_hbm, slot_ref[2 * t + 1]), sem.at[par]).start()
        return c
    lax.fori_loop(0, tok, issue, 0, unroll=4)

    def drain(which):
        def body(t, c):
            pltpu.make_async_copy(_token_row(buf.at[which], 0), _token_row(xs_hbm, 0), sem.at[which]).wait()
            return c
        lax.fori_loop(0, 2 * tok, body, 0, unroll=8)

    @pl.when(i > 0)
    def _():
        drain(1 - par)

    @pl.when(i == pl.num_programs(0) - 1)
    def _():
        drain(par)


def _moe_dispatch(slots, x, mod, *, n_ctx, t_s):
    m = x.shape[0]
    tok = 512
    return pl.pallas_call(
        functools.partial(_dispatch_kernel, tok=tok),
        out_shape=jax.ShapeDtypeStruct((2 * m * ROW_TILES, LANES), F32),
        grid=(m // tok,),
        in_specs=[pl.BlockSpec((2 * tok,), lambda i: (i,), memory_space=pltpu.SMEM),
                  pl.BlockSpec((tok, D_MODEL), lambda i: (i, 0)),
                  _mod_spec(4, tok, n_ctx, t_s), _mod_spec(3, tok, n_ctx, t_s)],
        out_specs=pl.BlockSpec(memory_space=pl.ANY),
        scratch_shapes=[pltpu.VMEM((2, tok * ROW_TILES, LANES), F32), pltpu.SemaphoreType.DMA((2,))],
        compiler_params=_cparams("arbitrary"),
        name="moe_dispatch",
    )(slots, x, mod, mod)


def _experts_kernel(wt_ref, we_ref, lo_ref, hi_ref, nw_ref, xs_ref, wg_ref, wu_ref, wd_ref, o_ref, *, rows):
    del we_ref
    w = pl.program_id(0)

    @pl.when(w < nw_ref[0])
    def _():
        xs = _from_token_rows(xs_ref, 0, rows).astype(BF16)
        g = jnp.dot(xs, wg_ref[...], preferred_element_type=F32)
        u = jnp.dot(xs, wu_ref[...], preferred_element_type=F32)
        act = ((g * jax.nn.sigmoid(g)) * u).astype(BF16)
        y = jnp.dot(act, wd_ref[...], preferred_element_type=F32)
        whole = jnp.logical_and(lo_ref[w] == 0, hi_ref[w] == rows)
        first = jnp.logical_or(w == 0, wt_ref[jnp.maximum(w - 1, 0)] != wt_ref[w])

        @pl.when(whole)
        def _():
            _to_token_rows(o_ref, 0, y)

        @pl.when(jnp.logical_not(whole))
        def _():
            row = lax.broadcasted_iota(jnp.int32, (rows, D_MODEL), 0)
            mine = (row >= lo_ref[w]) & (row < hi_ref[w])

            @pl.when(first)
            def _():
                _to_token_rows(o_ref, 0, jnp.where(mine, y, 0.0))

            @pl.when(jnp.logical_not(first))
            def _():
                _to_token_rows(o_ref, 0, jnp.where(mine, y, _from_token_rows(o_ref, 0, rows)))


def _moe_experts(items, xs, w_gu, w_down, layer):
    rows = MOE_TILE
    n_items = items[0].shape[0]
    tile = lambda w, wt, we, lo, hi, nw: (wt[w], 0)
    return pl.pallas_call(
        functools.partial(_experts_kernel, rows=rows),
        out_shape=jax.ShapeDtypeStruct(xs.shape, F32),
        grid_spec=pltpu.PrefetchScalarGridSpec(
            num_scalar_prefetch=5, grid=(n_items,),
            in_specs=[pl.BlockSpec((rows * ROW_TILES, LANES), tile),
                      pl.BlockSpec((None, None, D_MODEL, D_FF_EXPERT),
                                   lambda w, wt, we, lo, hi, nw: (layer, we[w], 0, 0)),
                      pl.BlockSpec((None, None, D_MODEL, D_FF_EXPERT),
                                   lambda w, wt, we, lo, hi, nw: (layer, we[w], 0, 1)),
                      pl.BlockSpec((None, None, D_FF_EXPERT, D_MODEL),
                                   lambda w, wt, we, lo, hi, nw: (layer, we[w], 0, 0))],
            out_specs=pl.BlockSpec((rows * ROW_TILES, LANES), tile)),
        compiler_params=_cparams("arbitrary"),
        name="moe_experts",
    )(*items, xs, w_gu, w_gu, w_down)


def _combine_kernel(idx_ref, nxt_ref, ys_hbm, gates_ref, x_ref, gate_ref, lng_ref, lnb_ref, o_ref, buf, sem, *, tm):
    i = pl.program_id(0)
    slot = i % 2

    def issue(ref, s):
        def body(t, c):
            pltpu.make_async_copy(_token_row(ys_hbm, ref[2 * t]), _token_row(buf.at[s], t), sem.at[s]).start()
            pltpu.make_async_copy(_token_row(ys_hbm, ref[2 * t + 1]), _token_row(buf.at[s], tm + t), sem.at[s]).start()
            return c
        lax.fori_loop(0, tm, body, 0, unroll=4)

    @pl.when(i == 0)
    def _():
        issue(idx_ref, 0)

    @pl.when(i + 1 < pl.num_programs(0))
    def _():
        issue(nxt_ref, 1 - slot)

    def drain(r, c):
        pltpu.make_async_copy(_token_row(ys_hbm, 0), _token_row(buf.at[slot], 0), sem.at[slot]).wait()
        return c
    lax.fori_loop(0, 2 * tm, drain, 0, unroll=8)
    cur = buf.at[slot]
    y = (gates_ref[:, 0:1] * _from_token_rows(cur, 0, tm)
         + gates_ref[:, 1:2] * _from_token_rows(cur, tm * ROW_TILES, tm))
    z = DEEPNORM_ALPHA * x_ref[...] + gate_ref[...] * y
    o_ref[...] = _layer_norm_rows(z, lng_ref[...], lnb_ref[...])


def _moe_combine(slots, ys, gates, x, mod, lng, lnb, *, n_ctx, t_s):
    m = x.shape[0]
    tm = 256
    n_blocks = m // tm
    idx = lambda f: pl.BlockSpec((2 * tm,), f, memory_space=pltpu.SMEM)
    const = lambda arr: pl.BlockSpec(arr.shape, lambda i: (0,) * arr.ndim)
    row = pl.BlockSpec((tm, D_MODEL), lambda i: (i, 0))
    return pl.pallas_call(
        functools.partial(_combine_kernel, tm=tm),
        out_shape=jax.ShapeDtypeStruct((m, D_MODEL), F32),
        grid=(n_blocks,),
        in_specs=[idx(lambda i: (i,)), idx(lambda i: (jnp.minimum(i + 1, n_blocks - 1),)),
                  pl.BlockSpec(memory_space=pl.ANY), pl.BlockSpec((tm, ROUTE_COLS), lambda i: (i, 0)),
                  row, _mod_spec(5, tm, n_ctx, t_s), const(lng), const(lnb)],
        out_specs=row,
        scratch_shapes=[pltpu.VMEM((2, 2 * tm * ROW_TILES, LANES), F32), pltpu.SemaphoreType.DMA((2,))],
        compiler_params=_cparams("arbitrary"),
        name="moe_combine",
    )(slots, slots, ys, gates, x, mod, lng, lnb)


def _moe_work_items(counts, n_rows):
    n_tiles = n_rows // MOE_TILE
    n_items = n_tiles + N_EXPERTS - 1
    g_end = jnp.cumsum(counts)
    g_start = g_end - counts
    row0 = jnp.arange(n_tiles, dtype=jnp.int32) * MOE_TILE
    first_e = jnp.sum(row0[:, None] >= g_end[None, :], axis=1)
    last_e = jnp.sum((row0 + MOE_TILE - 1)[:, None] >= g_end[None, :], axis=1)
    per_tile = last_e - first_e + 1
    item0 = jnp.cumsum(per_tile) - per_tile
    w = jnp.arange(n_items, dtype=jnp.int32)
    tile = jnp.clip(jnp.sum(w[:, None] >= item0[None, :], axis=1) - 1, 0, n_tiles - 1)
    expert = jnp.clip(first_e[tile] + w - item0[tile], 0, N_EXPERTS - 1)
    lo = jnp.clip(g_start[expert] - row0[tile], 0, MOE_TILE)
    hi = jnp.clip(g_end[expert] - row0[tile], 0, MOE_TILE)
    n_used = jnp.sum(per_tile).reshape(1)
    i32 = lambda a: a.astype(jnp.int32)
    return i32(tile), i32(expert), i32(lo), i32(hi), i32(n_used), g_start


def _moe(x, mod, w_router, w_gu, w_down, layer, lng, lnb, *, n_ctx, t_s):
    m = x.shape[0]
    assert (2 * m) % MOE_TILE == 0
    route, gates, cnt = _moe_route(x, mod, w_router, n_ctx=n_ctx, t_s=t_s)
    counts = cnt[0, :N_EXPERTS].astype(jnp.int32)
    *items, g_start = _moe_work_items(counts, 2 * m)
    slots = (g_start[route[:, 0:2]] + route[:, 2:4]).reshape(2 * m).astype(jnp.int32)
    xs = _moe_dispatch(slots, x, mod, n_ctx=n_ctx, t_s=t_s)
    ys = _moe_experts(items, xs, w_gu, w_down, layer)
    return _moe_combine(slots, ys, gates, x, mod, lng, lnb, n_ctx=n_ctx, t_s=t_s)


def _rope_tables(n_tokens):
    n_rows = n_tokens // GRID_W
    row = jnp.repeat(jnp.arange(n_rows, dtype=F32), GRID_W)
    col = jnp.tile(jnp.arange(GRID_W, dtype=F32), n_rows)
    n_axis = HEAD_DIM // 4
    inv = ROPE_THETA ** (-jnp.arange(n_axis, dtype=F32) / n_axis)
    ang = jnp.concatenate([row[:, None] * inv, col[:, None] * inv], axis=-1)
    cos, sin = jnp.cos(ang), jnp.sin(ang)
    reps = LANES // HEAD_DIM
    cos_t = jnp.tile(jnp.concatenate([cos, cos], axis=-1), (1, reps))
    sin_t = jnp.tile(jnp.concatenate([-sin, sin], axis=-1), (1, reps))
    return cos_t, sin_t


def _gqa_head_perm():
    cols = []
    per_pair = 2 * (N_Q_HEADS // N_KV_HEADS)
    for p in range(N_KV_HEADS // 2):
        for j in range(N_Q_HEADS // N_KV_HEADS):
            for hd in (per_pair * p + j, per_pair * p + per_pair // 2 + j):
                cols.append(np.arange(hd * HEAD_DIM, (hd + 1) * HEAD_DIM))
    return np.concatenate(cols)


def _block_diag_windows(w):
    eye = jnp.eye(N_RNN_BLOCKS, dtype=bool)[:, None, :, None]
    dense = jnp.where(eye, w[:, :, None, :], 0.0).reshape(D_RNN, D_RNN)
    for ct, k0 in enumerate(GATE_K0):
        lo, hi = ct * GATE_N, (ct + 1) * GATE_N
        assert k0 <= RNN_BLOCK * (lo // RNN_BLOCK) and RNN_BLOCK * -(-hi // RNN_BLOCK) <= k0 + GATE_K <= D_RNN
    return jnp.stack([dense[k0:k0 + GATE_K, ct * GATE_N:(ct + 1) * GATE_N] for ct, k0 in enumerate(GATE_K0)])


def kernel(x_prompt, x_sample, cache_attn_k, cache_attn_v, state_rglru, cache_diff_k, cache_diff_v, c, c_ctx, w_ada, b_ada, ln_g, ln_b, attn_w_qkv, attn_g_q, attn_g_k, attn_w_o, rnn_w_in, rnn_conv_w, rnn_conv_b, rnn_w_a, rnn_b_a, rnn_w_x, rnn_b_x, rnn_lambda, rnn_w_out, diff_w_qkv, diff_lambda, diff_g_sub, diff_w_o, ffn_w_gu, ffn_w_down, moe_w_router, moe_w_gu, moe_w_down):
    b_c, t_c, d = x_prompt.shape
    b_s, t_s, _ = x_sample.shape
    n_ctx = b_c * t_c
    n_s = b_s * t_s
    past = cache_attn_k.shape[2]
    assert d == D_MODEL and n_ctx % t_s == 0 and t_c == SCAN_CHUNK and t_s % SCAN_CHUNK == 0
    assert 1 + b_s <= COND_ROWS

    x = jnp.concatenate([x_prompt.reshape(n_ctx, d), x_sample.reshape(n_s, d)], axis=0)
    cond = jnp.zeros((COND_ROWS, d), F32).at[0].set(c_ctx).at[1:1 + b_s].set(c)
    mods = _ada_table(cond, w_ada, b_ada)[:, :1 + b_s].reshape(DEPTH, 1 + b_s, 1, 6 * d)

    cos_t, sin_t = _rope_tables(t_s)
    perm = _gqa_head_perm()
    nq = N_Q_HEADS * HEAD_DIM
    tile2 = lambda g: jnp.tile(g, LANES // g.shape[-1]).reshape(1, LANES)
    kw = dict(n_ctx=n_ctx, t_s=t_s)
    moe_gu_b = moe_w_gu.astype(BF16)
    moe_down_b = moe_w_down.astype(BF16)

    attn_k, attn_v, rnn_s, diff_k, diff_v = [], [], [], [], []
    for li in range(DEPTH):
        mod = mods[li]
        lng = ln_g[li].reshape(2, 1, d)
        lnb = ln_b[li].reshape(2, 1, d)
        j = li // N_MIXERS
        kind = li % N_MIXERS
        if kind == 0:
            w = attn_w_qkv[j]
            w = jnp.concatenate([w[:, :nq][:, perm], w[:, nq:]], axis=1).astype(BF16)
            q, kb, vb, kf, vf = _gqa_qkv(x, mod, w, tile2(attn_g_q[j]), tile2(attn_g_k[j]), cos_t, sin_t, **kw)
            ck = cache_attn_k[:, j].reshape(b_s, past, N_KV_HEADS * HEAD_DIM).astype(BF16)
            cv = cache_attn_v[:, j].reshape(b_s, past, N_KV_HEADS * HEAD_DIM).astype(BF16)
            o_c = _attention(q, kb, vb, None, None, row0=0, n_b=b_c, t=t_c, tq=t_c)
            o_s = _attention(q, kb, vb, ck, cv, row0=n_ctx, n_b=b_s, t=t_s, tq=256)
            x = _oproj(o_c, o_s, attn_w_o[j][perm, :].astype(BF16), x, mod, lng[0], lnb[0], **kw)
            attn_k.append(kf[:n_ctx])
            attn_v.append(vf[:n_ctx])
        elif kind == 1:
            bx = _rnn_in(x, mod, rnn_w_in[j].astype(BF16), **kw)
            wg = jnp.stack([_block_diag_windows(rnn_w_a[j, 0]), _block_diag_windows(rnn_w_x[j, 0]),
                            _block_diag_windows(rnn_w_a[j, 1]), _block_diag_windows(rnn_w_x[j, 1])]).astype(BF16)
            bg = jnp.concatenate([rnn_b_a[j, 0], rnn_b_x[j, 0], rnn_b_a[j, 1], rnn_b_x[j, 1]]).reshape(1, -1)
            a_f, u_f, a_b, u_b = _rnn_gates(bx, rnn_conv_w[j], rnn_conv_b[j].reshape(1, -1), wg, bg,
                                            rnn_lambda[j], n_ctx=n_ctx, t_c=t_c, t_s=t_s)
            h0 = jnp.concatenate([jnp.zeros((b_c, 2, D_RNN), F32), state_rglru[:, j]], axis=0)
            hs_f, hs_b, fin = _rnn_scan(a_f, u_f, a_b, u_b, h0, seq_lens=[t_c] * b_c + [t_s] * b_s)
            x = _rnn_oproj(hs_f, hs_b, bx, rnn_w_out[j].astype(BF16), x, mod, lng[0], lnb[0], **kw)
            rnn_s.append(fin[:b_c])
        else:
            lam_init = 0.8 - 0.6 * math.exp(-0.3 * li)
            q, kb, vb, kf, vf = _diff_qkv(x, mod, diff_w_qkv[j].astype(BF16), cos_t, sin_t, **kw)
            ck = cache_diff_k[:, j].reshape(b_s, past, D_MODEL).astype(BF16)
            cv = cache_diff_v[:, j].reshape(b_s, past, D_MODEL).astype(BF16)
            g_sub = diff_g_sub[j].reshape(1, LANES)
            dargs = (diff_lambda[j], g_sub, lam_init)
            o_c = _attention(q, kb, vb, None, None, row0=0, n_b=b_c, t=t_c, tq=t_c, diff=dargs)
            o_s = _attention(q, kb, vb, ck, cv, row0=n_ctx, n_b=b_s, t=t_s, tq=256, diff=dargs)
            x = _oproj(o_c, o_s, diff_w_o[j].astype(BF16), x, mod, lng[0], lnb[0], **kw)
            diff_k.append(kf[:n_ctx])
            diff_v.append(vf[:n_ctx])
        if li % 2 == 0:
            x = _ffn(x, mod, ffn_w_gu[li // 2].astype(BF16), ffn_w_down[li // 2].astype(BF16), lng[1], lnb[1], **kw)
        else:
            w_r = jnp.zeros((d, LANES), F32).at[:, :N_EXPERTS].set(moe_w_router[li // 2]).astype(BF16)
            x = _moe(x, mod, w_r, moe_gu_b, moe_down_b, li // 2, lng[1], lnb[1], **kw)

    y_prompt = x[:n_ctx].reshape(b_c, t_c, d)
    y_sample = x[n_ctx:].reshape(b_s, t_s, d)
    new_attn_k = jnp.stack([k.reshape(b_c, t_c, N_KV_HEADS, HEAD_DIM) for k in attn_k], axis=1)
    new_attn_v = jnp.stack([v.reshape(b_c, t_c, N_KV_HEADS, HEAD_DIM) for v in attn_v], axis=1)
    new_state = jnp.stack(rnn_s, axis=1)
    new_diff_k = jnp.stack([k.reshape(b_c, t_c, N_DIFF_HEADS, 2, HEAD_DIM) for k in diff_k], axis=1)
    new_diff_v = jnp.stack([v.reshape(b_c, t_c, N_DIFF_HEADS, 2 * HEAD_DIM) for v in diff_v], axis=1)
    return (y_prompt, y_sample, new_attn_k, new_attn_v, new_state, new_diff_k, new_diff_v)
```

```python
import functools
import math

import jax
import jax.numpy as jnp
import numpy as np
from jax import lax
from jax.experimental import pallas as pl
from jax.experimental.pallas import tpu as pltpu

F32 = jnp.float32
BF16 = jnp.bfloat16

D_MODEL = 1024
DEPTH = 4
GRID_W = 64
HEAD_DIM = 64
N_Q_HEADS = 16
N_KV_HEADS = 4
ROPE_THETA = 10000.0
N_DIFF_HEADS = 8
D_RNN = 1280
N_RNN_BLOCKS = 16
RNN_BLOCK = 80
CONV_W = 4
CONV_LEFT = 2
RGLRU_C = 8.0
D_FF = 2816
N_EXPERTS = 8
D_FF_EXPERT = 1408
N_MIXERS = 3
DEEPNORM_ALPHA = (2.0 * DEPTH) ** 0.25
LN_EPS = 1e-6
RMS_EPS = 1e-6

LANES = 128
SUBLANES = 8
VMEM_LIMIT = 56 * 1024 * 1024
COND_ROWS = 8
SCAN_CHUNK = 256
MOE_TILE = 512
ROUTE_COLS = 8
GATE_N = 256
GATE_K = 512
GATE_K0 = (0, 128, 384, 640, 768)
Q_SCALE = HEAD_DIM ** -0.5 * math.log2(math.e)


def _cparams(*sem):
    return pltpu.CompilerParams(dimension_semantics=sem, vmem_limit_bytes=VMEM_LIMIT)


def _sigmoid(x):
    return 0.5 * jnp.tanh(0.5 * x) + 0.5


def _layer_norm_rows(z, g, b):
    mu = jnp.mean(z, axis=-1, keepdims=True)
    zc = z - mu
    var = jnp.mean(zc * zc, axis=-1, keepdims=True)
    return zc * lax.rsqrt(var + LN_EPS) * g + b


def _seg_of_block(i, tm, n_ctx, t_s):
    r0 = i * tm
    return jnp.where(r0 < n_ctx, 0, 1 + (r0 - n_ctx) // t_s)


def _mod_spec(col, tm, n_ctx, t_s):
    return pl.BlockSpec((None, 1, D_MODEL), lambda i, *_: (_seg_of_block(i, tm, n_ctx, t_s), 0, col))


def _ada_kernel(c_ref, w_ref, b_ref, o_ref):
    c = c_ref[...]
    a = (c * _sigmoid(c)).astype(BF16)
    o_ref[...] = jnp.dot(a, w_ref[...].astype(BF16), preferred_element_type=F32) + b_ref[...]


def _ada_table(cond, w_ada, b_ada):
    n_l, d, n = w_ada.shape
    tn = 1536
    return pl.pallas_call(
        _ada_kernel,
        out_shape=jax.ShapeDtypeStruct((n_l, COND_ROWS, n), F32),
        grid=(n_l, n // tn),
        in_specs=[pl.BlockSpec((COND_ROWS, d), lambda l, j: (0, 0)),
                  pl.BlockSpec((None, d, tn), lambda l, j: (l, 0, j)),
                  pl.BlockSpec((None, 1, tn), lambda l, j: (l, 0, j))],
        out_specs=pl.BlockSpec((None, COND_ROWS, tn), lambda l, j: (l, 0, j)),
        compiler_params=_cparams("parallel", "parallel"),
        name="ada_table",
    )(cond, w_ada, b_ada.reshape(n_l, 1, n))


def _head_masks(tm):
    lane = lax.broadcasted_iota(jnp.int32, (tm, LANES), 1)
    return lane < HEAD_DIM, (lane & (HEAD_DIM - 1)) < HEAD_DIM // 2


def _rope_tile(t, first_half, cos, sin):
    partner = jnp.where(first_half, pltpu.roll(t, LANES - HEAD_DIM // 2, 1),
                        pltpu.roll(t, HEAD_DIM // 2, 1))
    return t * cos + partner * sin


def _gqa_qkv_kernel(x_ref, sc_ref, sh_ref, w_ref, gq_ref, gk_ref, cos_ref, sin_ref,
                    q_ref, kb_ref, vb_ref, kf_ref, vf_ref, *, tm, n_ctx):
    i = pl.program_id(0)
    h = (x_ref[...] * (1.0 + sc_ref[...]) + sh_ref[...]).astype(BF16)
    y = jnp.dot(h, w_ref[...], preferred_element_type=F32)
    use_rope = i * tm >= n_ctx
    cos = jnp.where(use_rope, cos_ref[...], 1.0)
    sin = jnp.where(use_rope, sin_ref[...], 0.0)
    low_head, first_half = _head_masks(tm)

    def norm_rope(t, g):
        t2 = t * t
        s_lo = jnp.sum(jnp.where(low_head, t2, 0.0), axis=-1, keepdims=True)
        s_hi = jnp.sum(jnp.where(low_head, 0.0, t2), axis=-1, keepdims=True)
        inv = jnp.where(low_head, lax.rsqrt(s_lo * (1.0 / HEAD_DIM) + RMS_EPS),
                        lax.rsqrt(s_hi * (1.0 / HEAD_DIM) + RMS_EPS))
        return _rope_tile(t * inv * g, first_half, cos, sin)

    nq = N_Q_HEADS * HEAD_DIM
    nkv = N_KV_HEADS * HEAD_DIM
    gq = gq_ref[...]
    gk = gk_ref[...]
    for t in range(nq // LANES):
        sl = slice(t * LANES, (t + 1) * LANES)
        q_ref[:, sl] = (norm_rope(y[:, sl], gq) * Q_SCALE).astype(BF16)
    for t in range(nkv // LANES):
        sl = slice(t * LANES, (t + 1) * LANES)
        k = norm_rope(y[:, nq + t * LANES: nq + (t + 1) * LANES], gk)
        kf_ref[:, sl] = k
        kb_ref[:, sl] = k.astype(BF16)
    v = y[:, nq + nkv:]
    vf_ref[...] = v
    vb_ref[...] = v.astype(BF16)


def _gqa_qkv(x, mod, w, gq, gk, cos, sin, *, n_ctx, t_s):
    m = x.shape[0]
    tm = 512
    nq = N_Q_HEADS * HEAD_DIM
    nkv = N_KV_HEADS * HEAD_DIM
    n_rope_blocks = t_s // tm
    rope_spec = pl.BlockSpec(
        (tm, LANES), lambda i: (jnp.where(i * tm >= n_ctx, ((i * tm - n_ctx) % t_s) // tm, 0) % n_rope_blocks, 0))
    row = lambda n: pl.BlockSpec((tm, n), lambda i: (i, 0))
    const = lambda a: pl.BlockSpec(a.shape, lambda i: (0,) * a.ndim)
    return pl.pallas_call(
        functools.partial(_gqa_qkv_kernel, tm=tm, n_ctx=n_ctx),
        out_shape=(jax.ShapeDtypeStruct((m, nq), BF16), jax.ShapeDtypeStruct((m, nkv), BF16),
                   jax.ShapeDtypeStruct((m, nkv), BF16), jax.ShapeDtypeStruct((m, nkv), F32),
                   jax.ShapeDtypeStruct((m, nkv), F32)),
        grid=(m // tm,),
        in_specs=[row(D_MODEL), _mod_spec(1, tm, n_ctx, t_s), _mod_spec(0, tm, n_ctx, t_s),
                  const(w), const(gq), const(gk), rope_spec, rope_spec],
        out_specs=(row(nq), row(nkv), row(nkv), row(nkv), row(nkv)),
        compiler_params=_cparams("parallel"),
        name="gqa_qkv",
    )(x, mod, mod, w, gq, gk, cos, sin)


def _diff_qkv_kernel(x_ref, sc_ref, sh_ref, w_ref, cos_ref, sin_ref,
                     q_ref, kb_ref, vb_ref, kf_ref, vf_ref, *, tm, n_ctx):
    i = pl.program_id(0)
    h = (x_ref[...] * (1.0 + sc_ref[...]) + sh_ref[...]).astype(BF16)
    y = jnp.dot(h, w_ref[...], preferred_element_type=F32)
    use_rope = i * tm >= n_ctx
    cos = jnp.where(use_rope, cos_ref[...], 1.0)
    sin = jnp.where(use_rope, sin_ref[...], 0.0)
    _, first_half = _head_masks(tm)
    for t in range(D_MODEL // LANES):
        sl = slice(t * LANES, (t + 1) * LANES)
        q = _rope_tile(y[:, sl], first_half, cos, sin)
        q_ref[:, sl] = (q * Q_SCALE).astype(BF16)
        k = _rope_tile(y[:, D_MODEL + t * LANES: D_MODEL + (t + 1) * LANES], first_half, cos, sin)
        kf_ref[:, sl] = k
        kb_ref[:, sl] = k.astype(BF16)
    v = y[:, 2 * D_MODEL:]
    vf_ref[...] = v
    vb_ref[...] = v.astype(BF16)


def _diff_qkv(x, mod, w, cos, sin, *, n_ctx, t_s):
    m = x.shape[0]
    tm = 256
    n_rope_blocks = t_s // tm
    rope_spec = pl.BlockSpec(
        (tm, LANES), lambda i: (jnp.where(i * tm >= n_ctx, ((i * tm - n_ctx) % t_s) // tm, 0) % n_rope_blocks, 0))
    row = pl.BlockSpec((tm, D_MODEL), lambda i: (i, 0))
    return pl.pallas_call(
        functools.partial(_diff_qkv_kernel, tm=tm, n_ctx=n_ctx),
        out_shape=(jax.ShapeDtypeStruct((m, D_MODEL), BF16), jax.ShapeDtypeStruct((m, D_MODEL), BF16),
                   jax.ShapeDtypeStruct((m, D_MODEL), BF16), jax.ShapeDtypeStruct((m, D_MODEL), F32),
                   jax.ShapeDtypeStruct((m, D_MODEL), F32)),
        grid=(m // tm,),
        in_specs=[row, _mod_spec(1, tm, n_ctx, t_s), _mod_spec(0, tm, n_ctx, t_s),
                  pl.BlockSpec(w.shape, lambda i: (0, 0)), rope_spec, rope_spec],
        out_specs=(row, row, row, row, row),
        compiler_params=_cparams("parallel"),
        name="diff_qkv",
    )(x, mod, mod, w, cos, sin)


ATT_GROUP = 4
ATT_KCH = 512
ATT_UNROLL = 3


def _attn_kernel(*refs, diff, has_cache, tq, kch, n_own, n_cache, lam_init):
    refs = list(refs)
    if diff:
        lam_ref, g_ref = refs[:2]
        refs = refs[2:]
    if has_cache:
        q_ref, k_ref, v_ref, ck_ref, cv_ref, o_ref, k_scr, vt_scr, s_scr = refs
    else:
        q_ref, k_ref, v_ref, o_ref, k_scr, vt_scr, s_scr = refs
    nch = n_own + n_cache
    n_kv = k_ref.shape[1] // LANES
    v_rows = LANES if diff else HEAD_DIM
    ones = jnp.where(lax.broadcasted_iota(jnp.int32, (SUBLANES, kch), 0) == 0, 1.0, 0.0).astype(BF16)

    @pl.when(pl.program_id(2) == 0)
    def _():
        for t in range(n_kv):
            lanes = slice(t * LANES, (t + 1) * LANES)
            for c in range(nch):
                if c < n_own:
                    rows = slice(c * kch, (c + 1) * kch)
                    kc, vc = k_ref[rows, lanes], v_ref[rows, lanes]
                else:
                    rows = slice((c - n_own) * kch, (c - n_own + 1) * kch)
                    kc, vc = ck_ref[rows, lanes], cv_ref[rows, lanes]
                k_scr[t, c] = kc
                vt = vc.T
                if diff:
                    vt_scr[t, c, 0] = jnp.concatenate([vt, ones], axis=0)
                else:
                    vt_scr[t, c, 0] = jnp.concatenate([vt[:HEAD_DIM], ones], axis=0)
                    vt_scr[t, c, 1] = jnp.concatenate([vt[HEAD_DIM:], ones], axis=0)

    low_q = lax.broadcasted_iota(jnp.int32, (tq, LANES), 1) < HEAD_DIM
    n_tiles = q_ref.shape[1] // LANES
    kv_of = (lambda j: j) if diff else (lambda j: 0)

    def weights(j):
        qt = q_ref[:, j * LANES:(j + 1) * LANES]
        zero = jnp.zeros_like(qt)
        both = jnp.concatenate([jnp.where(low_q, qt, zero), jnp.where(low_q, zero, qt)], axis=0)
        return both.T

    def stage(j_a, j_b, mx_b):
        w = weights(j_a) if j_a is not None else None

        def body(c, carry):
            m8, accs = carry
            if j_a is not None:
                s = jnp.dot(k_scr[kv_of(j_a), c], w, preferred_element_type=F32)
                s_scr[j_a % 2, c] = s
                m8 = jnp.maximum(m8, jnp.max(s.reshape(kch // SUBLANES, SUBLANES, 2 * tq), axis=0))
            if j_b is not None:
                e = jnp.exp2(s_scr[j_b % 2, c] - mx_b).astype(BF16)
                if diff:
                    accs = (accs[0] + jnp.dot(vt_scr[kv_of(j_b), c, 0], e, preferred_element_type=F32),)
                else:
                    accs = (accs[0] + jnp.dot(vt_scr[0, c, 0], e[:, :tq], preferred_element_type=F32),
                            accs[1] + jnp.dot(vt_scr[0, c, 1], e[:, tq:], preferred_element_type=F32))
            return m8, accs

        acc_shape = (v_rows + SUBLANES, 2 * tq if diff else tq)
        init = (jnp.full((SUBLANES, 2 * tq), -jnp.inf, F32),
                tuple(jnp.zeros(acc_shape, F32) for _ in range(1 if diff else 2)))
        m8, accs = lax.fori_loop(0, nch, body, init, unroll=min(nch, ATT_UNROLL))
        return jnp.max(m8, axis=0, keepdims=True), accs

    if diff:
        lv = lam_ref[...]
        lam = (jnp.exp(jnp.sum(lv[0:1] * lv[1:2], axis=-1, keepdims=True))
               - jnp.exp(jnp.sum(lv[2:3] * lv[3:4], axis=-1, keepdims=True)) + lam_init)

    mx_prev = None
    for st in range(n_tiles + 1):
        j_a = st if st < n_tiles else None
        j_b = st - 1 if st >= 1 else None
        mx_new, accs = stage(j_a, j_b, mx_prev)
        if j_b is not None:
            if diff:
                a = accs[0]
                o_t = (a[:v_rows, :tq] / a[v_rows:v_rows + 1, :tq]
                       - lam * (a[:v_rows, tq:] / a[v_rows:v_rows + 1, tq:]))
                o = o_t.T
                inv = lax.rsqrt(jnp.mean(o * o, axis=-1, keepdims=True) + RMS_EPS)
                o = (o * inv * g_ref[...]) * (1.0 - lam_init)
            else:
                o_t = jnp.concatenate([a[:v_rows] / a[v_rows:v_rows + 1] for a in accs], axis=0)
                o = o_t.T
            o_ref[:, j_b * LANES:(j_b + 1) * LANES] = o.astype(BF16)
        mx_prev = mx_new


def _attention(q, k, v, cache_k, cache_v, *, row0, n_b, t, tq, diff=None):
    is_diff = diff is not None
    qw = ATT_GROUP * LANES
    n_groups = D_MODEL // qw
    kvw = qw if is_diff else LANES
    nq_blocks = t // tq
    has_cache = cache_k is not None
    past = cache_k.shape[1] if has_cache else 0
    kch = min(ATT_KCH, t)
    assert t % kch == 0 and past % kch == 0
    n_own, n_cache = t // kch, past // kch
    nch = n_own + n_cache
    q_spec = pl.BlockSpec((tq, qw), lambda b, p, iq: (row0 // tq + b * nq_blocks + iq, p))
    kv_spec = pl.BlockSpec((t, kvw), lambda b, p, iq: (row0 // t + b, p))
    in_specs, args = [], []
    if is_diff:
        lam_vec, g_sub, lam_init = diff
        const = lambda a: pl.BlockSpec(a.shape, lambda b, p, iq: (0,) * a.ndim)
        in_specs += [const(lam_vec), const(g_sub)]
        args += [lam_vec, g_sub]
    else:
        lam_init = 0.0
    in_specs += [q_spec, kv_spec, kv_spec]
    args += [q, k, v]
    if has_cache:
        c_spec = pl.BlockSpec((None, past, kvw), lambda b, p, iq: (b, 0, p))
        in_specs += [c_spec, c_spec]
        args += [cache_k, cache_v]
    n_kv = kvw // LANES
    v_rows = (LANES if is_diff else HEAD_DIM) + SUBLANES
    return pl.pallas_call(
        functools.partial(_attn_kernel, diff=is_diff, has_cache=has_cache, tq=tq, kch=kch,
                          n_own=n_own, n_cache=n_cache, lam_init=lam_init),
        out_shape=jax.ShapeDtypeStruct((n_b * t, D_MODEL), BF16),
        grid=(n_b, n_groups, nq_blocks),
        in_specs=in_specs,
        out_specs=pl.BlockSpec((tq, qw), lambda b, p, iq: (b * nq_blocks + iq, p)),
        scratch_shapes=[pltpu.VMEM((n_kv, nch, kch, LANES), BF16),
                        pltpu.VMEM((n_kv, nch, 1 if is_diff else 2, v_rows, kch), BF16),
                        pltpu.VMEM((2, nch, kch, 2 * tq), F32)],
        compiler_params=_cparams("arbitrary", "arbitrary", "arbitrary"),
        name=("diff_attn" if is_diff else "gqa_attn") + ("_cache" if has_cache else ""),
    )(*args)


def _oproj_kernel(ac_ref, as_ref, w_ref, x_ref, gate_ref, lng_ref, lnb_ref, o_ref, *, n_ctx_blocks):
    is_ctx = pl.program_id(0) < n_ctx_blocks
    a = jnp.where(is_ctx, ac_ref[...], as_ref[...])
    out = jnp.dot(a, w_ref[...], preferred_element_type=F32)
    z = DEEPNORM_ALPHA * x_ref[...] + gate_ref[...] * out
    o_ref[...] = _layer_norm_rows(z, lng_ref[...], lnb_ref[...])


def _oproj(a_ctx, a_smp, w, x, mod, lng, lnb, *, n_ctx, t_s):
    m = x.shape[0]
    kdim = a_ctx.shape[1]
    tm = 512
    ncb = n_ctx // tm
    const = lambda arr: pl.BlockSpec(arr.shape, lambda i: (0,) * arr.ndim)
    row = pl.BlockSpec((tm, D_MODEL), lambda i: (i, 0))
    return pl.pallas_call(
        functools.partial(_oproj_kernel, n_ctx_blocks=ncb),
        out_shape=jax.ShapeDtypeStruct((m, D_MODEL), F32),
        grid=(m // tm,),
        in_specs=[pl.BlockSpec((tm, kdim), lambda i: (jnp.minimum(i, ncb - 1), 0)),
                  pl.BlockSpec((tm, kdim), lambda i: (jnp.maximum(i - ncb, 0), 0)),
                  const(w), row, _mod_spec(2, tm, n_ctx, t_s), const(lng), const(lnb)],
        out_specs=row,
        compiler_params=_cparams("parallel"),
        name="oproj_ln",
    )(a_ctx, a_smp, w, x, mod, lng, lnb)


def _gelu_tanh(x):
    return 0.5 * x * (1.0 + jnp.tanh(math.sqrt(2.0 / math.pi) * (x + 0.044715 * (x * x * x))))


def _rnn_oproj_kernel(hf_ref, hb_ref, br_ref, w_ref, x_ref, gate_ref, lng_ref, lnb_ref, o_ref):
    y = (hf_ref[...] + hb_ref[...]) * _gelu_tanh(br_ref[...])
    out = jnp.dot(y.astype(BF16), w_ref[...], preferred_element_type=F32)
    z = DEEPNORM_ALPHA * x_ref[...] + gate_ref[...] * out
    o_ref[...] = _layer_norm_rows(z, lng_ref[...], lnb_ref[...])


def _rnn_oproj(hs_f, hs_b, bx, w, x, mod, lng, lnb, *, n_ctx, t_s):
    m = x.shape[0]
    tm = 512
    const = lambda arr: pl.BlockSpec(arr.shape, lambda i: (0,) * arr.ndim)
    row = pl.BlockSpec((tm, D_MODEL), lambda i: (i, 0))
    rnn_row = pl.BlockSpec((tm, D_RNN), lambda i: (i, 0))
    return pl.pallas_call(
        _rnn_oproj_kernel,
        out_shape=jax.ShapeDtypeStruct((m, D_MODEL), F32),
        grid=(m // tm,),
        in_specs=[rnn_row, rnn_row, rnn_row, const(w), row,
                  _mod_spec(2, tm, n_ctx, t_s), const(lng), const(lnb)],
        out_specs=row,
        compiler_params=_cparams("parallel"),
        name="rnn_oproj_ln",
    )(hs_f, hs_b, bx, w, x, mod, lng, lnb)


def _mod_mm_kernel(x_ref, sc_ref, sh_ref, w_ref, o_ref):
    h = (x_ref[...] * (1.0 + sc_ref[...]) + sh_ref[...]).astype(BF16)
    o_ref[...] = jnp.dot(h, w_ref[...], preferred_element_type=F32)


def _rnn_in(x, mod, w, *, n_ctx, t_s):
    m = x.shape[0]
    n = w.shape[1]
    tm = 512
    return pl.pallas_call(
        _mod_mm_kernel,
        out_shape=jax.ShapeDtypeStruct((m, n), F32),
        grid=(m // tm,),
        in_specs=[pl.BlockSpec((tm, D_MODEL), lambda i: (i, 0)),
                  _mod_spec(1, tm, n_ctx, t_s), _mod_spec(0, tm, n_ctx, t_s),
                  pl.BlockSpec(w.shape, lambda i: (0, 0))],
        out_specs=pl.BlockSpec((tm, n), lambda i: (i, 0)),
        compiler_params=_cparams("parallel"),
        name="rnn_in",
    )(x, mod, mod, w)


def _rnn_gate_kernel(xb_ref, prev_ref, next_ref, cw_ref, cb_ref, wg_ref, bg_ref, lam_ref,
                     af_ref, uf_ref, ab_ref, ub_ref, ext_ref, *, tm, n_ctx, t_c, t_s):
    i = pl.program_id(0)
    r0 = i * tm
    pos = jnp.where(r0 < n_ctx, r0 % t_c, (r0 - n_ctx) % t_s)
    t_seq = jnp.where(r0 < n_ctx, t_c, t_s)
    at_start = pos == 0
    at_end = pos + tm == t_seq
    ext_ref[0:SUBLANES, :] = jnp.where(at_start, 0.0, prev_ref[...])
    ext_ref[SUBLANES:SUBLANES + tm, :] = xb_ref[...]
    ext_ref[SUBLANES + tm:, :] = jnp.where(at_end, 0.0, next_ref[...])
    xc = 0.0
    for j in range(CONV_W):
        off = SUBLANES - CONV_LEFT + j
        xc = xc + ext_ref[off:off + tm, :] * cw_ref[j:j + 1, :]
    xc = xc + cb_ref[...]
    xcb = xc.astype(BF16)
    neg_lam = -lam_ref[...]
    sp = jnp.maximum(neg_lam, 0.0) + jnp.log1p(jnp.exp(-jnp.abs(neg_lam)))

    def gate(g, ct, cols):
        k0 = GATE_K0[ct]
        pre = jnp.dot(xcb[:, k0:k0 + GATE_K], wg_ref[g, ct], preferred_element_type=F32)
        return _sigmoid(pre + bg_ref[:, g * D_RNN + cols.start:g * D_RNN + cols.stop])

    for z, (a_ref, u_ref) in enumerate(((af_ref, uf_ref), (ab_ref, ub_ref))):
        for ct in range(D_RNN // GATE_N):
            cols = slice(ct * GATE_N, (ct + 1) * GATE_N)
            r = gate(2 * z, ct, cols)
            g_in = gate(2 * z + 1, ct, cols)
            log_a = -RGLRU_C * r * sp[z:z + 1, cols]
            a = jnp.exp(log_a)
            a_ref[:, cols] = a
            u_ref[:, cols] = jnp.sqrt(-jnp.tanh(log_a) * (a * a + 1.0)) * (g_in * xc[:, cols])


def _rnn_gates(bx, conv_w, conv_b, wg, bg, lam, *, n_ctx, t_c, t_s):
    m = bx.shape[0]
    tm = 256
    n8 = tm // SUBLANES
    last8 = m // SUBLANES - 1
    const = lambda arr: pl.BlockSpec(arr.shape, lambda i: (0,) * arr.ndim)
    row = pl.BlockSpec((tm, D_RNN), lambda i: (i, 0))
    out = jax.ShapeDtypeStruct((m, D_RNN), F32)
    return pl.pallas_call(
        functools.partial(_rnn_gate_kernel, tm=tm, n_ctx=n_ctx, t_c=t_c, t_s=t_s),
        out_shape=(out, out, out, out),
        grid=(m // tm,),
        in_specs=[pl.BlockSpec((tm, D_RNN), lambda i: (i, 1)),
                  pl.BlockSpec((SUBLANES, D_RNN), lambda i: (jnp.maximum(i * n8 - 1, 0), 1)),
                  pl.BlockSpec((SUBLANES, D_RNN), lambda i: (jnp.minimum((i + 1) * n8, last8), 1)),
                  const(conv_w), const(conv_b), const(wg), const(bg), const(lam)],
        out_specs=(row, row, row, row),
        scratch_shapes=[pltpu.VMEM((tm + 2 * SUBLANES, D_RNN), F32)],
        compiler_params=_cparams("parallel"),
        name="rnn_gates",
    )(bx, bx, bx, conv_w, conv_b, wg, bg, lam)


def _rnn_scan_kernel(fblk_ref, bblk_ref, seq_ref, first_ref,
                     af_ref, uf_ref, ab_ref, ub_ref, h0_ref, hf_ref, hb_ref, fin_ref, carry_ref, *, tc):
    s = pl.program_id(0)

    @pl.when(first_ref[s] == 1)
    def _():
        carry_ref[...] = h0_ref[...]

    def step(t, carry):
        h_f, h_b = carry
        tb = tc - 1 - t
        h_f = af_ref[pl.ds(t, 1), :] * h_f + uf_ref[pl.ds(t, 1), :]
        h_b = ab_ref[pl.ds(tb, 1), :] * h_b + ub_ref[pl.ds(tb, 1), :]
        hf_ref[pl.ds(t, 1), :] = h_f
        hb_ref[pl.ds(tb, 1), :] = h_b
        return h_f, h_b

    h_f, h_b = lax.fori_loop(0, tc, step, (carry_ref[0:1, :], carry_ref[1:2, :]), unroll=8)
    carry_ref[0:1, :] = h_f
    carry_ref[1:2, :] = h_b
    fin_ref[...] = carry_ref[...]


def _rnn_scan(a_f, u_f, a_b, u_b, h0, *, seq_lens):
    m = a_f.shape[0]
    tc = SCAN_CHUNK
    fblk, bblk, seq, first = [], [], [], []
    blk0 = 0
    for si, t in enumerate(seq_lens):
        nch = t // tc
        for c in range(nch):
            fblk.append(blk0 + c)
            bblk.append(blk0 + nch - 1 - c)
            seq.append(si)
            first.append(1 if c == 0 else 0)
        blk0 += nch
    n_steps = len(fblk)
    tables = [jnp.asarray(np.array(v, np.int32)) for v in (fblk, bblk, seq, first)]
    f_spec = pl.BlockSpec((tc, D_RNN), lambda s, fb, bb, sq, fr: (fb[s], 0))
    b_spec = pl.BlockSpec((tc, D_RNN), lambda s, fb, bb, sq, fr: (bb[s], 0))
    st_spec = pl.BlockSpec((None, 2, D_RNN), lambda s, fb, bb, sq, fr: (sq[s], 0, 0))
    out = jax.ShapeDtypeStruct((m, D_RNN), F32)
    return pl.pallas_call(
        functools.partial(_rnn_scan_kernel, tc=tc),
        out_shape=(out, out, jax.ShapeDtypeStruct(h0.shape, F32)),
        grid_spec=pltpu.PrefetchScalarGridSpec(
            num_scalar_prefetch=4, grid=(n_steps,),
            in_specs=[f_spec, f_spec, b_spec, b_spec, st_spec],
            out_specs=(f_spec, b_spec, st_spec),
            scratch_shapes=[pltpu.VMEM((2, D_RNN), F32)]),
        compiler_params=_cparams("arbitrary"),
        name="rnn_scan",
    )(*tables, a_f, u_f, a_b, u_b, h0)


def _ffn_kernel(x_ref, sc_ref, sh_ref, gate_ref, wg_ref, wu_ref, wd_ref, lng_ref, lnb_ref,
                o_ref, h_ref, acc_ref):
    j = pl.program_id(1)

    @pl.when(j == 0)
    def _():
        h_ref[...] = (x_ref[...] * (1.0 + sc_ref[...]) + sh_ref[...]).astype(BF16)
        acc_ref[...] = jnp.zeros_like(acc_ref)

    h = h_ref[...]
    g = jnp.dot(h, wg_ref[...], preferred_element_type=F32)
    u = jnp.dot(h, wu_ref[...], preferred_element_type=F32)
    act = ((g * _sigmoid(g)) * u).astype(BF16)
    acc_ref[...] += jnp.dot(act, wd_ref[...], preferred_element_type=F32)

    @pl.when(j == pl.num_programs(1) - 1)
    def _():
        z = DEEPNORM_ALPHA * x_ref[...] + gate_ref[...] * acc_ref[...]
        o_ref[...] = _layer_norm_rows(z, lng_ref[...], lnb_ref[...])


def _ffn(x, mod, w_gu, w_down, lng, lnb, *, n_ctx, t_s):
    m = x.shape[0]
    tm = 512
    tf = D_FF // 2
    nf = D_FF // tf
    const = lambda arr: pl.BlockSpec(arr.shape, lambda i, j: (0,) * arr.ndim)
    row = pl.BlockSpec((tm, D_MODEL), lambda i, j: (i, 0))
    return pl.pallas_call(
        _ffn_kernel,
        out_shape=jax.ShapeDtypeStruct((m, D_MODEL), F32),
        grid=(m // tm, nf),
        in_specs=[row, _mod_spec(4, tm, n_ctx, t_s), _mod_spec(3, tm, n_ctx, t_s), _mod_spec(5, tm, n_ctx, t_s),
                  pl.BlockSpec((D_MODEL, tf), lambda i, j: (0, j)),
                  pl.BlockSpec((D_MODEL, tf), lambda i, j: (0, nf + j)),
                  pl.BlockSpec((tf, D_MODEL), lambda i, j: (j, 0)),
                  const(lng), const(lnb)],
        out_specs=row,
        scratch_shapes=[pltpu.VMEM((tm, D_MODEL), BF16), pltpu.VMEM((tm, D_MODEL), F32)],
        compiler_params=_cparams("parallel", "arbitrary"),
        name="ffn",
    )(x, mod, mod, mod, w_gu, w_gu, w_down, lng, lnb)


ROW_TILES = D_MODEL // LANES


def _to_token_rows(ref, base, value):
    n = value.shape[0]
    for j in range(ROW_TILES):
        ref[pl.ds(base + j, n, stride=ROW_TILES), :] = value[:, j * LANES:(j + 1) * LANES]


def _from_token_rows(ref, base, n):
    return jnp.concatenate([ref[pl.ds(base + j, n, stride=ROW_TILES), :] for j in range(ROW_TILES)], axis=1)


def _token_row(ref, r):
    start = r * ROW_TILES if isinstance(r, int) else pl.multiple_of(r * ROW_TILES, ROW_TILES)
    return ref.at[pl.ds(start, ROW_TILES)]


def _route_kernel(x_ref, sc_ref, sh_ref, wr_ref, route_ref, gates_ref, cnt_ref, carry_ref, *, tm):
    @pl.when(pl.program_id(0) == 0)
    def _():
        carry_ref[...] = jnp.zeros_like(carry_ref)

    lane = lax.broadcasted_iota(jnp.int32, (tm, LANES), 1)
    h = x_ref[...] * (1.0 + sc_ref[...]) + sh_ref[...]
    logits = jnp.dot(h.astype(BF16), wr_ref[...], preferred_element_type=F32)
    s1 = jnp.where(lane < N_EXPERTS, logits, -jnp.inf)
    m1 = jnp.max(s1, axis=-1, keepdims=True)
    i1 = jnp.min(jnp.where(s1 == m1, lane, LANES), axis=-1, keepdims=True)
    s2 = jnp.where(lane == i1, -jnp.inf, s1)
    m2 = jnp.max(s2, axis=-1, keepdims=True)
    i2 = jnp.min(jnp.where(s2 == m2, lane, LANES), axis=-1, keepdims=True)
    e2 = jnp.exp(m2 - m1)
    den = 1.0 + e2
    hit = jnp.where((lane == i1) | (lane == i2), 1.0, 0.0)
    earlier = lax.broadcasted_iota(jnp.int32, (tm, tm), 1) < lax.broadcasted_iota(jnp.int32, (tm, tm), 0)
    before = carry_ref[...] + jnp.dot(jnp.where(earlier, 1.0, 0.0).astype(BF16), hit.astype(BF16),
                                      preferred_element_type=F32)
    r1 = jnp.sum(jnp.where(lane == i1, before, 0.0), axis=-1, keepdims=True).astype(jnp.int32)
    r2 = jnp.sum(jnp.where(lane == i2, before, 0.0), axis=-1, keepdims=True).astype(jnp.int32)
    carry_ref[...] += jnp.sum(hit, axis=0, keepdims=True)
    route = jnp.where(lane == 0, i1, jnp.where(lane == 1, i2, jnp.where(lane == 2, r1, jnp.where(lane == 3, r2, 0))))
    route_ref[...] = route[:, :ROUTE_COLS]
    gates_ref[...] = jnp.where(lane == 0, 1.0 / den, jnp.where(lane == 1, e2 / den, 0.0))[:, :ROUTE_COLS]
    cnt_ref[...] = jnp.broadcast_to(carry_ref[...], cnt_ref.shape)


def _moe_route(x, mod, w_router, *, n_ctx, t_s):
    m = x.shape[0]
    tm = 512
    small = pl.BlockSpec((tm, ROUTE_COLS), lambda i: (i, 0))
    return pl.pallas_call(
        functools.partial(_route_kernel, tm=tm),
        out_shape=(jax.ShapeDtypeStruct((m, ROUTE_COLS), jnp.int32), jax.ShapeDtypeStruct((m, ROUTE_COLS), F32),
                   jax.ShapeDtypeStruct((SUBLANES, LANES), F32)),
        grid=(m // tm,),
        in_specs=[pl.BlockSpec((tm, D_MODEL), lambda i: (i, 0)), _mod_spec(4, tm, n_ctx, t_s),
                  _mod_spec(3, tm, n_ctx, t_s), pl.BlockSpec(w_router.shape, lambda i: (0, 0))],
        out_specs=(small, small, pl.BlockSpec((SUBLANES, LANES), lambda i: (0, 0))),
        scratch_shapes=[pltpu.VMEM((1, LANES), F32)],
        compiler_params=_cparams("arbitrary"),
        name="moe_route",
    )(x, mod, mod, w_router)


def _dispatch_kernel(slot_ref, x_ref, sc_ref, sh_ref, xs_hbm, buf, sem, *, tok):
    i = pl.program_id(0)
    par = i % 2
    cur = buf.at[par]
    _to_token_rows(cur, 0, x_ref[...] * (1.0 + sc_ref[...]) + sh_ref[...])

    def issue(t, c):
        src = _token_row(cur, t)
        pltpu.make_async_copy(src, _token_row(xs_hbm, slot_ref[2 * t]), sem.at[par]).start()
        pltpu.make_async_copy(src, _token_row(xs_hbm, slot_ref[2 * t + 1]), sem.at[par]).start()
        return c
    lax.fori_loop(0, tok, issue, 0, unroll=4)

    def drain(which):
        def body(t, c):
            pltpu.make_async_copy(_token_row(buf.at[which], 0), _token_row(xs_hbm, 0), sem.at[which]).wait()
            return c
        lax.fori_loop(0, 2 * tok, body, 0, unroll=8)

    @pl.when(i > 0)
    def _():
        drain(1 - par)

    @pl.when(i == pl.num_programs(0) - 1)
    def _():
        drain(par)


def _moe_dispatch(slots, x, mod, *, n_ctx, t_s):
    m = x.shape[0]
    tok = 512
    return pl.pallas_call(
        functools.partial(_dispatch_kernel, tok=tok),
        out_shape=jax.ShapeDtypeStruct((2 * m * ROW_TILES, LANES), F32),
        grid=(m // tok,),
        in_specs=[pl.BlockSpec((2 * tok,), lambda i: (i,), memory_space=pltpu.SMEM),
                  pl.BlockSpec((tok, D_MODEL), lambda i: (i, 0)),
                  _mod_spec(4, tok, n_ctx, t_s), _mod_spec(3, tok, n_ctx, t_s)],
        out_specs=pl.BlockSpec(memory_space=pl.ANY),
        scratch_shapes=[pltpu.VMEM((2, tok * ROW_TILES, LANES), F32), pltpu.SemaphoreType.DMA((2,))],
        compiler_params=_cparams("arbitrary"),
        name="moe_dispatch",
    )(slots, x, mod, mod)


def _experts_kernel(wt_ref, we_ref, lo_ref, hi_ref, nw_ref, xs_ref, wg_ref, wu_ref, wd_ref, o_ref, *, rows):
    del we_ref
    w = pl.program_id(0)

    @pl.when(w < nw_ref[0])
    def _():
        xs = _from_token_rows(xs_ref, 0, rows).astype(BF16)
        g = jnp.dot(xs, wg_ref[...], preferred_element_type=F32)
        u = jnp.dot(xs, wu_ref[...], preferred_element_type=F32)
        act = ((g * _sigmoid(g)) * u).astype(BF16)
        y = jnp.dot(act, wd_ref[...], preferred_element_type=F32)
        whole = jnp.logical_and(lo_ref[w] == 0, hi_ref[w] == rows)
        first = jnp.logical_or(w == 0, wt_ref[jnp.maximum(w - 1, 0)] != wt_ref[w])

        @pl.when(whole)
        def _():
            _to_token_rows(o_ref, 0, y)

        @pl.when(jnp.logical_not(whole))
        def _():
            row = lax.broadcasted_iota(jnp.int32, (rows, D_MODEL), 0)
            mine = (row >= lo_ref[w]) & (row < hi_ref[w])

            @pl.when(first)
            def _():
                _to_token_rows(o_ref, 0, jnp.where(mine, y, 0.0))

            @pl.when(jnp.logical_not(first))
            def _():
                _to_token_rows(o_ref, 0, jnp.where(mine, y, _from_token_rows(o_ref, 0, rows)))


def _moe_experts(items, xs, w_gu, w_down, layer):
    rows = MOE_TILE
    n_items = items[0].shape[0]
    tile = lambda w, wt, we, lo, hi, nw: (wt[w], 0)
    return pl.pallas_call(
        functools.partial(_experts_kernel, rows=rows),
        out_shape=jax.ShapeDtypeStruct(xs.shape, F32),
        grid_spec=pltpu.PrefetchScalarGridSpec(
            num_scalar_prefetch=5, grid=(n_items,),
            in_specs=[pl.BlockSpec((rows * ROW_TILES, LANES), tile),
                      pl.BlockSpec((None, None, D_MODEL, D_FF_EXPERT),
                                   lambda w, wt, we, lo, hi, nw: (layer, we[w], 0, 0)),
                      pl.BlockSpec((None, None, D_MODEL, D_FF_EXPERT),
                                   lambda w, wt, we, lo, hi, nw: (layer, we[w], 0, 1)),
                      pl.BlockSpec((None, None, D_FF_EXPERT, D_MODEL),
                                   lambda w, wt, we, lo, hi, nw: (layer, we[w], 0, 0))],
            out_specs=pl.BlockSpec((rows * ROW_TILES, LANES), tile)),
        compiler_params=_cparams("arbitrary"),
        name="moe_experts",
    )(*items, xs, w_gu, w_gu, w_down)


def _combine_kernel(idx_ref, nxt_ref, ys_hbm, gates_ref, x_ref, gate_ref, lng_ref, lnb_ref, o_ref, buf, sem, *, tm):
    i = pl.program_id(0)
    slot = i % 2

    def issue(ref, s):
        def body(t, c):
            pltpu.make_async_copy(_token_row(ys_hbm, ref[2 * t]), _token_row(buf.at[s], t), sem.at[s]).start()
            pltpu.make_async_copy(_token_row(ys_hbm, ref[2 * t + 1]), _token_row(buf.at[s], tm + t), sem.at[s]).start()
            return c
        lax.fori_loop(0, tm, body, 0, unroll=4)

    @pl.when(i == 0)
    def _():
        issue(idx_ref, 0)

    @pl.when(i + 1 < pl.num_programs(0))
    def _():
        issue(nxt_ref, 1 - slot)

    def drain(r, c):
        pltpu.make_async_copy(_token_row(ys_hbm, 0), _token_row(buf.at[slot], 0), sem.at[slot]).wait()
        return c
    lax.fori_loop(0, 2 * tm, drain, 0, unroll=8)
    cur = buf.at[slot]
    y = (gates_ref[:, 0:1] * _from_token_rows(cur, 0, tm)
         + gates_ref[:, 1:2] * _from_token_rows(cur, tm * ROW_TILES, tm))
    z = DEEPNORM_ALPHA * x_ref[...] + gate_ref[...] * y
    o_ref[...] = _layer_norm_rows(z, lng_ref[...], lnb_ref[...])


def _moe_combine(slots, ys, gates, x, mod, lng, lnb, *, n_ctx, t_s):
    m = x.shape[0]
    tm = 256
    n_blocks = m // tm
    idx = lambda f: pl.BlockSpec((2 * tm,), f, memory_space=pltpu.SMEM)
    const = lambda arr: pl.BlockSpec(arr.shape, lambda i: (0,) * arr.ndim)
    row = pl.BlockSpec((tm, D_MODEL), lambda i: (i, 0))
    return pl.pallas_call(
        functools.partial(_combine_kernel, tm=tm),
        out_shape=jax.ShapeDtypeStruct((m, D_MODEL), F32),
        grid=(n_blocks,),
        in_specs=[idx(lambda i: (i,)), idx(lambda i: (jnp.minimum(i + 1, n_blocks - 1),)),
                  pl.BlockSpec(memory_space=pl.ANY), pl.BlockSpec((tm, ROUTE_COLS), lambda i: (i, 0)),
                  row, _mod_spec(5, tm, n_ctx, t_s), const(lng), const(lnb)],
        out_specs=row,
        scratch_shapes=[pltpu.VMEM((2, 2 * tm * ROW_TILES, LANES), F32), pltpu.SemaphoreType.DMA((2,))],
        compiler_params=_cparams("arbitrary"),
        name="moe_combine",
    )(slots, slots, ys, gates, x, mod, lng, lnb)


def _moe_work_items(counts, n_rows):
    n_tiles = n_rows // MOE_TILE
    n_items = n_tiles + N_EXPERTS - 1
    g_end = jnp.cumsum(counts)
    g_start = g_end - counts
    row0 = jnp.arange(n_tiles, dtype=jnp.int32) * MOE_TILE
    first_e = jnp.sum(row0[:, None] >= g_end[None, :], axis=1)
    last_e = jnp.sum((row0 + MOE_TILE - 1)[:, None] >= g_end[None, :], axis=1)
    per_tile = last_e - first_e + 1
    item0 = jnp.cumsum(per_tile) - per_tile
    w = jnp.arange(n_items, dtype=jnp.int32)
    tile = jnp.clip(jnp.sum(w[:, None] >= item0[None, :], axis=1) - 1, 0, n_tiles - 1)
    expert = jnp.clip(first_e[tile] + w - item0[tile], 0, N_EXPERTS - 1)
    lo = jnp.clip(g_start[expert] - row0[tile], 0, MOE_TILE)
    hi = jnp.clip(g_end[expert] - row0[tile], 0, MOE_TILE)
    n_used = jnp.sum(per_tile).reshape(1)
    i32 = lambda a: a.astype(jnp.int32)
    return i32(tile), i32(expert), i32(lo), i32(hi), i32(n_used), g_start


def _moe(x, mod, w_router, w_gu, w_down, layer, lng, lnb, *, n_ctx, t_s):
    m = x.shape[0]
    assert (2 * m) % MOE_TILE == 0
    route, gates, cnt = _moe_route(x, mod, w_router, n_ctx=n_ctx, t_s=t_s)
    counts = cnt[0, :N_EXPERTS].astype(jnp.int32)
    *items, g_start = _moe_work_items(counts, 2 * m)
    slots = (g_start[route[:, 0:2]] + route[:, 2:4]).reshape(2 * m).astype(jnp.int32)
    xs = _moe_dispatch(slots, x, mod, n_ctx=n_ctx, t_s=t_s)
    ys = _moe_experts(items, xs, w_gu, w_down, layer)
    return _moe_combine(slots, ys, gates, x, mod, lng, lnb, n_ctx=n_ctx, t_s=t_s)


def _rope_tables(n_tokens):
    n_rows = n_tokens // GRID_W
    row = jnp.repeat(jnp.arange(n_rows, dtype=F32), GRID_W)
    col = jnp.tile(jnp.arange(GRID_W, dtype=F32), n_rows)
    n_axis = HEAD_DIM // 4
    inv = ROPE_THETA ** (-jnp.arange(n_axis, dtype=F32) / n_axis)
    ang = jnp.concatenate([row[:, None] * inv, col[:, None] * inv], axis=-1)
    cos, sin = jnp.cos(ang), jnp.sin(ang)
    reps = LANES // HEAD_DIM
    cos_t = jnp.tile(jnp.concatenate([cos, cos], axis=-1), (1, reps))
    sin_t = jnp.tile(jnp.concatenate([-sin, sin], axis=-1), (1, reps))
    return cos_t, sin_t


def _gqa_head_perm():
    cols = []
    per_pair = 2 * (N_Q_HEADS // N_KV_HEADS)
    for p in range(N_KV_HEADS // 2):
        for j in range(N_Q_HEADS // N_KV_HEADS):
            for hd in (per_pair * p + j, per_pair * p + per_pair // 2 + j):
                cols.append(np.arange(hd * HEAD_DIM, (hd + 1) * HEAD_DIM))
    return np.concatenate(cols)


def _block_diag_windows(w):
    eye = jnp.eye(N_RNN_BLOCKS, dtype=bool)[:, None, :, None]
    dense = jnp.where(eye, w[:, :, None, :], 0.0).reshape(D_RNN, D_RNN)
    for ct, k0 in enumerate(GATE_K0):
        lo, hi = ct * GATE_N, (ct + 1) * GATE_N
        assert k0 <= RNN_BLOCK * (lo // RNN_BLOCK) and RNN_BLOCK * -(-hi // RNN_BLOCK) <= k0 + GATE_K <= D_RNN
    return jnp.stack([dense[k0:k0 + GATE_K, ct * GATE_N:(ct + 1) * GATE_N] for ct, k0 in enumerate(GATE_K0)])


def kernel(x_prompt, x_sample, cache_attn_k, cache_attn_v, state_rglru, cache_diff_k, cache_diff_v, c, c_ctx, w_ada, b_ada, ln_g, ln_b, attn_w_qkv, attn_g_q, attn_g_k, attn_w_o, rnn_w_in, rnn_conv_w, rnn_conv_b, rnn_w_a, rnn_b_a, rnn_w_x, rnn_b_x, rnn_lambda, rnn_w_out, diff_w_qkv, diff_lambda, diff_g_sub, diff_w_o, ffn_w_gu, ffn_w_down, moe_w_router, moe_w_gu, moe_w_down):
    b_c, t_c, d = x_prompt.shape
    b_s, t_s, _ = x_sample.shape
    n_ctx = b_c * t_c
    n_s = b_s * t_s
    past = cache_attn_k.shape[2]
    assert d == D_MODEL and n_ctx % t_s == 0 and t_c == SCAN_CHUNK and t_s % SCAN_CHUNK == 0
    assert 1 + b_s <= COND_ROWS

    x = jnp.concatenate([x_prompt.reshape(n_ctx, d), x_sample.reshape(n_s, d)], axis=0)
    cond = jnp.zeros((COND_ROWS, d), F32).at[0].set(c_ctx).at[1:1 + b_s].set(c)
    mods = _ada_table(cond, w_ada, b_ada)[:, :1 + b_s].reshape(DEPTH, 1 + b_s, 1, 6 * d)

    cos_t, sin_t = _rope_tables(t_s)
    perm = _gqa_head_perm()
    nq = N_Q_HEADS * HEAD_DIM
    tile2 = lambda g: jnp.tile(g, LANES // g.shape[-1]).reshape(1, LANES)
    kw = dict(n_ctx=n_ctx, t_s=t_s)
    moe_gu_b = moe_w_gu.astype(BF16)
    moe_down_b = moe_w_down.astype(BF16)

    attn_k, attn_v, rnn_s, diff_k, diff_v = [], [], [], [], []
    for li in range(DEPTH):
        mod = mods[li]
        lng = ln_g[li].reshape(2, 1, d)
        lnb = ln_b[li].reshape(2, 1, d)
        j = li // N_MIXERS
        kind = li % N_MIXERS
        if kind == 0:
            w = attn_w_qkv[j]
            w = jnp.concatenate([w[:, :nq][:, perm], w[:, nq:]], axis=1).astype(BF16)
            q, kb, vb, kf, vf = _gqa_qkv(x, mod, w, tile2(attn_g_q[j]), tile2(attn_g_k[j]), cos_t, sin_t, **kw)
            ck = cache_attn_k[:, j].reshape(b_s, past, N_KV_HEADS * HEAD_DIM).astype(BF16)
            cv = cache_attn_v[:, j].reshape(b_s, past, N_KV_HEADS * HEAD_DIM).astype(BF16)
            o_c = _attention(q, kb, vb, None, None, row0=0, n_b=b_c, t=t_c, tq=t_c)
            o_s = _attention(q, kb, vb, ck, cv, row0=n_ctx, n_b=b_s, t=t_s, tq=512)
            x = _oproj(o_c, o_s, attn_w_o[j][perm, :].astype(BF16), x, mod, lng[0], lnb[0], **kw)
            attn_k.append(kf[:n_ctx])
            attn_v.append(vf[:n_ctx])
        elif kind == 1:
            bx = _rnn_in(x, mod, rnn_w_in[j].astype(BF16), **kw)
            wg = jnp.stack([_block_diag_windows(rnn_w_a[j, 0]), _block_diag_windows(rnn_w_x[j, 0]),
                            _block_diag_windows(rnn_w_a[j, 1]), _block_diag_windows(rnn_w_x[j, 1])]).astype(BF16)
            bg = jnp.concatenate([rnn_b_a[j, 0], rnn_b_x[j, 0], rnn_b_a[j, 1], rnn_b_x[j, 1]]).reshape(1, -1)
            a_f, u_f, a_b, u_b = _rnn_gates(bx, rnn_conv_w[j], rnn_conv_b[j].reshape(1, -1), wg, bg,
                                            rnn_lambda[j], n_ctx=n_ctx, t_c=t_c, t_s=t_s)
            h0 = jnp.concatenate([jnp.zeros((b_c, 2, D_RNN), F32), state_rglru[:, j]], axis=0)
            hs_f, hs_b, fin = _rnn_scan(a_f, u_f, a_b, u_b, h0, seq_lens=[t_c] * b_c + [t_s] * b_s)
            x = _rnn_oproj(hs_f, hs_b, bx, rnn_w_out[j].astype(BF16), x, mod, lng[0], lnb[0], **kw)
            rnn_s.append(fin[:b_c])
        else:
            lam_init = 0.8 - 0.6 * math.exp(-0.3 * li)
            q, kb, vb, kf, vf = _diff_qkv(x, mod, diff_w_qkv[j].astype(BF16), cos_t, sin_t, **kw)
            ck = cache_diff_k[:, j].reshape(b_s, past, D_MODEL).astype(BF16)
            cv = cache_diff_v[:, j].reshape(b_s, past, D_MODEL).astype(BF16)
            g_sub = diff_g_sub[j].reshape(1, LANES)
            dargs = (diff_lambda[j], g_sub, lam_init)
            o_c = _attention(q, kb, vb, None, None, row0=0, n_b=b_c, t=t_c, tq=t_c, diff=dargs)
            o_s = _attention(q, kb, vb, ck, cv, row0=n_ctx, n_b=b_s, t=t_s, tq=256, diff=dargs)
            x = _oproj(o_c, o_s, diff_w_o[j].astype(BF16), x, mod, lng[0], lnb[0], **kw)
            diff_k.append(kf[:n_ctx])
            diff_v.append(vf[:n_ctx])
        if li % 2 == 0:
            x = _ffn(x, mod, ffn_w_gu[li // 2].astype(BF16), ffn_w_down[li // 2].astype(BF16), lng[1], lnb[1], **kw)
        else:
            w_r = jnp.zeros((d, LANES), F32).at[:, :N_EXPERTS].set(moe_w_router[li // 2]).astype(BF16)
            x = _moe(x, mod, w_r, moe_gu_b, moe_down_b, li // 2, lng[1], lnb[1], **kw)

    y_prompt = x[:n_ctx].reshape(b_c, t_c, d)
    y_sample = x[n_ctx:].reshape(b_s, t_s, d)
    new_attn_k = jnp.stack([k.reshape(b_c, t_c, N_KV_HEADS, HEAD_DIM) for k in attn_k], axis=1)
    new_attn_v = jnp.stack([v.reshape(b_c, t_c, N_KV_HEADS, HEAD_DIM) for v in attn_v], axis=1)
    new_state = jnp.stack(rnn_s, axis=1)
    new_diff_k = jnp.stack([k.reshape(b_c, t_c, N_DIFF_HEADS, 2, HEAD_DIM) for k in diff_k], axis=1)
    new_diff_v = jnp.stack([v.reshape(b_c, t_c, N_DIFF_HEADS, 2 * HEAD_DIM) for v in diff_v], axis=1)
    return (y_prompt, y_sample, new_attn_k, new_attn_v, new_state, new_diff_k, new_diff_v)
```

```python
import functools
import math

import jax
import jax.numpy as jnp
import numpy as np
from jax import lax
from jax.experimental import pallas as pl
from jax.experimental.pallas import tpu as pltpu

F32 = jnp.float32
BF16 = jnp.bfloat16

D_MODEL = 1024
DEPTH = 4
GRID_W = 64
HEAD_DIM = 64
N_Q_HEADS = 16
N_KV_HEADS = 4
ROPE_THETA = 10000.0
N_DIFF_HEADS = 8
D_RNN = 1280
N_RNN_BLOCKS = 16
RNN_BLOCK = 80
CONV_W = 4
CONV_LEFT = 2
RGLRU_C = 8.0
D_FF = 2816
N_EXPERTS = 8
D_FF_EXPERT = 1408
N_MIXERS = 3
DEEPNORM_ALPHA = (2.0 * DEPTH) ** 0.25
LN_EPS = 1e-6
RMS_EPS = 1e-6

LANES = 128
SUBLANES = 8
VMEM_LIMIT = 56 * 1024 * 1024
COND_ROWS = 8
SCAN_CHUNK = 256
MOE_TILE = 512
ROUTE_COLS = 8
GATE_N = 256
GATE_K = 512
GATE_K0 = (0, 128, 384, 640, 768)
Q_SCALE = HEAD_DIM ** -0.5 * math.log2(math.e)


def _cparams(*sem):
    return pltpu.CompilerParams(dimension_semantics=sem, vmem_limit_bytes=VMEM_LIMIT)


def _sigmoid(x):
    return 0.5 * jnp.tanh(0.5 * x) + 0.5


def _layer_norm_rows(z, g, b):
    mu = jnp.mean(z, axis=-1, keepdims=True)
    zc = z - mu
    var = jnp.mean(zc * zc, axis=-1, keepdims=True)
    return zc * lax.rsqrt(var + LN_EPS) * g + b


def _seg_of_block(i, tm, n_ctx, t_s):
    r0 = i * tm
    return jnp.where(r0 < n_ctx, 0, 1 + (r0 - n_ctx) // t_s)


def _mod_spec(col, tm, n_ctx, t_s):
    return pl.BlockSpec((None, 1, D_MODEL), lambda i, *_: (_seg_of_block(i, tm, n_ctx, t_s), 0, col))


def _ada_kernel(c_ref, w_ref, b_ref, o_ref):
    c = c_ref[...]
    a = (c * _sigmoid(c)).astype(BF16)
    o_ref[...] = jnp.dot(a, w_ref[...].astype(BF16), preferred_element_type=F32) + b_ref[...]


def _ada_table(cond, w_ada, b_ada):
    n_l, d, n = w_ada.shape
    tn = 1536
    return pl.pallas_call(
        _ada_kernel,
        out_shape=jax.ShapeDtypeStruct((n_l, COND_ROWS, n), F32),
        grid=(n_l, n // tn),
        in_specs=[pl.BlockSpec((COND_ROWS, d), lambda l, j: (0, 0)),
                  pl.BlockSpec((None, d, tn), lambda l, j: (l, 0, j)),
                  pl.BlockSpec((None, 1, tn), lambda l, j: (l, 0, j))],
        out_specs=pl.BlockSpec((None, COND_ROWS, tn), lambda l, j: (l, 0, j)),
        compiler_params=_cparams("parallel", "parallel"),
        name="ada_table",
    )(cond, w_ada, b_ada.reshape(n_l, 1, n))


def _head_masks(tm):
    lane = lax.broadcasted_iota(jnp.int32, (tm, LANES), 1)
    return lane < HEAD_DIM, (lane & (HEAD_DIM - 1)) < HEAD_DIM // 2


def _rope_tile(t, first_half, cos, sin):
    partner = jnp.where(first_half, pltpu.roll(t, LANES - HEAD_DIM // 2, 1),
                        pltpu.roll(t, HEAD_DIM // 2, 1))
    return t * cos + partner * sin


def _gqa_qkv_kernel(x_ref, sc_ref, sh_ref, w_ref, gk_ref, cos_ref, sin_ref,
                    q_ref, kb_ref, vb_ref, kf_ref, vf_ref, *, tm, n_ctx):
    i = pl.program_id(0)
    h = (x_ref[...] * (1.0 + sc_ref[...]) + sh_ref[...]).astype(BF16)
    y = jnp.dot(h, w_ref[...], preferred_element_type=F32)
    use_rope = i * tm >= n_ctx
    cos = jnp.where(use_rope, cos_ref[...], 1.0)
    sin = jnp.where(use_rope, sin_ref[...], 0.0)
    low_head, first_half = _head_masks(tm)

    def norm_rope(t, g):
        t2 = t * t
        s_lo = jnp.sum(jnp.where(low_head, t2, 0.0), axis=-1, keepdims=True)
        s_hi = jnp.sum(jnp.where(low_head, 0.0, t2), axis=-1, keepdims=True)
        inv = jnp.where(low_head, lax.rsqrt(s_lo * (1.0 / HEAD_DIM) + RMS_EPS),
                        lax.rsqrt(s_hi * (1.0 / HEAD_DIM) + RMS_EPS))
        return _rope_tile(t * inv * g, first_half, cos, sin)

    nq = N_Q_HEADS * HEAD_DIM
    nkv = N_KV_HEADS * HEAD_DIM
    gk = gk_ref[...]
    q_ref[...] = y[:, :nq].astype(BF16)
    for t in range(nkv // LANES):
        sl = slice(t * LANES, (t + 1) * LANES)
        k = norm_rope(y[:, nq + t * LANES: nq + (t + 1) * LANES], gk)
        kf_ref[:, sl] = k
        kb_ref[:, sl] = k.astype(BF16)
    v = y[:, nq + nkv:]
    vf_ref[...] = v
    vb_ref[...] = v.astype(BF16)


def _gqa_qkv(x, mod, w, gk, cos, sin, *, n_ctx, t_s):
    m = x.shape[0]
    tm = 512
    nq = N_Q_HEADS * HEAD_DIM
    nkv = N_KV_HEADS * HEAD_DIM
    n_rope_blocks = t_s // tm
    rope_spec = pl.BlockSpec(
        (tm, LANES), lambda i: (jnp.where(i * tm >= n_ctx, ((i * tm - n_ctx) % t_s) // tm, 0) % n_rope_blocks, 0))
    row = lambda n: pl.BlockSpec((tm, n), lambda i: (i, 0))
    const = lambda a: pl.BlockSpec(a.shape, lambda i: (0,) * a.ndim)
    return pl.pallas_call(
        functools.partial(_gqa_qkv_kernel, tm=tm, n_ctx=n_ctx),
        out_shape=(jax.ShapeDtypeStruct((m, nq), BF16), jax.ShapeDtypeStruct((m, nkv), BF16),
                   jax.ShapeDtypeStruct((m, nkv), BF16), jax.ShapeDtypeStruct((m, nkv), F32),
                   jax.ShapeDtypeStruct((m, nkv), F32)),
        grid=(m // tm,),
        in_specs=[row(D_MODEL), _mod_spec(1, tm, n_ctx, t_s), _mod_spec(0, tm, n_ctx, t_s),
                  const(w), const(gk), rope_spec, rope_spec],
        out_specs=(row(nq), row(nkv), row(nkv), row(nkv), row(nkv)),
        compiler_params=_cparams("parallel"),
        name="gqa_qkv",
    )(x, mod, mod, w, gk, cos, sin)


def _diff_qkv_kernel(x_ref, sc_ref, sh_ref, w_ref, cos_ref, sin_ref,
                     q_ref, kb_ref, vb_ref, kf_ref, vf_ref, *, tm, n_ctx):
    i = pl.program_id(0)
    h = (x_ref[...] * (1.0 + sc_ref[...]) + sh_ref[...]).astype(BF16)
    y = jnp.dot(h, w_ref[...], preferred_element_type=F32)
    use_rope = i * tm >= n_ctx
    cos = jnp.where(use_rope, cos_ref[...], 1.0)
    sin = jnp.where(use_rope, sin_ref[...], 0.0)
    _, first_half = _head_masks(tm)
    q_ref[...] = y[:, :D_MODEL].astype(BF16)
    for t in range(D_MODEL // LANES):
        sl = slice(t * LANES, (t + 1) * LANES)
        k = _rope_tile(y[:, D_MODEL + t * LANES: D_MODEL + (t + 1) * LANES], first_half, cos, sin)
        kf_ref[:, sl] = k
        kb_ref[:, sl] = k.astype(BF16)
    v = y[:, 2 * D_MODEL:]
    vf_ref[...] = v
    vb_ref[...] = v.astype(BF16)


def _diff_qkv(x, mod, w, cos, sin, *, n_ctx, t_s):
    m = x.shape[0]
    tm = 256
    n_rope_blocks = t_s // tm
    rope_spec = pl.BlockSpec(
        (tm, LANES), lambda i: (jnp.where(i * tm >= n_ctx, ((i * tm - n_ctx) % t_s) // tm, 0) % n_rope_blocks, 0))
    row = pl.BlockSpec((tm, D_MODEL), lambda i: (i, 0))
    return pl.pallas_call(
        functools.partial(_diff_qkv_kernel, tm=tm, n_ctx=n_ctx),
        out_shape=(jax.ShapeDtypeStruct((m, D_MODEL), BF16), jax.ShapeDtypeStruct((m, D_MODEL), BF16),
                   jax.ShapeDtypeStruct((m, D_MODEL), BF16), jax.ShapeDtypeStruct((m, D_MODEL), F32),
                   jax.ShapeDtypeStruct((m, D_MODEL), F32)),
        grid=(m // tm,),
        in_specs=[row, _mod_spec(1, tm, n_ctx, t_s), _mod_spec(0, tm, n_ctx, t_s),
                  pl.BlockSpec(w.shape, lambda i: (0, 0)), rope_spec, rope_spec],
        out_specs=(row, row, row, row, row),
        compiler_params=_cparams("parallel"),
        name="diff_qkv",
    )(x, mod, mod, w, cos, sin)


ATT_GROUP = 4
ATT_KCH = 512
ATT_UNROLL = 3


def _attn_kernel(*refs, diff, rope, has_cache, tq, kch, n_own, n_cache, lam_init):
    refs = list(refs)
    if diff:
        lam_ref, g_ref = refs[:2]
        refs = refs[2:]
    else:
        gq_ref = refs[0]
        refs = refs[1:]
    if rope:
        cos_ref, sin_ref = refs[:2]
        refs = refs[2:]
    if has_cache:
        q_ref, k_ref, v_ref, ck_ref, cv_ref, o_ref, k_scr, vt_scr, s_scr = refs
    else:
        q_ref, k_ref, v_ref, o_ref, k_scr, vt_scr, s_scr = refs
    nch = n_own + n_cache
    n_kv = k_ref.shape[1] // LANES
    v_rows = LANES if diff else HEAD_DIM
    ones = jnp.where(lax.broadcasted_iota(jnp.int32, (SUBLANES, kch), 0) == 0, 1.0, 0.0).astype(BF16)

    @pl.when(pl.program_id(2) == 0)
    def _():
        for t in range(n_kv):
            lanes = slice(t * LANES, (t + 1) * LANES)
            for c in range(nch):
                if c < n_own:
                    rows = slice(c * kch, (c + 1) * kch)
                    kc, vc = k_ref[rows, lanes], v_ref[rows, lanes]
                else:
                    rows = slice((c - n_own) * kch, (c - n_own + 1) * kch)
                    kc, vc = ck_ref[rows, lanes], cv_ref[rows, lanes]
                k_scr[t, c] = kc
                vt = vc.T
                if diff:
                    vt_scr[t, c, 0] = jnp.concatenate([vt, ones], axis=0)
                else:
                    vt_scr[t, c, 0] = jnp.concatenate([vt[:HEAD_DIM], ones], axis=0)
                    vt_scr[t, c, 1] = jnp.concatenate([vt[HEAD_DIM:], ones], axis=0)

    low_row = lax.broadcasted_iota(jnp.int32, (LANES, tq), 0) < HEAD_DIM
    n_tiles = q_ref.shape[1] // LANES
    kv_of = (lambda j: j) if diff else (lambda j: 0)
    half, quarter = HEAD_DIM, HEAD_DIM // 2

    def weights(j):
        t = q_ref[:, j * LANES:(j + 1) * LANES].astype(F32).T
        if not diff:
            inv = [lax.rsqrt(jnp.mean(h * h, axis=0, keepdims=True) + RMS_EPS) for h in (t[:half], t[half:])]
            t = jnp.concatenate([t[:half] * inv[0], t[half:] * inv[1]], axis=0) * gq_ref[...]
        if rope:
            partner = jnp.concatenate([t[quarter:half], t[:quarter], t[half + quarter:], t[half:half + quarter]], axis=0)
            t = t * cos_ref[...] + partner * sin_ref[...]
        tb = (t * Q_SCALE).astype(BF16)
        zero = jnp.zeros_like(tb)
        return jnp.concatenate([jnp.where(low_row, tb, zero), jnp.where(low_row, zero, tb)], axis=1)

    def stage(j_a, j_b, mx_b):
        w = weights(j_a) if j_a is not None else None

        def body(c, carry):
            m8, accs = carry
            if j_a is not None:
                s = jnp.dot(k_scr[kv_of(j_a), c], w, preferred_element_type=F32)
                s_scr[j_a % 2, c] = s
                m8 = jnp.maximum(m8, jnp.max(s.reshape(kch // SUBLANES, SUBLANES, 2 * tq), axis=0))
            if j_b is not None:
                e = jnp.exp2(s_scr[j_b % 2, c] - mx_b).astype(BF16)
                if diff:
                    accs = (accs[0] + jnp.dot(vt_scr[kv_of(j_b), c, 0], e, preferred_element_type=F32),)
                else:
                    accs = (accs[0] + jnp.dot(vt_scr[0, c, 0], e[:, :tq], preferred_element_type=F32),
                            accs[1] + jnp.dot(vt_scr[0, c, 1], e[:, tq:], preferred_element_type=F32))
            return m8, accs

        acc_shape = (v_rows + SUBLANES, 2 * tq if diff else tq)
        init = (jnp.full((SUBLANES, 2 * tq), -jnp.inf, F32),
                tuple(jnp.zeros(acc_shape, F32) for _ in range(1 if diff else 2)))
        m8, accs = lax.fori_loop(0, nch, body, init, unroll=min(nch, ATT_UNROLL))
        return jnp.max(m8, axis=0, keepdims=True), accs

    if diff:
        lv = lam_ref[...]
        lam = (jnp.exp(jnp.sum(lv[0:1] * lv[1:2], axis=-1, keepdims=True))
               - jnp.exp(jnp.sum(lv[2:3] * lv[3:4], axis=-1, keepdims=True)) + lam_init)

    mx_prev = None
    for st in range(n_tiles + 1):
        j_a = st if st < n_tiles else None
        j_b = st - 1 if st >= 1 else None
        mx_new, accs = stage(j_a, j_b, mx_prev)
        if j_b is not None:
            if diff:
                a = accs[0]
                o_t = (a[:v_rows, :tq] / a[v_rows:v_rows + 1, :tq]
                       - lam * (a[:v_rows, tq:] / a[v_rows:v_rows + 1, tq:]))
                o = o_t.T
                inv = lax.rsqrt(jnp.mean(o * o, axis=-1, keepdims=True) + RMS_EPS)
                o = (o * inv * g_ref[...]) * (1.0 - lam_init)
            else:
                o_t = jnp.concatenate([a[:v_rows] / a[v_rows:v_rows + 1] for a in accs], axis=0)
                o = o_t.T
            o_ref[:, j_b * LANES:(j_b + 1) * LANES] = o.astype(BF16)
        mx_prev = mx_new


def _attention(q, k, v, cache_k, cache_v, *, row0, n_b, t, tq, gq=None, rope=None, diff=None):
    is_diff = diff is not None
    qw = ATT_GROUP * LANES
    n_groups = D_MODEL // qw
    kvw = qw if is_diff else LANES
    nq_blocks = t // tq
    has_cache = cache_k is not None
    past = cache_k.shape[1] if has_cache else 0
    kch = min(ATT_KCH, t)
    assert t % kch == 0 and past % kch == 0
    n_own, n_cache = t // kch, past // kch
    nch = n_own + n_cache
    q_spec = pl.BlockSpec((tq, qw), lambda b, p, iq: (row0 // tq + b * nq_blocks + iq, p))
    kv_spec = pl.BlockSpec((t, kvw), lambda b, p, iq: (row0 // t + b, p))
    in_specs, args = [], []
    const = lambda a: pl.BlockSpec(a.shape, lambda b, p, iq: (0,) * a.ndim)
    if is_diff:
        lam_vec, g_sub, lam_init = diff
        in_specs += [const(lam_vec), const(g_sub)]
        args += [lam_vec, g_sub]
    else:
        lam_init = 0.0
        in_specs.append(const(gq))
        args.append(gq)
    if rope is not None:
        rope_spec = pl.BlockSpec((LANES, tq), lambda b, p, iq: (0, iq))
        in_specs += [rope_spec, rope_spec]
        args += list(rope)
    in_specs += [q_spec, kv_spec, kv_spec]
    args += [q, k, v]
    if has_cache:
        c_spec = pl.BlockSpec((None, past, kvw), lambda b, p, iq: (b, 0, p))
        in_specs += [c_spec, c_spec]
        args += [cache_k, cache_v]
    n_kv = kvw // LANES
    v_rows = (LANES if is_diff else HEAD_DIM) + SUBLANES
    return pl.pallas_call(
        functools.partial(_attn_kernel, diff=is_diff, rope=rope is not None, has_cache=has_cache, tq=tq, kch=kch,
                          n_own=n_own, n_cache=n_cache, lam_init=lam_init),
        out_shape=jax.ShapeDtypeStruct((n_b * t, D_MODEL), BF16),
        grid=(n_b, n_groups, nq_blocks),
        in_specs=in_specs,
        out_specs=pl.BlockSpec((tq, qw), lambda b, p, iq: (b * nq_blocks + iq, p)),
        scratch_shapes=[pltpu.VMEM((n_kv, nch, kch, LANES), BF16),
                        pltpu.VMEM((n_kv, nch, 1 if is_diff else 2, v_rows, kch), BF16),
                        pltpu.VMEM((2, nch, kch, 2 * tq), F32)],
        compiler_params=_cparams("arbitrary", "arbitrary", "arbitrary"),
        name=("diff_attn" if is_diff else "gqa_attn") + ("_cache" if has_cache else ""),
    )(*args)


def _oproj_kernel(ac_ref, as_ref, w_ref, x_ref, gate_ref, lng_ref, lnb_ref, o_ref, *, n_ctx_blocks):
    is_ctx = pl.program_id(0) < n_ctx_blocks
    a = jnp.where(is_ctx, ac_ref[...], as_ref[...])
    out = jnp.dot(a, w_ref[...], preferred_element_type=F32)
    z = DEEPNORM_ALPHA * x_ref[...] + gate_ref[...] * out
    o_ref[...] = _layer_norm_rows(z, lng_ref[...], lnb_ref[...])


def _oproj(a_ctx, a_smp, w, x, mod, lng, lnb, *, n_ctx, t_s):
    m = x.shape[0]
    kdim = a_ctx.shape[1]
    tm = 512
    ncb = n_ctx // tm
    const = lambda arr: pl.BlockSpec(arr.shape, lambda i: (0,) * arr.ndim)
    row = pl.BlockSpec((tm, D_MODEL), lambda i: (i, 0))
    return pl.pallas_call(
        functools.partial(_oproj_kernel, n_ctx_blocks=ncb),
        out_shape=jax.ShapeDtypeStruct((m, D_MODEL), F32),
        grid=(m // tm,),
        in_specs=[pl.BlockSpec((tm, kdim), lambda i: (jnp.minimum(i, ncb - 1), 0)),
                  pl.BlockSpec((tm, kdim), lambda i: (jnp.maximum(i - ncb, 0), 0)),
                  const(w), row, _mod_spec(2, tm, n_ctx, t_s), const(lng), const(lnb)],
        out_specs=row,
        compiler_params=_cparams("parallel"),
        name="oproj_ln",
    )(a_ctx, a_smp, w, x, mod, lng, lnb)


def _gelu_tanh(x):
    return 0.5 * x * (1.0 + jnp.tanh(math.sqrt(2.0 / math.pi) * (x + 0.044715 * (x * x * x))))


def _rnn_oproj_kernel(hf_ref, hb_ref, br_ref, w_ref, x_ref, gate_ref, lng_ref, lnb_ref, o_ref):
    y = (hf_ref[...] + hb_ref[...]) * _gelu_tanh(br_ref[...])
    out = jnp.dot(y.astype(BF16), w_ref[...], preferred_element_type=F32)
    z = DEEPNORM_ALPHA * x_ref[...] + gate_ref[...] * out
    o_ref[...] = _layer_norm_rows(z, lng_ref[...], lnb_ref[...])


def _rnn_oproj(hs_f, hs_b, bx, w, x, mod, lng, lnb, *, n_ctx, t_s):
    m = x.shape[0]
    tm = 512
    const = lambda arr: pl.BlockSpec(arr.shape, lambda i: (0,) * arr.ndim)
    row = pl.BlockSpec((tm, D_MODEL), lambda i: (i, 0))
    rnn_row = pl.BlockSpec((tm, D_RNN), lambda i: (i, 0))
    return pl.pallas_call(
        _rnn_oproj_kernel,
        out_shape=jax.ShapeDtypeStruct((m, D_MODEL), F32),
        grid=(m // tm,),
        in_specs=[rnn_row, rnn_row, rnn_row, const(w), row,
                  _mod_spec(2, tm, n_ctx, t_s), const(lng), const(lnb)],
        out_specs=row,
        compiler_params=_cparams("parallel"),
        name="rnn_oproj_ln",
    )(hs_f, hs_b, bx, w, x, mod, lng, lnb)


def _mod_mm_kernel(x_ref, sc_ref, sh_ref, w_ref, o_ref):
    h = (x_ref[...] * (1.0 + sc_ref[...]) + sh_ref[...]).astype(BF16)
    o_ref[...] = jnp.dot(h, w_ref[...], preferred_element_type=F32)


def _rnn_in(x, mod, w, *, n_ctx, t_s):
    m = x.shape[0]
    n = w.shape[1]
    tm = 512
    return pl.pallas_call(
        _mod_mm_kernel,
        out_shape=jax.ShapeDtypeStruct((m, n), F32),
        grid=(m // tm,),
        in_specs=[pl.BlockSpec((tm, D_MODEL), lambda i: (i, 0)),
                  _mod_spec(1, tm, n_ctx, t_s), _mod_spec(0, tm, n_ctx, t_s),
                  pl.BlockSpec(w.shape, lambda i: (0, 0))],
        out_specs=pl.BlockSpec((tm, n), lambda i: (i, 0)),
        compiler_params=_cparams("parallel"),
        name="rnn_in",
    )(x, mod, mod, w)


def _rnn_gate_kernel(xb_ref, prev_ref, next_ref, cw_ref, cb_ref, wg_ref, bg_ref, lam_ref,
                     af_ref, uf_ref, ab_ref, ub_ref, ext_ref, *, tm, n_ctx, t_c, t_s):
    i = pl.program_id(0)
    r0 = i * tm
    pos = jnp.where(r0 < n_ctx, r0 % t_c, (r0 - n_ctx) % t_s)
    t_seq = jnp.where(r0 < n_ctx, t_c, t_s)
    at_start = pos == 0
    at_end = pos + tm == t_seq
    ext_ref[0:SUBLANES, :] = jnp.where(at_start, 0.0, prev_ref[...])
    ext_ref[SUBLANES:SUBLANES + tm, :] = xb_ref[...]
    ext_ref[SUBLANES + tm:, :] = jnp.where(at_end, 0.0, next_ref[...])
    xc = 0.0
    for j in range(CONV_W):
        off = SUBLANES - CONV_LEFT + j
        xc = xc + ext_ref[off:off + tm, :] * cw_ref[j:j + 1, :]
    xc = xc + cb_ref[...]
    xcb = xc.astype(BF16)
    neg_lam = -lam_ref[...]
    sp = jnp.maximum(neg_lam, 0.0) + jnp.log1p(jnp.exp(-jnp.abs(neg_lam)))

    def gate(g, ct, cols):
        k0 = GATE_K0[ct]
        pre = jnp.dot(xcb[:, k0:k0 + GATE_K], wg_ref[g, ct], preferred_element_type=F32)
        return _sigmoid(pre + bg_ref[:, g * D_RNN + cols.start:g * D_RNN + cols.stop])

    for z, (a_ref, u_ref) in enumerate(((af_ref, uf_ref), (ab_ref, ub_ref))):
        for ct in range(D_RNN // GATE_N):
            cols = slice(ct * GATE_N, (ct + 1) * GATE_N)
            r = gate(2 * z, ct, cols)
            g_in = gate(2 * z + 1, ct, cols)
            log_a = -RGLRU_C * r * sp[z:z + 1, cols]
            a = jnp.exp(log_a)
            a_ref[:, cols] = a
            u_ref[:, cols] = jnp.sqrt(-jnp.tanh(log_a) * (a * a + 1.0)) * (g_in * xc[:, cols])


def _rnn_gates(bx, conv_w, conv_b, wg, bg, lam, *, n_ctx, t_c, t_s):
    m = bx.shape[0]
    tm = 256
    n8 = tm // SUBLANES
    last8 = m // SUBLANES - 1
    const = lambda arr: pl.BlockSpec(arr.shape, lambda i: (0,) * arr.ndim)
    row = pl.BlockSpec((tm, D_RNN), lambda i: (i, 0))
    out = jax.ShapeDtypeStruct((m, D_RNN), F32)
    return pl.pallas_call(
        functools.partial(_rnn_gate_kernel, tm=tm, n_ctx=n_ctx, t_c=t_c, t_s=t_s),
        out_shape=(out, out, out, out),
        grid=(m // tm,),
        in_specs=[pl.BlockSpec((tm, D_RNN), lambda i: (i, 1)),
                  pl.BlockSpec((SUBLANES, D_RNN), lambda i: (jnp.maximum(i * n8 - 1, 0), 1)),
                  pl.BlockSpec((SUBLANES, D_RNN), lambda i: (jnp.minimum((i + 1) * n8, last8), 1)),
                  const(conv_w), const(conv_b), const(wg), const(bg), const(lam)],
        out_specs=(row, row, row, row),
        scratch_shapes=[pltpu.VMEM((tm + 2 * SUBLANES, D_RNN), F32)],
        compiler_params=_cparams("parallel"),
        name="rnn_gates",
    )(bx, bx, bx, conv_w, conv_b, wg, bg, lam)


def _rnn_scan_kernel(fblk_ref, bblk_ref, seq_ref, first_ref,
                     af_ref, uf_ref, ab_ref, ub_ref, h0_ref, hf_ref, hb_ref, fin_ref, carry_ref, *, tc):
    s = pl.program_id(0)

    @pl.when(first_ref[s] == 1)
    def _():
        carry_ref[...] = h0_ref[...]

    def step(t, carry):
        h_f, h_b = carry
        tb = tc - 1 - t
        h_f = af_ref[pl.ds(t, 1), :] * h_f + uf_ref[pl.ds(t, 1), :]
        h_b = ab_ref[pl.ds(tb, 1), :] * h_b + ub_ref[pl.ds(tb, 1), :]
        hf_ref[pl.ds(t, 1), :] = h_f
        hb_ref[pl.ds(tb, 1), :] = h_b
        return h_f, h_b

    h_f, h_b = lax.fori_loop(0, tc, step, (carry_ref[0:1, :], carry_ref[1:2, :]), unroll=8)
    carry_ref[0:1, :] = h_f
    carry_ref[1:2, :] = h_b
    fin_ref[...] = carry_ref[...]


def _rnn_scan(a_f, u_f, a_b, u_b, h0, *, seq_lens):
    m = a_f.shape[0]
    tc = SCAN_CHUNK
    fblk, bblk, seq, first = [], [], [], []
    blk0 = 0
    for si, t in enumerate(seq_lens):
        nch = t // tc
        for c in range(nch):
            fblk.append(blk0 + c)
            bblk.append(blk0 + nch - 1 - c)
            seq.append(si)
            first.append(1 if c == 0 else 0)
        blk0 += nch
    n_steps = len(fblk)
    tables = [jnp.asarray(np.array(v, np.int32)) for v in (fblk, bblk, seq, first)]
    f_spec = pl.BlockSpec((tc, D_RNN), lambda s, fb, bb, sq, fr: (fb[s], 0))
    b_spec = pl.BlockSpec((tc, D_RNN), lambda s, fb, bb, sq, fr: (bb[s], 0))
    st_spec = pl.BlockSpec((None, 2, D_RNN), lambda s, fb, bb, sq, fr: (sq[s], 0, 0))
    out = jax.ShapeDtypeStruct((m, D_RNN), F32)
    return pl.pallas_call(
        functools.partial(_rnn_scan_kernel, tc=tc),
        out_shape=(out, out, jax.ShapeDtypeStruct(h0.shape, F32)),
        grid_spec=pltpu.PrefetchScalarGridSpec(
            num_scalar_prefetch=4, grid=(n_steps,),
            in_specs=[f_spec, f_spec, b_spec, b_spec, st_spec],
            out_specs=(f_spec, b_spec, st_spec),
            scratch_shapes=[pltpu.VMEM((2, D_RNN), F32)]),
        compiler_params=_cparams("arbitrary"),
        name="rnn_scan",
    )(*tables, a_f, u_f, a_b, u_b, h0)


def _ffn_kernel(x_ref, sc_ref, sh_ref, gate_ref, wg_ref, wu_ref, wd_ref, lng_ref, lnb_ref,
                o_ref, h_ref, acc_ref):
    j = pl.program_id(1)

    @pl.when(j == 0)
    def _():
        h_ref[...] = (x_ref[...] * (1.0 + sc_ref[...]) + sh_ref[...]).astype(BF16)
        acc_ref[...] = jnp.zeros_like(acc_ref)

    h = h_ref[...]
    g = jnp.dot(h, wg_ref[...], preferred_element_type=F32)
    u = jnp.dot(h, wu_ref[...], preferred_element_type=F32)
    act = ((g * _sigmoid(g)) * u).astype(BF16)
    acc_ref[...] += jnp.dot(act, wd_ref[...], preferred_element_type=F32)

    @pl.when(j == pl.num_programs(1) - 1)
    def _():
        z = DEEPNORM_ALPHA * x_ref[...] + gate_ref[...] * acc_ref[...]
        o_ref[...] = _layer_norm_rows(z, lng_ref[...], lnb_ref[...])


def _ffn(x, mod, w_gu, w_down, lng, lnb, *, n_ctx, t_s):
    m = x.shape[0]
    tm = 512
    tf = D_FF // 2
    nf = D_FF // tf
    const = lambda arr: pl.BlockSpec(arr.shape, lambda i, j: (0,) * arr.ndim)
    row = pl.BlockSpec((tm, D_MODEL), lambda i, j: (i, 0))
    return pl.pallas_call(
        _ffn_kernel,
        out_shape=jax.ShapeDtypeStruct((m, D_MODEL), F32),
        grid=(m // tm, nf),
        in_specs=[row, _mod_spec(4, tm, n_ctx, t_s), _mod_spec(3, tm, n_ctx, t_s), _mod_spec(5, tm, n_ctx, t_s),
                  pl.BlockSpec((D_MODEL, tf), lambda i, j: (0, j)),
                  pl.BlockSpec((D_MODEL, tf), lambda i, j: (0, nf + j)),
                  pl.BlockSpec((tf, D_MODEL), lambda i, j: (j, 0)),
                  const(lng), const(lnb)],
        out_specs=row,
        scratch_shapes=[pltpu.VMEM((tm, D_MODEL), BF16), pltpu.VMEM((tm, D_MODEL), F32)],
        compiler_params=_cparams("parallel", "arbitrary"),
        name="ffn",
    )(x, mod, mod, mod, w_gu, w_gu, w_down, lng, lnb)


ROW_TILES = D_MODEL // LANES


def _to_token_rows(ref, base, value):
    n = value.shape[0]
    for j in range(ROW_TILES):
        ref[pl.ds(base + j, n, stride=ROW_TILES), :] = value[:, j * LANES:(j + 1) * LANES]


def _from_token_rows(ref, base, n):
    return jnp.concatenate([ref[pl.ds(base + j, n, stride=ROW_TILES), :] for j in range(ROW_TILES)], axis=1)


def _token_row(ref, r):
    start = r * ROW_TILES if isinstance(r, int) else pl.multiple_of(r * ROW_TILES, ROW_TILES)
    return ref.at[pl.ds(start, ROW_TILES)]


def _route_kernel(x_ref, sc_ref, sh_ref, wr_ref, route_ref, gates_ref, cnt_ref, carry_ref, *, tm):
    @pl.when(pl.program_id(0) == 0)
    def _():
        carry_ref[...] = jnp.zeros_like(carry_ref)

    lane = lax.broadcasted_iota(jnp.int32, (tm, LANES), 1)
    h = x_ref[...] * (1.0 + sc_ref[...]) + sh_ref[...]
    logits = jnp.dot(h.astype(BF16), wr_ref[...], preferred_element_type=F32)
    s1 = jnp.where(lane < N_EXPERTS, logits, -jnp.inf)
    m1 = jnp.max(s1, axis=-1, keepdims=True)
    i1 = jnp.min(jnp.where(s1 == m1, lane, LANES), axis=-1, keepdims=True)
    s2 = jnp.where(lane == i1, -jnp.inf, s1)
    m2 = jnp.max(s2, axis=-1, keepdims=True)
    i2 = jnp.min(jnp.where(s2 == m2, lane, LANES), axis=-1, keepdims=True)
    e2 = jnp.exp(m2 - m1)
    den = 1.0 + e2
    hit = jnp.where((lane == i1) | (lane == i2), 1.0, 0.0)
    earlier = lax.broadcasted_iota(jnp.int32, (tm, tm), 1) < lax.broadcasted_iota(jnp.int32, (tm, tm), 0)
    before = carry_ref[...] + jnp.dot(jnp.where(earlier, 1.0, 0.0).astype(BF16), hit.astype(BF16),
                                      preferred_element_type=F32)
    r1 = jnp.sum(jnp.where(lane == i1, before, 0.0), axis=-1, keepdims=True).astype(jnp.int32)
    r2 = jnp.sum(jnp.where(lane == i2, before, 0.0), axis=-1, keepdims=True).astype(jnp.int32)
    carry_ref[...] += jnp.sum(hit, axis=0, keepdims=True)
    route = jnp.where(lane == 0, i1, jnp.where(lane == 1, i2, jnp.where(lane == 2, r1, jnp.where(lane == 3, r2, 0))))
    route_ref[...] = route[:, :ROUTE_COLS]
    gates_ref[...] = jnp.where(lane == 0, 1.0 / den, jnp.where(lane == 1, e2 / den, 0.0))[:, :ROUTE_COLS]
    cnt_ref[...] = jnp.broadcast_to(carry_ref[...], cnt_ref.shape)


def _moe_route(x, mod, w_router, *, n_ctx, t_s):
    m = x.shape[0]
    tm = 512
    small = pl.BlockSpec((tm, ROUTE_COLS), lambda i: (i, 0))
    return pl.pallas_call(
        functools.partial(_route_kernel, tm=tm),
        out_shape=(jax.ShapeDtypeStruct((m, ROUTE_COLS), jnp.int32), jax.ShapeDtypeStruct((m, ROUTE_COLS), F32),
                   jax.ShapeDtypeStruct((SUBLANES, LANES), F32)),
        grid=(m // tm,),
        in_specs=[pl.BlockSpec((tm, D_MODEL), lambda i: (i, 0)), _mod_spec(4, tm, n_ctx, t_s),
                  _mod_spec(3, tm, n_ctx, t_s), pl.BlockSpec(w_router.shape, lambda i: (0, 0))],
        out_specs=(small, small, pl.BlockSpec((SUBLANES, LANES), lambda i: (0, 0))),
        scratch_shapes=[pltpu.VMEM((1, LANES), F32)],
        compiler_params=_cparams("arbitrary"),
        name="moe_route",
    )(x, mod, mod, w_router)


def _dispatch_kernel(slot_ref, x_ref, sc_ref, sh_ref, xs_hbm, buf, sem, *, tok):
    i = pl.program_id(0)
    par = i % 2
    cur = buf.at[par]
    _to_token_rows(cur, 0, x_ref[...] * (1.0 + sc_ref[...]) + sh_ref[...])

    def issue(t, c):
        src = _token_row(cur, t)
        pltpu.make_async_copy(src, _token_row(xs_hbm, slot_ref[2 * t]), sem.at[par]).start(priority=0)
        pltpu.make_async_copy(src, _token_row(xs_hbm, slot_ref[2 * t + 1]), sem.at[par]).start(priority=1)
        return c
    lax.fori_loop(0, tok, issue, 0, unroll=4)

    def drain(which):
        def body(t, c):
            pltpu.make_async_copy(_token_row(buf.at[which], 0), _token_row(xs_hbm, 0), sem.at[which]).wait()
            return c
        lax.fori_loop(0, 2 * tok, body, 0, unroll=8)

    @pl.when(i > 0)
    def _():
        drain(1 - par)

    @pl.when(i == pl.num_programs(0) - 1)
    def _():
        drain(par)


def _moe_dispatch(slots, x, mod, *, n_ctx, t_s):
    m = x.shape[0]
    tok = 512
    return pl.pallas_call(
        functools.partial(_dispatch_kernel, tok=tok),
        out_shape=jax.ShapeDtypeStruct((2 * m * ROW_TILES, LANES), F32),
        grid=(m // tok,),
        in_specs=[pl.BlockSpec((2 * tok,), lambda i: (i,), memory_space=pltpu.SMEM),
                  pl.BlockSpec((tok, D_MODEL), lambda i: (i, 0)),
                  _mod_spec(4, tok, n_ctx, t_s), _mod_spec(3, tok, n_ctx, t_s)],
        out_specs=pl.BlockSpec(memory_space=pl.ANY),
        scratch_shapes=[pltpu.VMEM((2, tok * ROW_TILES, LANES), F32), pltpu.SemaphoreType.DMA((2,))],
        compiler_params=_cparams("arbitrary"),
        name="moe_dispatch",
    )(slots, x, mod, mod)


def _experts_kernel(wt_ref, we_ref, lo_ref, hi_ref, nw_ref, xs_ref, wg_ref, wu_ref, wd_ref, o_ref, *, rows):
    del we_ref
    w = pl.program_id(0)

    @pl.when(w < nw_ref[0])
    def _():
        xs = _from_token_rows(xs_ref, 0, rows).astype(BF16)
        g = jnp.dot(xs, wg_ref[...], preferred_element_type=F32)
        u = jnp.dot(xs, wu_ref[...], preferred_element_type=F32)
        act = ((g * _sigmoid(g)) * u).astype(BF16)
        y = jnp.dot(act, wd_ref[...], preferred_element_type=F32)
        whole = jnp.logical_and(lo_ref[w] == 0, hi_ref[w] == rows)
        first = jnp.logical_or(w == 0, wt_ref[jnp.maximum(w - 1, 0)] != wt_ref[w])

        @pl.when(whole)
        def _():
            _to_token_rows(o_ref, 0, y)

        @pl.when(jnp.logical_not(whole))
        def _():
            row = lax.broadcasted_iota(jnp.int32, (rows, D_MODEL), 0)
            mine = (row >= lo_ref[w]) & (row < hi_ref[w])

            @pl.when(first)
            def _():
                _to_token_rows(o_ref, 0, jnp.where(mine, y, 0.0))

            @pl.when(jnp.logical_not(first))
            def _():
                _to_token_rows(o_ref, 0, jnp.where(mine, y, _from_token_rows(o_ref, 0, rows)))


def _moe_experts(items, xs, w_gu, w_down, layer):
    rows = MOE_TILE
    n_items = items[0].shape[0]
    tile = lambda w, wt, we, lo, hi, nw: (wt[w], 0)
    return pl.pallas_call(
        functools.partial(_experts_kernel, rows=rows),
        out_shape=jax.ShapeDtypeStruct(xs.shape, F32),
        grid_spec=pltpu.PrefetchScalarGridSpec(
            num_scalar_prefetch=5, grid=(n_items,),
            in_specs=[pl.BlockSpec((rows * ROW_TILES, LANES), tile),
                      pl.BlockSpec((None, None, D_MODEL, D_FF_EXPERT),
                                   lambda w, wt, we, lo, hi, nw: (layer, we[w], 0, 0)),
                      pl.BlockSpec((None, None, D_MODEL, D_FF_EXPERT),
                                   lambda w, wt, we, lo, hi, nw: (layer, we[w], 0, 1)),
                      pl.BlockSpec((None, None, D_FF_EXPERT, D_MODEL),
                                   lambda w, wt, we, lo, hi, nw: (layer, we[w], 0, 0))],
            out_specs=pl.BlockSpec((rows * ROW_TILES, LANES), tile)),
        compiler_params=_cparams("arbitrary"),
        name="moe_experts",
    )(*items, xs, w_gu, w_gu, w_down)


def _combine_kernel(idx_ref, nxt_ref, ys_hbm, gates_ref, x_ref, gate_ref, lng_ref, lnb_ref, o_ref, buf, sem, *, tm):
    i = pl.program_id(0)
    slot = i % 2

    def issue(ref, s):
        def body(t, c):
            pltpu.make_async_copy(_token_row(ys_hbm, ref[2 * t]), _token_row(buf.at[s], t),
                                  sem.at[s]).start(priority=0)
            pltpu.make_async_copy(_token_row(ys_hbm, ref[2 * t + 1]), _token_row(buf.at[s], tm + t),
                                  sem.at[s]).start(priority=1)
            return c
        lax.fori_loop(0, tm, body, 0, unroll=4)

    @pl.when(i == 0)
    def _():
        issue(idx_ref, 0)

    @pl.when(i + 1 < pl.num_programs(0))
    def _():
        issue(nxt_ref, 1 - slot)

    def drain(r, c):
        pltpu.make_async_copy(_token_row(ys_hbm, 0), _token_row(buf.at[slot], 0), sem.at[slot]).wait()
        return c
    lax.fori_loop(0, 2 * tm, drain, 0, unroll=8)
    cur = buf.at[slot]
    y = (gates_ref[:, 0:1] * _from_token_rows(cur, 0, tm)
         + gates_ref[:, 1:2] * _from_token_rows(cur, tm * ROW_TILES, tm))
    z = DEEPNORM_ALPHA * x_ref[...] + gate_ref[...] * y
    o_ref[...] = _layer_norm_rows(z, lng_ref[...], lnb_ref[...])


def _moe_combine(slots, ys, gates, x, mod, lng, lnb, *, n_ctx, t_s):
    m = x.shape[0]
    tm = 256
    n_blocks = m // tm
    idx = lambda f: pl.BlockSpec((2 * tm,), f, memory_space=pltpu.SMEM)
    const = lambda arr: pl.BlockSpec(arr.shape, lambda i: (0,) * arr.ndim)
    row = pl.BlockSpec((tm, D_MODEL), lambda i: (i, 0))
    return pl.pallas_call(
        functools.partial(_combine_kernel, tm=tm),
        out_shape=jax.ShapeDtypeStruct((m, D_MODEL), F32),
        grid=(n_blocks,),
        in_specs=[idx(lambda i: (i,)), idx(lambda i: (jnp.minimum(i + 1, n_blocks - 1),)),
                  pl.BlockSpec(memory_space=pl.ANY), pl.BlockSpec((tm, ROUTE_COLS), lambda i: (i, 0)),
                  row, _mod_spec(5, tm, n_ctx, t_s), const(lng), const(lnb)],
        out_specs=row,
        scratch_shapes=[pltpu.VMEM((2, 2 * tm * ROW_TILES, LANES), F32), pltpu.SemaphoreType.DMA((2,))],
        compiler_params=_cparams("arbitrary"),
        name="moe_combine",
    )(slots, slots, ys, gates, x, mod, lng, lnb)


def _moe_work_items(counts, n_rows):
    n_tiles = n_rows // MOE_TILE
    n_items = n_tiles + N_EXPERTS - 1
    g_end = jnp.cumsum(counts)
    g_start = g_end - counts
    row0 = jnp.arange(n_tiles, dtype=jnp.int32) * MOE_TILE
    first_e = jnp.sum(row0[:, None] >= g_end[None, :], axis=1)
    last_e = jnp.sum((row0 + MOE_TILE - 1)[:, None] >= g_end[None, :], axis=1)
    per_tile = last_e - first_e + 1
    item0 = jnp.cumsum(per_tile) - per_tile
    w = jnp.arange(n_items, dtype=jnp.int32)
    tile = jnp.clip(jnp.sum(w[:, None] >= item0[None, :], axis=1) - 1, 0, n_tiles - 1)
    expert = jnp.clip(first_e[tile] + w - item0[tile], 0, N_EXPERTS - 1)
    lo = jnp.clip(g_start[expert] - row0[tile], 0, MOE_TILE)
    hi = jnp.clip(g_end[expert] - row0[tile], 0, MOE_TILE)
    n_used = jnp.sum(per_tile).reshape(1)
    i32 = lambda a: a.astype(jnp.int32)
    return i32(tile), i32(expert), i32(lo), i32(hi), i32(n_used), g_start


def _moe(x, mod, w_router, w_gu, w_down, layer, lng, lnb, *, n_ctx, t_s):
    m = x.shape[0]
    assert (2 * m) % MOE_TILE == 0
    route, gates, cnt = _moe_route(x, mod, w_router, n_ctx=n_ctx, t_s=t_s)
    counts = cnt[0, :N_EXPERTS].astype(jnp.int32)
    *items, g_start = _moe_work_items(counts, 2 * m)
    slots = (g_start[route[:, 0:2]] + route[:, 2:4]).reshape(2 * m).astype(jnp.int32)
    xs = _moe_dispatch(slots, x, mod, n_ctx=n_ctx, t_s=t_s)
    ys = _moe_experts(items, xs, w_gu, w_down, layer)
    return _moe_combine(slots, ys, gates, x, mod, lng, lnb, n_ctx=n_ctx, t_s=t_s)


def _rope_tables(n_tokens):
    n_rows = n_tokens // GRID_W
    row = jnp.repeat(jnp.arange(n_rows, dtype=F32), GRID_W)
    col = jnp.tile(jnp.arange(GRID_W, dtype=F32), n_rows)
    n_axis = HEAD_DIM // 4
    inv = ROPE_THETA ** (-jnp.arange(n_axis, dtype=F32) / n_axis)
    ang = jnp.concatenate([row[:, None] * inv, col[:, None] * inv], axis=-1)
    cos, sin = jnp.cos(ang), jnp.sin(ang)
    reps = LANES // HEAD_DIM
    cos_t = jnp.tile(jnp.concatenate([cos, cos], axis=-1), (1, reps))
    sin_t = jnp.tile(jnp.concatenate([-sin, sin], axis=-1), (1, reps))
    return cos_t, sin_t


def _gqa_head_perm():
    cols = []
    per_pair = 2 * (N_Q_HEADS // N_KV_HEADS)
    for p in range(N_KV_HEADS // 2):
        for j in range(N_Q_HEADS // N_KV_HEADS):
            for hd in (per_pair * p + j, per_pair * p + per_pair // 2 + j):
                cols.append(np.arange(hd * HEAD_DIM, (hd + 1) * HEAD_DIM))
    return np.concatenate(cols)


def _block_diag_windows(w):
    eye = jnp.eye(N_RNN_BLOCKS, dtype=bool)[:, None, :, None]
    dense = jnp.where(eye, w[:, :, None, :], 0.0).reshape(D_RNN, D_RNN)
    for ct, k0 in enumerate(GATE_K0):
        lo, hi = ct * GATE_N, (ct + 1) * GATE_N
        assert k0 <= RNN_BLOCK * (lo // RNN_BLOCK) and RNN_BLOCK * -(-hi // RNN_BLOCK) <= k0 + GATE_K <= D_RNN
    return jnp.stack([dense[k0:k0 + GATE_K, ct * GATE_N:(ct + 1) * GATE_N] for ct, k0 in enumerate(GATE_K0)])


def kernel(x_prompt, x_sample, cache_attn_k, cache_attn_v, state_rglru, cache_diff_k, cache_diff_v, c, c_ctx, w_ada, b_ada, ln_g, ln_b, attn_w_qkv, attn_g_q, attn_g_k, attn_w_o, rnn_w_in, rnn_conv_w, rnn_conv_b, rnn_w_a, rnn_b_a, rnn_w_x, rnn_b_x, rnn_lambda, rnn_w_out, diff_w_qkv, diff_lambda, diff_g_sub, diff_w_o, ffn_w_gu, ffn_w_down, moe_w_router, moe_w_gu, moe_w_down):
    b_c, t_c, d = x_prompt.shape
    b_s, t_s, _ = x_sample.shape
    n_ctx = b_c * t_c
    n_s = b_s * t_s
    past = cache_attn_k.shape[2]
    assert d == D_MODEL and n_ctx % t_s == 0 and t_c == SCAN_CHUNK and t_s % SCAN_CHUNK == 0
    assert 1 + b_s <= COND_ROWS

    x = jnp.concatenate([x_prompt.reshape(n_ctx, d), x_sample.reshape(n_s, d)], axis=0)
    cond = jnp.zeros((COND_ROWS, d), F32).at[0].set(c_ctx).at[1:1 + b_s].set(c)
    mods = _ada_table(cond, w_ada, b_ada)[:, :1 + b_s].reshape(DEPTH, 1 + b_s, 1, 6 * d)

    cos_t, sin_t = _rope_tables(t_s)
    rope_t = (cos_t.T, sin_t.T)
    perm = _gqa_head_perm()
    nq = N_Q_HEADS * HEAD_DIM
    tile2 = lambda g: jnp.tile(g, LANES // g.shape[-1]).reshape(1, LANES)
    kw = dict(n_ctx=n_ctx, t_s=t_s)
    moe_gu_b = moe_w_gu.astype(BF16)
    moe_down_b = moe_w_down.astype(BF16)

    attn_k, attn_v, rnn_s, diff_k, diff_v = [], [], [], [], []
    for li in range(DEPTH):
        mod = mods[li]
        lng = ln_g[li].reshape(2, 1, d)
        lnb = ln_b[li].reshape(2, 1, d)
        j = li // N_MIXERS
        kind = li % N_MIXERS
        if kind == 0:
            w = attn_w_qkv[j]
            w = jnp.concatenate([w[:, :nq][:, perm], w[:, nq:]], axis=1).astype(BF16)
            q, kb, vb, kf, vf = _gqa_qkv(x, mod, w, tile2(attn_g_k[j]), cos_t, sin_t, **kw)
            gq = tile2(attn_g_q[j]).reshape(LANES, 1)
            ck = cache_attn_k[:, j].reshape(b_s, past, N_KV_HEADS * HEAD_DIM).astype(BF16)
            cv = cache_attn_v[:, j].reshape(b_s, past, N_KV_HEADS * HEAD_DIM).astype(BF16)
            o_c = _attention(q, kb, vb, None, None, row0=0, n_b=b_c, t=t_c, tq=t_c, gq=gq)
            o_s = _attention(q, kb, vb, ck, cv, row0=n_ctx, n_b=b_s, t=t_s, tq=512, gq=gq, rope=rope_t)
            x = _oproj(o_c, o_s, attn_w_o[j][perm, :].astype(BF16), x, mod, lng[0], lnb[0], **kw)
            attn_k.append(kf[:n_ctx])
            attn_v.append(vf[:n_ctx])
        elif kind == 1:
            bx = _rnn_in(x, mod, rnn_w_in[j].astype(BF16), **kw)
            wg = jnp.stack([_block_diag_windows(rnn_w_a[j, 0]), _block_diag_windows(rnn_w_x[j, 0]),
                            _block_diag_windows(rnn_w_a[j, 1]), _block_diag_windows(rnn_w_x[j, 1])]).astype(BF16)
            bg = jnp.concatenate([rnn_b_a[j, 0], rnn_b_x[j, 0], rnn_b_a[j, 1], rnn_b_x[j, 1]]).reshape(1, -1)
            a_f, u_f, a_b, u_b = _rnn_gates(bx, rnn_conv_w[j], rnn_conv_b[j].reshape(1, -1), wg, bg,
                                            rnn_lambda[j], n_ctx=n_ctx, t_c=t_c, t_s=t_s)
            h0 = jnp.concatenate([jnp.zeros((b_c, 2, D_RNN), F32), state_rglru[:, j]], axis=0)
            hs_f, hs_b, fin = _rnn_scan(a_f, u_f, a_b, u_b, h0, seq_lens=[t_c] * b_c + [t_s] * b_s)
            x = _rnn_oproj(hs_f, hs_b, bx, rnn_w_out[j].astype(BF16), x, mod, lng[0], lnb[0], **kw)
            rnn_s.append(fin[:b_c])
        else:
            lam_init = 0.8 - 0.6 * math.exp(-0.3 * li)
            q, kb, vb, kf, vf = _diff_qkv(x, mod, diff_w_qkv[j].astype(BF16), cos_t, sin_t, **kw)
            ck = cache_diff_k[:, j].reshape(b_s, past, D_MODEL).astype(BF16)
            cv = cache_diff_v[:, j].reshape(b_s, past, D_MODEL).astype(BF16)
            g_sub = diff_g_sub[j].reshape(1, LANES)
            dargs = (diff_lambda[j], g_sub, lam_init)
            o_c = _attention(q, kb, vb, None, None, row0=0, n_b=b_c, t=t_c, tq=t_c, diff=dargs)
            o_s = _attention(q, kb, vb, ck, cv, row0=n_ctx, n_b=b_s, t=t_s, tq=256, rope=rope_t, diff=dargs)
            x = _oproj(o_c, o_s, diff_w_o[j].astype(BF16), x, mod, lng[0], lnb[0], **kw)
            diff_k.append(kf[:n_ctx])
            diff_v.append(vf[:n_ctx])
        if li % 2 == 0:
            x = _ffn(x, mod, ffn_w_gu[li // 2].astype(BF16), ffn_w_down[li // 2].astype(BF16), lng[1], lnb[1], **kw)
        else:
            w_r = jnp.zeros((d, LANES), F32).at[:, :N_EXPERTS].set(moe_w_router[li // 2]).astype(BF16)
            x = _moe(x, mod, w_r, moe_gu_b, moe_down_b, li // 2, lng[1], lnb[1], **kw)

    y_prompt = x[:n_ctx].reshape(b_c, t_c, d)
    y_sample = x[n_ctx:].reshape(b_s, t_s, d)
    new_attn_k = jnp.stack([k.reshape(b_c, t_c, N_KV_HEADS, HEAD_DIM) for k in attn_k], axis=1)
    new_attn_v = jnp.stack([v.reshape(b_c, t_c, N_KV_HEADS, HEAD_DIM) for v in attn_v], axis=1)
    new_state = jnp.stack(rnn_s, axis=1)
    new_diff_k = jnp.stack([k.reshape(b_c, t_c, N_DIFF_HEADS, 2, HEAD_DIM) for k in diff_k], axis=1)
    new_diff_v = jnp.stack([v.reshape(b_c, t_c, N_DIFF_HEADS, 2 * HEAD_DIM) for v in diff_v], axis=1)
    return (y_prompt, y_sample, new_attn_k, new_attn_v, new_state, new_diff_k, new_diff_v)
```

```python
import functools
import math

import jax
import jax.numpy as jnp
import numpy as np
from jax import lax
from jax.experimental import pallas as pl
from jax.experimental.pallas import tpu as pltpu

F32 = jnp.float32
BF16 = jnp.bfloat16

D_MODEL = 1024
DEPTH = 4
GRID_W = 64
HEAD_DIM = 64
N_Q_HEADS = 16
N_KV_HEADS = 4
ROPE_THETA = 10000.0
N_DIFF_HEADS = 8
D_RNN = 1280
N_RNN_BLOCKS = 16
RNN_BLOCK = 80
CONV_W = 4
CONV_LEFT = 2
RGLRU_C = 8.0
D_FF = 2816
N_EXPERTS = 8
D_FF_EXPERT = 1408
N_MIXERS = 3
DEEPNORM_ALPHA = (2.0 * DEPTH) ** 0.25
LN_EPS = 1e-6
RMS_EPS = 1e-6

LANES = 128
SUBLANES = 8
VMEM_LIMIT = 56 * 1024 * 1024
COND_ROWS = 8
SCAN_CHUNK = 256
MOE_TILE = 512
ROUTE_COLS = 8
GATE_N = 256
GATE_K = 512
GATE_K0 = (0, 128, 384, 640, 768)
Q_SCALE = HEAD_DIM ** -0.5 * math.log2(math.e)


def _cparams(*sem):
    return pltpu.CompilerParams(dimension_semantics=sem, vmem_limit_bytes=VMEM_LIMIT)


def _sigmoid(x):
    return 0.5 * jnp.tanh(0.5 * x) + 0.5


def _layer_norm_rows(z, g, b):
    mu = jnp.mean(z, axis=-1, keepdims=True)
    zc = z - mu
    var = jnp.mean(zc * zc, axis=-1, keepdims=True)
    return zc * lax.rsqrt(var + LN_EPS) * g + b


def _seg_of_block(i, tm, n_ctx, t_s):
    r0 = i * tm
    return jnp.where(r0 < n_ctx, 0, 1 + (r0 - n_ctx) // t_s)


def _mod_spec(col, tm, n_ctx, t_s):
    return pl.BlockSpec((None, 1, D_MODEL), lambda i, *_: (_seg_of_block(i, tm, n_ctx, t_s), 0, col))


def _ada_kernel(c_ref, w_ref, b_ref, o_ref):
    c = c_ref[...]
    a = (c * _sigmoid(c)).astype(BF16)
    o_ref[...] = jnp.dot(a, w_ref[...].astype(BF16), preferred_element_type=F32) + b_ref[...]


def _ada_table(cond, w_ada, b_ada):
    n_l, d, n = w_ada.shape
    tn = 1536
    return pl.pallas_call(
        _ada_kernel,
        out_shape=jax.ShapeDtypeStruct((n_l, COND_ROWS, n), F32),
        grid=(n_l, n // tn),
        in_specs=[pl.BlockSpec((COND_ROWS, d), lambda l, j: (0, 0)),
                  pl.BlockSpec((None, d, tn), lambda l, j: (l, 0, j)),
                  pl.BlockSpec((None, 1, tn), lambda l, j: (l, 0, j))],
        out_specs=pl.BlockSpec((None, COND_ROWS, tn), lambda l, j: (l, 0, j)),
        compiler_params=_cparams("parallel", "parallel"),
        name="ada_table",
    )(cond, w_ada, b_ada.reshape(n_l, 1, n))


def _head_masks(tm):
    lane = lax.broadcasted_iota(jnp.int32, (tm, LANES), 1)
    return lane < HEAD_DIM, (lane & (HEAD_DIM - 1)) < HEAD_DIM // 2


def _rope_tile(t, first_half, cos, sin):
    partner = jnp.where(first_half, pltpu.roll(t, LANES - HEAD_DIM // 2, 1),
                        pltpu.roll(t, HEAD_DIM // 2, 1))
    return t * cos + partner * sin


def _x_specs(x, tm, n_ctx):
    if not isinstance(x, tuple):
        return [pl.BlockSpec((tm, D_MODEL), lambda i, *_: (i, 0))], [x]
    ncb = n_ctx // tm
    return ([pl.BlockSpec((tm, D_MODEL), lambda i, *_: (jnp.minimum(i, ncb - 1), 0)),
             pl.BlockSpec((tm, D_MODEL), lambda i, *_: (jnp.maximum(i - ncb, 0), 0))], list(x))


def _x_value(x_refs, tm, n_ctx):
    if len(x_refs) == 1:
        return x_refs[0][...]
    return jnp.where(pl.program_id(0) * tm < n_ctx, x_refs[0][...], x_refs[1][...])


def _gqa_qkv_kernel(*refs, tm, n_ctx, n_x):
    x_refs = refs[:n_x]
    sc_ref, sh_ref, w_ref, gk_ref, cos_ref, sin_ref, q_ref, kb_ref, vb_ref, kf_ref, vf_ref = refs[n_x:]
    i = pl.program_id(0)
    h = (_x_value(x_refs, tm, n_ctx) * (1.0 + sc_ref[...]) + sh_ref[...]).astype(BF16)
    y = jnp.dot(h, w_ref[...], preferred_element_type=F32)
    use_rope = i * tm >= n_ctx
    cos = jnp.where(use_rope, cos_ref[...], 1.0)
    sin = jnp.where(use_rope, sin_ref[...], 0.0)
    low_head, first_half = _head_masks(tm)

    def norm_rope(t, g):
        t2 = t * t
        s_lo = jnp.sum(jnp.where(low_head, t2, 0.0), axis=-1, keepdims=True)
        s_hi = jnp.sum(jnp.where(low_head, 0.0, t2), axis=-1, keepdims=True)
        inv = jnp.where(low_head, lax.rsqrt(s_lo * (1.0 / HEAD_DIM) + RMS_EPS),
                        lax.rsqrt(s_hi * (1.0 / HEAD_DIM) + RMS_EPS))
        return _rope_tile(t * inv * g, first_half, cos, sin)

    nq = N_Q_HEADS * HEAD_DIM
    nkv = N_KV_HEADS * HEAD_DIM
    gk = gk_ref[...]
    q_ref[...] = y[:, :nq].astype(BF16)
    for t in range(nkv // LANES):
        sl = slice(t * LANES, (t + 1) * LANES)
        k = norm_rope(y[:, nq + t * LANES: nq + (t + 1) * LANES], gk)
        kf_ref[:, sl] = k
        kb_ref[:, sl] = k.astype(BF16)
    v = y[:, nq + nkv:]
    vf_ref[...] = v
    vb_ref[...] = v.astype(BF16)


def _gqa_qkv(x, mod, w, gk, cos, sin, *, n_ctx, t_s):
    m = sum(a.shape[0] for a in x) if isinstance(x, tuple) else x.shape[0]
    tm = 512
    x_specs, x_args = _x_specs(x, tm, n_ctx)
    nq = N_Q_HEADS * HEAD_DIM
    nkv = N_KV_HEADS * HEAD_DIM
    n_rope_blocks = t_s // tm
    rope_spec = pl.BlockSpec(
        (tm, LANES), lambda i: (jnp.where(i * tm >= n_ctx, ((i * tm - n_ctx) % t_s) // tm, 0) % n_rope_blocks, 0))
    row = lambda n: pl.BlockSpec((tm, n), lambda i: (i, 0))
    const = lambda a: pl.BlockSpec(a.shape, lambda i: (0,) * a.ndim)
    return pl.pallas_call(
        functools.partial(_gqa_qkv_kernel, tm=tm, n_ctx=n_ctx, n_x=len(x_args)),
        out_shape=(jax.ShapeDtypeStruct((m, nq), BF16), jax.ShapeDtypeStruct((m, nkv), BF16),
                   jax.ShapeDtypeStruct((m, nkv), BF16), jax.ShapeDtypeStruct((m, nkv), F32),
                   jax.ShapeDtypeStruct((m, nkv), F32)),
        grid=(m // tm,),
        in_specs=x_specs + [_mod_spec(1, tm, n_ctx, t_s), _mod_spec(0, tm, n_ctx, t_s),
                            const(w), const(gk), rope_spec, rope_spec],
        out_specs=(row(nq), row(nkv), row(nkv), row(nkv), row(nkv)),
        compiler_params=_cparams("parallel"),
        name="gqa_qkv",
    )(*x_args, mod, mod, w, gk, cos, sin)


def _diff_qkv_kernel(x_ref, sc_ref, sh_ref, w_ref, cos_ref, sin_ref,
                     q_ref, kb_ref, vb_ref, kf_ref, vf_ref, *, tm, n_ctx):
    i = pl.program_id(0)
    h = (x_ref[...] * (1.0 + sc_ref[...]) + sh_ref[...]).astype(BF16)
    y = jnp.dot(h, w_ref[...], preferred_element_type=F32)
    use_rope = i * tm >= n_ctx
    cos = jnp.where(use_rope, cos_ref[...], 1.0)
    sin = jnp.where(use_rope, sin_ref[...], 0.0)
    _, first_half = _head_masks(tm)
    q_ref[...] = y[:, :D_MODEL].astype(BF16)
    for t in range(D_MODEL // LANES):
        sl = slice(t * LANES, (t + 1) * LANES)
        k = _rope_tile(y[:, D_MODEL + t * LANES: D_MODEL + (t + 1) * LANES], first_half, cos, sin)
        kf_ref[:, sl] = k
        kb_ref[:, sl] = k.astype(BF16)
    v = y[:, 2 * D_MODEL:]
    vf_ref[...] = v
    vb_ref[...] = v.astype(BF16)


def _diff_qkv(x, mod, w, cos, sin, *, n_ctx, t_s):
    m = x.shape[0]
    tm = 256
    n_rope_blocks = t_s // tm
    rope_spec = pl.BlockSpec(
        (tm, LANES), lambda i: (jnp.where(i * tm >= n_ctx, ((i * tm - n_ctx) % t_s) // tm, 0) % n_rope_blocks, 0))
    row = pl.BlockSpec((tm, D_MODEL), lambda i: (i, 0))
    return pl.pallas_call(
        functools.partial(_diff_qkv_kernel, tm=tm, n_ctx=n_ctx),
        out_shape=(jax.ShapeDtypeStruct((m, D_MODEL), BF16), jax.ShapeDtypeStruct((m, D_MODEL), BF16),
                   jax.ShapeDtypeStruct((m, D_MODEL), BF16), jax.ShapeDtypeStruct((m, D_MODEL), F32),
                   jax.ShapeDtypeStruct((m, D_MODEL), F32)),
        grid=(m // tm,),
        in_specs=[row, _mod_spec(1, tm, n_ctx, t_s), _mod_spec(0, tm, n_ctx, t_s),
                  pl.BlockSpec(w.shape, lambda i: (0, 0)), rope_spec, rope_spec],
        out_specs=(row, row, row, row, row),
        compiler_params=_cparams("parallel"),
        name="diff_qkv",
    )(x, mod, mod, w, cos, sin)


ATT_GROUP = 4
ATT_KCH = 512
ATT_UNROLL = 3


def _attn_kernel(*refs, diff, rope, has_cache, tq, kch, n_own, n_cache, lam_init):
    refs = list(refs)
    if diff:
        lam_ref, g_ref = refs[:2]
        refs = refs[2:]
    else:
        gq_ref = refs[0]
        refs = refs[1:]
    if rope:
        cos_ref, sin_ref = refs[:2]
        refs = refs[2:]
    if has_cache:
        q_ref, k_ref, v_ref, ck_ref, cv_ref, o_ref, k_scr, vt_scr, s_scr = refs
    else:
        q_ref, k_ref, v_ref, o_ref, k_scr, vt_scr, s_scr = refs
    nch = n_own + n_cache
    n_kv = k_ref.shape[1] // LANES
    v_rows = LANES if diff else HEAD_DIM
    ones = jnp.where(lax.broadcasted_iota(jnp.int32, (SUBLANES, kch), 0) == 0, 1.0, 0.0).astype(BF16)

    @pl.when(pl.program_id(2) == 0)
    def _():
        for t in range(n_kv):
            lanes = slice(t * LANES, (t + 1) * LANES)
            for c in range(nch):
                if c < n_own:
                    rows = slice(c * kch, (c + 1) * kch)
                    kc, vc = k_ref[rows, lanes], v_ref[rows, lanes]
                else:
                    rows = slice((c - n_own) * kch, (c - n_own + 1) * kch)
                    kc, vc = ck_ref[rows, lanes], cv_ref[rows, lanes]
                k_scr[t, c] = kc
                vt = vc.T
                if diff:
                    vt_scr[t, c, 0] = jnp.concatenate([vt, ones], axis=0)
                else:
                    vt_scr[t, c, 0] = jnp.concatenate([vt[:HEAD_DIM], ones], axis=0)
                    vt_scr[t, c, 1] = jnp.concatenate([vt[HEAD_DIM:], ones], axis=0)

    low_row = lax.broadcasted_iota(jnp.int32, (LANES, tq), 0) < HEAD_DIM
    n_tiles = q_ref.shape[1] // LANES
    kv_of = (lambda j: j) if diff else (lambda j: j // (N_Q_HEADS // N_KV_HEADS))
    half, quarter = HEAD_DIM, HEAD_DIM // 2

    def weights(j):
        t = q_ref[:, j * LANES:(j + 1) * LANES].astype(F32).T
        if not diff:
            inv = [lax.rsqrt(jnp.mean(h * h, axis=0, keepdims=True) + RMS_EPS) for h in (t[:half], t[half:])]
            t = jnp.concatenate([t[:half] * inv[0], t[half:] * inv[1]], axis=0) * gq_ref[...]
        if rope:
            partner = jnp.concatenate([t[quarter:half], t[:quarter], t[half + quarter:], t[half:half + quarter]], axis=0)
            t = t * cos_ref[...] + partner * sin_ref[...]
        tb = (t * Q_SCALE).astype(BF16)
        zero = jnp.zeros_like(tb)
        return jnp.concatenate([jnp.where(low_row, tb, zero), jnp.where(low_row, zero, tb)], axis=1)

    def stage(j_a, j_b, mx_b):
        w = weights(j_a) if j_a is not None else None

        def body(c, carry):
            m8, accs = carry
            if j_a is not None:
                s = jnp.dot(k_scr[kv_of(j_a), c], w, preferred_element_type=F32)
                s_scr[j_a % 2, c] = s
                m8 = jnp.maximum(m8, jnp.max(s.reshape(kch // SUBLANES, SUBLANES, 2 * tq), axis=0))
            if j_b is not None:
                e = jnp.exp2(s_scr[j_b % 2, c] - mx_b).astype(BF16)
                if diff:
                    accs = (accs[0] + jnp.dot(vt_scr[kv_of(j_b), c, 0], e, preferred_element_type=F32),)
                else:
                    accs = (accs[0] + jnp.dot(vt_scr[kv_of(j_b), c, 0], e[:, :tq], preferred_element_type=F32),
                            accs[1] + jnp.dot(vt_scr[kv_of(j_b), c, 1], e[:, tq:], preferred_element_type=F32))
            return m8, accs

        acc_shape = (v_rows + SUBLANES, 2 * tq if diff else tq)
        init = (jnp.full((SUBLANES, 2 * tq), -jnp.inf, F32),
                tuple(jnp.zeros(acc_shape, F32) for _ in range(1 if diff else 2)))
        m8, accs = lax.fori_loop(0, nch, body, init, unroll=min(nch, ATT_UNROLL))
        return jnp.max(m8, axis=0, keepdims=True), accs

    if diff:
        lv = lam_ref[...]
        lam = (jnp.exp(jnp.sum(lv[0:1] * lv[1:2], axis=-1, keepdims=True))
               - jnp.exp(jnp.sum(lv[2:3] * lv[3:4], axis=-1, keepdims=True)) + lam_init)

    mx_prev = None
    for st in range(n_tiles + 1):
        j_a = st if st < n_tiles else None
        j_b = st - 1 if st >= 1 else None
        mx_new, accs = stage(j_a, j_b, mx_prev)
        if j_b is not None:
            if diff:
                a = accs[0]
                o_t = (a[:v_rows, :tq] / a[v_rows:v_rows + 1, :tq]
                       - lam * (a[:v_rows, tq:] / a[v_rows:v_rows + 1, tq:]))
                o = o_t.T
                inv = lax.rsqrt(jnp.mean(o * o, axis=-1, keepdims=True) + RMS_EPS)
                o = (o * inv * g_ref[...]) * (1.0 - lam_init)
            else:
                o_t = jnp.concatenate([a[:v_rows] / a[v_rows:v_rows + 1] for a in accs], axis=0)
                o = o_t.T
            o_ref[:, j_b * LANES:(j_b + 1) * LANES] = o.astype(BF16)
        mx_prev = mx_new


def _attention(q, k, v, cache_k, cache_v, *, row0, n_b, t, tq, gq=None, rope=None, diff=None):
    is_diff = diff is not None
    qw = ATT_GROUP * LANES if is_diff else D_MODEL
    n_groups = D_MODEL // qw
    kvw = qw if is_diff else N_KV_HEADS * HEAD_DIM
    nq_blocks = t // tq
    has_cache = cache_k is not None
    past = cache_k.shape[1] if has_cache else 0
    kch = min(ATT_KCH, t)
    assert t % kch == 0 and past % kch == 0
    n_own, n_cache = t // kch, past // kch
    nch = n_own + n_cache
    q_spec = pl.BlockSpec((tq, qw), lambda b, p, iq: (row0 // tq + b * nq_blocks + iq, p))
    kv_spec = pl.BlockSpec((t, kvw), lambda b, p, iq: (row0 // t + b, p), pipeline_mode=pl.Buffered(1))
    in_specs, args = [], []
    const = lambda a: pl.BlockSpec(a.shape, lambda b, p, iq: (0,) * a.ndim)
    if is_diff:
        lam_vec, g_sub, lam_init = diff
        in_specs += [const(lam_vec), const(g_sub)]
        args += [lam_vec, g_sub]
    else:
        lam_init = 0.0
        in_specs.append(const(gq))
        args.append(gq)
    if rope is not None:
        rope_spec = pl.BlockSpec((LANES, tq), lambda b, p, iq: (0, iq))
        in_specs += [rope_spec, rope_spec]
        args += list(rope)
    in_specs += [q_spec, kv_spec, kv_spec]
    args += [q, k, v]
    if has_cache:
        c_spec = pl.BlockSpec((None, past, kvw), lambda b, p, iq: (b, 0, p))
        in_specs += [c_spec, c_spec]
        args += [cache_k, cache_v]
    n_kv = kvw // LANES
    v_rows = (LANES if is_diff else HEAD_DIM) + SUBLANES
    return pl.pallas_call(
        functools.partial(_attn_kernel, diff=is_diff, rope=rope is not None, has_cache=has_cache, tq=tq, kch=kch,
                          n_own=n_own, n_cache=n_cache, lam_init=lam_init),
        out_shape=jax.ShapeDtypeStruct((n_b * t, D_MODEL), BF16),
        grid=(n_b, n_groups, nq_blocks),
        in_specs=in_specs,
        out_specs=pl.BlockSpec((tq, qw), lambda b, p, iq: (b * nq_blocks + iq, p)),
        scratch_shapes=[pltpu.VMEM((n_kv, nch, kch, LANES), BF16),
                        pltpu.VMEM((n_kv, nch, 1 if is_diff else 2, v_rows, kch), BF16),
                        pltpu.VMEM((2, nch, kch, 2 * tq), F32)],
        compiler_params=_cparams("arbitrary", "arbitrary", "arbitrary"),
        name=("diff_attn" if is_diff else "gqa_attn") + ("_cache" if has_cache else ""),
    )(*args)


def _oproj_kernel(*refs, tm, n_ctx, n_x):
    x_refs = refs[:n_x]
    ac_ref, as_ref, w_ref, gate_ref, lng_ref, lnb_ref, o_ref = refs[n_x:]
    a = jnp.where(pl.program_id(0) * tm < n_ctx, ac_ref[...], as_ref[...])
    out = jnp.dot(a, w_ref[...], preferred_element_type=F32)
    z = DEEPNORM_ALPHA * _x_value(x_refs, tm, n_ctx) + gate_ref[...] * out
    o_ref[...] = _layer_norm_rows(z, lng_ref[...], lnb_ref[...])


def _oproj(a_ctx, a_smp, w, x, mod, lng, lnb, *, n_ctx, t_s):
    m = a_ctx.shape[0] + a_smp.shape[0]
    kdim = a_ctx.shape[1]
    tm = 512
    ncb = n_ctx // tm
    const = lambda arr: pl.BlockSpec(arr.shape, lambda i: (0,) * arr.ndim)
    x_specs, x_args = _x_specs(x, tm, n_ctx)
    return pl.pallas_call(
        functools.partial(_oproj_kernel, tm=tm, n_ctx=n_ctx, n_x=len(x_args)),
        out_shape=jax.ShapeDtypeStruct((m, D_MODEL), F32),
        grid=(m // tm,),
        in_specs=x_specs + [pl.BlockSpec((tm, kdim), lambda i: (jnp.minimum(i, ncb - 1), 0)),
                            pl.BlockSpec((tm, kdim), lambda i: (jnp.maximum(i - ncb, 0), 0)),
                            const(w), _mod_spec(2, tm, n_ctx, t_s), const(lng), const(lnb)],
        out_specs=pl.BlockSpec((tm, D_MODEL), lambda i: (i, 0)),
        compiler_params=_cparams("parallel"),
        name="oproj_ln",
    )(*x_args, a_ctx, a_smp, w, mod, lng, lnb)


def _gelu_tanh(x):
    return 0.5 * x * (1.0 + jnp.tanh(math.sqrt(2.0 / math.pi) * (x + 0.044715 * (x * x * x))))


def _rnn_oproj_kernel(hf_ref, hb_ref, br_ref, w_ref, x_ref, gate_ref, lng_ref, lnb_ref, o_ref):
    y = (hf_ref[...] + hb_ref[...]) * _gelu_tanh(br_ref[...])
    out = jnp.dot(y.astype(BF16), w_ref[...], preferred_element_type=F32)
    z = DEEPNORM_ALPHA * x_ref[...] + gate_ref[...] * out
    o_ref[...] = _layer_norm_rows(z, lng_ref[...], lnb_ref[...])


def _rnn_oproj(hs_f, hs_b, bx, w, x, mod, lng, lnb, *, n_ctx, t_s):
    m = x.shape[0]
    tm = 512
    const = lambda arr: pl.BlockSpec(arr.shape, lambda i: (0,) * arr.ndim)
    row = pl.BlockSpec((tm, D_MODEL), lambda i: (i, 0))
    rnn_row = pl.BlockSpec((tm, D_RNN), lambda i: (i, 0))
    return pl.pallas_call(
        _rnn_oproj_kernel,
        out_shape=jax.ShapeDtypeStruct((m, D_MODEL), F32),
        grid=(m // tm,),
        in_specs=[rnn_row, rnn_row, rnn_row, const(w), row,
                  _mod_spec(2, tm, n_ctx, t_s), const(lng), const(lnb)],
        out_specs=row,
        compiler_params=_cparams("parallel"),
        name="rnn_oproj_ln",
    )(hs_f, hs_b, bx, w, x, mod, lng, lnb)


def _mod_mm_kernel(x_ref, sc_ref, sh_ref, w_ref, o_ref):
    h = (x_ref[...] * (1.0 + sc_ref[...]) + sh_ref[...]).astype(BF16)
    o_ref[...] = jnp.dot(h, w_ref[...], preferred_element_type=F32)


def _rnn_in(x, mod, w, *, n_ctx, t_s):
    m = x.shape[0]
    n = w.shape[1]
    tm = 512
    return pl.pallas_call(
        _mod_mm_kernel,
        out_shape=jax.ShapeDtypeStruct((m, n), F32),
        grid=(m // tm,),
        in_specs=[pl.BlockSpec((tm, D_MODEL), lambda i: (i, 0)),
                  _mod_spec(1, tm, n_ctx, t_s), _mod_spec(0, tm, n_ctx, t_s),
                  pl.BlockSpec(w.shape, lambda i: (0, 0))],
        out_specs=pl.BlockSpec((tm, n), lambda i: (i, 0)),
        compiler_params=_cparams("parallel"),
        name="rnn_in",
    )(x, mod, mod, w)


def _rnn_gate_kernel(xb_ref, prev_ref, next_ref, cw_ref, cb_ref, wg_ref, bg_ref, lam_ref,
                     af_ref, uf_ref, ab_ref, ub_ref, ext_ref, *, tm, n_ctx, t_c, t_s):
    i = pl.program_id(0)
    r0 = i * tm
    pos = jnp.where(r0 < n_ctx, r0 % t_c, (r0 - n_ctx) % t_s)
    t_seq = jnp.where(r0 < n_ctx, t_c, t_s)
    at_start = pos == 0
    at_end = pos + tm == t_seq
    ext_ref[0:SUBLANES, :] = jnp.where(at_start, 0.0, prev_ref[...])
    ext_ref[SUBLANES:SUBLANES + tm, :] = xb_ref[...]
    ext_ref[SUBLANES + tm:, :] = jnp.where(at_end, 0.0, next_ref[...])
    xc = 0.0
    for j in range(CONV_W):
        off = SUBLANES - CONV_LEFT + j
        xc = xc + ext_ref[off:off + tm, :] * cw_ref[j:j + 1, :]
    xc = xc + cb_ref[...]
    xcb = xc.astype(BF16)
    neg_lam = -lam_ref[...]
    sp = jnp.maximum(neg_lam, 0.0) + jnp.log1p(jnp.exp(-jnp.abs(neg_lam)))

    def gate(g, ct, cols):
        k0 = GATE_K0[ct]
        pre = jnp.dot(xcb[:, k0:k0 + GATE_K], wg_ref[g, ct], preferred_element_type=F32)
        return _sigmoid(pre + bg_ref[:, g * D_RNN + cols.start:g * D_RNN + cols.stop])

    for z, (a_ref, u_ref) in enumerate(((af_ref, uf_ref), (ab_ref, ub_ref))):
        for ct in range(D_RNN // GATE_N):
            cols = slice(ct * GATE_N, (ct + 1) * GATE_N)
            r = gate(2 * z, ct, cols)
            g_in = gate(2 * z + 1, ct, cols)
            log_a = -RGLRU_C * r * sp[z:z + 1, cols]
            a = jnp.exp(log_a)
            a_ref[:, cols] = a
            u_ref[:, cols] = jnp.sqrt(-jnp.tanh(log_a) * (a * a + 1.0)) * (g_in * xc[:, cols])


def _rnn_gates(bx, conv_w, conv_b, wg, bg, lam, *, n_ctx, t_c, t_s):
    m = bx.shape[0]
    tm = 256
    n8 = tm // SUBLANES
    last8 = m // SUBLANES - 1
    const = lambda arr: pl.BlockSpec(arr.shape, lambda i: (0,) * arr.ndim)
    row = pl.BlockSpec((tm, D_RNN), lambda i: (i, 0))
    out = jax.ShapeDtypeStruct((m, D_RNN), F32)
    return pl.pallas_call(
        functools.partial(_rnn_gate_kernel, tm=tm, n_ctx=n_ctx, t_c=t_c, t_s=t_s),
        out_shape=(out, out, out, out),
        grid=(m // tm,),
        in_specs=[pl.BlockSpec((tm, D_RNN), lambda i: (i, 1)),
                  pl.BlockSpec((SUBLANES, D_RNN), lambda i: (jnp.maximum(i * n8 - 1, 0), 1)),
                  pl.BlockSpec((SUBLANES, D_RNN), lambda i: (jnp.minimum((i + 1) * n8, last8), 1)),
                  const(conv_w), const(conv_b), const(wg), const(bg), const(lam)],
        out_specs=(row, row, row, row),
        scratch_shapes=[pltpu.VMEM((tm + 2 * SUBLANES, D_RNN), F32)],
        compiler_params=_cparams("parallel"),
        name="rnn_gates",
    )(bx, bx, bx, conv_w, conv_b, wg, bg, lam)


def _rnn_scan_kernel(fblk_ref, bblk_ref, seq_ref, first_ref,
                     af_ref, uf_ref, ab_ref, ub_ref, h0_ref, hf_ref, hb_ref, fin_ref, carry_ref, *, tc):
    s = pl.program_id(0)

    @pl.when(first_ref[s] == 1)
    def _():
        carry_ref[...] = h0_ref[...]

    def step(t, carry):
        h_f, h_b = carry
        tb = tc - 1 - t
        h_f = af_ref[pl.ds(t, 1), :] * h_f + uf_ref[pl.ds(t, 1), :]
        h_b = ab_ref[pl.ds(tb, 1), :] * h_b + ub_ref[pl.ds(tb, 1), :]
        hf_ref[pl.ds(t, 1), :] = h_f
        hb_ref[pl.ds(tb, 1), :] = h_b
        return h_f, h_b

    h_f, h_b = lax.fori_loop(0, tc, step, (carry_ref[0:1, :], carry_ref[1:2, :]), unroll=8)
    carry_ref[0:1, :] = h_f
    carry_ref[1:2, :] = h_b
    fin_ref[...] = carry_ref[...]


def _rnn_scan(a_f, u_f, a_b, u_b, h0, *, seq_lens):
    m = a_f.shape[0]
    tc = SCAN_CHUNK
    fblk, bblk, seq, first = [], [], [], []
    blk0 = 0
    for si, t in enumerate(seq_lens):
        nch = t // tc
        for c in range(nch):
            fblk.append(blk0 + c)
            bblk.append(blk0 + nch - 1 - c)
            seq.append(si)
            first.append(1 if c == 0 else 0)
        blk0 += nch
    n_steps = len(fblk)
    tables = [jnp.asarray(np.array(v, np.int32)) for v in (fblk, bblk, seq, first)]
    f_spec = pl.BlockSpec((tc, D_RNN), lambda s, fb, bb, sq, fr: (fb[s], 0))
    b_spec = pl.BlockSpec((tc, D_RNN), lambda s, fb, bb, sq, fr: (bb[s], 0))
    st_spec = pl.BlockSpec((None, 2, D_RNN), lambda s, fb, bb, sq, fr: (sq[s], 0, 0))
    out = jax.ShapeDtypeStruct((m, D_RNN), F32)
    return pl.pallas_call(
        functools.partial(_rnn_scan_kernel, tc=tc),
        out_shape=(out, out, jax.ShapeDtypeStruct(h0.shape, F32)),
        grid_spec=pltpu.PrefetchScalarGridSpec(
            num_scalar_prefetch=4, grid=(n_steps,),
            in_specs=[f_spec, f_spec, b_spec, b_spec, st_spec],
            out_specs=(f_spec, b_spec, st_spec),
            scratch_shapes=[pltpu.VMEM((2, D_RNN), F32)]),
        compiler_params=_cparams("arbitrary"),
        name="rnn_scan",
    )(*tables, a_f, u_f, a_b, u_b, h0)


def _ffn_kernel(x_ref, sc_ref, sh_ref, gate_ref, wg_ref, wu_ref, wd_ref, lng_ref, lnb_ref,
                o_ref, h_ref, acc_ref):
    j = pl.program_id(1)

    @pl.when(j == 0)
    def _():
        h_ref[...] = (x_ref[...] * (1.0 + sc_ref[...]) + sh_ref[...]).astype(BF16)
        acc_ref[...] = jnp.zeros_like(acc_ref)

    h = h_ref[...]
    g = jnp.dot(h, wg_ref[...], preferred_element_type=F32)
    u = jnp.dot(h, wu_ref[...], preferred_element_type=F32)
    act = ((g * _sigmoid(g)) * u).astype(BF16)
    acc_ref[...] += jnp.dot(act, wd_ref[...], preferred_element_type=F32)

    @pl.when(j == pl.num_programs(1) - 1)
    def _():
        z = DEEPNORM_ALPHA * x_ref[...] + gate_ref[...] * acc_ref[...]
        o_ref[...] = _layer_norm_rows(z, lng_ref[...], lnb_ref[...])


def _ffn(x, mod, w_gu, w_down, lng, lnb, *, n_ctx, t_s):
    m = x.shape[0]
    tm = 512
    tf = D_FF // 2
    nf = D_FF // tf
    const = lambda arr: pl.BlockSpec(arr.shape, lambda i, j: (0,) * arr.ndim)
    row = pl.BlockSpec((tm, D_MODEL), lambda i, j: (i, 0))
    return pl.pallas_call(
        _ffn_kernel,
        out_shape=jax.ShapeDtypeStruct((m, D_MODEL), F32),
        grid=(m // tm, nf),
        in_specs=[row, _mod_spec(4, tm, n_ctx, t_s), _mod_spec(3, tm, n_ctx, t_s), _mod_spec(5, tm, n_ctx, t_s),
                  pl.BlockSpec((D_MODEL, tf), lambda i, j: (0, j)),
                  pl.BlockSpec((D_MODEL, tf), lambda i, j: (0, nf + j)),
                  pl.BlockSpec((tf, D_MODEL), lambda i, j: (j, 0)),
                  const(lng), const(lnb)],
        out_specs=row,
        scratch_shapes=[pltpu.VMEM((tm, D_MODEL), BF16), pltpu.VMEM((tm, D_MODEL), F32)],
        compiler_params=_cparams("parallel", "arbitrary"),
        name="ffn",
    )(x, mod, mod, mod, w_gu, w_gu, w_down, lng, lnb)


ROW_TILES = D_MODEL // LANES


def _to_token_rows(ref, base, value):
    n = value.shape[0]
    for j in range(ROW_TILES):
        ref[pl.ds(base + j, n, stride=ROW_TILES), :] = value[:, j * LANES:(j + 1) * LANES]


def _from_token_rows(ref, base, n):
    return jnp.concatenate([ref[pl.ds(base + j, n, stride=ROW_TILES), :] for j in range(ROW_TILES)], axis=1)


def _token_row(ref, r):
    start = r * ROW_TILES if isinstance(r, int) else pl.multiple_of(r * ROW_TILES, ROW_TILES)
    return ref.at[pl.ds(start, ROW_TILES)]


def _route_kernel(x_ref, sc_ref, sh_ref, wr_ref, route_ref, gates_ref, cnt_ref, carry_ref, *, tm):
    @pl.when(pl.program_id(0) == 0)
    def _():
        carry_ref[...] = jnp.zeros_like(carry_ref)

    lane = lax.broadcasted_iota(jnp.int32, (tm, LANES), 1)
    h = x_ref[...] * (1.0 + sc_ref[...]) + sh_ref[...]
    logits = jnp.dot(h.astype(BF16), wr_ref[...], preferred_element_type=F32)
    s1 = jnp.where(lane < N_EXPERTS, logits, -jnp.inf)
    m1 = jnp.max(s1, axis=-1, keepdims=True)
    i1 = jnp.min(jnp.where(s1 == m1, lane, LANES), axis=-1, keepdims=True)
    s2 = jnp.where(lane == i1, -jnp.inf, s1)
    m2 = jnp.max(s2, axis=-1, keepdims=True)
    i2 = jnp.min(jnp.where(s2 == m2, lane, LANES), axis=-1, keepdims=True)
    e2 = jnp.exp(m2 - m1)
    den = 1.0 + e2
    hit = jnp.where((lane == i1) | (lane == i2), 1.0, 0.0)
    earlier = lax.broadcasted_iota(jnp.int32, (tm, tm), 1) < lax.broadcasted_iota(jnp.int32, (tm, tm), 0)
    before = carry_ref[...] + jnp.dot(jnp.where(earlier, 1.0, 0.0).astype(BF16), hit.astype(BF16),
                                      preferred_element_type=F32)
    r1 = jnp.sum(jnp.where(lane == i1, before, 0.0), axis=-1, keepdims=True).astype(jnp.int32)
    r2 = jnp.sum(jnp.where(lane == i2, before, 0.0), axis=-1, keepdims=True).astype(jnp.int32)
    carry_ref[...] += jnp.sum(hit, axis=0, keepdims=True)
    route = jnp.where(lane == 0, i1, jnp.where(lane == 1, i2, jnp.where(lane == 2, r1, jnp.where(lane == 3, r2, 0))))
    route_ref[...] = route[:, :ROUTE_COLS]
    gates_ref[...] = jnp.where(lane == 0, 1.0 / den, jnp.where(lane == 1, e2 / den, 0.0))[:, :ROUTE_COLS]
    cnt_ref[...] = jnp.broadcast_to(carry_ref[...], cnt_ref.shape)


def _moe_route(x, mod, w_router, *, n_ctx, t_s):
    m = x.shape[0]
    tm = 512
    small = pl.BlockSpec((tm, ROUTE_COLS), lambda i: (i, 0))
    return pl.pallas_call(
        functools.partial(_route_kernel, tm=tm),
        out_shape=(jax.ShapeDtypeStruct((m, ROUTE_COLS), jnp.int32), jax.ShapeDtypeStruct((m, ROUTE_COLS), F32),
                   jax.ShapeDtypeStruct((SUBLANES, LANES), F32)),
        grid=(m // tm,),
        in_specs=[pl.BlockSpec((tm, D_MODEL), lambda i: (i, 0)), _mod_spec(4, tm, n_ctx, t_s),
                  _mod_spec(3, tm, n_ctx, t_s), pl.BlockSpec(w_router.shape, lambda i: (0, 0))],
        out_specs=(small, small, pl.BlockSpec((SUBLANES, LANES), lambda i: (0, 0))),
        scratch_shapes=[pltpu.VMEM((1, LANES), F32)],
        compiler_params=_cparams("arbitrary"),
        name="moe_route",
    )(x, mod, mod, w_router)


def _dispatch_kernel(slot_ref, x_ref, sc_ref, sh_ref, xs_hbm, buf, sem, *, tok):
    i = pl.program_id(0)
    par = i % 2
    cur = buf.at[par]
    _to_token_rows(cur, 0, x_ref[...] * (1.0 + sc_ref[...]) + sh_ref[...])

    def issue(t, c):
        src = _token_row(cur, t)
        pltpu.make_async_copy(src, _token_row(xs_hbm, slot_ref[2 * t]), sem.at[par]).start(priority=0)
        pltpu.make_async_copy(src, _token_row(xs_hbm, slot_ref[2 * t + 1]), sem.at[par]).start(priority=1)
        return c
    lax.fori_loop(0, tok, issue, 0, unroll=4)

    def drain(which):
        def body(t, c):
            pltpu.make_async_copy(_token_row(buf.at[which], 0), _token_row(xs_hbm, 0), sem.at[which]).wait()
            return c
        lax.fori_loop(0, 2 * tok, body, 0, unroll=8)

    @pl.when(i > 0)
    def _():
        drain(1 - par)

    @pl.when(i == pl.num_programs(0) - 1)
    def _():
        drain(par)


def _moe_dispatch(slots, x, mod, *, n_ctx, t_s):
    m = x.shape[0]
    tok = 512
    return pl.pallas_call(
        functools.partial(_dispatch_kernel, tok=tok),
        out_shape=jax.ShapeDtypeStruct((2 * m * ROW_TILES, LANES), F32),
        grid=(m // tok,),
        in_specs=[pl.BlockSpec((2 * tok,), lambda i: (i,), memory_space=pltpu.SMEM),
                  pl.BlockSpec((tok, D_MODEL), lambda i: (i, 0)),
                  _mod_spec(4, tok, n_ctx, t_s), _mod_spec(3, tok, n_ctx, t_s)],
        out_specs=pl.BlockSpec(memory_space=pl.ANY),
        scratch_shapes=[pltpu.VMEM((2, tok * ROW_TILES, LANES), F32), pltpu.SemaphoreType.DMA((2,))],
        compiler_params=_cparams("arbitrary"),
        name="moe_dispatch",
    )(slots, x, mod, mod)


def _experts_kernel(wt_ref, we_ref, lo_ref, hi_ref, nw_ref, xs_ref, wg_ref, wu_ref, wd_ref, o_ref, *, rows):
    del we_ref
    w = pl.program_id(0)

    @pl.when(w < nw_ref[0])
    def _():
        xs = _from_token_rows(xs_ref, 0, rows).astype(BF16)
        g = jnp.dot(xs, wg_ref[...], preferred_element_type=F32)
        u = jnp.dot(xs, wu_ref[...], preferred_element_type=F32)
        act = ((g * _sigmoid(g)) * u).astype(BF16)
        y = jnp.dot(act, wd_ref[...], preferred_element_type=F32)
        whole = jnp.logical_and(lo_ref[w] == 0, hi_ref[w] == rows)
        first = jnp.logical_or(w == 0, wt_ref[jnp.maximum(w - 1, 0)] != wt_ref[w])

        @pl.when(whole)
        def _():
            _to_token_rows(o_ref, 0, y)

        @pl.when(jnp.logical_not(whole))
        def _():
            row = lax.broadcasted_iota(jnp.int32, (rows, D_MODEL), 0)
            mine = (row >= lo_ref[w]) & (row < hi_ref[w])

            @pl.when(first)
            def _():
                _to_token_rows(o_ref, 0, jnp.where(mine, y, 0.0))

            @pl.when(jnp.logical_not(first))
            def _():
                _to_token_rows(o_ref, 0, jnp.where(mine, y, _from_token_rows(o_ref, 0, rows)))


def _moe_experts(items, xs, w_gu, w_down, layer):
    rows = MOE_TILE
    n_items = items[0].shape[0]
    tile = lambda w, wt, we, lo, hi, nw: (wt[w], 0)
    return pl.pallas_call(
        functools.partial(_experts_kernel, rows=rows),
        out_shape=jax.ShapeDtypeStruct(xs.shape, F32),
        grid_spec=pltpu.PrefetchScalarGridSpec(
            num_scalar_prefetch=5, grid=(n_items,),
            in_specs=[pl.BlockSpec((rows * ROW_TILES, LANES), tile),
                      pl.BlockSpec((None, None, D_MODEL, D_FF_EXPERT),
                                   lambda w, wt, we, lo, hi, nw: (layer, we[w], 0, 0)),
                      pl.BlockSpec((None, None, D_MODEL, D_FF_EXPERT),
                                   lambda w, wt, we, lo, hi, nw: (layer, we[w], 0, 1)),
                      pl.BlockSpec((None, None, D_FF_EXPERT, D_MODEL),
                                   lambda w, wt, we, lo, hi, nw: (layer, we[w], 0, 0))],
            out_specs=pl.BlockSpec((rows * ROW_TILES, LANES), tile)),
        compiler_params=_cparams("arbitrary"),
        name="moe_experts",
    )(*items, xs, w_gu, w_gu, w_down)


def _combine_kernel(idx_ref, nxt_ref, ys_hbm, gates_ref, x_ref, gate_ref, lng_ref, lnb_ref, o_ref, buf, sem, *, tm):
    i = pl.program_id(0)
    slot = i % 2

    def issue(ref, s):
        def body(t, c):
            pltpu.make_async_copy(_token_row(ys_hbm, ref[2 * t]), _token_row(buf.at[s], t),
                                  sem.at[s]).start(priority=0)
            pltpu.make_async_copy(_token_row(ys_hbm, ref[2 * t + 1]), _token_row(buf.at[s], tm + t),
                                  sem.at[s]).start(priority=1)
            return c
        lax.fori_loop(0, tm, body, 0, unroll=4)

    @pl.when(i == 0)
    def _():
        issue(idx_ref, 0)

    @pl.when(i + 1 < pl.num_programs(0))
    def _():
        issue(nxt_ref, 1 - slot)

    def drain(r, c):
        pltpu.make_async_copy(_token_row(ys_hbm, 0), _token_row(buf.at[slot], 0), sem.at[slot]).wait()
        return c
    lax.fori_loop(0, 2 * tm, drain, 0, unroll=8)
    cur = buf.at[slot]
    y = (gates_ref[:, 0:1] * _from_token_rows(cur, 0, tm)
         + gates_ref[:, 1:2] * _from_token_rows(cur, tm * ROW_TILES, tm))
    z = DEEPNORM_ALPHA * x_ref[...] + gate_ref[...] * y
    o_ref[...] = _layer_norm_rows(z, lng_ref[...], lnb_ref[...])


def _moe_combine(slots, ys, gates, x, mod, lng, lnb, *, n_ctx, t_s, row0=0, n_rows=None):
    tm = 256
    n_blocks = (x.shape[0] if n_rows is None else n_rows) // tm
    b0 = row0 // tm
    idx = lambda f: pl.BlockSpec((2 * tm,), f, memory_space=pltpu.SMEM)
    const = lambda arr: pl.BlockSpec(arr.shape, lambda i: (0,) * arr.ndim)
    return pl.pallas_call(
        functools.partial(_combine_kernel, tm=tm),
        out_shape=jax.ShapeDtypeStruct((n_blocks * tm, D_MODEL), F32),
        grid=(n_blocks,),
        in_specs=[idx(lambda i: (b0 + i,)), idx(lambda i: (b0 + jnp.minimum(i + 1, n_blocks - 1),)),
                  pl.BlockSpec(memory_space=pl.ANY), pl.BlockSpec((tm, ROUTE_COLS), lambda i: (b0 + i, 0)),
                  pl.BlockSpec((tm, D_MODEL), lambda i: (b0 + i, 0)),
                  pl.BlockSpec((None, 1, D_MODEL), lambda i: (_seg_of_block(b0 + i, tm, n_ctx, t_s), 0, 5)),
                  const(lng), const(lnb)],
        out_specs=pl.BlockSpec((tm, D_MODEL), lambda i: (i, 0)),
        scratch_shapes=[pltpu.VMEM((2, 2 * tm * ROW_TILES, LANES), F32), pltpu.SemaphoreType.DMA((2,))],
        compiler_params=_cparams("arbitrary"),
        name="moe_combine",
    )(slots, slots, ys, gates, x, mod, lng, lnb)


def _moe_work_items(counts, n_rows):
    n_tiles = n_rows // MOE_TILE
    n_items = n_tiles + N_EXPERTS - 1
    g_end = jnp.cumsum(counts)
    g_start = g_end - counts
    row0 = jnp.arange(n_tiles, dtype=jnp.int32) * MOE_TILE
    first_e = jnp.sum(row0[:, None] >= g_end[None, :], axis=1)
    last_e = jnp.sum((row0 + MOE_TILE - 1)[:, None] >= g_end[None, :], axis=1)
    per_tile = last_e - first_e + 1
    item0 = jnp.cumsum(per_tile) - per_tile
    w = jnp.arange(n_items, dtype=jnp.int32)
    tile = jnp.clip(jnp.sum(w[:, None] >= item0[None, :], axis=1) - 1, 0, n_tiles - 1)
    expert = jnp.clip(first_e[tile] + w - item0[tile], 0, N_EXPERTS - 1)
    lo = jnp.clip(g_start[expert] - row0[tile], 0, MOE_TILE)
    hi = jnp.clip(g_end[expert] - row0[tile], 0, MOE_TILE)
    n_used = jnp.sum(per_tile).reshape(1)
    i32 = lambda a: a.astype(jnp.int32)
    return i32(tile), i32(expert), i32(lo), i32(hi), i32(n_used), g_start


def _moe(x, mod, w_router, w_gu, w_down, layer, lng, lnb, *, n_ctx, t_s, split=False):
    m = x.shape[0]
    assert (2 * m) % MOE_TILE == 0
    route, gates, cnt = _moe_route(x, mod, w_router, n_ctx=n_ctx, t_s=t_s)
    counts = cnt[0, :N_EXPERTS].astype(jnp.int32)
    *items, g_start = _moe_work_items(counts, 2 * m)
    slots = (g_start[route[:, 0:2]] + route[:, 2:4]).reshape(2 * m).astype(jnp.int32)
    xs = _moe_dispatch(slots, x, mod, n_ctx=n_ctx, t_s=t_s)
    ys = _moe_experts(items, xs, w_gu, w_down, layer)
    comb = functools.partial(_moe_combine, slots, ys, gates, x, mod, lng, lnb, n_ctx=n_ctx, t_s=t_s)
    if split:
        return comb(row0=0, n_rows=n_ctx), comb(row0=n_ctx, n_rows=m - n_ctx)
    return comb()


def _rope_tables(n_tokens):
    n_rows = n_tokens // GRID_W
    row = jnp.repeat(jnp.arange(n_rows, dtype=F32), GRID_W)
    col = jnp.tile(jnp.arange(GRID_W, dtype=F32), n_rows)
    n_axis = HEAD_DIM // 4
    inv = ROPE_THETA ** (-jnp.arange(n_axis, dtype=F32) / n_axis)
    ang = jnp.concatenate([row[:, None] * inv, col[:, None] * inv], axis=-1)
    cos, sin = jnp.cos(ang), jnp.sin(ang)
    reps = LANES // HEAD_DIM
    cos_t = jnp.tile(jnp.concatenate([cos, cos], axis=-1), (1, reps))
    sin_t = jnp.tile(jnp.concatenate([-sin, sin], axis=-1), (1, reps))
    return cos_t, sin_t


def _gqa_head_perm():
    cols = []
    per_pair = 2 * (N_Q_HEADS // N_KV_HEADS)
    for p in range(N_KV_HEADS // 2):
        for j in range(N_Q_HEADS // N_KV_HEADS):
            for hd in (per_pair * p + j, per_pair * p + per_pair // 2 + j):
                cols.append(np.arange(hd * HEAD_DIM, (hd + 1) * HEAD_DIM))
    return np.concatenate(cols)


def _block_diag_windows(w):
    eye = jnp.eye(N_RNN_BLOCKS, dtype=bool)[:, None, :, None]
    dense = jnp.where(eye, w[:, :, None, :], 0.0).reshape(D_RNN, D_RNN)
    for ct, k0 in enumerate(GATE_K0):
        lo, hi = ct * GATE_N, (ct + 1) * GATE_N
        assert k0 <= RNN_BLOCK * (lo // RNN_BLOCK) and RNN_BLOCK * -(-hi // RNN_BLOCK) <= k0 + GATE_K <= D_RNN
    return jnp.stack([dense[k0:k0 + GATE_K, ct * GATE_N:(ct + 1) * GATE_N] for ct, k0 in enumerate(GATE_K0)])


def kernel(x_prompt, x_sample, cache_attn_k, cache_attn_v, state_rglru, cache_diff_k, cache_diff_v, c, c_ctx, w_ada, b_ada, ln_g, ln_b, attn_w_qkv, attn_g_q, attn_g_k, attn_w_o, rnn_w_in, rnn_conv_w, rnn_conv_b, rnn_w_a, rnn_b_a, rnn_w_x, rnn_b_x, rnn_lambda, rnn_w_out, diff_w_qkv, diff_lambda, diff_g_sub, diff_w_o, ffn_w_gu, ffn_w_down, moe_w_router, moe_w_gu, moe_w_down):
    b_c, t_c, d = x_prompt.shape
    b_s, t_s, _ = x_sample.shape
    n_ctx = b_c * t_c
    n_s = b_s * t_s
    past = cache_attn_k.shape[2]
    assert d == D_MODEL and n_ctx % t_s == 0 and t_c == SCAN_CHUNK and t_s % SCAN_CHUNK == 0
    assert 1 + b_s <= COND_ROWS

    assert DEPTH % 2 == 0
    x = (x_prompt.reshape(n_ctx, d), x_sample.reshape(n_s, d))
    cond =jnp.zeros((COND_ROWS, d), F32).at[0].set(c_ctx).at[1:1 + b_s].set(c)
    mods = _ada_table(cond, w_ada, b_ada)[:, :1 + b_s].reshape(DEPTH, 1 + b_s, 1, 6 * d)

    cos_t, sin_t = _rope_tables(t_s)
    rope_t = (cos_t.T, sin_t.T)
    perm = _gqa_head_perm()
    nq = N_Q_HEADS * HEAD_DIM
    tile2 = lambda g: jnp.tile(g, LANES // g.shape[-1]).reshape(1, LANES)
    kw = dict(n_ctx=n_ctx, t_s=t_s)
    moe_gu_b = moe_w_gu.astype(BF16)
    moe_down_b = moe_w_down.astype(BF16)

    attn_k, attn_v, rnn_s, diff_k, diff_v = [], [], [], [], []
    for li in range(DEPTH):
        mod = mods[li]
        lng = ln_g[li].reshape(2, 1, d)
        lnb = ln_b[li].reshape(2, 1, d)
        j = li // N_MIXERS
        kind = li % N_MIXERS
        if kind == 0:
            w = attn_w_qkv[j]
            w = jnp.concatenate([w[:, :nq][:, perm], w[:, nq:]], axis=1).astype(BF16)
            q, kb, vb, kf, vf = _gqa_qkv(x, mod, w, tile2(attn_g_k[j]), cos_t, sin_t, **kw)
            gq = tile2(attn_g_q[j]).reshape(LANES, 1)
            ck = cache_attn_k[:, j].reshape(b_s, past, N_KV_HEADS * HEAD_DIM).astype(BF16)
            cv = cache_attn_v[:, j].reshape(b_s, past, N_KV_HEADS * HEAD_DIM).astype(BF16)
            o_c = _attention(q, kb, vb, None, None, row0=0, n_b=b_c, t=t_c, tq=t_c, gq=gq)
            o_s = _attention(q, kb, vb, ck, cv, row0=n_ctx, n_b=b_s, t=t_s, tq=512, gq=gq, rope=rope_t)
            x = _oproj(o_c, o_s, attn_w_o[j][perm, :].astype(BF16), x, mod, lng[0], lnb[0], **kw)
            attn_k.append(kf[:n_ctx])
            attn_v.append(vf[:n_ctx])
        elif kind == 1:
            bx = _rnn_in(x, mod, rnn_w_in[j].astype(BF16), **kw)
            wg = jnp.stack([_block_diag_windows(rnn_w_a[j, 0]), _block_diag_windows(rnn_w_x[j, 0]),
                            _block_diag_windows(rnn_w_a[j, 1]), _block_diag_windows(rnn_w_x[j, 1])]).astype(BF16)
            bg = jnp.concatenate([rnn_b_a[j, 0], rnn_b_x[j, 0], rnn_b_a[j, 1], rnn_b_x[j, 1]]).reshape(1, -1)
            a_f, u_f, a_b, u_b = _rnn_gates(bx, rnn_conv_w[j], rnn_conv_b[j].reshape(1, -1), wg, bg,
                                            rnn_lambda[j], n_ctx=n_ctx, t_c=t_c, t_s=t_s)
            h0 = jnp.concatenate([jnp.zeros((b_c, 2, D_RNN), F32), state_rglru[:, j]], axis=0)
            hs_f, hs_b, fin = _rnn_scan(a_f, u_f, a_b, u_b, h0, seq_lens=[t_c] * b_c + [t_s] * b_s)
            x = _rnn_oproj(hs_f, hs_b, bx, rnn_w_out[j].astype(BF16), x, mod, lng[0], lnb[0], **kw)
            rnn_s.append(fin[:b_c])
        else:
            lam_init = 0.8 - 0.6 * math.exp(-0.3 * li)
            q, kb, vb, kf, vf = _diff_qkv(x, mod, diff_w_qkv[j].astype(BF16), cos_t, sin_t, **kw)
            ck = cache_diff_k[:, j].reshape(b_s, past, D_MODEL).astype(BF16)
            cv = cache_diff_v[:, j].reshape(b_s, past, D_MODEL).astype(BF16)
            g_sub = diff_g_sub[j].reshape(1, LANES)
            dargs = (diff_lambda[j], g_sub, lam_init)
            o_c = _attention(q, kb, vb, None, None, row0=0, n_b=b_c, t=t_c, tq=t_c, diff=dargs)
            o_s = _attention(q, kb, vb, ck, cv, row0=n_ctx, n_b=b_s, t=t_s, tq=256, rope=rope_t, diff=dargs)
            x = _oproj(o_c, o_s, diff_w_o[j].astype(BF16), x, mod, lng[0], lnb[0], **kw)
            diff_k.append(kf[:n_ctx])
            diff_v.append(vf[:n_ctx])
        if li % 2 == 0:
            x = _ffn(x, mod, ffn_w_gu[li // 2].astype(BF16), ffn_w_down[li // 2].astype(BF16), lng[1], lnb[1], **kw)
        else:
            w_r = jnp.zeros((d, LANES), F32).at[:, :N_EXPERTS].set(moe_w_router[li // 2]).astype(BF16)
            x = _moe(x, mod, w_r, moe_gu_b, moe_down_b, li // 2, lng[1], lnb[1], split=li == DEPTH - 1, **kw)

    y_prompt = x[0].reshape(b_c, t_c, d)
    y_sample = x[1].reshape(b_s, t_s, d)
    new_attn_k = jnp.stack([k.reshape(b_c, t_c, N_KV_HEADS, HEAD_DIM) for k in attn_k], axis=1)
    new_attn_v = jnp.stack([v.reshape(b_c, t_c, N_KV_HEADS, HEAD_DIM) for v in attn_v], axis=1)
    new_state = jnp.stack(rnn_s, axis=1)
    new_diff_k = jnp.stack([k.reshape(b_c, t_c, N_DIFF_HEADS, 2, HEAD_DIM) for k in diff_k], axis=1)
    new_diff_v = jnp.stack([v.reshape(b_c, t_c, N_DIFF_HEADS, 2 * HEAD_DIM) for v in diff_v], axis=1)
    return (y_prompt, y_sample, new_attn_k, new_attn_v, new_state, new_diff_k, new_diff_v)
```

```python
import functools
import math

import jax
import jax.numpy as jnp
import numpy as np
from jax import lax
from jax.experimental import pallas as pl
from jax.experimental.pallas import tpu as pltpu

F32 = jnp.float32
BF16 = jnp.bfloat16

D_MODEL = 1024
DEPTH = 4
GRID_W = 64
HEAD_DIM = 64
N_Q_HEADS = 16
N_KV_HEADS = 4
ROPE_THETA = 10000.0
N_DIFF_HEADS = 8
D_RNN = 1280
N_RNN_BLOCKS = 16
RNN_BLOCK = 80
CONV_W = 4
CONV_LEFT = 2
RGLRU_C = 8.0
D_FF = 2816
N_EXPERTS = 8
D_FF_EXPERT = 1408
N_MIXERS = 3
DEEPNORM_ALPHA = (2.0 * DEPTH) ** 0.25
LN_EPS = 1e-6
RMS_EPS = 1e-6

LANES = 128
SUBLANES = 8
VMEM_LIMIT = 56 * 1024 * 1024
COND_ROWS = 8
SCAN_CHUNK = 256
MOE_TILE = 512
ROUTE_COLS = 8
GATE_N = 256
GATE_K = 512
GATE_K0 = (0, 128, 384, 640, 768)
Q_SCALE = HEAD_DIM ** -0.5 * math.log2(math.e)


def _cparams(*sem):
    return pltpu.CompilerParams(dimension_semantics=sem, vmem_limit_bytes=VMEM_LIMIT)


def _sigmoid(x):
    return 0.5 * jnp.tanh(0.5 * x) + 0.5


def _layer_norm_rows(z, g, b):
    mu = jnp.mean(z, axis=-1, keepdims=True)
    zc = z - mu
    var = jnp.mean(zc * zc, axis=-1, keepdims=True)
    return zc * lax.rsqrt(var + LN_EPS) * g + b


def _seg_of_block(i, tm, n_ctx, t_s):
    r0 = i * tm
    return jnp.where(r0 < n_ctx, 0, 1 + (r0 - n_ctx) // t_s)


def _mod_spec(col, tm, n_ctx, t_s):
    return pl.BlockSpec((None, 1, D_MODEL), lambda i, *_: (_seg_of_block(i, tm, n_ctx, t_s), 0, col))


def _ada_kernel(c_ref, w_ref, b_ref, o_ref):
    c = c_ref[...]
    a = (c * _sigmoid(c)).astype(BF16)
    o_ref[...] = jnp.dot(a, w_ref[...].astype(BF16), preferred_element_type=F32) + b_ref[...]


def _ada_table(cond, w_ada, b_ada):
    n_l, d, n = w_ada.shape
    tn = 1536
    return pl.pallas_call(
        _ada_kernel,
        out_shape=jax.ShapeDtypeStruct((n_l, COND_ROWS, n), F32),
        grid=(n_l, n // tn),
        in_specs=[pl.BlockSpec((COND_ROWS, d), lambda l, j: (0, 0)),
                  pl.BlockSpec((None, d, tn), lambda l, j: (l, 0, j)),
                  pl.BlockSpec((None, 1, tn), lambda l, j: (l, 0, j))],
        out_specs=pl.BlockSpec((None, COND_ROWS, tn), lambda l, j: (l, 0, j)),
        compiler_params=_cparams("parallel", "parallel"),
        name="ada_table",
    )(cond, w_ada, b_ada.reshape(n_l, 1, n))


def _head_masks(tm):
    lane = lax.broadcasted_iota(jnp.int32, (tm, LANES), 1)
    return lane < HEAD_DIM, (lane & (HEAD_DIM - 1)) < HEAD_DIM // 2


def _rope_tile(t, first_half, cos, sin):
    partner = jnp.where(first_half, pltpu.roll(t, LANES - HEAD_DIM // 2, 1),
                        pltpu.roll(t, HEAD_DIM // 2, 1))
    return t * cos + partner * sin


def _x_specs(x, tm, n_ctx):
    if not isinstance(x, tuple):
        return [pl.BlockSpec((tm, D_MODEL), lambda i, *_: (i, 0))], [x]
    ncb = n_ctx // tm
    return ([pl.BlockSpec((tm, D_MODEL), lambda i, *_: (jnp.minimum(i, ncb - 1), 0)),
             pl.BlockSpec((tm, D_MODEL), lambda i, *_: (jnp.maximum(i - ncb, 0), 0))], list(x))


def _x_value(x_refs, tm, n_ctx):
    if len(x_refs) == 1:
        return x_refs[0][...]
    return jnp.where(pl.program_id(0) * tm < n_ctx, x_refs[0][...], x_refs[1][...])


def _gqa_qkv_kernel(*refs, tm, n_ctx, n_x):
    x_refs = refs[:n_x]
    sc_ref, sh_ref, w_ref, gk_ref, cos_ref, sin_ref, q_ref, kb_ref, vb_ref, kf_ref, vf_ref = refs[n_x:]
    i = pl.program_id(0)
    h = (_x_value(x_refs, tm, n_ctx) * (1.0 + sc_ref[...]) + sh_ref[...]).astype(BF16)
    y = jnp.dot(h, w_ref[...], preferred_element_type=F32)
    use_rope = i * tm >= n_ctx
    cos = jnp.where(use_rope, cos_ref[...], 1.0)
    sin = jnp.where(use_rope, sin_ref[...], 0.0)
    low_head, first_half = _head_masks(tm)

    def norm_rope(t, g):
        t2 = t * t
        s_lo = jnp.sum(jnp.where(low_head, t2, 0.0), axis=-1, keepdims=True)
        s_hi = jnp.sum(jnp.where(low_head, 0.0, t2), axis=-1, keepdims=True)
        inv = jnp.where(low_head, lax.rsqrt(s_lo * (1.0 / HEAD_DIM) + RMS_EPS),
                        lax.rsqrt(s_hi * (1.0 / HEAD_DIM) + RMS_EPS))
        return _rope_tile(t * inv * g, first_half, cos, sin)

    nq = N_Q_HEADS * HEAD_DIM
    nkv = N_KV_HEADS * HEAD_DIM
    gk = gk_ref[...]
    q_ref[...] = y[:, :nq].astype(BF16)
    k = jnp.concatenate([norm_rope(y[:, nq + t * LANES: nq + (t + 1) * LANES], gk) for t in range(nkv // LANES)],
                        axis=1)
    v = y[:, nq + nkv:]
    kb_ref[...] = k.astype(BF16)
    vb_ref[...] = v.astype(BF16)

    @pl.when(i * tm < n_ctx)
    def _():
        kf_ref[...] = k
        vf_ref[...] = v


def _gqa_qkv(x, mod, w, gk, cos, sin, *, n_ctx, t_s):
    m = sum(a.shape[0] for a in x) if isinstance(x, tuple) else x.shape[0]
    tm = 512
    x_specs, x_args = _x_specs(x, tm, n_ctx)
    nq = N_Q_HEADS * HEAD_DIM
    nkv = N_KV_HEADS * HEAD_DIM
    n_rope_blocks = t_s // tm
    rope_spec = pl.BlockSpec(
        (tm, LANES), lambda i: (jnp.where(i * tm >= n_ctx, ((i * tm - n_ctx) % t_s) // tm, 0) % n_rope_blocks, 0))
    row = lambda n: pl.BlockSpec((tm, n), lambda i: (i, 0))
    ctx_row = lambda n: pl.BlockSpec((tm, n), lambda i: (jnp.minimum(i, n_ctx // tm - 1), 0))
    const = lambda a: pl.BlockSpec(a.shape, lambda i: (0,) * a.ndim)
    return pl.pallas_call(
        functools.partial(_gqa_qkv_kernel, tm=tm, n_ctx=n_ctx, n_x=len(x_args)),
        out_shape=(jax.ShapeDtypeStruct((m, nq), BF16), jax.ShapeDtypeStruct((m, nkv), BF16),
                   jax.ShapeDtypeStruct((m, nkv), BF16), jax.ShapeDtypeStruct((n_ctx, nkv), F32),
                   jax.ShapeDtypeStruct((n_ctx, nkv), F32)),
        grid=(m // tm,),
        in_specs=x_specs + [_mod_spec(1, tm, n_ctx, t_s), _mod_spec(0, tm, n_ctx, t_s),
                            const(w), const(gk), rope_spec, rope_spec],
        out_specs=(row(nq), row(nkv), row(nkv), ctx_row(nkv), ctx_row(nkv)),
        compiler_params=_cparams("arbitrary"),
        name="gqa_qkv",
    )(*x_args, mod, mod, w, gk, cos, sin)


def _diff_qkv_kernel(x_ref, sc_ref, sh_ref, w_ref, cos_ref, sin_ref,
                     q_ref, kb_ref, vb_ref, kf_ref, vf_ref, *, tm, n_ctx):
    i = pl.program_id(0)
    h = (x_ref[...] * (1.0 + sc_ref[...]) + sh_ref[...]).astype(BF16)
    y = jnp.dot(h, w_ref[...], preferred_element_type=F32)
    use_rope = i * tm >= n_ctx
    cos = jnp.where(use_rope, cos_ref[...], 1.0)
    sin = jnp.where(use_rope, sin_ref[...], 0.0)
    _, first_half = _head_masks(tm)
    q_ref[...] = y[:, :D_MODEL].astype(BF16)
    k = jnp.concatenate([_rope_tile(y[:, D_MODEL + t * LANES: D_MODEL + (t + 1) * LANES], first_half, cos, sin)
                         for t in range(D_MODEL // LANES)], axis=1)
    v = y[:, 2 * D_MODEL:]
    kb_ref[...] = k.astype(BF16)
    vb_ref[...] = v.astype(BF16)

    @pl.when(i * tm < n_ctx)
    def _():
        kf_ref[...] = k
        vf_ref[...] = v


def _diff_qkv(x, mod, w, cos, sin, *, n_ctx, t_s):
    m = x.shape[0]
    tm = 256
    n_rope_blocks = t_s // tm
    rope_spec = pl.BlockSpec(
        (tm, LANES), lambda i: (jnp.where(i * tm >= n_ctx, ((i * tm - n_ctx) % t_s) // tm, 0) % n_rope_blocks, 0))
    row = pl.BlockSpec((tm, D_MODEL), lambda i: (i, 0))
    ctx_row = pl.BlockSpec((tm, D_MODEL), lambda i: (jnp.minimum(i, n_ctx // tm - 1), 0))
    return pl.pallas_call(
        functools.partial(_diff_qkv_kernel, tm=tm, n_ctx=n_ctx),
        out_shape=(jax.ShapeDtypeStruct((m, D_MODEL), BF16), jax.ShapeDtypeStruct((m, D_MODEL), BF16),
                   jax.ShapeDtypeStruct((m, D_MODEL), BF16), jax.ShapeDtypeStruct((n_ctx, D_MODEL), F32),
                   jax.ShapeDtypeStruct((n_ctx, D_MODEL), F32)),
        grid=(m // tm,),
        in_specs=[row, _mod_spec(1, tm, n_ctx, t_s), _mod_spec(0, tm, n_ctx, t_s),
                  pl.BlockSpec(w.shape, lambda i: (0, 0)), rope_spec, rope_spec],
        out_specs=(row, row, row, ctx_row, ctx_row),
        compiler_params=_cparams("arbitrary"),
        name="diff_qkv",
    )(x, mod, mod, w, cos, sin)


ATT_GROUP = 4
ATT_KCH = 512
ATT_UNROLL = 3


def _attn_kernel(*refs, diff, rope, has_cache, tq, kch, n_own, n_cache, lam_init):
    refs = list(refs)
    if diff:
        lam_ref, g_ref = refs[:2]
        refs = refs[2:]
    else:
        gq_ref = refs[0]
        refs = refs[1:]
    if rope:
        cos_ref, sin_ref = refs[:2]
        refs = refs[2:]
    if has_cache:
        q_ref, k_ref, v_ref, ck_ref, cv_ref, o_ref, k_scr, vt_scr, s_scr = refs
    else:
        q_ref, k_ref, v_ref, o_ref, k_scr, vt_scr, s_scr = refs
    nch = n_own + n_cache
    n_kv = k_ref.shape[1] // LANES
    v_rows = LANES if diff else HEAD_DIM
    ones = jnp.where(lax.broadcasted_iota(jnp.int32, (SUBLANES, kch), 0) == 0, 1.0, 0.0).astype(BF16)

    @pl.when(pl.program_id(2) == 0)
    def _():
        for t in range(n_kv):
            lanes = slice(t * LANES, (t + 1) * LANES)
            for c in range(nch):
                if c < n_own:
                    rows = slice(c * kch, (c + 1) * kch)
                    kc, vc = k_ref[rows, lanes], v_ref[rows, lanes]
                else:
                    rows = slice((c - n_own) * kch, (c - n_own + 1) * kch)
                    kc, vc = ck_ref[rows, lanes], cv_ref[rows, lanes]
                k_scr[t, c] = kc
                vt = vc.T
                if diff:
                    vt_scr[t, c, 0] = jnp.concatenate([vt, ones], axis=0)
                else:
                    vt_scr[t, c, 0] = jnp.concatenate([vt[:HEAD_DIM], ones], axis=0)
                    vt_scr[t, c, 1] = jnp.concatenate([vt[HEAD_DIM:], ones], axis=0)

    low_row = lax.broadcasted_iota(jnp.int32, (LANES, tq), 0) < HEAD_DIM
    n_tiles = q_ref.shape[1] // LANES
    kv_of = (lambda j: j) if diff else (lambda j: j // (N_Q_HEADS // N_KV_HEADS))
    half, quarter = HEAD_DIM, HEAD_DIM // 2

    def weights(j):
        t = q_ref[:, j * LANES:(j + 1) * LANES].astype(F32).T
        if not diff:
            inv = [lax.rsqrt(jnp.mean(h * h, axis=0, keepdims=True) + RMS_EPS) for h in (t[:half], t[half:])]
            t = jnp.concatenate([t[:half] * inv[0], t[half:] * inv[1]], axis=0) * gq_ref[...]
        if rope:
            partner = jnp.concatenate([t[quarter:half], t[:quarter], t[half + quarter:], t[half:half + quarter]], axis=0)
            t = t * cos_ref[...] + partner * sin_ref[...]
        tb = (t * Q_SCALE).astype(BF16)
        zero = jnp.zeros_like(tb)
        return jnp.concatenate([jnp.where(low_row, tb, zero), jnp.where(low_row, zero, tb)], axis=1)

    def stage(j_a, j_b, mx_b):
        w = weights(j_a) if j_a is not None else None

        def body(c, carry):
            m8, accs = carry
            if j_a is not None:
                s = jnp.dot(k_scr[kv_of(j_a), c], w, preferred_element_type=F32)
                s_scr[j_a % 2, c] = s
                m8 = jnp.maximum(m8, jnp.max(s.reshape(kch // SUBLANES, SUBLANES, 2 * tq), axis=0))
            if j_b is not None:
                e = jnp.exp2(s_scr[j_b % 2, c] - mx_b).astype(BF16)
                if diff:
                    accs = (accs[0] + jnp.dot(vt_scr[kv_of(j_b), c, 0], e, preferred_element_type=F32),)
                else:
                    accs = (accs[0] + jnp.dot(vt_scr[kv_of(j_b), c, 0], e[:, :tq], preferred_element_type=F32),
                            accs[1] + jnp.dot(vt_scr[kv_of(j_b), c, 1], e[:, tq:], preferred_element_type=F32))
            return m8, accs

        acc_shape = (v_rows + SUBLANES, 2 * tq if diff else tq)
        init = (jnp.full((SUBLANES, 2 * tq), -jnp.inf, F32),
                tuple(jnp.zeros(acc_shape, F32) for _ in range(1 if diff else 2)))
        m8, accs = lax.fori_loop(0, nch, body, init, unroll=min(nch, ATT_UNROLL))
        return jnp.max(m8, axis=0, keepdims=True), accs

    if diff:
        lv = lam_ref[...]
        lam = (jnp.exp(jnp.sum(lv[0:1] * lv[1:2], axis=-1, keepdims=True))
               - jnp.exp(jnp.sum(lv[2:3] * lv[3:4], axis=-1, keepdims=True)) + lam_init)

    mx_prev = None
    for st in range(n_tiles + 1):
        j_a = st if st < n_tiles else None
        j_b = st - 1 if st >= 1 else None
        mx_new, accs = stage(j_a, j_b, mx_prev)
        if j_b is not None:
            if diff:
                a = accs[0]
                o_t = (a[:v_rows, :tq] / a[v_rows:v_rows + 1, :tq]
                       - lam * (a[:v_rows, tq:] / a[v_rows:v_rows + 1, tq:]))
                o = o_t.T
                inv = lax.rsqrt(jnp.mean(o * o, axis=-1, keepdims=True) + RMS_EPS)
                o = (o * inv * g_ref[...]) * (1.0 - lam_init)
            else:
                o_t = jnp.concatenate([a[:v_rows] / a[v_rows:v_rows + 1] for a in accs], axis=0)
                o = o_t.T
            o_ref[:, j_b * LANES:(j_b + 1) * LANES] = o.astype(BF16)
        mx_prev = mx_new


def _attention(q, k, v, cache_k, cache_v, *, row0, n_b, t, tq, gq=None, rope=None, diff=None):
    is_diff = diff is not None
    qw = ATT_GROUP * LANES if is_diff else D_MODEL
    n_groups = D_MODEL // qw
    kvw = qw if is_diff else N_KV_HEADS * HEAD_DIM
    nq_blocks = t // tq
    has_cache = cache_k is not None
    past = cache_k.shape[1] if has_cache else 0
    kch = min(ATT_KCH, t)
    assert t % kch == 0 and past % kch == 0
    n_own, n_cache = t // kch, past // kch
    nch = n_own + n_cache
    q_spec = pl.BlockSpec((tq, qw), lambda b, p, iq: (row0 // tq + b * nq_blocks + iq, p))
    kv_spec = pl.BlockSpec((t, kvw), lambda b, p, iq: (row0 // t + b, p),
                           pipeline_mode=pl.Buffered(2 if is_diff else 1))
    in_specs, args = [], []
    const = lambda a: pl.BlockSpec(a.shape, lambda b, p, iq: (0,) * a.ndim)
    if is_diff:
        lam_vec, g_sub, lam_init = diff
        in_specs += [const(lam_vec), const(g_sub)]
        args += [lam_vec, g_sub]
    else:
        lam_init = 0.0
        in_specs.append(const(gq))
        args.append(gq)
    if rope is not None:
        rope_spec = pl.BlockSpec((LANES, tq), lambda b, p, iq: (0, iq))
        in_specs += [rope_spec, rope_spec]
        args += list(rope)
    in_specs += [q_spec, kv_spec, kv_spec]
    args += [q, k, v]
    if has_cache:
        c_spec = pl.BlockSpec((None, past, kvw), lambda b, p, iq: (b, 0, p))
        in_specs += [c_spec, c_spec]
        args += [cache_k, cache_v]
    n_kv = kvw // LANES
    v_rows = (LANES if is_diff else HEAD_DIM) + SUBLANES
    return pl.pallas_call(
        functools.partial(_attn_kernel, diff=is_diff, rope=rope is not None, has_cache=has_cache, tq=tq, kch=kch,
                          n_own=n_own, n_cache=n_cache, lam_init=lam_init),
        out_shape=jax.ShapeDtypeStruct((n_b * t, D_MODEL), BF16),
        grid=(n_b, n_groups, nq_blocks),
        in_specs=in_specs,
        out_specs=pl.BlockSpec((tq, qw), lambda b, p, iq: (b * nq_blocks + iq, p)),
        scratch_shapes=[pltpu.VMEM((n_kv, nch, kch, LANES), BF16),
                        pltpu.VMEM((n_kv, nch, 1 if is_diff else 2, v_rows, kch), BF16),
                        pltpu.VMEM((2, nch, kch, 2 * tq), F32)],
        compiler_params=_cparams("arbitrary", "arbitrary", "arbitrary"),
        name=("diff_attn" if is_diff else "gqa_attn") + ("_cache" if has_cache else ""),
    )(*args)


def _oproj_kernel(*refs, tm, n_ctx, n_x):
    x_refs = refs[:n_x]
    ac_ref, as_ref, w_ref, gate_ref, lng_ref, lnb_ref, o_ref = refs[n_x:]
    a = jnp.where(pl.program_id(0) * tm < n_ctx, ac_ref[...], as_ref[...])
    out = jnp.dot(a, w_ref[...], preferred_element_type=F32)
    z = DEEPNORM_ALPHA * _x_value(x_refs, tm, n_ctx) + gate_ref[...] * out
    o_ref[...] = _layer_norm_rows(z, lng_ref[...], lnb_ref[...])


def _oproj(a_ctx, a_smp, w, x, mod, lng, lnb, *, n_ctx, t_s):
    m = a_ctx.shape[0] + a_smp.shape[0]
    kdim = a_ctx.shape[1]
    tm = 512
    ncb = n_ctx // tm
    const = lambda arr: pl.BlockSpec(arr.shape, lambda i: (0,) * arr.ndim)
    x_specs, x_args = _x_specs(x, tm, n_ctx)
    return pl.pallas_call(
        functools.partial(_oproj_kernel, tm=tm, n_ctx=n_ctx, n_x=len(x_args)),
        out_shape=jax.ShapeDtypeStruct((m, D_MODEL), F32),
        grid=(m // tm,),
        in_specs=x_specs + [pl.BlockSpec((tm, kdim), lambda i: (jnp.minimum(i, ncb - 1), 0)),
                            pl.BlockSpec((tm, kdim), lambda i: (jnp.maximum(i - ncb, 0), 0)),
                            const(w), _mod_spec(2, tm, n_ctx, t_s), const(lng), const(lnb)],
        out_specs=pl.BlockSpec((tm, D_MODEL), lambda i: (i, 0)),
        compiler_params=_cparams("parallel"),
        name="oproj_ln",
    )(*x_args, a_ctx, a_smp, w, mod, lng, lnb)


def _gelu_tanh(x):
    return 0.5 * x * (1.0 + jnp.tanh(math.sqrt(2.0 / math.pi) * (x + 0.044715 * (x * x * x))))


def _rnn_oproj_kernel(hf_ref, hb_ref, br_ref, w_ref, x_ref, gate_ref, lng_ref, lnb_ref, o_ref):
    y = (hf_ref[...] + hb_ref[...]) * _gelu_tanh(br_ref[...])
    out = jnp.dot(y.astype(BF16), w_ref[...], preferred_element_type=F32)
    z = DEEPNORM_ALPHA * x_ref[...] + gate_ref[...] * out
    o_ref[...] = _layer_norm_rows(z, lng_ref[...], lnb_ref[...])


def _rnn_oproj(hs_f, hs_b, bx, w, x, mod, lng, lnb, *, n_ctx, t_s):
    m = x.shape[0]
    tm = 512
    const = lambda arr: pl.BlockSpec(arr.shape, lambda i: (0,) * arr.ndim)
    row = pl.BlockSpec((tm, D_MODEL), lambda i: (i, 0))
    rnn_row = pl.BlockSpec((tm, D_RNN), lambda i: (i, 0))
    return pl.pallas_call(
        _rnn_oproj_kernel,
        out_shape=jax.ShapeDtypeStruct((m, D_MODEL), F32),
        grid=(m // tm,),
        in_specs=[rnn_row, rnn_row, rnn_row, const(w), row,
                  _mod_spec(2, tm, n_ctx, t_s), const(lng), const(lnb)],
        out_specs=row,
        compiler_params=_cparams("parallel"),
        name="rnn_oproj_ln",
    )(hs_f, hs_b, bx, w, x, mod, lng, lnb)


def _mod_mm_kernel(x_ref, sc_ref, sh_ref, w_ref, o_ref):
    h = (x_ref[...] * (1.0 + sc_ref[...]) + sh_ref[...]).astype(BF16)
    o_ref[...] = jnp.dot(h, w_ref[...], preferred_element_type=F32)


def _rnn_in(x, mod, w, *, n_ctx, t_s):
    m = x.shape[0]
    n = w.shape[1]
    tm = 512
    return pl.pallas_call(
        _mod_mm_kernel,
        out_shape=jax.ShapeDtypeStruct((m, n), F32),
        grid=(m // tm,),
        in_specs=[pl.BlockSpec((tm, D_MODEL), lambda i: (i, 0)),
                  _mod_spec(1, tm, n_ctx, t_s), _mod_spec(0, tm, n_ctx, t_s),
                  pl.BlockSpec(w.shape, lambda i: (0, 0))],
        out_specs=pl.BlockSpec((tm, n), lambda i: (i, 0)),
        compiler_params=_cparams("parallel"),
        name="rnn_in",
    )(x, mod, mod, w)


def _rnn_gate_kernel(xb_ref, prev_ref, next_ref, cw_ref, cb_ref, wg_ref, bg_ref, lam_ref,
                     af_ref, uf_ref, ab_ref, ub_ref, ext_ref, *, tm, n_ctx, t_c, t_s):
    i = pl.program_id(0)
    r0 = i * tm
    pos = jnp.where(r0 < n_ctx, r0 % t_c, (r0 - n_ctx) % t_s)
    t_seq = jnp.where(r0 < n_ctx, t_c, t_s)
    at_start = pos == 0
    at_end = pos + tm == t_seq
    ext_ref[0:SUBLANES, :] = jnp.where(at_start, 0.0, prev_ref[...])
    ext_ref[SUBLANES:SUBLANES + tm, :] = xb_ref[...]
    ext_ref[SUBLANES + tm:, :] = jnp.where(at_end, 0.0, next_ref[...])
    xc = 0.0
    for j in range(CONV_W):
        off = SUBLANES - CONV_LEFT + j
        xc = xc + ext_ref[off:off + tm, :] * cw_ref[j:j + 1, :]
    xc = xc + cb_ref[...]
    xcb = xc.astype(BF16)
    neg_lam = -lam_ref[...]
    sp = jnp.maximum(neg_lam, 0.0) + jnp.log1p(jnp.exp(-jnp.abs(neg_lam)))

    def gate(g, ct, cols):
        k0 = GATE_K0[ct]
        pre = jnp.dot(xcb[:, k0:k0 + GATE_K], wg_ref[g, ct], preferred_element_type=F32)
        return _sigmoid(pre + bg_ref[:, g * D_RNN + cols.start:g * D_RNN + cols.stop])

    for z, (a_ref, u_ref) in enumerate(((af_ref, uf_ref), (ab_ref, ub_ref))):
        for ct in range(D_RNN // GATE_N):
            cols = slice(ct * GATE_N, (ct + 1) * GATE_N)
            r = gate(2 * z, ct, cols)
            g_in = gate(2 * z + 1, ct, cols)
            log_a = -RGLRU_C * r * sp[z:z + 1, cols]
            a = jnp.exp(log_a)
            a_ref[:, cols] = a
            u_ref[:, cols] = jnp.sqrt(-jnp.tanh(log_a) * (a * a + 1.0)) * (g_in * xc[:, cols])


def _rnn_gates(bx, conv_w, conv_b, wg, bg, lam, *, n_ctx, t_c, t_s):
    m = bx.shape[0]
    tm = 256
    n8 = tm // SUBLANES
    last8 = m // SUBLANES - 1
    const = lambda arr: pl.BlockSpec(arr.shape, lambda i: (0,) * arr.ndim)
    row = pl.BlockSpec((tm, D_RNN), lambda i: (i, 0))
    out = jax.ShapeDtypeStruct((m, D_RNN), F32)
    return pl.pallas_call(
        functools.partial(_rnn_gate_kernel, tm=tm, n_ctx=n_ctx, t_c=t_c, t_s=t_s),
        out_shape=(out, out, out, out),
        grid=(m // tm,),
        in_specs=[pl.BlockSpec((tm, D_RNN), lambda i: (i, 1)),
                  pl.BlockSpec((SUBLANES, D_RNN), lambda i: (jnp.maximum(i * n8 - 1, 0), 1)),
                  pl.BlockSpec((SUBLANES, D_RNN), lambda i: (jnp.minimum((i + 1) * n8, last8), 1)),
                  const(conv_w), const(conv_b), const(wg), const(bg), const(lam)],
        out_specs=(row, row, row, row),
        scratch_shapes=[pltpu.VMEM((tm + 2 * SUBLANES, D_RNN), F32)],
        compiler_params=_cparams("parallel"),
        name="rnn_gates",
    )(bx, bx, bx, conv_w, conv_b, wg, bg, lam)


def _rnn_scan_kernel(fblk_ref, bblk_ref, seq_ref, first_ref,
                     af_ref, uf_ref, ab_ref, ub_ref, h0_ref, hf_ref, hb_ref, fin_ref, carry_ref, *, tc):
    s = pl.program_id(0)

    @pl.when(first_ref[s] == 1)
    def _():
        carry_ref[...] = h0_ref[...]

    def step(t, carry):
        h_f, h_b = carry
        tb = tc - 1 - t
        h_f = af_ref[pl.ds(t, 1), :] * h_f + uf_ref[pl.ds(t, 1), :]
        h_b = ab_ref[pl.ds(tb, 1), :] * h_b + ub_ref[pl.ds(tb, 1), :]
        hf_ref[pl.ds(t, 1), :] = h_f
        hb_ref[pl.ds(tb, 1), :] = h_b
        return h_f, h_b

    h_f, h_b = lax.fori_loop(0, tc, step, (carry_ref[0:1, :], carry_ref[1:2, :]), unroll=8)
    carry_ref[0:1, :] = h_f
    carry_ref[1:2, :] = h_b
    fin_ref[...] = carry_ref[...]


def _rnn_scan(a_f, u_f, a_b, u_b, h0, *, seq_lens):
    m = a_f.shape[0]
    tc = SCAN_CHUNK
    fblk, bblk, seq, first = [], [], [], []
    blk0 = 0
    for si, t in enumerate(seq_lens):
        nch = t // tc
        for c in range(nch):
            fblk.append(blk0 + c)
            bblk.append(blk0 + nch - 1 - c)
            seq.append(si)
            first.append(1 if c == 0 else 0)
        blk0 += nch
    n_steps = len(fblk)
    tables = [jnp.asarray(np.array(v, np.int32)) for v in (fblk, bblk, seq, first)]
    f_spec = pl.BlockSpec((tc, D_RNN), lambda s, fb, bb, sq, fr: (fb[s], 0))
    b_spec = pl.BlockSpec((tc, D_RNN), lambda s, fb, bb, sq, fr: (bb[s], 0))
    st_spec = pl.BlockSpec((None, 2, D_RNN), lambda s, fb, bb, sq, fr: (sq[s], 0, 0))
    out = jax.ShapeDtypeStruct((m, D_RNN), F32)
    return pl.pallas_call(
        functools.partial(_rnn_scan_kernel, tc=tc),
        out_shape=(out, out, jax.ShapeDtypeStruct(h0.shape, F32)),
        grid_spec=pltpu.PrefetchScalarGridSpec(
            num_scalar_prefetch=4, grid=(n_steps,),
            in_specs=[f_spec, f_spec, b_spec, b_spec, st_spec],
            out_specs=(f_spec, b_spec, st_spec),
            scratch_shapes=[pltpu.VMEM((2, D_RNN), F32)]),
        compiler_params=_cparams("arbitrary"),
        name="rnn_scan",
    )(*tables, a_f, u_f, a_b, u_b, h0)


def _ffn_kernel(x_ref, sc_ref, sh_ref, gate_ref, wg_ref, wu_ref, wd_ref, lng_ref, lnb_ref,
                o_ref, h_ref, acc_ref):
    j = pl.program_id(1)

    @pl.when(j == 0)
    def _():
        h_ref[...] = (x_ref[...] * (1.0 + sc_ref[...]) + sh_ref[...]).astype(BF16)
        acc_ref[...] = jnp.zeros_like(acc_ref)

    h = h_ref[...]
    g = jnp.dot(h, wg_ref[...], preferred_element_type=F32)
    u = jnp.dot(h, wu_ref[...], preferred_element_type=F32)
    act = ((g * _sigmoid(g)) * u).astype(BF16)
    acc_ref[...] += jnp.dot(act, wd_ref[...], preferred_element_type=F32)

    @pl.when(j == pl.num_programs(1) - 1)
    def _():
        z = DEEPNORM_ALPHA * x_ref[...] + gate_ref[...] * acc_ref[...]
        o_ref[...] = _layer_norm_rows(z, lng_ref[...], lnb_ref[...])


def _ffn(x, mod, w_gu, w_down, layer, lng, lnb, *, n_ctx, t_s):
    m = x.shape[0]
    tm = 512
    tf = D_FF // 2
    nf = D_FF // tf
    const = lambda arr: pl.BlockSpec(arr.shape, lambda i, j: (0,) * arr.ndim)
    row = pl.BlockSpec((tm, D_MODEL), lambda i, j: (i, 0))
    return pl.pallas_call(
        _ffn_kernel,
        out_shape=jax.ShapeDtypeStruct((m, D_MODEL), F32),
        grid=(m // tm, nf),
        in_specs=[row, _mod_spec(4, tm, n_ctx, t_s), _mod_spec(3, tm, n_ctx, t_s), _mod_spec(5, tm, n_ctx, t_s),
                  pl.BlockSpec((None, D_MODEL, tf), lambda i, j: (layer, 0, j)),
                  pl.BlockSpec((None, D_MODEL, tf), lambda i, j: (layer, 0, nf + j)),
                  pl.BlockSpec((None, tf, D_MODEL), lambda i, j: (layer, j, 0)),
                  const(lng), const(lnb)],
        out_specs=row,
        scratch_shapes=[pltpu.VMEM((tm, D_MODEL), BF16), pltpu.VMEM((tm, D_MODEL), F32)],
        compiler_params=_cparams("parallel", "arbitrary"),
        name="ffn",
    )(x, mod, mod, mod, w_gu, w_gu, w_down, lng, lnb)


ROW_TILES = D_MODEL // LANES


def _to_token_rows(ref, base, value):
    n = value.shape[0]
    for j in range(ROW_TILES):
        ref[pl.ds(base + j, n, stride=ROW_TILES), :] = value[:, j * LANES:(j + 1) * LANES]


def _from_token_rows(ref, base, n):
    return jnp.concatenate([ref[pl.ds(base + j, n, stride=ROW_TILES), :] for j in range(ROW_TILES)], axis=1)


def _token_row(ref, r):
    start = r * ROW_TILES if isinstance(r, int) else pl.multiple_of(r * ROW_TILES, ROW_TILES)
    return ref.at[pl.ds(start, ROW_TILES)]


def _route_kernel(x_ref, sc_ref, sh_ref, wr_ref, route_ref, gates_ref, cnt_ref, carry_ref, *, tm):
    @pl.when(pl.program_id(0) == 0)
    def _():
        carry_ref[...] = jnp.zeros_like(carry_ref)

    lane = lax.broadcasted_iota(jnp.int32, (tm, LANES), 1)
    h = x_ref[...] * (1.0 + sc_ref[...]) + sh_ref[...]
    logits = jnp.dot(h.astype(BF16), wr_ref[...], preferred_element_type=F32)
    s1 = jnp.where(lane < N_EXPERTS, logits, -jnp.inf)
    m1 = jnp.max(s1, axis=-1, keepdims=True)
    i1 = jnp.min(jnp.where(s1 == m1, lane, LANES), axis=-1, keepdims=True)
    s2 = jnp.where(lane == i1, -jnp.inf, s1)
    m2 = jnp.max(s2, axis=-1, keepdims=True)
    i2 = jnp.min(jnp.where(s2 == m2, lane, LANES), axis=-1, keepdims=True)
    e2 = jnp.exp(m2 - m1)
    den = 1.0 + e2
    hit = jnp.where((lane == i1) | (lane == i2), 1.0, 0.0)
    earlier = lax.broadcasted_iota(jnp.int32, (tm, tm), 1) < lax.broadcasted_iota(jnp.int32, (tm, tm), 0)
    before = carry_ref[...] + jnp.dot(jnp.where(earlier, 1.0, 0.0).astype(BF16), hit.astype(BF16),
                                      preferred_element_type=F32)
    r1 = jnp.sum(jnp.where(lane == i1, before, 0.0), axis=-1, keepdims=True).astype(jnp.int32)
    r2 = jnp.sum(jnp.where(lane == i2, before, 0.0), axis=-1, keepdims=True).astype(jnp.int32)
    carry_ref[...] += jnp.sum(hit, axis=0, keepdims=True)
    route = jnp.where(lane == 0, i1, jnp.where(lane == 1, i2, jnp.where(lane == 2, r1, jnp.where(lane == 3, r2, 0))))
    route_ref[...] = route[:, :ROUTE_COLS]
    gates_ref[...] = jnp.where(lane == 0, 1.0 / den, jnp.where(lane == 1, e2 / den, 0.0))[:, :ROUTE_COLS]
    cnt_ref[...] = jnp.broadcast_to(carry_ref[...], cnt_ref.shape)


def _moe_route(x, mod, w_router, *, n_ctx, t_s):
    m = x.shape[0]
    tm = 512
    small = pl.BlockSpec((tm, ROUTE_COLS), lambda i: (i, 0))
    return pl.pallas_call(
        functools.partial(_route_kernel, tm=tm),
        out_shape=(jax.ShapeDtypeStruct((m, ROUTE_COLS), jnp.int32), jax.ShapeDtypeStruct((m, ROUTE_COLS), F32),
                   jax.ShapeDtypeStruct((SUBLANES, LANES), F32)),
        grid=(m // tm,),
        in_specs=[pl.BlockSpec((tm, D_MODEL), lambda i: (i, 0)), _mod_spec(4, tm, n_ctx, t_s),
                  _mod_spec(3, tm, n_ctx, t_s), pl.BlockSpec(w_router.shape, lambda i: (0, 0))],
        out_specs=(small, small, pl.BlockSpec((SUBLANES, LANES), lambda i: (0, 0))),
        scratch_shapes=[pltpu.VMEM((1, LANES), F32)],
        compiler_params=_cparams("arbitrary"),
        name="moe_route",
    )(x, mod, mod, w_router)


def _dispatch_kernel(slot_ref, x_ref, sc_ref, sh_ref, xs_hbm, buf, sem, *, tok):
    i = pl.program_id(0)
    par = i % 2
    cur = buf.at[par]
    _to_token_rows(cur, 0, x_ref[...] * (1.0 + sc_ref[...]) + sh_ref[...])

    def issue(t, c):
        src = _token_row(cur, t)
        pltpu.make_async_copy(src, _token_row(xs_hbm, slot_ref[2 * t]), sem.at[par]).start(priority=0)
        pltpu.make_async_copy(src, _token_row(xs_hbm, slot_ref[2 * t + 1]), sem.at[par]).start(priority=1)
        return c
    lax.fori_loop(0, tok, issue, 0, unroll=4)

    def drain(which):
        def body(t, c):
            pltpu.make_async_copy(_token_row(buf.at[which], 0), _token_row(xs_hbm, 0), sem.at[which]).wait()
            return c
        lax.fori_loop(0, 2 * tok, body, 0, unroll=8)

    @pl.when(i > 0)
    def _():
        drain(1 - par)

    @pl.when(i == pl.num_programs(0) - 1)
    def _():
        drain(par)


def _moe_dispatch(slots, x, mod, *, n_ctx, t_s):
    m = x.shape[0]
    tok = 512
    return pl.pallas_call(
        functools.partial(_dispatch_kernel, tok=tok),
        out_shape=jax.ShapeDtypeStruct((2 * m * ROW_TILES, LANES), F32),
        grid=(m // tok,),
        in_specs=[pl.BlockSpec((2 * tok,), lambda i: (i,), memory_space=pltpu.SMEM),
                  pl.BlockSpec((tok, D_MODEL), lambda i: (i, 0)),
                  _mod_spec(4, tok, n_ctx, t_s), _mod_spec(3, tok, n_ctx, t_s)],
        out_specs=pl.BlockSpec(memory_space=pl.ANY),
        scratch_shapes=[pltpu.VMEM((2, tok * ROW_TILES, LANES), F32), pltpu.SemaphoreType.DMA((2,))],
        compiler_params=_cparams("arbitrary"),
        name="moe_dispatch",
    )(slots, x, mod, mod)


def _experts_kernel(wt_ref, we_ref, lo_ref, hi_ref, nw_ref, xs_ref, wg_ref, wu_ref, wd_ref, o_ref, *, rows):
    del we_ref
    w = pl.program_id(0)

    @pl.when(w < nw_ref[0])
    def _():
        xs = _from_token_rows(xs_ref, 0, rows).astype(BF16)
        g = jnp.dot(xs, wg_ref[...], preferred_element_type=F32)
        u = jnp.dot(xs, wu_ref[...], preferred_element_type=F32)
        act = ((g * _sigmoid(g)) * u).astype(BF16)
        y = jnp.dot(act, wd_ref[...], preferred_element_type=F32)
        whole = jnp.logical_and(lo_ref[w] == 0, hi_ref[w] == rows)
        first = jnp.logical_or(w == 0, wt_ref[jnp.maximum(w - 1, 0)] != wt_ref[w])

        @pl.when(whole)
        def _():
            _to_token_rows(o_ref, 0, y)

        @pl.when(jnp.logical_not(whole))
        def _():
            row = lax.broadcasted_iota(jnp.int32, (rows, D_MODEL), 0)
            mine = (row >= lo_ref[w]) & (row < hi_ref[w])

            @pl.when(first)
            def _():
                _to_token_rows(o_ref, 0, jnp.where(mine, y, 0.0))

            @pl.when(jnp.logical_not(first))
            def _():
                _to_token_rows(o_ref, 0, jnp.where(mine, y, _from_token_rows(o_ref, 0, rows)))


def _moe_experts(items, xs, w_gu, w_down, layer):
    rows = MOE_TILE
    n_items = items[0].shape[0]
    tile = lambda w, wt, we, lo, hi, nw: (wt[w], 0)
    return pl.pallas_call(
        functools.partial(_experts_kernel, rows=rows),
        out_shape=jax.ShapeDtypeStruct(xs.shape, F32),
        grid_spec=pltpu.PrefetchScalarGridSpec(
            num_scalar_prefetch=5, grid=(n_items,),
            in_specs=[pl.BlockSpec((rows * ROW_TILES, LANES), tile),
                      pl.BlockSpec((None, None, D_MODEL, D_FF_EXPERT),
                                   lambda w, wt, we, lo, hi, nw: (layer, we[w], 0, 0)),
                      pl.BlockSpec((None, None, D_MODEL, D_FF_EXPERT),
                                   lambda w, wt, we, lo, hi, nw: (layer, we[w], 0, 1)),
                      pl.BlockSpec((None, None, D_FF_EXPERT, D_MODEL),
                                   lambda w, wt, we, lo, hi, nw: (layer, we[w], 0, 0))],
            out_specs=pl.BlockSpec((rows * ROW_TILES, LANES), tile)),
        compiler_params=_cparams("arbitrary"),
        name="moe_experts",
    )(*items, xs, w_gu, w_gu, w_down)


def _combine_kernel(idx_ref, nxt_ref, ys_hbm, gates_ref, x_ref, gate_ref, lng_ref, lnb_ref, o_ref, buf, sem, *, tm):
    i = pl.program_id(0)
    slot = i % 2

    def issue(ref, s):
        def body(t, c):
            pltpu.make_async_copy(_token_row(ys_hbm, ref[2 * t]), _token_row(buf.at[s], t),
                                  sem.at[s]).start(priority=0)
            pltpu.make_async_copy(_token_row(ys_hbm, ref[2 * t + 1]), _token_row(buf.at[s], tm + t),
                                  sem.at[s]).start(priority=1)
            return c
        lax.fori_loop(0, tm, body, 0, unroll=4)

    @pl.when(i == 0)
    def _():
        issue(idx_ref, 0)

    @pl.when(i + 1 < pl.num_programs(0))
    def _():
        issue(nxt_ref, 1 - slot)

    def drain(r, c):
        pltpu.make_async_copy(_token_row(ys_hbm, 0), _token_row(buf.at[slot], 0), sem.at[slot]).wait()
        return c
    lax.fori_loop(0, 2 * tm, drain, 0, unroll=8)
    cur = buf.at[slot]
    y = (gates_ref[:, 0:1] * _from_token_rows(cur, 0, tm)
         + gates_ref[:, 1:2] * _from_token_rows(cur, tm * ROW_TILES, tm))
    z = DEEPNORM_ALPHA * x_ref[...] + gate_ref[...] * y
    o_ref[...] = _layer_norm_rows(z, lng_ref[...], lnb_ref[...])


def _moe_combine(slots, ys, gates, x, mod, lng, lnb, *, n_ctx, t_s, row0=0, n_rows=None):
    tm = 256
    n_blocks = (x.shape[0] if n_rows is None else n_rows) // tm
    b0 = row0 // tm
    idx = lambda f: pl.BlockSpec((2 * tm,), f, memory_space=pltpu.SMEM)
    const = lambda arr: pl.BlockSpec(arr.shape, lambda i: (0,) * arr.ndim)
    return pl.pallas_call(
        functools.partial(_combine_kernel, tm=tm),
        out_shape=jax.ShapeDtypeStruct((n_blocks * tm, D_MODEL), F32),
        grid=(n_blocks,),
        in_specs=[idx(lambda i: (b0 + i,)), idx(lambda i: (b0 + jnp.minimum(i + 1, n_blocks - 1),)),
                  pl.BlockSpec(memory_space=pl.ANY), pl.BlockSpec((tm, ROUTE_COLS), lambda i: (b0 + i, 0)),
                  pl.BlockSpec((tm, D_MODEL), lambda i: (b0 + i, 0)),
                  pl.BlockSpec((None, 1, D_MODEL), lambda i: (_seg_of_block(b0 + i, tm, n_ctx, t_s), 0, 5)),
                  const(lng), const(lnb)],
        out_specs=pl.BlockSpec((tm, D_MODEL), lambda i: (i, 0)),
        scratch_shapes=[pltpu.VMEM((2, 2 * tm * ROW_TILES, LANES), F32), pltpu.SemaphoreType.DMA((2,))],
        compiler_params=_cparams("arbitrary"),
        name="moe_combine",
    )(slots, slots, ys, gates, x, mod, lng, lnb)


def _moe_work_items(counts, n_rows):
    n_tiles = n_rows // MOE_TILE
    n_items = n_tiles + N_EXPERTS - 1
    g_end = jnp.cumsum(counts)
    g_start = g_end - counts
    row0 = jnp.arange(n_tiles, dtype=jnp.int32) * MOE_TILE
    first_e = jnp.sum(row0[:, None] >= g_end[None, :], axis=1)
    last_e = jnp.sum((row0 + MOE_TILE - 1)[:, None] >= g_end[None, :], axis=1)
    per_tile = last_e - first_e + 1
    item0 = jnp.cumsum(per_tile) - per_tile
    w = jnp.arange(n_items, dtype=jnp.int32)
    tile = jnp.clip(jnp.sum(w[:, None] >= item0[None, :], axis=1) - 1, 0, n_tiles - 1)
    expert = jnp.clip(first_e[tile] + w - item0[tile], 0, N_EXPERTS - 1)
    lo = jnp.clip(g_start[expert] - row0[tile], 0, MOE_TILE)
    hi = jnp.clip(g_end[expert] - row0[tile], 0, MOE_TILE)
    n_used = jnp.sum(per_tile).reshape(1)
    i32 = lambda a: a.astype(jnp.int32)
    return i32(tile), i32(expert), i32(lo), i32(hi), i32(n_used), g_start


def _moe(x, mod, w_router, w_gu, w_down, layer, lng, lnb, *, n_ctx, t_s, split=False):
    m = x.shape[0]
    assert (2 * m) % MOE_TILE == 0
    route, gates, cnt = _moe_route(x, mod, w_router, n_ctx=n_ctx, t_s=t_s)
    counts = cnt[0, :N_EXPERTS].astype(jnp.int32)
    *items, g_start = _moe_work_items(counts, 2 * m)
    slots = (g_start[route[:, 0:2]] + route[:, 2:4]).reshape(2 * m).astype(jnp.int32)
    xs = _moe_dispatch(slots, x, mod, n_ctx=n_ctx, t_s=t_s)
    ys = _moe_experts(items, xs, w_gu, w_down, layer)
    comb = functools.partial(_moe_combine, slots, ys, gates, x, mod, lng, lnb, n_ctx=n_ctx, t_s=t_s)
    if split:
        return comb(row0=0, n_rows=n_ctx), comb(row0=n_ctx, n_rows=m - n_ctx)
    return comb()


def _rope_tables(n_tokens):
    n_rows = n_tokens // GRID_W
    row = jnp.repeat(jnp.arange(n_rows, dtype=F32), GRID_W)
    col = jnp.tile(jnp.arange(GRID_W, dtype=F32), n_rows)
    n_axis = HEAD_DIM // 4
    inv = ROPE_THETA ** (-jnp.arange(n_axis, dtype=F32) / n_axis)
    ang = jnp.concatenate([row[:, None] * inv, col[:, None] * inv], axis=-1)
    cos, sin = jnp.cos(ang), jnp.sin(ang)
    reps = LANES // HEAD_DIM
    cos_t = jnp.tile(jnp.concatenate([cos, cos], axis=-1), (1, reps))
    sin_t = jnp.tile(jnp.concatenate([-sin, sin], axis=-1), (1, reps))
    return cos_t, sin_t


def _gqa_head_perm():
    cols = []
    per_pair = 2 * (N_Q_HEADS // N_KV_HEADS)
    for p in range(N_KV_HEADS // 2):
        for j in range(N_Q_HEADS // N_KV_HEADS):
            for hd in (per_pair * p + j, per_pair * p + per_pair // 2 + j):
                cols.append(np.arange(hd * HEAD_DIM, (hd + 1) * HEAD_DIM))
    return np.concatenate(cols)


def _block_diag_windows(w):
    eye = jnp.eye(N_RNN_BLOCKS, dtype=bool)[:, None, :, None]
    dense = jnp.where(eye, w[:, :, None, :], 0.0).reshape(D_RNN, D_RNN)
    for ct, k0 in enumerate(GATE_K0):
        lo, hi = ct * GATE_N, (ct + 1) * GATE_N
        assert k0 <= RNN_BLOCK * (lo // RNN_BLOCK) and RNN_BLOCK * -(-hi // RNN_BLOCK) <= k0 + GATE_K <= D_RNN
    return jnp.stack([dense[k0:k0 + GATE_K, ct * GATE_N:(ct + 1) * GATE_N] for ct, k0 in enumerate(GATE_K0)])


def kernel(x_prompt, x_sample, cache_attn_k, cache_attn_v, state_rglru, cache_diff_k, cache_diff_v, c, c_ctx, w_ada, b_ada, ln_g, ln_b, attn_w_qkv, attn_g_q, attn_g_k, attn_w_o, rnn_w_in, rnn_conv_w, rnn_conv_b, rnn_w_a, rnn_b_a, rnn_w_x, rnn_b_x, rnn_lambda, rnn_w_out, diff_w_qkv, diff_lambda, diff_g_sub, diff_w_o, ffn_w_gu, ffn_w_down, moe_w_router, moe_w_gu, moe_w_down):
    b_c, t_c, d = x_prompt.shape
    b_s, t_s, _ = x_sample.shape
    n_ctx = b_c * t_c
    n_s = b_s * t_s
    past = cache_attn_k.shape[2]
    assert d == D_MODEL and n_ctx % t_s == 0 and t_c == SCAN_CHUNK and t_s % SCAN_CHUNK == 0
    assert 1 + b_s <= COND_ROWS

    assert DEPTH % 2 == 0
    x = (x_prompt.reshape(n_ctx, d), x_sample.reshape(n_s, d))
    cond =jnp.zeros((COND_ROWS, d), F32).at[0].set(c_ctx).at[1:1 + b_s].set(c)
    mods = _ada_table(cond, w_ada, b_ada)[:, :1 + b_s].reshape(DEPTH, 1 + b_s, 1, 6 * d)

    cos_t, sin_t = _rope_tables(t_s)
    rope_t = (cos_t.T, sin_t.T)
    perm = _gqa_head_perm()
    nq = N_Q_HEADS * HEAD_DIM
    tile2 = lambda g: jnp.tile(g, LANES // g.shape[-1]).reshape(1, LANES)
    kw = dict(n_ctx=n_ctx, t_s=t_s)
    moe_gu_b = moe_w_gu.astype(BF16)
    moe_down_b = moe_w_down.astype(BF16)
    ffn_gu_b = ffn_w_gu.astype(BF16)
    ffn_down_b = ffn_w_down.astype(BF16)

    attn_k, attn_v, rnn_s, diff_k, diff_v = [], [], [], [], []
    for li in range(DEPTH):
        mod = mods[li]
        lng = ln_g[li].reshape(2, 1, d)
        lnb = ln_b[li].reshape(2, 1, d)
        j = li // N_MIXERS
        kind = li % N_MIXERS
        if kind == 0:
            w = attn_w_qkv[j]
            w = jnp.concatenate([w[:, :nq][:, perm], w[:, nq:]], axis=1).astype(BF16)
            q, kb, vb, kf, vf = _gqa_qkv(x, mod, w, tile2(attn_g_k[j]), cos_t, sin_t, **kw)
            gq = tile2(attn_g_q[j]).reshape(LANES, 1)
            ck = cache_attn_k[:, j].reshape(b_s, past, N_KV_HEADS * HEAD_DIM).astype(BF16)
            cv = cache_attn_v[:, j].reshape(b_s, past, N_KV_HEADS * HEAD_DIM).astype(BF16)
            o_c = _attention(q, kb, vb, None, None, row0=0, n_b=b_c, t=t_c, tq=t_c, gq=gq)
            o_s = _attention(q, kb, vb, ck, cv, row0=n_ctx, n_b=b_s, t=t_s, tq=512, gq=gq, rope=rope_t)
            x = _oproj(o_c, o_s, attn_w_o[j][perm, :].astype(BF16), x, mod, lng[0], lnb[0], **kw)
            attn_k.append(kf)
            attn_v.append(vf)
        elif kind == 1:
            bx = _rnn_in(x, mod, rnn_w_in[j].astype(BF16), **kw)
            wg = jnp.stack([_block_diag_windows(rnn_w_a[j, 0]), _block_diag_windows(rnn_w_x[j, 0]),
                            _block_diag_windows(rnn_w_a[j, 1]), _block_diag_windows(rnn_w_x[j, 1])]).astype(BF16)
            bg = jnp.concatenate([rnn_b_a[j, 0], rnn_b_x[j, 0], rnn_b_a[j, 1], rnn_b_x[j, 1]]).reshape(1, -1)
            a_f, u_f, a_b, u_b = _rnn_gates(bx, rnn_conv_w[j], rnn_conv_b[j].reshape(1, -1), wg, bg,
                                            rnn_lambda[j], n_ctx=n_ctx, t_c=t_c, t_s=t_s)
            h0 = jnp.concatenate([jnp.zeros((b_c, 2, D_RNN), F32), state_rglru[:, j]], axis=0)
            hs_f, hs_b, fin = _rnn_scan(a_f, u_f, a_b, u_b, h0, seq_lens=[t_c] * b_c + [t_s] * b_s)
            x = _rnn_oproj(hs_f, hs_b, bx, rnn_w_out[j].astype(BF16), x, mod, lng[0], lnb[0], **kw)
            rnn_s.append(fin[:b_c])
        else:
            lam_init = 0.8 - 0.6 * math.exp(-0.3 * li)
            q, kb, vb, kf, vf = _diff_qkv(x, mod, diff_w_qkv[j].astype(BF16), cos_t, sin_t, **kw)
            ck = cache_diff_k[:, j].reshape(b_s, past, D_MODEL).astype(BF16)
            cv = cache_diff_v[:, j].reshape(b_s, past, D_MODEL).astype(BF16)
            g_sub = diff_g_sub[j].reshape(1, LANES)
            dargs = (diff_lambda[j], g_sub, lam_init)
            o_c = _attention(q, kb, vb, None, None, row0=0, n_b=b_c, t=t_c, tq=t_c, diff=dargs)
            o_s = _attention(q, kb, vb, ck, cv, row0=n_ctx, n_b=b_s, t=t_s, tq=256, rope=rope_t, diff=dargs)
            x = _oproj(o_c, o_s, diff_w_o[j].astype(BF16), x, mod, lng[0], lnb[0], **kw)
            diff_k.append(kf)
            diff_v.append(vf)
        if li % 2 == 0:
            x = _ffn(x, mod, ffn_gu_b, ffn_down_b, li // 2, lng[1], lnb[1], **kw)
        else:
            w_r = jnp.zeros((d, LANES), F32).at[:, :N_EXPERTS].set(moe_w_router[li // 2]).astype(BF16)
            x = _moe(x, mod, w_r, moe_gu_b, moe_down_b, li // 2, lng[1], lnb[1], split=li == DEPTH - 1, **kw)

    y_prompt = x[0].reshape(b_c, t_c, d)
    y_sample = x[1].reshape(b_s, t_s, d)
    new_attn_k = jnp.stack([k.reshape(b_c, t_c, N_KV_HEADS, HEAD_DIM) for k in attn_k], axis=1)
    new_attn_v = jnp.stack([v.reshape(b_c, t_c, N_KV_HEADS, HEAD_DIM) for v in attn_v], axis=1)
    new_state = jnp.stack(rnn_s, axis=1)
    new_diff_k = jnp.stack([k.reshape(b_c, t_c, N_DIFF_HEADS, 2, HEAD_DIM) for k in diff_k], axis=1)
    new_diff_v = jnp.stack([v.reshape(b_c, t_c, N_DIFF_HEADS, 2 * HEAD_DIM) for v in diff_v], axis=1)
    return (y_prompt, y_sample, new_attn_k, new_attn_v, new_state, new_diff_k, new_diff_v)
```

```python
import functools
import math

import jax
import jax.numpy as jnp
import numpy as np
from jax import lax
from jax.experimental import pallas as pl
from jax.experimental.pallas import tpu as pltpu

F32 = jnp.float32
BF16 = jnp.bfloat16

D_MODEL = 1024
DEPTH = 4
GRID_W = 64
HEAD_DIM = 64
N_Q_HEADS = 16
N_KV_HEADS = 4
ROPE_THETA = 10000.0
N_DIFF_HEADS = 8
D_RNN = 1280
N_RNN_BLOCKS = 16
RNN_BLOCK = 80
CONV_W = 4
CONV_LEFT = 2
RGLRU_C = 8.0
D_FF = 2816
N_EXPERTS = 8
D_FF_EXPERT = 1408
N_MIXERS = 3
DEEPNORM_ALPHA = (2.0 * DEPTH) ** 0.25
LN_EPS = 1e-6
RMS_EPS = 1e-6

LANES = 128
SUBLANES = 8
VMEM_LIMIT = 56 * 1024 * 1024
COND_ROWS = 8
SCAN_CHUNK = 256
MOE_TILE = 512
ROUTE_COLS = 8
GATE_N = 256
GATE_K = 512
GATE_K0 = (0, 128, 384, 640, 768)
Q_SCALE = HEAD_DIM ** -0.5 * math.log2(math.e)


def _cparams(*sem):
    return pltpu.CompilerParams(dimension_semantics=sem, vmem_limit_bytes=VMEM_LIMIT)


def _sigmoid(x):
    return 0.5 * jnp.tanh(0.5 * x) + 0.5


def _layer_norm_rows(z, g, b):
    mu = jnp.mean(z, axis=-1, keepdims=True)
    zc = z - mu
    var = jnp.mean(zc * zc, axis=-1, keepdims=True)
    return zc * lax.rsqrt(var + LN_EPS) * g + b


def _seg_of_block(i, tm, n_ctx, t_s):
    r0 = i * tm
    return jnp.where(r0 < n_ctx, 0, 1 + (r0 - n_ctx) // t_s)


def _mod_spec(col, tm, n_ctx, t_s):
    return pl.BlockSpec((None, 1, D_MODEL), lambda i, *_: (_seg_of_block(i, tm, n_ctx, t_s), 0, col))


def _ada_kernel(c_ref, w_ref, b_ref, o_ref):
    c = c_ref[...]
    a = (c * _sigmoid(c)).astype(BF16)
    o_ref[...] = jnp.dot(a, w_ref[...].astype(BF16), preferred_element_type=F32) + b_ref[...]


def _ada_table(cond, w_ada, b_ada):
    n_l, d, n = w_ada.shape
    tn = 1536
    return pl.pallas_call(
        _ada_kernel,
        out_shape=jax.ShapeDtypeStruct((n_l, COND_ROWS, n), F32),
        grid=(n_l, n // tn),
        in_specs=[pl.BlockSpec((COND_ROWS, d), lambda l, j: (0, 0)),
                  pl.BlockSpec((None, d, tn), lambda l, j: (l, 0, j)),
                  pl.BlockSpec((None, 1, tn), lambda l, j: (l, 0, j))],
        out_specs=pl.BlockSpec((None, COND_ROWS, tn), lambda l, j: (l, 0, j)),
        compiler_params=_cparams("parallel", "parallel"),
        name="ada_table",
    )(cond, w_ada, b_ada.reshape(n_l, 1, n))


def _head_masks(tm):
    lane = lax.broadcasted_iota(jnp.int32, (tm, LANES), 1)
    return lane < HEAD_DIM, (lane & (HEAD_DIM - 1)) < HEAD_DIM // 2


def _rope_tile(t, first_half, cos, sin):
    partner = jnp.where(first_half, pltpu.roll(t, LANES - HEAD_DIM // 2, 1),
                        pltpu.roll(t, HEAD_DIM // 2, 1))
    return t * cos + partner * sin


def _x_specs(x, tm, n_ctx):
    if not isinstance(x, tuple):
        return [pl.BlockSpec((tm, D_MODEL), lambda i, *_: (i, 0))], [x]
    ncb = n_ctx // tm
    return ([pl.BlockSpec((tm, D_MODEL), lambda i, *_: (jnp.minimum(i, ncb - 1), 0)),
             pl.BlockSpec((tm, D_MODEL), lambda i, *_: (jnp.maximum(i - ncb, 0), 0))], list(x))


def _x_value(x_refs, tm, n_ctx):
    if len(x_refs) == 1:
        return x_refs[0][...]
    return jnp.where(pl.program_id(0) * tm < n_ctx, x_refs[0][...], x_refs[1][...])


def _gqa_qkv_kernel(*refs, tm, n_ctx, n_x):
    x_refs = refs[:n_x]
    sc_ref, sh_ref, w_ref, gk_ref, cos_ref, sin_ref, q_ref, kb_ref, vb_ref, kf_ref, vf_ref = refs[n_x:]
    i = pl.program_id(0)
    h = (_x_value(x_refs, tm, n_ctx) * (1.0 + sc_ref[...]) + sh_ref[...]).astype(BF16)
    y = jnp.dot(h, w_ref[...], preferred_element_type=F32)
    use_rope = i * tm >= n_ctx
    cos = jnp.where(use_rope, cos_ref[...], 1.0)
    sin = jnp.where(use_rope, sin_ref[...], 0.0)
    low_head, first_half = _head_masks(tm)

    def norm_rope(t, g):
        t2 = t * t
        s_lo = jnp.sum(jnp.where(low_head, t2, 0.0), axis=-1, keepdims=True)
        s_hi = jnp.sum(jnp.where(low_head, 0.0, t2), axis=-1, keepdims=True)
        inv = jnp.where(low_head, lax.rsqrt(s_lo * (1.0 / HEAD_DIM) + RMS_EPS),
                        lax.rsqrt(s_hi * (1.0 / HEAD_DIM) + RMS_EPS))
        return _rope_tile(t * inv * g, first_half, cos, sin)

    nq = N_Q_HEADS * HEAD_DIM
    nkv = N_KV_HEADS * HEAD_DIM
    gk = gk_ref[...]
    q_ref[...] = y[:, :nq].astype(BF16)
    k = jnp.concatenate([norm_rope(y[:, nq + t * LANES: nq + (t + 1) * LANES], gk) for t in range(nkv // LANES)],
                        axis=1)
    v = y[:, nq + nkv:]
    kb_ref[...] = k.astype(BF16)
    vb_ref[...] = v.astype(BF16)

    @pl.when(i * tm < n_ctx)
    def _():
        kf_ref[...] = k
        vf_ref[...] = v


def _gqa_qkv(x, mod, w, gk, cos, sin, *, n_ctx, t_s):
    m = sum(a.shape[0] for a in x) if isinstance(x, tuple) else x.shape[0]
    tm = 512
    x_specs, x_args = _x_specs(x, tm, n_ctx)
    nq = N_Q_HEADS * HEAD_DIM
    nkv = N_KV_HEADS * HEAD_DIM
    n_rope_blocks = t_s // tm
    rope_spec = pl.BlockSpec(
        (tm, LANES), lambda i: (jnp.where(i * tm >= n_ctx, ((i * tm - n_ctx) % t_s) // tm, 0) % n_rope_blocks, 0))
    row = lambda n: pl.BlockSpec((tm, n), lambda i: (i, 0))
    ctx_row = lambda n: pl.BlockSpec((tm, n), lambda i: (jnp.minimum(i, n_ctx // tm - 1), 0))
    const = lambda a: pl.BlockSpec(a.shape, lambda i: (0,) * a.ndim)
    return pl.pallas_call(
        functools.partial(_gqa_qkv_kernel, tm=tm, n_ctx=n_ctx, n_x=len(x_args)),
        out_shape=(jax.ShapeDtypeStruct((m, nq), BF16), jax.ShapeDtypeStruct((m, nkv), BF16),
                   jax.ShapeDtypeStruct((m, nkv), BF16), jax.ShapeDtypeStruct((n_ctx, nkv), F32),
                   jax.ShapeDtypeStruct((n_ctx, nkv), F32)),
        grid=(m // tm,),
        in_specs=x_specs + [_mod_spec(1, tm, n_ctx, t_s), _mod_spec(0, tm, n_ctx, t_s),
                            const(w), const(gk), rope_spec, rope_spec],
        out_specs=(row(nq), row(nkv), row(nkv), ctx_row(nkv), ctx_row(nkv)),
        compiler_params=_cparams("arbitrary"),
        name="gqa_qkv",
    )(*x_args, mod, mod, w, gk, cos, sin)


def _diff_qkv_kernel(x_ref, sc_ref, sh_ref, w_ref, cos_ref, sin_ref,
                     q_ref, kb_ref, vb_ref, kf_ref, vf_ref, *, tm, n_ctx):
    i = pl.program_id(0)
    h = (x_ref[...] * (1.0 + sc_ref[...]) + sh_ref[...]).astype(BF16)
    y = jnp.dot(h, w_ref[...], preferred_element_type=F32)
    use_rope = i * tm >= n_ctx
    cos = jnp.where(use_rope, cos_ref[...], 1.0)
    sin = jnp.where(use_rope, sin_ref[...], 0.0)
    _, first_half = _head_masks(tm)
    q_ref[...] = y[:, :D_MODEL].astype(BF16)
    k = jnp.concatenate([_rope_tile(y[:, D_MODEL + t * LANES: D_MODEL + (t + 1) * LANES], first_half, cos, sin)
                         for t in range(D_MODEL // LANES)], axis=1)
    v = y[:, 2 * D_MODEL:]
    kb_ref[...] = k.astype(BF16)
    vb_ref[...] = v.astype(BF16)

    @pl.when(i * tm < n_ctx)
    def _():
        kf_ref[...] = k
        vf_ref[...] = v


def _diff_qkv(x, mod, w, cos, sin, *, n_ctx, t_s):
    m = x.shape[0]
    tm = 256
    n_rope_blocks = t_s // tm
    rope_spec = pl.BlockSpec(
        (tm, LANES), lambda i: (jnp.where(i * tm >= n_ctx, ((i * tm - n_ctx) % t_s) // tm, 0) % n_rope_blocks, 0))
    row = pl.BlockSpec((tm, D_MODEL), lambda i: (i, 0))
    ctx_row = pl.BlockSpec((tm, D_MODEL), lambda i: (jnp.minimum(i, n_ctx // tm - 1), 0))
    return pl.pallas_call(
        functools.partial(_diff_qkv_kernel, tm=tm, n_ctx=n_ctx),
        out_shape=(jax.ShapeDtypeStruct((m, D_MODEL), BF16), jax.ShapeDtypeStruct((m, D_MODEL), BF16),
                   jax.ShapeDtypeStruct((m, D_MODEL), BF16), jax.ShapeDtypeStruct((n_ctx, D_MODEL), F32),
                   jax.ShapeDtypeStruct((n_ctx, D_MODEL), F32)),
        grid=(m // tm,),
        in_specs=[row, _mod_spec(1, tm, n_ctx, t_s), _mod_spec(0, tm, n_ctx, t_s),
                  pl.BlockSpec(w.shape, lambda i: (0, 0)), rope_spec, rope_spec],
        out_specs=(row, row, row, ctx_row, ctx_row),
        compiler_params=_cparams("arbitrary"),
        name="diff_qkv",
    )(x, mod, mod, w, cos, sin)


ATT_GROUP = 4
ATT_KCH = 512
ATT_UNROLL = 3


def _attn_kernel(*refs, diff, rope, has_cache, tq, kch, n_own, n_cache, lam_init):
    refs = list(refs)
    if diff:
        lam_ref, g_ref = refs[:2]
        refs = refs[2:]
    else:
        gq_ref = refs[0]
        refs = refs[1:]
    if rope:
        cos_ref, sin_ref = refs[:2]
        refs = refs[2:]
    if has_cache:
        q_ref, k_ref, v_ref, ck_ref, cv_ref, o_ref, k_scr, vt_scr, s_scr = refs
    else:
        q_ref, k_ref, v_ref, o_ref, k_scr, vt_scr, s_scr = refs
    nch = n_own + n_cache
    n_kv = k_ref.shape[1] // LANES
    v_rows = LANES if diff else HEAD_DIM
    ones = jnp.where(lax.broadcasted_iota(jnp.int32, (SUBLANES, kch), 0) == 0, 1.0, 0.0).astype(BF16)

    @pl.when(pl.program_id(2) == 0)
    def _():
        for t in range(n_kv):
            lanes = slice(t * LANES, (t + 1) * LANES)
            for c in range(nch):
                if c < n_own:
                    rows = slice(c * kch, (c + 1) * kch)
                    kc, vc = k_ref[rows, lanes], v_ref[rows, lanes]
                else:
                    rows = slice((c - n_own) * kch, (c - n_own + 1) * kch)
                    kc, vc = ck_ref[rows, lanes], cv_ref[rows, lanes]
                k_scr[t, c] = kc
                vt = vc.T
                if diff:
                    vt_scr[t, c, 0] = jnp.concatenate([vt, ones], axis=0)
                else:
                    vt_scr[t, c, 0] = jnp.concatenate([vt[:HEAD_DIM], ones], axis=0)
                    vt_scr[t, c, 1] = jnp.concatenate([vt[HEAD_DIM:], ones], axis=0)

    low_row = lax.broadcasted_iota(jnp.int32, (LANES, tq), 0) < HEAD_DIM
    n_tiles = q_ref.shape[1] // LANES
    kv_of = (lambda j: j) if diff else (lambda j: j // (N_Q_HEADS // N_KV_HEADS))
    half, quarter = HEAD_DIM, HEAD_DIM // 2

    def weights(j):
        t = q_ref[:, j * LANES:(j + 1) * LANES].astype(F32).T
        if not diff:
            inv = [lax.rsqrt(jnp.mean(h * h, axis=0, keepdims=True) + RMS_EPS) for h in (t[:half], t[half:])]
            t = jnp.concatenate([t[:half] * inv[0], t[half:] * inv[1]], axis=0) * gq_ref[...]
        if rope:
            partner = jnp.concatenate([t[quarter:half], t[:quarter], t[half + quarter:], t[half:half + quarter]], axis=0)
            t = t * cos_ref[...] + partner * sin_ref[...]
        tb = (t * Q_SCALE).astype(BF16)
        zero = jnp.zeros_like(tb)
        return jnp.concatenate([jnp.where(low_row, tb, zero), jnp.where(low_row, zero, tb)], axis=1)

    def stage(j_a, j_b, mx_b):
        w = weights(j_a) if j_a is not None else None

        def body(c, carry):
            m8, accs = carry
            if j_a is not None:
                s = jnp.dot(k_scr[kv_of(j_a), c], w, preferred_element_type=F32)
                s_scr[j_a % 2, c] = s
                m8 = jnp.maximum(m8, jnp.max(s.reshape(kch // SUBLANES, SUBLANES, 2 * tq), axis=0))
            if j_b is not None:
                e = jnp.exp2(s_scr[j_b % 2, c] - mx_b).astype(BF16)
                if diff:
                    accs = (accs[0] + jnp.dot(vt_scr[kv_of(j_b), c, 0], e, preferred_element_type=F32),)
                else:
                    accs = (accs[0] + jnp.dot(vt_scr[kv_of(j_b), c, 0], e[:, :tq], preferred_element_type=F32),
                            accs[1] + jnp.dot(vt_scr[kv_of(j_b), c, 1], e[:, tq:], preferred_element_type=F32))
            return m8, accs

        acc_shape = (v_rows + SUBLANES, 2 * tq if diff else tq)
        init = (jnp.full((SUBLANES, 2 * tq), -jnp.inf, F32),
                tuple(jnp.zeros(acc_shape, F32) for _ in range(1 if diff else 2)))
        m8, accs = lax.fori_loop(0, nch, body, init, unroll=min(nch, ATT_UNROLL))
        return jnp.max(m8, axis=0, keepdims=True), accs

    if diff:
        lv = lam_ref[...]
        lam = (jnp.exp(jnp.sum(lv[0:1] * lv[1:2], axis=-1, keepdims=True))
               - jnp.exp(jnp.sum(lv[2:3] * lv[3:4], axis=-1, keepdims=True)) + lam_init)

    mx_prev = None
    for st in range(n_tiles + 1):
        j_a = st if st < n_tiles else None
        j_b = st - 1 if st >= 1 else None
        mx_new, accs = stage(j_a, j_b, mx_prev)
        if j_b is not None:
            if diff:
                a = accs[0]
                o_t = (a[:v_rows, :tq] / a[v_rows:v_rows + 1, :tq]
                       - lam * (a[:v_rows, tq:] / a[v_rows:v_rows + 1, tq:]))
                o = o_t.T
                inv = lax.rsqrt(jnp.mean(o * o, axis=-1, keepdims=True) + RMS_EPS)
                o = (o * inv * g_ref[...]) * (1.0 - lam_init)
            else:
                o_t = jnp.concatenate([a[:v_rows] / a[v_rows:v_rows + 1] for a in accs], axis=0)
                o = o_t.T
            o_ref[:, j_b * LANES:(j_b + 1) * LANES] = o.astype(BF16)
        mx_prev = mx_new


def _attention(q, k, v, cache_k, cache_v, *, row0, n_b, t, tq, gq=None, rope=None, diff=None):
    is_diff = diff is not None
    qw = ATT_GROUP * LANES if is_diff else D_MODEL
    n_groups = D_MODEL // qw
    kvw = qw if is_diff else N_KV_HEADS * HEAD_DIM
    nq_blocks = t // tq
    has_cache = cache_k is not None
    past = cache_k.shape[1] if has_cache else 0
    kch = min(ATT_KCH, t)
    assert t % kch == 0 and past % kch == 0
    n_own, n_cache = t // kch, past // kch
    nch = n_own + n_cache
    q_spec = pl.BlockSpec((tq, qw), lambda b, p, iq: (row0 // tq + b * nq_blocks + iq, p))
    kv_spec = pl.BlockSpec((t, kvw), lambda b, p, iq: (row0 // t + b, p),
                           pipeline_mode=pl.Buffered(2 if is_diff else 1))
    in_specs, args = [], []
    const = lambda a: pl.BlockSpec(a.shape, lambda b, p, iq: (0,) * a.ndim)
    if is_diff:
        lam_vec, g_sub, lam_init = diff
        in_specs += [const(lam_vec), const(g_sub)]
        args += [lam_vec, g_sub]
    else:
        lam_init = 0.0
        in_specs.append(const(gq))
        args.append(gq)
    if rope is not None:
        rope_spec = pl.BlockSpec((LANES, tq), lambda b, p, iq: (0, iq))
        in_specs += [rope_spec, rope_spec]
        args += list(rope)
    in_specs += [q_spec, kv_spec, kv_spec]
    args += [q, k, v]
    if has_cache:
        c_spec = pl.BlockSpec((None, past, kvw), lambda b, p, iq: (b, 0, p))
        in_specs += [c_spec, c_spec]
        args += [cache_k, cache_v]
    n_kv = kvw // LANES
    v_rows = (LANES if is_diff else HEAD_DIM) + SUBLANES
    return pl.pallas_call(
        functools.partial(_attn_kernel, diff=is_diff, rope=rope is not None, has_cache=has_cache, tq=tq, kch=kch,
                          n_own=n_own, n_cache=n_cache, lam_init=lam_init),
        out_shape=jax.ShapeDtypeStruct((n_b * t, D_MODEL), BF16),
        grid=(n_b, n_groups, nq_blocks),
        in_specs=in_specs,
        out_specs=pl.BlockSpec((tq, qw), lambda b, p, iq: (b * nq_blocks + iq, p)),
        scratch_shapes=[pltpu.VMEM((n_kv, nch, kch, LANES), BF16),
                        pltpu.VMEM((n_kv, nch, 1 if is_diff else 2, v_rows, kch), BF16),
                        pltpu.VMEM((2, nch, kch, 2 * tq), F32)],
        compiler_params=_cparams("arbitrary", "arbitrary", "arbitrary"),
        name=("diff_attn" if is_diff else "gqa_attn") + ("_cache" if has_cache else ""),
    )(*args)


def _oproj_kernel(*refs, tm, n_ctx, n_x):
    x_refs = refs[:n_x]
    ac_ref, as_ref, w_ref, gate_ref, lng_ref, lnb_ref, o_ref = refs[n_x:]
    a = jnp.where(pl.program_id(0) * tm < n_ctx, ac_ref[...], as_ref[...])
    out = jnp.dot(a, w_ref[...], preferred_element_type=F32)
    z = DEEPNORM_ALPHA * _x_value(x_refs, tm, n_ctx) + gate_ref[...] * out
    o_ref[...] = _layer_norm_rows(z, lng_ref[...], lnb_ref[...])


def _oproj(a_ctx, a_smp, w, x, mod, lng, lnb, *, n_ctx, t_s):
    m = a_ctx.shape[0] + a_smp.shape[0]
    kdim = a_ctx.shape[1]
    tm = 512
    ncb = n_ctx // tm
    const = lambda arr: pl.BlockSpec(arr.shape, lambda i: (0,) * arr.ndim)
    x_specs, x_args = _x_specs(x, tm, n_ctx)
    return pl.pallas_call(
        functools.partial(_oproj_kernel, tm=tm, n_ctx=n_ctx, n_x=len(x_args)),
        out_shape=jax.ShapeDtypeStruct((m, D_MODEL), F32),
        grid=(m // tm,),
        in_specs=x_specs + [pl.BlockSpec((tm, kdim), lambda i: (jnp.minimum(i, ncb - 1), 0)),
                            pl.BlockSpec((tm, kdim), lambda i: (jnp.maximum(i - ncb, 0), 0)),
                            const(w), _mod_spec(2, tm, n_ctx, t_s), const(lng), const(lnb)],
        out_specs=pl.BlockSpec((tm, D_MODEL), lambda i: (i, 0)),
        compiler_params=_cparams("parallel"),
        name="oproj_ln",
    )(*x_args, a_ctx, a_smp, w, mod, lng, lnb)


def _gelu_tanh(x):
    return 0.5 * x * (1.0 + jnp.tanh(math.sqrt(2.0 / math.pi) * (x + 0.044715 * (x * x * x))))


def _rnn_oproj_kernel(hf_ref, hb_ref, br_ref, w_ref, x_ref, gate_ref, lng_ref, lnb_ref, o_ref):
    y = (hf_ref[...] + hb_ref[...]) * _gelu_tanh(br_ref[...])
    out = jnp.dot(y.astype(BF16), w_ref[...], preferred_element_type=F32)
    z = DEEPNORM_ALPHA * x_ref[...] + gate_ref[...] * out
    o_ref[...] = _layer_norm_rows(z, lng_ref[...], lnb_ref[...])


def _rnn_oproj(hs_f, hs_b, bx, w, x, mod, lng, lnb, *, n_ctx, t_s):
    m = x.shape[0]
    tm = 512
    const = lambda arr: pl.BlockSpec(arr.shape, lambda i: (0,) * arr.ndim)
    row = pl.BlockSpec((tm, D_MODEL), lambda i: (i, 0))
    rnn_row = pl.BlockSpec((tm, D_RNN), lambda i: (i, 0))
    return pl.pallas_call(
        _rnn_oproj_kernel,
        out_shape=jax.ShapeDtypeStruct((m, D_MODEL), F32),
        grid=(m // tm,),
        in_specs=[rnn_row, rnn_row, rnn_row, const(w), row,
                  _mod_spec(2, tm, n_ctx, t_s), const(lng), const(lnb)],
        out_specs=row,
        compiler_params=_cparams("parallel"),
        name="rnn_oproj_ln",
    )(hs_f, hs_b, bx, w, x, mod, lng, lnb)


def _mod_mm_kernel(x_ref, sc_ref, sh_ref, w_ref, o_ref):
    h = (x_ref[...] * (1.0 + sc_ref[...]) + sh_ref[...]).astype(BF16)
    o_ref[...] = jnp.dot(h, w_ref[...], preferred_element_type=F32)


def _rnn_in(x, mod, w, *, n_ctx, t_s):
    m = x.shape[0]
    n = w.shape[1]
    tm = 512
    return pl.pallas_call(
        _mod_mm_kernel,
        out_shape=jax.ShapeDtypeStruct((m, n), F32),
        grid=(m // tm,),
        in_specs=[pl.BlockSpec((tm, D_MODEL), lambda i: (i, 0)),
                  _mod_spec(1, tm, n_ctx, t_s), _mod_spec(0, tm, n_ctx, t_s),
                  pl.BlockSpec(w.shape, lambda i: (0, 0))],
        out_specs=pl.BlockSpec((tm, n), lambda i: (i, 0)),
        compiler_params=_cparams("parallel"),
        name="rnn_in",
    )(x, mod, mod, w)


def _rnn_gate_kernel(xb_ref, prev_ref, next_ref, cw_ref, cb_ref, wg_ref, bg_ref, lam_ref,
                     af_ref, uf_ref, ab_ref, ub_ref, ext_ref, *, tm, n_ctx, t_c, t_s):
    i = pl.program_id(0)
    r0 = i * tm
    pos = jnp.where(r0 < n_ctx, r0 % t_c, (r0 - n_ctx) % t_s)
    t_seq = jnp.where(r0 < n_ctx, t_c, t_s)
    at_start = pos == 0
    at_end = pos + tm == t_seq
    ext_ref[0:SUBLANES, :] = jnp.where(at_start, 0.0, prev_ref[...])
    ext_ref[SUBLANES:SUBLANES + tm, :] = xb_ref[...]
    ext_ref[SUBLANES + tm:, :] = jnp.where(at_end, 0.0, next_ref[...])
    xc = 0.0
    for j in range(CONV_W):
        off = SUBLANES - CONV_LEFT + j
        xc = xc + ext_ref[off:off + tm, :] * cw_ref[j:j + 1, :]
    xc = xc + cb_ref[...]
    xcb = xc.astype(BF16)
    neg_lam = -lam_ref[...]
    sp = jnp.maximum(neg_lam, 0.0) + jnp.log1p(jnp.exp(-jnp.abs(neg_lam)))

    def gate(g, ct, cols):
        k0 = GATE_K0[ct]
        pre = jnp.dot(xcb[:, k0:k0 + GATE_K], wg_ref[g, ct], preferred_element_type=F32)
        return _sigmoid(pre + bg_ref[:, g * D_RNN + cols.start:g * D_RNN + cols.stop])

    for z, (a_ref, u_ref) in enumerate(((af_ref, uf_ref), (ab_ref, ub_ref))):
        for ct in range(D_RNN // GATE_N):
            cols = slice(ct * GATE_N, (ct + 1) * GATE_N)
            r = gate(2 * z, ct, cols)
            g_in = gate(2 * z + 1, ct, cols)
            log_a = -RGLRU_C * r * sp[z:z + 1, cols]
            a = jnp.exp(log_a)
            a_ref[:, cols] = a
            u_ref[:, cols] = jnp.sqrt(-jnp.tanh(log_a) * (a * a + 1.0)) * (g_in * xc[:, cols])


def _rnn_gates(bx, conv_w, conv_b, wg, bg, lam, *, n_ctx, t_c, t_s):
    m = bx.shape[0]
    tm = 256
    n8 = tm // SUBLANES
    last8 = m // SUBLANES - 1
    const = lambda arr: pl.BlockSpec(arr.shape, lambda i: (0,) * arr.ndim)
    row = pl.BlockSpec((tm, D_RNN), lambda i: (i, 0))
    out = jax.ShapeDtypeStruct((m, D_RNN), F32)
    return pl.pallas_call(
        functools.partial(_rnn_gate_kernel, tm=tm, n_ctx=n_ctx, t_c=t_c, t_s=t_s),
        out_shape=(out, out, out, out),
        grid=(m // tm,),
        in_specs=[pl.BlockSpec((tm, D_RNN), lambda i: (i, 1)),
                  pl.BlockSpec((SUBLANES, D_RNN), lambda i: (jnp.maximum(i * n8 - 1, 0), 1)),
                  pl.BlockSpec((SUBLANES, D_RNN), lambda i: (jnp.minimum((i + 1) * n8, last8), 1)),
                  const(conv_w), const(conv_b), const(wg), const(bg), const(lam)],
        out_specs=(row, row, row, row),
        scratch_shapes=[pltpu.VMEM((tm + 2 * SUBLANES, D_RNN), F32)],
        compiler_params=_cparams("parallel"),
        name="rnn_gates",
    )(bx, bx, bx, conv_w, conv_b, wg, bg, lam)


def _rnn_scan_kernel(fblk_ref, bblk_ref, seq_ref, first_ref,
                     af_ref, uf_ref, ab_ref, ub_ref, h0_ref, hf_ref, hb_ref, fin_ref, carry_ref, *, tc):
    s = pl.program_id(0)

    @pl.when(first_ref[s] == 1)
    def _():
        carry_ref[...] = h0_ref[...]

    def step(t, carry):
        h_f, h_b = carry
        tb = tc - 1 - t
        h_f = af_ref[pl.ds(t, 1), :] * h_f + uf_ref[pl.ds(t, 1), :]
        h_b = ab_ref[pl.ds(tb, 1), :] * h_b + ub_ref[pl.ds(tb, 1), :]
        hf_ref[pl.ds(t, 1), :] = h_f
        hb_ref[pl.ds(tb, 1), :] = h_b
        return h_f, h_b

    h_f, h_b = lax.fori_loop(0, tc, step, (carry_ref[0:1, :], carry_ref[1:2, :]), unroll=8)
    carry_ref[0:1, :] = h_f
    carry_ref[1:2, :] = h_b
    fin_ref[...] = carry_ref[...]


def _rnn_scan(a_f, u_f, a_b, u_b, h0, *, seq_lens):
    m = a_f.shape[0]
    tc = SCAN_CHUNK
    fblk, bblk, seq, first = [], [], [], []
    blk0 = 0
    for si, t in enumerate(seq_lens):
        nch = t // tc
        for c in range(nch):
            fblk.append(blk0 + c)
            bblk.append(blk0 + nch - 1 - c)
            seq.append(si)
            first.append(1 if c == 0 else 0)
        blk0 += nch
    n_steps = len(fblk)
    tables = [jnp.asarray(np.array(v, np.int32)) for v in (fblk, bblk, seq, first)]
    f_spec = pl.BlockSpec((tc, D_RNN), lambda s, fb, bb, sq, fr: (fb[s], 0))
    b_spec = pl.BlockSpec((tc, D_RNN), lambda s, fb, bb, sq, fr: (bb[s], 0))
    st_spec = pl.BlockSpec((None, 2, D_RNN), lambda s, fb, bb, sq, fr: (sq[s], 0, 0))
    out = jax.ShapeDtypeStruct((m, D_RNN), F32)
    return pl.pallas_call(
        functools.partial(_rnn_scan_kernel, tc=tc),
        out_shape=(out, out, jax.ShapeDtypeStruct(h0.shape, F32)),
        grid_spec=pltpu.PrefetchScalarGridSpec(
            num_scalar_prefetch=4, grid=(n_steps,),
            in_specs=[f_spec, f_spec, b_spec, b_spec, st_spec],
            out_specs=(f_spec, b_spec, st_spec),
            scratch_shapes=[pltpu.VMEM((2, D_RNN), F32)]),
        compiler_params=_cparams("arbitrary"),
        name="rnn_scan",
    )(*tables, a_f, u_f, a_b, u_b, h0)


def _ffn_kernel(x_ref, sc_ref, sh_ref, gate_ref, wg_ref, wu_ref, wd_ref, lng_ref, lnb_ref,
                o_ref, h_ref, acc_ref):
    j = pl.program_id(1)

    def half(h):
        g = jnp.dot(h, wg_ref[...], preferred_element_type=F32)
        u = jnp.dot(h, wu_ref[...], preferred_element_type=F32)
        act = ((g * _sigmoid(g)) * u).astype(BF16)
        return jnp.dot(act, wd_ref[...], preferred_element_type=F32)

    @pl.when(j == 0)
    def _():
        h = (x_ref[...] * (1.0 + sc_ref[...]) + sh_ref[...]).astype(BF16)
        h_ref[...] = h
        acc_ref[...] = half(h)

    @pl.when(j == 1)
    def _():
        z = DEEPNORM_ALPHA * x_ref[...] + gate_ref[...] * (acc_ref[...] + half(h_ref[...]))
        o_ref[...] = _layer_norm_rows(z, lng_ref[...], lnb_ref[...])


def _ffn(x, mod, w_gu, w_down, layer, lng, lnb, *, n_ctx, t_s):
    m = x.shape[0]
    tm = 512
    tf = D_FF // 2
    nf = D_FF // tf
    assert nf == 2
    const = lambda arr: pl.BlockSpec(arr.shape, lambda i, j: (0,) * arr.ndim)
    row = pl.BlockSpec((tm, D_MODEL), lambda i, j: (i, 0))
    return pl.pallas_call(
        _ffn_kernel,
        out_shape=jax.ShapeDtypeStruct((m, D_MODEL), F32),
        grid=(m // tm, nf),
        in_specs=[row, _mod_spec(4, tm, n_ctx, t_s), _mod_spec(3, tm, n_ctx, t_s), _mod_spec(5, tm, n_ctx, t_s),
                  pl.BlockSpec((None, D_MODEL, tf), lambda i, j: (layer, 0, j)),
                  pl.BlockSpec((None, D_MODEL, tf), lambda i, j: (layer, 0, nf + j)),
                  pl.BlockSpec((None, tf, D_MODEL), lambda i, j: (layer, j, 0)),
                  const(lng), const(lnb)],
        out_specs=row,
        scratch_shapes=[pltpu.VMEM((tm, D_MODEL), BF16), pltpu.VMEM((tm, D_MODEL), F32)],
        compiler_params=_cparams("parallel", "arbitrary"),
        name="ffn",
    )(x, mod, mod, mod, w_gu, w_gu, w_down, lng, lnb)


ROW_TILES = D_MODEL // LANES


def _to_token_rows(ref, base, value):
    n = value.shape[0]
    for j in range(ROW_TILES):
        ref[pl.ds(base + j, n, stride=ROW_TILES), :] = value[:, j * LANES:(j + 1) * LANES]


def _from_token_rows(ref, base, n):
    return jnp.concatenate([ref[pl.ds(base + j, n, stride=ROW_TILES), :] for j in range(ROW_TILES)], axis=1)


def _token_row(ref, r):
    start = r * ROW_TILES if isinstance(r, int) else pl.multiple_of(r * ROW_TILES, ROW_TILES)
    return ref.at[pl.ds(start, ROW_TILES)]


def _route_kernel(x_ref, sc_ref, sh_ref, wr_ref, route_ref, gates_ref, cnt_ref, carry_ref, *, tm):
    @pl.when(pl.program_id(0) == 0)
    def _():
        carry_ref[...] = jnp.zeros_like(carry_ref)

    lane = lax.broadcasted_iota(jnp.int32, (tm, LANES), 1)
    h = x_ref[...] * (1.0 + sc_ref[...]) + sh_ref[...]
    logits = jnp.dot(h.astype(BF16), wr_ref[...], preferred_element_type=F32)
    s1 = jnp.where(lane < N_EXPERTS, logits, -jnp.inf)
    m1 = jnp.max(s1, axis=-1, keepdims=True)
    i1 = jnp.min(jnp.where(s1 == m1, lane, LANES), axis=-1, keepdims=True)
    s2 = jnp.where(lane == i1, -jnp.inf, s1)
    m2 = jnp.max(s2, axis=-1, keepdims=True)
    i2 = jnp.min(jnp.where(s2 == m2, lane, LANES), axis=-1, keepdims=True)
    e2 = jnp.exp(m2 - m1)
    den = 1.0 + e2
    hit = jnp.where((lane == i1) | (lane == i2), 1.0, 0.0)
    earlier = lax.broadcasted_iota(jnp.int32, (tm, tm), 1) < lax.broadcasted_iota(jnp.int32, (tm, tm), 0)
    before = carry_ref[...] + jnp.dot(jnp.where(earlier, 1.0, 0.0).astype(BF16), hit.astype(BF16),
                                      preferred_element_type=F32)
    r1 = jnp.sum(jnp.where(lane == i1, before, 0.0), axis=-1, keepdims=True).astype(jnp.int32)
    r2 = jnp.sum(jnp.where(lane == i2, before, 0.0), axis=-1, keepdims=True).astype(jnp.int32)
    carry_ref[...] += jnp.sum(hit, axis=0, keepdims=True)
    route = jnp.where(lane == 0, i1, jnp.where(lane == 1, i2, jnp.where(lane == 2, r1, jnp.where(lane == 3, r2, 0))))
    route_ref[...] = route[:, :ROUTE_COLS]
    gates_ref[...] = jnp.where(lane == 0, 1.0 / den, jnp.where(lane == 1, e2 / den, 0.0))[:, :ROUTE_COLS]
    cnt_ref[...] = jnp.broadcast_to(carry_ref[...], cnt_ref.shape)


def _moe_route(x, mod, w_router, *, n_ctx, t_s):
    m = x.shape[0]
    tm = 512
    small = pl.BlockSpec((tm, ROUTE_COLS), lambda i: (i, 0))
    return pl.pallas_call(
        functools.partial(_route_kernel, tm=tm),
        out_shape=(jax.ShapeDtypeStruct((m, ROUTE_COLS), jnp.int32), jax.ShapeDtypeStruct((m, ROUTE_COLS), F32),
                   jax.ShapeDtypeStruct((SUBLANES, LANES), F32)),
        grid=(m // tm,),
        in_specs=[pl.BlockSpec((tm, D_MODEL), lambda i: (i, 0)), _mod_spec(4, tm, n_ctx, t_s),
                  _mod_spec(3, tm, n_ctx, t_s), pl.BlockSpec(w_router.shape, lambda i: (0, 0))],
        out_specs=(small, small, pl.BlockSpec((SUBLANES, LANES), lambda i: (0, 0))),
        scratch_shapes=[pltpu.VMEM((1, LANES), F32)],
        compiler_params=_cparams("arbitrary"),
        name="moe_route",
    )(x, mod, mod, w_router)


def _dispatch_kernel(slot_ref, x_ref, sc_ref, sh_ref, xs_hbm, buf, sem, *, tok):
    i = pl.program_id(0)
    par = i % 2
    cur = buf.at[par]
    _to_token_rows(cur, 0, x_ref[...] * (1.0 + sc_ref[...]) + sh_ref[...])

    def issue(t, c):
        src = _token_row(cur, t)
        pltpu.make_async_copy(src, _token_row(xs_hbm, slot_ref[2 * t]), sem.at[par]).start(priority=0)
        pltpu.make_async_copy(src, _token_row(xs_hbm, slot_ref[2 * t + 1]), sem.at[par]).start(priority=1)
        return c
    lax.fori_loop(0, tok, issue, 0, unroll=4)

    def drain(which):
        def body(t, c):
            pltpu.make_async_copy(_token_row(buf.at[which], 0), _token_row(xs_hbm, 0), sem.at[which]).wait()
            return c
        lax.fori_loop(0, 2 * tok, body, 0, unroll=8)

    @pl.when(i > 0)
    def _():
        drain(1 - par)

    @pl.when(i == pl.num_programs(0) - 1)
    def _():
        drain(par)


def _moe_dispatch(slots, x, mod, *, n_ctx, t_s):
    m = x.shape[0]
    tok = 512
    return pl.pallas_call(
        functools.partial(_dispatch_kernel, tok=tok),
        out_shape=jax.ShapeDtypeStruct((2 * m * ROW_TILES, LANES), F32),
        grid=(m // tok,),
        in_specs=[pl.BlockSpec((2 * tok,), lambda i: (i,), memory_space=pltpu.SMEM),
                  pl.BlockSpec((tok, D_MODEL), lambda i: (i, 0)),
                  _mod_spec(4, tok, n_ctx, t_s), _mod_spec(3, tok, n_ctx, t_s)],
        out_specs=pl.BlockSpec(memory_space=pl.ANY),
        scratch_shapes=[pltpu.VMEM((2, tok * ROW_TILES, LANES), F32), pltpu.SemaphoreType.DMA((2,))],
        compiler_params=_cparams("arbitrary"),
        name="moe_dispatch",
    )(slots, x, mod, mod)


def _experts_kernel(wt_ref, we_ref, lo_ref, hi_ref, nw_ref, xs_ref, wg_ref, wu_ref, wd_ref, o_ref, *, rows):
    del we_ref
    w = pl.program_id(0)

    def expert_rows():
        xs = _from_token_rows(xs_ref, 0, rows).astype(BF16)
        g = jnp.dot(xs, wg_ref[...], preferred_element_type=F32)
        u = jnp.dot(xs, wu_ref[...], preferred_element_type=F32)
        act = ((g * _sigmoid(g)) * u).astype(BF16)
        return jnp.dot(act, wd_ref[...], preferred_element_type=F32)

    active = w < nw_ref[0]
    first = jnp.logical_or(w == 0, wt_ref[jnp.maximum(w - 1, 0)] != wt_ref[w])

    @pl.when(jnp.logical_and(active, first))
    def _():
        _to_token_rows(o_ref, 0, expert_rows())

    @pl.when(jnp.logical_and(active, jnp.logical_not(first)))
    def _():
        row = lax.broadcasted_iota(jnp.int32, (rows, D_MODEL), 0)
        mine = (row >= lo_ref[w]) & (row < hi_ref[w])
        _to_token_rows(o_ref, 0, jnp.where(mine, expert_rows(), _from_token_rows(o_ref, 0, rows)))


def _moe_experts(items, xs, w_gu, w_down, layer):
    rows = MOE_TILE
    n_items = items[0].shape[0]
    tile = lambda w, wt, we, lo, hi, nw: (wt[w], 0)
    return pl.pallas_call(
        functools.partial(_experts_kernel, rows=rows),
        out_shape=jax.ShapeDtypeStruct(xs.shape, F32),
        grid_spec=pltpu.PrefetchScalarGridSpec(
            num_scalar_prefetch=5, grid=(n_items,),
            in_specs=[pl.BlockSpec((rows * ROW_TILES, LANES), tile),
                      pl.BlockSpec((None, None, D_MODEL, D_FF_EXPERT),
                                   lambda w, wt, we, lo, hi, nw: (layer, we[w], 0, 0)),
                      pl.BlockSpec((None, None, D_MODEL, D_FF_EXPERT),
                                   lambda w, wt, we, lo, hi, nw: (layer, we[w], 0, 1)),
                      pl.BlockSpec((None, None, D_FF_EXPERT, D_MODEL),
                                   lambda w, wt, we, lo, hi, nw: (layer, we[w], 0, 0))],
            out_specs=pl.BlockSpec((rows * ROW_TILES, LANES), tile)),
        compiler_params=_cparams("arbitrary"),
        name="moe_experts",
    )(*items, xs, w_gu, w_gu, w_down)


def _combine_kernel(idx_ref, nxt_ref, ys_hbm, gates_ref, x_ref, gate_ref, lng_ref, lnb_ref, o_ref, buf, sem, *, tm):
    i = pl.program_id(0)
    slot = i % 2

    def issue(ref, s):
        def body(t, c):
            pltpu.make_async_copy(_token_row(ys_hbm, ref[2 * t]), _token_row(buf.at[s], t),
                                  sem.at[s]).start(priority=0)
            pltpu.make_async_copy(_token_row(ys_hbm, ref[2 * t + 1]), _token_row(buf.at[s], tm + t),
                                  sem.at[s]).start(priority=1)
            return c
        lax.fori_loop(0, tm, body, 0, unroll=4)

    @pl.when(i == 0)
    def _():
        issue(idx_ref, 0)

    @pl.when(i + 1 < pl.num_programs(0))
    def _():
        issue(nxt_ref, 1 - slot)

    def drain(r, c):
        pltpu.make_async_copy(_token_row(ys_hbm, 0), _token_row(buf.at[slot], 0), sem.at[slot]).wait()
        return c
    lax.fori_loop(0, 2 * tm, drain, 0, unroll=8)
    cur = buf.at[slot]
    y = (gates_ref[:, 0:1] * _from_token_rows(cur, 0, tm)
         + gates_ref[:, 1:2] * _from_token_rows(cur, tm * ROW_TILES, tm))
    z = DEEPNORM_ALPHA * x_ref[...] + gate_ref[...] * y
    o_ref[...] = _layer_norm_rows(z, lng_ref[...], lnb_ref[...])


def _moe_combine(slots, ys, gates, x, mod, lng, lnb, *, n_ctx, t_s, row0=0, n_rows=None):
    tm = 256
    n_blocks = (x.shape[0] if n_rows is None else n_rows) // tm
    b0 = row0 // tm
    idx = lambda f: pl.BlockSpec((2 * tm,), f, memory_space=pltpu.SMEM)
    const = lambda arr: pl.BlockSpec(arr.shape, lambda i: (0,) * arr.ndim)
    return pl.pallas_call(
        functools.partial(_combine_kernel, tm=tm),
        out_shape=jax.ShapeDtypeStruct((n_blocks * tm, D_MODEL), F32),
        grid=(n_blocks,),
        in_specs=[idx(lambda i: (b0 + i,)), idx(lambda i: (b0 + jnp.minimum(i + 1, n_blocks - 1),)),
                  pl.BlockSpec(memory_space=pl.ANY), pl.BlockSpec((tm, ROUTE_COLS), lambda i: (b0 + i, 0)),
                  pl.BlockSpec((tm, D_MODEL), lambda i: (b0 + i, 0)),
                  pl.BlockSpec((None, 1, D_MODEL), lambda i: (_seg_of_block(b0 + i, tm, n_ctx, t_s), 0, 5)),
                  const(lng), const(lnb)],
        out_specs=pl.BlockSpec((tm, D_MODEL), lambda i: (i, 0)),
        scratch_shapes=[pltpu.VMEM((2, 2 * tm * ROW_TILES, LANES), F32), pltpu.SemaphoreType.DMA((2,))],
        compiler_params=_cparams("arbitrary"),
        name="moe_combine",
    )(slots, slots, ys, gates, x, mod, lng, lnb)


def _moe_work_items(counts, n_rows):
    n_tiles = n_rows // MOE_TILE
    n_items = n_tiles + N_EXPERTS - 1
    g_end = jnp.cumsum(counts)
    g_start = g_end - counts
    row0 = jnp.arange(n_tiles, dtype=jnp.int32) * MOE_TILE
    first_e = jnp.sum(row0[:, None] >= g_end[None, :], axis=1)
    last_e = jnp.sum((row0 + MOE_TILE - 1)[:, None] >= g_end[None, :], axis=1)
    per_tile = last_e - first_e + 1
    item0 = jnp.cumsum(per_tile) - per_tile
    w = jnp.arange(n_items, dtype=jnp.int32)
    tile = jnp.clip(jnp.sum(w[:, None] >= item0[None, :], axis=1) - 1, 0, n_tiles - 1)
    expert = jnp.clip(first_e[tile] + w - item0[tile], 0, N_EXPERTS - 1)
    lo = jnp.clip(g_start[expert] - row0[tile], 0, MOE_TILE)
    hi = jnp.clip(g_end[expert] - row0[tile], 0, MOE_TILE)
    n_used = jnp.sum(per_tile).reshape(1)
    i32 = lambda a: a.astype(jnp.int32)
    return i32(tile), i32(expert), i32(lo), i32(hi), i32(n_used), g_start


def _moe(x, mod, w_router, w_gu, w_down, layer, lng, lnb, *, n_ctx, t_s, split=False):
    m = x.shape[0]
    assert (2 * m) % MOE_TILE == 0
    route, gates, cnt = _moe_route(x, mod, w_router, n_ctx=n_ctx, t_s=t_s)
    counts = cnt[0, :N_EXPERTS].astype(jnp.int32)
    *items, g_start = _moe_work_items(counts, 2 * m)
    slots = (g_start[route[:, 0:2]] + route[:, 2:4]).reshape(2 * m).astype(jnp.int32)
    xs = _moe_dispatch(slots, x, mod, n_ctx=n_ctx, t_s=t_s)
    ys = _moe_experts(items, xs, w_gu, w_down, layer)
    comb = functools.partial(_moe_combine, slots, ys, gates, x, mod, lng, lnb, n_ctx=n_ctx, t_s=t_s)
    if split:
        return comb(row0=0, n_rows=n_ctx), comb(row0=n_ctx, n_rows=m - n_ctx)
    return comb()


def _rope_tables(n_tokens):
    n_rows = n_tokens // GRID_W
    row = jnp.repeat(jnp.arange(n_rows, dtype=F32), GRID_W)
    col = jnp.tile(jnp.arange(GRID_W, dtype=F32), n_rows)
    n_axis = HEAD_DIM // 4
    inv = ROPE_THETA ** (-jnp.arange(n_axis, dtype=F32) / n_axis)
    ang = jnp.concatenate([row[:, None] * inv, col[:, None] * inv], axis=-1)
    cos, sin = jnp.cos(ang), jnp.sin(ang)
    reps = LANES // HEAD_DIM
    cos_t = jnp.tile(jnp.concatenate([cos, cos], axis=-1), (1, reps))
    sin_t = jnp.tile(jnp.concatenate([-sin, sin], axis=-1), (1, reps))
    return cos_t, sin_t


def _gqa_head_perm():
    cols = []
    per_pair = 2 * (N_Q_HEADS // N_KV_HEADS)
    for p in range(N_KV_HEADS // 2):
        for j in range(N_Q_HEADS // N_KV_HEADS):
            for hd in (per_pair * p + j, per_pair * p + per_pair // 2 + j):
                cols.append(np.arange(hd * HEAD_DIM, (hd + 1) * HEAD_DIM))
    return np.concatenate(cols)


def _block_diag_windows(w):
    eye = jnp.eye(N_RNN_BLOCKS, dtype=bool)[:, None, :, None]
    dense = jnp.where(eye, w[:, :, None, :], 0.0).reshape(D_RNN, D_RNN)
    for ct, k0 in enumerate(GATE_K0):
        lo, hi = ct * GATE_N, (ct + 1) * GATE_N
        assert k0 <= RNN_BLOCK * (lo // RNN_BLOCK) and RNN_BLOCK * -(-hi // RNN_BLOCK) <= k0 + GATE_K <= D_RNN
    return jnp.stack([dense[k0:k0 + GATE_K, ct * GATE_N:(ct + 1) * GATE_N] for ct, k0 in enumerate(GATE_K0)])


def kernel(x_prompt, x_sample, cache_attn_k, cache_attn_v, state_rglru, cache_diff_k, cache_diff_v, c, c_ctx, w_ada, b_ada, ln_g, ln_b, attn_w_qkv, attn_g_q, attn_g_k, attn_w_o, rnn_w_in, rnn_conv_w, rnn_conv_b, rnn_w_a, rnn_b_a, rnn_w_x, rnn_b_x, rnn_lambda, rnn_w_out, diff_w_qkv, diff_lambda, diff_g_sub, diff_w_o, ffn_w_gu, ffn_w_down, moe_w_router, moe_w_gu, moe_w_down):
    b_c, t_c, d = x_prompt.shape
    b_s, t_s, _ = x_sample.shape
    n_ctx = b_c * t_c
    n_s = b_s * t_s
    past = cache_attn_k.shape[2]
    assert d == D_MODEL and n_ctx % t_s == 0 and t_c == SCAN_CHUNK and t_s % SCAN_CHUNK == 0
    assert 1 + b_s <= COND_ROWS

    assert DEPTH % 2 == 0
    x = (x_prompt.reshape(n_ctx, d), x_sample.reshape(n_s, d))
    cond =jnp.zeros((COND_ROWS, d), F32).at[0].set(c_ctx).at[1:1 + b_s].set(c)
    mods = _ada_table(cond, w_ada, b_ada)[:, :1 + b_s].reshape(DEPTH, 1 + b_s, 1, 6 * d)

    cos_t, sin_t = _rope_tables(t_s)
    rope_t = (cos_t.T, sin_t.T)
    perm = _gqa_head_perm()
    nq = N_Q_HEADS * HEAD_DIM
    tile2 = lambda g: jnp.tile(g, LANES // g.shape[-1]).reshape(1, LANES)
    kw = dict(n_ctx=n_ctx, t_s=t_s)
    moe_gu_b = moe_w_gu.astype(BF16)
    moe_down_b = moe_w_down.astype(BF16)
    ffn_gu_b = ffn_w_gu.astype(BF16)
    ffn_down_b = ffn_w_down.astype(BF16)

    attn_k, attn_v, rnn_s, diff_k, diff_v = [], [], [], [], []
    for li in range(DEPTH):
        mod = mods[li]
        lng = ln_g[li].reshape(2, 1, d)
        lnb = ln_b[li].reshape(2, 1, d)
        j = li // N_MIXERS
        kind = li % N_MIXERS
        if kind == 0:
            w = attn_w_qkv[j]
            w = jnp.concatenate([w[:, :nq][:, perm], w[:, nq:]], axis=1).astype(BF16)
            q, kb, vb, kf, vf = _gqa_qkv(x, mod, w, tile2(attn_g_k[j]), cos_t, sin_t, **kw)
            gq = tile2(attn_g_q[j]).reshape(LANES, 1)
            ck = cache_attn_k[:, j].reshape(b_s, past, N_KV_HEADS * HEAD_DIM).astype(BF16)
            cv = cache_attn_v[:, j].reshape(b_s, past, N_KV_HEADS * HEAD_DIM).astype(BF16)
            o_c = _attention(q, kb, vb, None, None, row0=0, n_b=b_c, t=t_c, tq=t_c, gq=gq)
            o_s = _attention(q, kb, vb, ck, cv, row0=n_ctx, n_b=b_s, t=t_s, tq=512, gq=gq, rope=rope_t)
            x = _oproj(o_c, o_s, attn_w_o[j][perm, :].astype(BF16), x, mod, lng[0], lnb[0], **kw)
            attn_k.append(kf)
            attn_v.append(vf)
        elif kind == 1:
            bx = _rnn_in(x, mod, rnn_w_in[j].astype(BF16), **kw)
            wg = jnp.stack([_block_diag_windows(rnn_w_a[j, 0]), _block_diag_windows(rnn_w_x[j, 0]),
                            _block_diag_windows(rnn_w_a[j, 1]), _block_diag_windows(rnn_w_x[j, 1])]).astype(BF16)
            bg = jnp.concatenate([rnn_b_a[j, 0], rnn_b_x[j, 0], rnn_b_a[j, 1], rnn_b_x[j, 1]]).reshape(1, -1)
            a_f, u_f, a_b, u_b = _rnn_gates(bx, rnn_conv_w[j], rnn_conv_b[j].reshape(1, -1), wg, bg,
                                            rnn_lambda[j], n_ctx=n_ctx, t_c=t_c, t_s=t_s)
            h0 = jnp.concatenate([jnp.zeros((b_c, 2, D_RNN), F32), state_rglru[:, j]], axis=0)
            hs_f, hs_b, fin = _rnn_scan(a_f, u_f, a_b, u_b, h0, seq_lens=[t_c] * b_c + [t_s] * b_s)
            x = _rnn_oproj(hs_f, hs_b, bx, rnn_w_out[j].astype(BF16), x, mod, lng[0], lnb[0], **kw)
            rnn_s.append(fin[:b_c])
        else:
            lam_init = 0.8 - 0.6 * math.exp(-0.3 * li)
            q, kb, vb, kf, vf = _diff_qkv(x, mod, diff_w_qkv[j].astype(BF16), cos_t, sin_t, **kw)
            ck = cache_diff_k[:, j].reshape(b_s, past, D_MODEL).astype(BF16)
            cv = cache_diff_v[:, j].reshape(b_s, past, D_MODEL).astype(BF16)
            g_sub = diff_g_sub[j].reshape(1, LANES)
            dargs = (diff_lambda[j], g_sub, lam_init)
            o_c = _attention(q, kb, vb, None, None, row0=0, n_b=b_c, t=t_c, tq=t_c, diff=dargs)
            o_s = _attention(q, kb, vb, ck, cv, row0=n_ctx, n_b=b_s, t=t_s, tq=256, rope=rope_t, diff=dargs)
            x = _oproj(o_c, o_s, diff_w_o[j].astype(BF16), x, mod, lng[0], lnb[0], **kw)
            diff_k.append(kf)
            diff_v.append(vf)
        if li % 2 == 0:
            x = _ffn(x, mod, ffn_gu_b, ffn_down_b, li // 2, lng[1], lnb[1], **kw)
        else:
            w_r = jnp.zeros((d, LANES), F32).at[:, :N_EXPERTS].set(moe_w_router[li // 2]).astype(BF16)
            x = _moe(x, mod, w_r, moe_gu_b, moe_down_b, li // 2, lng[1], lnb[1], split=li == DEPTH - 1, **kw)

    y_prompt = x[0].reshape(b_c, t_c, d)
    y_sample = x[1].reshape(b_s, t_s, d)
    new_attn_k = jnp.stack([k.reshape(b_c, t_c, N_KV_HEADS, HEAD_DIM) for k in attn_k], axis=1)
    new_attn_v = jnp.stack([v.reshape(b_c, t_c, N_KV_HEADS, HEAD_DIM) for v in attn_v], axis=1)
    new_state = jnp.stack(rnn_s, axis=1)
    new_diff_k = jnp.stack([k.reshape(b_c, t_c, N_DIFF_HEADS, 2, HEAD_DIM) for k in diff_k], axis=1)
    new_diff_v = jnp.stack([v.reshape(b_c, t_c, N_DIFF_HEADS, 2 * HEAD_DIM) for v in diff_v], axis=1)
    return (y_prompt, y_sample, new_attn_k, new_attn_v, new_state, new_diff_k, new_diff_v)
```

```python
import functools
import math

import jax
import jax.numpy as jnp
import numpy as np
from jax import lax
from jax.experimental import pallas as pl
from jax.experimental.pallas import tpu as pltpu

F32 = jnp.float32
BF16 = jnp.bfloat16

D_MODEL = 1024
DEPTH = 4
GRID_W = 64
HEAD_DIM = 64
N_Q_HEADS = 16
N_KV_HEADS = 4
ROPE_THETA = 10000.0
N_DIFF_HEADS = 8
D_RNN = 1280
N_RNN_BLOCKS = 16
RNN_BLOCK = 80
CONV_W = 4
CONV_LEFT = 2
RGLRU_C = 8.0
D_FF = 2816
N_EXPERTS = 8
D_FF_EXPERT = 1408
N_MIXERS = 3
DEEPNORM_ALPHA = (2.0 * DEPTH) ** 0.25
LN_EPS = 1e-6
RMS_EPS = 1e-6

LANES = 128
SUBLANES = 8
VMEM_LIMIT = 56 * 1024 * 1024
COND_ROWS = 8
SCAN_CHUNK = 256
MOE_TILE = 512
ROUTE_COLS = 8
GATE_N = 256
GATE_K = 512
GATE_K0 = (0, 128, 384, 640, 768)
Q_SCALE = HEAD_DIM ** -0.5 * math.log2(math.e)


def _cparams(*sem):
    return pltpu.CompilerParams(dimension_semantics=sem, vmem_limit_bytes=VMEM_LIMIT)


def _sigmoid(x):
    return 0.5 * jnp.tanh(0.5 * x) + 0.5


def _layer_norm_rows(z, g, b):
    mu = jnp.mean(z, axis=-1, keepdims=True)
    zc = z - mu
    var = jnp.mean(zc * zc, axis=-1, keepdims=True)
    return zc * lax.rsqrt(var + LN_EPS) * g + b


def _seg_of_block(i, tm, n_ctx, t_s):
    r0 = i * tm
    return jnp.where(r0 < n_ctx, 0, 1 + (r0 - n_ctx) // t_s)


def _mod_spec(col, tm, n_ctx, t_s):
    return pl.BlockSpec((None, 1, D_MODEL), lambda i, *_: (_seg_of_block(i, tm, n_ctx, t_s), 0, col))


def _ada_kernel(c_ref, w_ref, b_ref, o_ref):
    c = c_ref[...]
    a = (c * _sigmoid(c)).astype(BF16)
    o_ref[...] = jnp.dot(a, w_ref[...].astype(BF16), preferred_element_type=F32) + b_ref[...]


def _ada_table(cond, w_ada, b_ada):
    n_l, d, n = w_ada.shape
    tn = 1536
    return pl.pallas_call(
        _ada_kernel,
        out_shape=jax.ShapeDtypeStruct((n_l, COND_ROWS, n), F32),
        grid=(n_l, n // tn),
        in_specs=[pl.BlockSpec((COND_ROWS, d), lambda l, j: (0, 0)),
                  pl.BlockSpec((None, d, tn), lambda l, j: (l, 0, j)),
                  pl.BlockSpec((None, 1, tn), lambda l, j: (l, 0, j))],
        out_specs=pl.BlockSpec((None, COND_ROWS, tn), lambda l, j: (l, 0, j)),
        compiler_params=_cparams("parallel", "parallel"),
        name="ada_table",
    )(cond, w_ada, b_ada.reshape(n_l, 1, n))


def _head_masks(tm):
    lane = lax.broadcasted_iota(jnp.int32, (tm, LANES), 1)
    return lane < HEAD_DIM, (lane & (HEAD_DIM - 1)) < HEAD_DIM // 2


def _rope_tile(t, first_half, cos, sin):
    partner = jnp.where(first_half, pltpu.roll(t, LANES - HEAD_DIM // 2, 1),
                        pltpu.roll(t, HEAD_DIM // 2, 1))
    return t * cos + partner * sin


def _x_specs(x, tm, n_ctx):
    if not isinstance(x, tuple):
        return [pl.BlockSpec((tm, D_MODEL), lambda i, *_: (i, 0))], [x]
    ncb = n_ctx // tm
    return ([pl.BlockSpec((tm, D_MODEL), lambda i, *_: (jnp.minimum(i, ncb - 1), 0)),
             pl.BlockSpec((tm, D_MODEL), lambda i, *_: (jnp.maximum(i - ncb, 0), 0))], list(x))


def _x_value(x_refs, tm, n_ctx):
    if len(x_refs) == 1:
        return x_refs[0][...]
    return jnp.where(pl.program_id(0) * tm < n_ctx, x_refs[0][...], x_refs[1][...])


def _gqa_qkv_kernel(*refs, tm, n_ctx, n_x):
    x_refs = refs[:n_x]
    sc_ref, sh_ref, w_ref, gk_ref, cos_ref, sin_ref, q_ref, kb_ref, vb_ref, kf_ref, vf_ref = refs[n_x:]
    i = pl.program_id(0)
    h = (_x_value(x_refs, tm, n_ctx) * (1.0 + sc_ref[...]) + sh_ref[...]).astype(BF16)
    y = jnp.dot(h, w_ref[...], preferred_element_type=F32)
    use_rope = i * tm >= n_ctx
    cos = jnp.where(use_rope, cos_ref[...], 1.0)
    sin = jnp.where(use_rope, sin_ref[...], 0.0)
    low_head, first_half = _head_masks(tm)

    def norm_rope(t, g):
        t2 = t * t
        s_lo = jnp.sum(jnp.where(low_head, t2, 0.0), axis=-1, keepdims=True)
        s_hi = jnp.sum(jnp.where(low_head, 0.0, t2), axis=-1, keepdims=True)
        inv = jnp.where(low_head, lax.rsqrt(s_lo * (1.0 / HEAD_DIM) + RMS_EPS),
                        lax.rsqrt(s_hi * (1.0 / HEAD_DIM) + RMS_EPS))
        return _rope_tile(t * inv * g, first_half, cos, sin)

    nq = N_Q_HEADS * HEAD_DIM
    nkv = N_KV_HEADS * HEAD_DIM
    gk = gk_ref[...]
    q_ref[...] = y[:, :nq].astype(BF16)
    k = jnp.concatenate([norm_rope(y[:, nq + t * LANES: nq + (t + 1) * LANES], gk) for t in range(nkv // LANES)],
                        axis=1)
    v = y[:, nq + nkv:]
    kb_ref[...] = k.astype(BF16)
    vb_ref[...] = v.astype(BF16)

    @pl.when(i * tm < n_ctx)
    def _():
        kf_ref[...] = k
        vf_ref[...] = v


def _gqa_qkv(x, mod, w, gk, cos, sin, *, n_ctx, t_s):
    m = sum(a.shape[0] for a in x) if isinstance(x, tuple) else x.shape[0]
    tm = 512
    x_specs, x_args = _x_specs(x, tm, n_ctx)
    nq = N_Q_HEADS * HEAD_DIM
    nkv = N_KV_HEADS * HEAD_DIM
    n_rope_blocks = t_s // tm
    rope_spec = pl.BlockSpec(
        (tm, LANES), lambda i: (jnp.where(i * tm >= n_ctx, ((i * tm - n_ctx) % t_s) // tm, 0) % n_rope_blocks, 0))
    row = lambda n: pl.BlockSpec((tm, n), lambda i: (i, 0))
    ctx_row = lambda n: pl.BlockSpec((tm, n), lambda i: (jnp.minimum(i, n_ctx // tm - 1), 0))
    const = lambda a: pl.BlockSpec(a.shape, lambda i: (0,) * a.ndim)
    return pl.pallas_call(
        functools.partial(_gqa_qkv_kernel, tm=tm, n_ctx=n_ctx, n_x=len(x_args)),
        out_shape=(jax.ShapeDtypeStruct((m, nq), BF16), jax.ShapeDtypeStruct((m, nkv), BF16),
                   jax.ShapeDtypeStruct((m, nkv), BF16), jax.ShapeDtypeStruct((n_ctx, nkv), F32),
                   jax.ShapeDtypeStruct((n_ctx, nkv), F32)),
        grid=(m // tm,),
        in_specs=x_specs + [_mod_spec(1, tm, n_ctx, t_s), _mod_spec(0, tm, n_ctx, t_s),
                            const(w), const(gk), rope_spec, rope_spec],
        out_specs=(row(nq), row(nkv), row(nkv), ctx_row(nkv), ctx_row(nkv)),
        compiler_params=_cparams("arbitrary"),
        name="gqa_qkv",
    )(*x_args, mod, mod, w, gk, cos, sin)


def _diff_qkv_kernel(x_ref, sc_ref, sh_ref, w_ref, cos_ref, sin_ref,
                     q_ref, kb_ref, vb_ref, kf_ref, vf_ref, *, tm, n_ctx):
    i = pl.program_id(0)
    h = (x_ref[...] * (1.0 + sc_ref[...]) + sh_ref[...]).astype(BF16)
    y = jnp.dot(h, w_ref[...], preferred_element_type=F32)
    use_rope = i * tm >= n_ctx
    cos = jnp.where(use_rope, cos_ref[...], 1.0)
    sin = jnp.where(use_rope, sin_ref[...], 0.0)
    _, first_half = _head_masks(tm)
    q_ref[...] = y[:, :D_MODEL].astype(BF16)
    k = jnp.concatenate([_rope_tile(y[:, D_MODEL + t * LANES: D_MODEL + (t + 1) * LANES], first_half, cos, sin)
                         for t in range(D_MODEL // LANES)], axis=1)
    v = y[:, 2 * D_MODEL:]
    kb_ref[...] = k.astype(BF16)
    vb_ref[...] = v.astype(BF16)

    @pl.when(i * tm < n_ctx)
    def _():
        kf_ref[...] = k
        vf_ref[...] = v


def _diff_qkv(x, mod, w, cos, sin, *, n_ctx, t_s):
    m = x.shape[0]
    tm = 256
    n_rope_blocks = t_s // tm
    rope_spec = pl.BlockSpec(
        (tm, LANES), lambda i: (jnp.where(i * tm >= n_ctx, ((i * tm - n_ctx) % t_s) // tm, 0) % n_rope_blocks, 0))
    row = pl.BlockSpec((tm, D_MODEL), lambda i: (i, 0))
    ctx_row = pl.BlockSpec((tm, D_MODEL), lambda i: (jnp.minimum(i, n_ctx // tm - 1), 0))
    return pl.pallas_call(
        functools.partial(_diff_qkv_kernel, tm=tm, n_ctx=n_ctx),
        out_shape=(jax.ShapeDtypeStruct((m, D_MODEL), BF16), jax.ShapeDtypeStruct((m, D_MODEL), BF16),
                   jax.ShapeDtypeStruct((m, D_MODEL), BF16), jax.ShapeDtypeStruct((n_ctx, D_MODEL), F32),
                   jax.ShapeDtypeStruct((n_ctx, D_MODEL), F32)),
        grid=(m // tm,),
        in_specs=[row, _mod_spec(1, tm, n_ctx, t_s), _mod_spec(0, tm, n_ctx, t_s),
                  pl.BlockSpec(w.shape, lambda i: (0, 0)), rope_spec, rope_spec],
        out_specs=(row, row, row, ctx_row, ctx_row),
        compiler_params=_cparams("arbitrary"),
        name="diff_qkv",
    )(x, mod, mod, w, cos, sin)


ATT_GROUP = 4
ATT_KCH = 512
ATT_UNROLL = 3


def _attn_kernel(*refs, diff, rope, has_cache, tq, kch, n_own, n_cache, lam_init):
    refs = list(refs)
    if diff:
        lam_ref, g_ref = refs[:2]
        refs = refs[2:]
    else:
        gq_ref = refs[0]
        refs = refs[1:]
    if rope:
        cos_ref, sin_ref = refs[:2]
        refs = refs[2:]
    if has_cache:
        q_ref, k_ref, v_ref, ck_ref, cv_ref, o_ref, k_scr, vt_scr, s_scr = refs
    else:
        q_ref, k_ref, v_ref, o_ref, k_scr, vt_scr, s_scr = refs
    nch = n_own + n_cache
    n_kv = k_ref.shape[1] // LANES
    v_rows = LANES if diff else HEAD_DIM
    ones = jnp.where(lax.broadcasted_iota(jnp.int32, (SUBLANES, kch), 0) == 0, 1.0, 0.0).astype(BF16)

    @pl.when(pl.program_id(2) == 0)
    def _():
        for t in range(n_kv):
            lanes = slice(t * LANES, (t + 1) * LANES)
            for c in range(nch):
                if c < n_own:
                    rows = slice(c * kch, (c + 1) * kch)
                    kc, vc = k_ref[rows, lanes], v_ref[rows, lanes]
                else:
                    rows = slice((c - n_own) * kch, (c - n_own + 1) * kch)
                    kc, vc = ck_ref[rows, lanes], cv_ref[rows, lanes]
                k_scr[t, c] = kc
                vt = vc.T
                if diff:
                    vt_scr[t, c, 0] = jnp.concatenate([vt, ones], axis=0)
                else:
                    vt_scr[t, c, 0] = jnp.concatenate([vt[:HEAD_DIM], ones], axis=0)
                    vt_scr[t, c, 1] = jnp.concatenate([vt[HEAD_DIM:], ones], axis=0)

    low_row = lax.broadcasted_iota(jnp.int32, (LANES, tq), 0) < HEAD_DIM
    n_tiles = q_ref.shape[1] // LANES
    kv_of = (lambda j: j) if diff else (lambda j: j // (N_Q_HEADS // N_KV_HEADS))
    half, quarter = HEAD_DIM, HEAD_DIM // 2

    def weights(j):
        t = q_ref[:, j * LANES:(j + 1) * LANES].astype(F32).T
        if not diff:
            inv = [lax.rsqrt(jnp.mean(h * h, axis=0, keepdims=True) + RMS_EPS) for h in (t[:half], t[half:])]
            t = jnp.concatenate([t[:half] * inv[0], t[half:] * inv[1]], axis=0) * gq_ref[...]
        if rope:
            partner = jnp.concatenate([t[quarter:half], t[:quarter], t[half + quarter:], t[half:half + quarter]], axis=0)
            t = t * cos_ref[...] + partner * sin_ref[...]
        tb = (t * Q_SCALE).astype(BF16)
        zero = jnp.zeros_like(tb)
        return jnp.concatenate([jnp.where(low_row, tb, zero), jnp.where(low_row, zero, tb)], axis=1)

    def stage(j_a, j_b, mx_b):
        w = weights(j_a) if j_a is not None else None

        def body(c, carry):
            m8, accs = carry
            if j_a is not None:
                s = jnp.dot(k_scr[kv_of(j_a), c], w, preferred_element_type=F32)
                s_scr[j_a % 2, c] = s
                m8 = jnp.maximum(m8, jnp.max(s.reshape(kch // SUBLANES, SUBLANES, 2 * tq), axis=0))
            if j_b is not None:
                e = jnp.exp2(s_scr[j_b % 2, c] - mx_b).astype(BF16)
                if diff:
                    accs = (accs[0] + jnp.dot(vt_scr[kv_of(j_b), c, 0], e, preferred_element_type=F32),)
                else:
                    accs = (accs[0] + jnp.dot(vt_scr[kv_of(j_b), c, 0], e[:, :tq], preferred_element_type=F32),
                            accs[1] + jnp.dot(vt_scr[kv_of(j_b), c, 1], e[:, tq:], preferred_element_type=F32))
            return m8, accs

        acc_shape = (v_rows + SUBLANES, 2 * tq if diff else tq)
        init = (jnp.full((SUBLANES, 2 * tq), -jnp.inf, F32),
                tuple(jnp.zeros(acc_shape, F32) for _ in range(1 if diff else 2)))
        m8, accs = lax.fori_loop(0, nch, body, init, unroll=min(nch, ATT_UNROLL))
        return jnp.max(m8, axis=0, keepdims=True), accs

    if diff:
        lv = lam_ref[...]
        lam = (jnp.exp(jnp.sum(lv[0:1] * lv[1:2], axis=-1, keepdims=True))
               - jnp.exp(jnp.sum(lv[2:3] * lv[3:4], axis=-1, keepdims=True)) + lam_init)

    mx_prev = None
    for st in range(n_tiles + 1):
        j_a = st if st < n_tiles else None
        j_b = st - 1 if st >= 1 else None
        mx_new, accs = stage(j_a, j_b, mx_prev)
        if j_b is not None:
            if diff:
                a = accs[0]
                o_t = (a[:v_rows, :tq] / a[v_rows:v_rows + 1, :tq]
                       - lam * (a[:v_rows, tq:] / a[v_rows:v_rows + 1, tq:]))
                o = o_t.T
                inv = lax.rsqrt(jnp.mean(o * o, axis=-1, keepdims=True) + RMS_EPS)
                o = (o * inv * g_ref[...]) * (1.0 - lam_init)
            else:
                o_t = jnp.concatenate([a[:v_rows] / a[v_rows:v_rows + 1] for a in accs], axis=0)
                o = o_t.T
            o_ref[:, j_b * LANES:(j_b + 1) * LANES] = o.astype(BF16)
        mx_prev = mx_new


def _attention(q, k, v, cache_k, cache_v, *, row0, n_b, t, tq, gq=None, rope=None, diff=None):
    is_diff = diff is not None
    qw = ATT_GROUP * LANES if is_diff else D_MODEL
    n_groups = D_MODEL // qw
    kvw = qw if is_diff else N_KV_HEADS * HEAD_DIM
    nq_blocks = t // tq
    has_cache = cache_k is not None
    past = cache_k.shape[1] if has_cache else 0
    kch = min(ATT_KCH, t)
    assert t % kch == 0 and past % kch == 0
    n_own, n_cache = t // kch, past // kch
    nch = n_own + n_cache
    q_spec = pl.BlockSpec((tq, qw), lambda b, p, iq: (row0 // tq + b * nq_blocks + iq, p))
    kv_spec = pl.BlockSpec((t, kvw), lambda b, p, iq: (row0 // t + b, p),
                           pipeline_mode=pl.Buffered(2 if is_diff else 1))
    in_specs, args = [], []
    const = lambda a: pl.BlockSpec(a.shape, lambda b, p, iq: (0,) * a.ndim)
    if is_diff:
        lam_vec, g_sub, lam_init = diff
        in_specs += [const(lam_vec), const(g_sub)]
        args += [lam_vec, g_sub]
    else:
        lam_init = 0.0
        in_specs.append(const(gq))
        args.append(gq)
    if rope is not None:
        rope_spec = pl.BlockSpec((LANES, tq), lambda b, p, iq: (0, iq))
        in_specs += [rope_spec, rope_spec]
        args += list(rope)
    in_specs += [q_spec, kv_spec, kv_spec]
    args += [q, k, v]
    if has_cache:
        c_spec = pl.BlockSpec((None, past, kvw), lambda b, p, iq: (b, 0, p))
        in_specs += [c_spec, c_spec]
        args += [cache_k, cache_v]
    n_kv = kvw // LANES
    v_rows = (LANES if is_diff else HEAD_DIM) + SUBLANES
    return pl.pallas_call(
        functools.partial(_attn_kernel, diff=is_diff, rope=rope is not None, has_cache=has_cache, tq=tq, kch=kch,
                          n_own=n_own, n_cache=n_cache, lam_init=lam_init),
        out_shape=jax.ShapeDtypeStruct((n_b * t, D_MODEL), BF16),
        grid=(n_b, n_groups, nq_blocks),
        in_specs=in_specs,
        out_specs=pl.BlockSpec((tq, qw), lambda b, p, iq: (b * nq_blocks + iq, p)),
        scratch_shapes=[pltpu.VMEM((n_kv, nch, kch, LANES), BF16),
                        pltpu.VMEM((n_kv, nch, 1 if is_diff else 2, v_rows, kch), BF16),
                        pltpu.VMEM((2, nch, kch, 2 * tq), F32)],
        compiler_params=_cparams("arbitrary", "arbitrary", "arbitrary"),
        name=("diff_attn" if is_diff else "gqa_attn") + ("_cache" if has_cache else ""),
    )(*args)


def _oproj_kernel(*refs, tm, n_ctx, n_x):
    x_refs = refs[:n_x]
    ac_ref, as_ref, w_ref, gate_ref, lng_ref, lnb_ref, o_ref = refs[n_x:]
    a = jnp.where(pl.program_id(0) * tm < n_ctx, ac_ref[...], as_ref[...])
    out = jnp.dot(a, w_ref[...], preferred_element_type=F32)
    z = DEEPNORM_ALPHA * _x_value(x_refs, tm, n_ctx) + gate_ref[...] * out
    o_ref[...] = _layer_norm_rows(z, lng_ref[...], lnb_ref[...])


def _oproj(a_ctx, a_smp, w, x, mod, lng, lnb, *, n_ctx, t_s):
    m = a_ctx.shape[0] + a_smp.shape[0]
    kdim = a_ctx.shape[1]
    tm = 512
    ncb = n_ctx // tm
    const = lambda arr: pl.BlockSpec(arr.shape, lambda i: (0,) * arr.ndim)
    x_specs, x_args = _x_specs(x, tm, n_ctx)
    return pl.pallas_call(
        functools.partial(_oproj_kernel, tm=tm, n_ctx=n_ctx, n_x=len(x_args)),
        out_shape=jax.ShapeDtypeStruct((m, D_MODEL), F32),
        grid=(m // tm,),
        in_specs=x_specs + [pl.BlockSpec((tm, kdim), lambda i: (jnp.minimum(i, ncb - 1), 0)),
                            pl.BlockSpec((tm, kdim), lambda i: (jnp.maximum(i - ncb, 0), 0)),
                            const(w), _mod_spec(2, tm, n_ctx, t_s), const(lng), const(lnb)],
        out_specs=pl.BlockSpec((tm, D_MODEL), lambda i: (i, 0)),
        compiler_params=_cparams("parallel"),
        name="oproj_ln",
    )(*x_args, a_ctx, a_smp, w, mod, lng, lnb)


def _gelu_tanh(x):
    return 0.5 * x * (1.0 + jnp.tanh(math.sqrt(2.0 / math.pi) * (x + 0.044715 * (x * x * x))))


def _rnn_oproj_kernel(hf_ref, hb_ref, br_ref, w_ref, x_ref, gate_ref, lng_ref, lnb_ref, o_ref):
    y = (hf_ref[...] + hb_ref[...]) * _gelu_tanh(br_ref[...])
    out = jnp.dot(y.astype(BF16), w_ref[...], preferred_element_type=F32)
    z = DEEPNORM_ALPHA * x_ref[...] + gate_ref[...] * out
    o_ref[...] = _layer_norm_rows(z, lng_ref[...], lnb_ref[...])


def _rnn_oproj(hs_f, hs_b, bx, w, x, mod, lng, lnb, *, n_ctx, t_s):
    m = x.shape[0]
    tm = 512
    const = lambda arr: pl.BlockSpec(arr.shape, lambda i: (0,) * arr.ndim)
    row = pl.BlockSpec((tm, D_MODEL), lambda i: (i, 0))
    rnn_row = pl.BlockSpec((tm, D_RNN), lambda i: (i, 0))
    return pl.pallas_call(
        _rnn_oproj_kernel,
        out_shape=jax.ShapeDtypeStruct((m, D_MODEL), F32),
        grid=(m // tm,),
        in_specs=[rnn_row, rnn_row, rnn_row, const(w), row,
                  _mod_spec(2, tm, n_ctx, t_s), const(lng), const(lnb)],
        out_specs=row,
        compiler_params=_cparams("parallel"),
        name="rnn_oproj_ln",
    )(hs_f, hs_b, bx, w, x, mod, lng, lnb)


def _mod_mm_kernel(x_ref, sc_ref, sh_ref, w_ref, o_ref):
    h = (x_ref[...] * (1.0 + sc_ref[...]) + sh_ref[...]).astype(BF16)
    o_ref[...] = jnp.dot(h, w_ref[...], preferred_element_type=F32)


def _rnn_in(x, mod, w, *, n_ctx, t_s):
    m = x.shape[0]
    n = w.shape[1]
    tm = 512
    return pl.pallas_call(
        _mod_mm_kernel,
        out_shape=jax.ShapeDtypeStruct((m, n), F32),
        grid=(m // tm,),
        in_specs=[pl.BlockSpec((tm, D_MODEL), lambda i: (i, 0)),
                  _mod_spec(1, tm, n_ctx, t_s), _mod_spec(0, tm, n_ctx, t_s),
                  pl.BlockSpec(w.shape, lambda i: (0, 0))],
        out_specs=pl.BlockSpec((tm, n), lambda i: (i, 0)),
        compiler_params=_cparams("parallel"),
        name="rnn_in",
    )(x, mod, mod, w)


def _rnn_gate_kernel(xb_ref, prev_ref, next_ref, cw_ref, cb_ref, wg_ref, bg_ref, lam_ref,
                     af_ref, uf_ref, ab_ref, ub_ref, ext_ref, *, tm, n_ctx, t_c, t_s):
    i = pl.program_id(0)
    r0 = i * tm
    pos = jnp.where(r0 < n_ctx, r0 % t_c, (r0 - n_ctx) % t_s)
    t_seq = jnp.where(r0 < n_ctx, t_c, t_s)
    at_start = pos == 0
    at_end = pos + tm == t_seq
    ext_ref[0:SUBLANES, :] = jnp.where(at_start, 0.0, prev_ref[...])
    ext_ref[SUBLANES:SUBLANES + tm, :] = xb_ref[...]
    ext_ref[SUBLANES + tm:, :] = jnp.where(at_end, 0.0, next_ref[...])
    xc = 0.0
    for j in range(CONV_W):
        off = SUBLANES - CONV_LEFT + j
        xc = xc + ext_ref[off:off + tm, :] * cw_ref[j:j + 1, :]
    xc = xc + cb_ref[...]
    xcb = xc.astype(BF16)
    neg_lam = -lam_ref[...]
    sp = jnp.maximum(neg_lam, 0.0) + jnp.log1p(jnp.exp(-jnp.abs(neg_lam)))

    def gate(g, ct, cols):
        k0 = GATE_K0[ct]
        pre = jnp.dot(xcb[:, k0:k0 + GATE_K], wg_ref[g, ct], preferred_element_type=F32)
        return _sigmoid(pre + bg_ref[:, g * D_RNN + cols.start:g * D_RNN + cols.stop])

    for z, (a_ref, u_ref) in enumerate(((af_ref, uf_ref), (ab_ref, ub_ref))):
        for ct in range(D_RNN // GATE_N):
            cols = slice(ct * GATE_N, (ct + 1) * GATE_N)
            r = gate(2 * z, ct, cols)
            g_in = gate(2 * z + 1, ct, cols)
            log_a = -RGLRU_C * r * sp[z:z + 1, cols]
            a = jnp.exp(log_a)
            a_ref[:, cols] = a
            u_ref[:, cols] = jnp.sqrt(-jnp.tanh(log_a) * (a * a + 1.0)) * (g_in * xc[:, cols])


def _rnn_gates(bx, conv_w, conv_b, wg, bg, lam, *, n_ctx, t_c, t_s):
    m = bx.shape[0]
    tm = 256
    n8 = tm // SUBLANES
    last8 = m // SUBLANES - 1
    const = lambda arr: pl.BlockSpec(arr.shape, lambda i: (0,) * arr.ndim)
    row = pl.BlockSpec((tm, D_RNN), lambda i: (i, 0))
    out = jax.ShapeDtypeStruct((m, D_RNN), F32)
    return pl.pallas_call(
        functools.partial(_rnn_gate_kernel, tm=tm, n_ctx=n_ctx, t_c=t_c, t_s=t_s),
        out_shape=(out, out, out, out),
        grid=(m // tm,),
        in_specs=[pl.BlockSpec((tm, D_RNN), lambda i: (i, 1)),
                  pl.BlockSpec((SUBLANES, D_RNN), lambda i: (jnp.maximum(i * n8 - 1, 0), 1)),
                  pl.BlockSpec((SUBLANES, D_RNN), lambda i: (jnp.minimum((i + 1) * n8, last8), 1)),
                  const(conv_w), const(conv_b), const(wg), const(bg), const(lam)],
        out_specs=(row, row, row, row),
        scratch_shapes=[pltpu.VMEM((tm + 2 * SUBLANES, D_RNN), F32)],
        compiler_params=_cparams("parallel"),
        name="rnn_gates",
    )(bx, bx, bx, conv_w, conv_b, wg, bg, lam)


def _rnn_scan_kernel(fblk_ref, bblk_ref, seq_ref, first_ref,
                     af_ref, uf_ref, ab_ref, ub_ref, h0_ref, hf_ref, hb_ref, fin_ref, carry_ref, *, tc):
    s = pl.program_id(0)

    @pl.when(first_ref[s] == 1)
    def _():
        carry_ref[...] = h0_ref[...]

    def step(t, carry):
        h_f, h_b = carry
        tb = tc - 1 - t
        h_f = af_ref[pl.ds(t, 1), :] * h_f + uf_ref[pl.ds(t, 1), :]
        h_b = ab_ref[pl.ds(tb, 1), :] * h_b + ub_ref[pl.ds(tb, 1), :]
        hf_ref[pl.ds(t, 1), :] = h_f
        hb_ref[pl.ds(tb, 1), :] = h_b
        return h_f, h_b

    h_f, h_b = lax.fori_loop(0, tc, step, (carry_ref[0:1, :], carry_ref[1:2, :]), unroll=8)
    carry_ref[0:1, :] = h_f
    carry_ref[1:2, :] = h_b
    fin_ref[...] = carry_ref[...]


def _rnn_scan(a_f, u_f, a_b, u_b, h0, *, seq_lens):
    m = a_f.shape[0]
    tc = SCAN_CHUNK
    fblk, bblk, seq, first = [], [], [], []
    blk0 = 0
    for si, t in enumerate(seq_lens):
        nch = t // tc
        for c in range(nch):
            fblk.append(blk0 + c)
            bblk.append(blk0 + nch - 1 - c)
            seq.append(si)
            first.append(1 if c == 0 else 0)
        blk0 += nch
    n_steps = len(fblk)
    tables = [jnp.asarray(np.array(v, np.int32)) for v in (fblk, bblk, seq, first)]
    f_spec = pl.BlockSpec((tc, D_RNN), lambda s, fb, bb, sq, fr: (fb[s], 0))
    b_spec = pl.BlockSpec((tc, D_RNN), lambda s, fb, bb, sq, fr: (bb[s], 0))
    st_spec = pl.BlockSpec((None, 2, D_RNN), lambda s, fb, bb, sq, fr: (sq[s], 0, 0))
    out = jax.ShapeDtypeStruct((m, D_RNN), F32)
    return pl.pallas_call(
        functools.partial(_rnn_scan_kernel, tc=tc),
        out_shape=(out, out, jax.ShapeDtypeStruct(h0.shape, F32)),
        grid_spec=pltpu.PrefetchScalarGridSpec(
            num_scalar_prefetch=4, grid=(n_steps,),
            in_specs=[f_spec, f_spec, b_spec, b_spec, st_spec],
            out_specs=(f_spec, b_spec, st_spec),
            scratch_shapes=[pltpu.VMEM((2, D_RNN), F32)]),
        compiler_params=_cparams("arbitrary"),
        name="rnn_scan",
    )(*tables, a_f, u_f, a_b, u_b, h0)


def _ffn_kernel(x_ref, sc_ref, sh_ref, gate_ref, wg_ref, wu_ref, wd_ref, lng_ref, lnb_ref,
                o_ref, h_ref, acc_ref):
    j = pl.program_id(1)

    def half(h):
        g = jnp.dot(h, wg_ref[...], preferred_element_type=F32)
        u = jnp.dot(h, wu_ref[...], preferred_element_type=F32)
        act = ((g * _sigmoid(g)) * u).astype(BF16)
        return jnp.dot(act, wd_ref[...], preferred_element_type=F32)

    @pl.when(j == 0)
    def _():
        h = (x_ref[...] * (1.0 + sc_ref[...]) + sh_ref[...]).astype(BF16)
        h_ref[...] = h
        acc_ref[...] = half(h)

    @pl.when(j == 1)
    def _():
        z = DEEPNORM_ALPHA * x_ref[...] + gate_ref[...] * (acc_ref[...] + half(h_ref[...]))
        o_ref[...] = _layer_norm_rows(z, lng_ref[...], lnb_ref[...])


def _ffn(x, mod, w_gu, w_down, layer, lng, lnb, *, n_ctx, t_s):
    m = x.shape[0]
    tm = 512
    tf = D_FF // 2
    nf = D_FF // tf
    assert nf == 2
    const = lambda arr: pl.BlockSpec(arr.shape, lambda i, j: (0,) * arr.ndim)
    row = pl.BlockSpec((tm, D_MODEL), lambda i, j: (i, 0))
    return pl.pallas_call(
        _ffn_kernel,
        out_shape=jax.ShapeDtypeStruct((m, D_MODEL), F32),
        grid=(m // tm, nf),
        in_specs=[row, _mod_spec(4, tm, n_ctx, t_s), _mod_spec(3, tm, n_ctx, t_s), _mod_spec(5, tm, n_ctx, t_s),
                  pl.BlockSpec((None, D_MODEL, tf), lambda i, j: (layer, 0, j)),
                  pl.BlockSpec((None, D_MODEL, tf), lambda i, j: (layer, 0, nf + j)),
                  pl.BlockSpec((None, tf, D_MODEL), lambda i, j: (layer, j, 0)),
                  const(lng), const(lnb)],
        out_specs=row,
        scratch_shapes=[pltpu.VMEM((tm, D_MODEL), BF16), pltpu.VMEM((tm, D_MODEL), F32)],
        compiler_params=_cparams("parallel", "arbitrary"),
        name="ffn",
    )(x, mod, mod, mod, w_gu, w_gu, w_down, lng, lnb)


ROW_TILES = D_MODEL // LANES
DMA_GROUP = 8


def _to_token_rows(ref, base, value):
    n = value.shape[0]
    for j in range(ROW_TILES):
        ref[pl.ds(base + j, n, stride=ROW_TILES), :] = value[:, j * LANES:(j + 1) * LANES]


def _from_token_rows(ref, base, n):
    return jnp.concatenate([ref[pl.ds(base + j, n, stride=ROW_TILES), :] for j in range(ROW_TILES)], axis=1)


def _token_row(ref, r):
    start = r * ROW_TILES if isinstance(r, int) else pl.multiple_of(r * ROW_TILES, ROW_TILES)
    return ref.at[pl.ds(start, ROW_TILES)]


def _route_kernel(x_ref, sc_ref, sh_ref, wr_ref, route_ref, gates_ref, cnt_ref, carry_ref, *, tm):
    @pl.when(pl.program_id(0) == 0)
    def _():
        carry_ref[...] = jnp.zeros_like(carry_ref)

    lane = lax.broadcasted_iota(jnp.int32, (tm, LANES), 1)
    h = x_ref[...] * (1.0 + sc_ref[...]) + sh_ref[...]
    logits = jnp.dot(h.astype(BF16), wr_ref[...], preferred_element_type=F32)
    s1 = jnp.where(lane < N_EXPERTS, logits, -jnp.inf)
    m1 = jnp.max(s1, axis=-1, keepdims=True)
    i1 = jnp.min(jnp.where(s1 == m1, lane, LANES), axis=-1, keepdims=True)
    s2 = jnp.where(lane == i1, -jnp.inf, s1)
    m2 = jnp.max(s2, axis=-1, keepdims=True)
    i2 = jnp.min(jnp.where(s2 == m2, lane, LANES), axis=-1, keepdims=True)
    e2 = jnp.exp(m2 - m1)
    den = 1.0 + e2
    hit = jnp.where((lane == i1) | (lane == i2), 1.0, 0.0)
    earlier = lax.broadcasted_iota(jnp.int32, (tm, tm), 1) < lax.broadcasted_iota(jnp.int32, (tm, tm), 0)
    before = carry_ref[...] + jnp.dot(jnp.where(earlier, 1.0, 0.0).astype(BF16), hit.astype(BF16),
                                      preferred_element_type=F32)
    r1 = jnp.sum(jnp.where(lane == i1, before, 0.0), axis=-1, keepdims=True).astype(jnp.int32)
    r2 = jnp.sum(jnp.where(lane == i2, before, 0.0), axis=-1, keepdims=True).astype(jnp.int32)
    carry_ref[...] += jnp.sum(hit, axis=0, keepdims=True)
    route = jnp.where(lane == 0, i1, jnp.where(lane == 1, i2, jnp.where(lane == 2, r1, jnp.where(lane == 3, r2, 0))))
    route_ref[...] = route[:, :ROUTE_COLS]
    gates_ref[...] = jnp.where(lane == 0, 1.0 / den, jnp.where(lane == 1, e2 / den, 0.0))[:, :ROUTE_COLS]
    cnt_ref[...] = jnp.broadcast_to(carry_ref[...], cnt_ref.shape)


def _moe_route(x, mod, w_router, *, n_ctx, t_s):
    m = x.shape[0]
    tm = 512
    small = pl.BlockSpec((tm, ROUTE_COLS), lambda i: (i, 0))
    return pl.pallas_call(
        functools.partial(_route_kernel, tm=tm),
        out_shape=(jax.ShapeDtypeStruct((m, ROUTE_COLS), jnp.int32), jax.ShapeDtypeStruct((m, ROUTE_COLS), F32),
                   jax.ShapeDtypeStruct((SUBLANES, LANES), F32)),
        grid=(m // tm,),
        in_specs=[pl.BlockSpec((tm, D_MODEL), lambda i: (i, 0)), _mod_spec(4, tm, n_ctx, t_s),
                  _mod_spec(3, tm, n_ctx, t_s), pl.BlockSpec(w_router.shape, lambda i: (0, 0))],
        out_specs=(small, small, pl.BlockSpec((SUBLANES, LANES), lambda i: (0, 0))),
        scratch_shapes=[pltpu.VMEM((1, LANES), F32)],
        compiler_params=_cparams("arbitrary"),
        name="moe_route",
    )(x, mod, mod, w_router)


def _dispatch_kernel(slot_ref, x_ref, sc_ref, sh_ref, xs_hbm, buf, sem, *, tok):
    i = pl.program_id(0)
    par = i % 2
    cur = buf.at[par]
    _to_token_rows(cur, 0, x_ref[...] * (1.0 + sc_ref[...]) + sh_ref[...])

    def issue(g, c):
        t0 = g * DMA_GROUP
        slots = [slot_ref[2 * t0 + k] for k in range(2 * DMA_GROUP)]
        for k in range(DMA_GROUP):
            src = _token_row(cur, t0 + k)
            pltpu.make_async_copy(src, _token_row(xs_hbm, slots[2 * k]), sem.at[par]).start(priority=0)
            pltpu.make_async_copy(src, _token_row(xs_hbm, slots[2 * k + 1]), sem.at[par]).start(priority=1)
        return c
    lax.fori_loop(0, tok // DMA_GROUP, issue, 0)

    def drain(which):
        def body(t, c):
            pltpu.make_async_copy(_token_row(buf.at[which], 0), _token_row(xs_hbm, 0), sem.at[which]).wait()
            return c
        lax.fori_loop(0, 2 * tok, body, 0, unroll=8)

    @pl.when(i > 0)
    def _():
        drain(1 - par)

    @pl.when(i == pl.num_programs(0) - 1)
    def _():
        drain(par)


def _moe_dispatch(slots, x, mod, *, n_ctx, t_s):
    m = x.shape[0]
    tok = 512
    return pl.pallas_call(
        functools.partial(_dispatch_kernel, tok=tok),
        out_shape=jax.ShapeDtypeStruct((2 * m * ROW_TILES, LANES), F32),
        grid=(m // tok,),
        in_specs=[pl.BlockSpec((2 * tok,), lambda i: (i,), memory_space=pltpu.SMEM),
                  pl.BlockSpec((tok, D_MODEL), lambda i: (i, 0)),
                  _mod_spec(4, tok, n_ctx, t_s), _mod_spec(3, tok, n_ctx, t_s)],
        out_specs=pl.BlockSpec(memory_space=pl.ANY),
        scratch_shapes=[pltpu.VMEM((2, tok * ROW_TILES, LANES), F32), pltpu.SemaphoreType.DMA((2,))],
        compiler_params=_cparams("arbitrary"),
        name="moe_dispatch",
    )(slots, x, mod, mod)


def _experts_kernel(wt_ref, we_ref, lo_ref, hi_ref, nw_ref, xs_ref, wg_ref, wu_ref, wd_ref, o_ref, *, rows):
    del we_ref
    w = pl.program_id(0)

    def expert_rows():
        xs = _from_token_rows(xs_ref, 0, rows).astype(BF16)
        g = jnp.dot(xs, wg_ref[...], preferred_element_type=F32)
        u = jnp.dot(xs, wu_ref[...], preferred_element_type=F32)
        act = ((g * _sigmoid(g)) * u).astype(BF16)
        return jnp.dot(act, wd_ref[...], preferred_element_type=F32)

    active = w < nw_ref[0]
    first = jnp.logical_or(w == 0, wt_ref[jnp.maximum(w - 1, 0)] != wt_ref[w])

    @pl.when(jnp.logical_and(active, first))
    def _():
        _to_token_rows(o_ref, 0, expert_rows())

    @pl.when(jnp.logical_and(active, jnp.logical_not(first)))
    def _():
        row = lax.broadcasted_iota(jnp.int32, (rows, D_MODEL), 0)
        mine = (row >= lo_ref[w]) & (row < hi_ref[w])
        _to_token_rows(o_ref, 0, jnp.where(mine, expert_rows(), _from_token_rows(o_ref, 0, rows)))


def _moe_experts(items, xs, w_gu, w_down, layer):
    rows = MOE_TILE
    n_items = items[0].shape[0]
    tile = lambda w, wt, we, lo, hi, nw: (wt[w], 0)
    return pl.pallas_call(
        functools.partial(_experts_kernel, rows=rows),
        out_shape=jax.ShapeDtypeStruct(xs.shape, F32),
        grid_spec=pltpu.PrefetchScalarGridSpec(
            num_scalar_prefetch=5, grid=(n_items,),
            in_specs=[pl.BlockSpec((rows * ROW_TILES, LANES), tile),
                      pl.BlockSpec((None, None, D_MODEL, D_FF_EXPERT),
                                   lambda w, wt, we, lo, hi, nw: (layer, we[w], 0, 0)),
                      pl.BlockSpec((None, None, D_MODEL, D_FF_EXPERT),
                                   lambda w, wt, we, lo, hi, nw: (layer, we[w], 0, 1)),
                      pl.BlockSpec((None, None, D_FF_EXPERT, D_MODEL),
                                   lambda w, wt, we, lo, hi, nw: (layer, we[w], 0, 0))],
            out_specs=pl.BlockSpec((rows * ROW_TILES, LANES), tile)),
        compiler_params=_cparams("arbitrary"),
        name="moe_experts",
    )(*items, xs, w_gu, w_gu, w_down)


def _combine_kernel(idx_ref, nxt_ref, ys_hbm, gates_ref, x_ref, gate_ref, lng_ref, lnb_ref, o_ref, buf, sem, *, tm):
    i = pl.program_id(0)
    slot = i % 2

    def issue(ref, s):
        def body(g, c):
            t0 = g * DMA_GROUP
            slots = [ref[2 * t0 + k] for k in range(2 * DMA_GROUP)]
            for k in range(DMA_GROUP):
                pltpu.make_async_copy(_token_row(ys_hbm, slots[2 * k]), _token_row(buf.at[s], t0 + k),
                                      sem.at[s]).start(priority=0)
                pltpu.make_async_copy(_token_row(ys_hbm, slots[2 * k + 1]), _token_row(buf.at[s], tm + t0 + k),
                                      sem.at[s]).start(priority=1)
            return c
        lax.fori_loop(0, tm // DMA_GROUP, body, 0)

    @pl.when(i == 0)
    def _():
        issue(idx_ref, 0)

    @pl.when(i + 1 < pl.num_programs(0))
    def _():
        issue(nxt_ref, 1 - slot)

    def drain(r, c):
        pltpu.make_async_copy(_token_row(ys_hbm, 0), _token_row(buf.at[slot], 0), sem.at[slot]).wait()
        return c
    lax.fori_loop(0, 2 * tm, drain, 0, unroll=8)
    cur = buf.at[slot]
    y = (gates_ref[:, 0:1] * _from_token_rows(cur, 0, tm)
         + gates_ref[:, 1:2] * _from_token_rows(cur, tm * ROW_TILES, tm))
    z = DEEPNORM_ALPHA * x_ref[...] + gate_ref[...] * y
    o_ref[...] = _layer_norm_rows(z, lng_ref[...], lnb_ref[...])


def _moe_combine(slots, ys, gates, x, mod, lng, lnb, *, n_ctx, t_s, row0=0, n_rows=None):
    tm = 256
    n_blocks = (x.shape[0] if n_rows is None else n_rows) // tm
    b0 = row0 // tm
    idx = lambda f: pl.BlockSpec((2 * tm,), f, memory_space=pltpu.SMEM)
    const = lambda arr: pl.BlockSpec(arr.shape, lambda i: (0,) * arr.ndim)
    return pl.pallas_call(
        functools.partial(_combine_kernel, tm=tm),
        out_shape=jax.ShapeDtypeStruct((n_blocks * tm, D_MODEL), F32),
        grid=(n_blocks,),
        in_specs=[idx(lambda i: (b0 + i,)), idx(lambda i: (b0 + jnp.minimum(i + 1, n_blocks - 1),)),
                  pl.BlockSpec(memory_space=pl.ANY), pl.BlockSpec((tm, ROUTE_COLS), lambda i: (b0 + i, 0)),
                  pl.BlockSpec((tm, D_MODEL), lambda i: (b0 + i, 0)),
                  pl.BlockSpec((None, 1, D_MODEL), lambda i: (_seg_of_block(b0 + i, tm, n_ctx, t_s), 0, 5)),
                  const(lng), const(lnb)],
        out_specs=pl.BlockSpec((tm, D_MODEL), lambda i: (i, 0)),
        scratch_shapes=[pltpu.VMEM((2, 2 * tm * ROW_TILES, LANES), F32), pltpu.SemaphoreType.DMA((2,))],
        compiler_params=_cparams("arbitrary"),
        name="moe_combine",
    )(slots, slots, ys, gates, x, mod, lng, lnb)


def _moe_work_items(counts, n_rows):
    n_tiles = n_rows // MOE_TILE
    n_items = n_tiles + N_EXPERTS - 1
    g_end = jnp.cumsum(counts)
    g_start = g_end - counts
    row0 = jnp.arange(n_tiles, dtype=jnp.int32) * MOE_TILE
    first_e = jnp.sum(row0[:, None] >= g_end[None, :], axis=1)
    last_e = jnp.sum((row0 + MOE_TILE - 1)[:, None] >= g_end[None, :], axis=1)
    per_tile = last_e - first_e + 1
    item0 = jnp.cumsum(per_tile) - per_tile
    w = jnp.arange(n_items, dtype=jnp.int32)
    tile = jnp.clip(jnp.sum(w[:, None] >= item0[None, :], axis=1) - 1, 0, n_tiles - 1)
    expert = jnp.clip(first_e[tile] + w - item0[tile], 0, N_EXPERTS - 1)
    lo = jnp.clip(g_start[expert] - row0[tile], 0, MOE_TILE)
    hi = jnp.clip(g_end[expert] - row0[tile], 0, MOE_TILE)
    n_used = jnp.sum(per_tile).reshape(1)
    i32 = lambda a: a.astype(jnp.int32)
    return i32(tile), i32(expert), i32(lo), i32(hi), i32(n_used), g_start


def _moe(x, mod, w_router, w_gu, w_down, layer, lng, lnb, *, n_ctx, t_s, split=False):
    m = x.shape[0]
    assert (2 * m) % MOE_TILE == 0
    route, gates, cnt = _moe_route(x, mod, w_router, n_ctx=n_ctx, t_s=t_s)
    counts = cnt[0, :N_EXPERTS].astype(jnp.int32)
    *items, g_start = _moe_work_items(counts, 2 * m)
    slots = (g_start[route[:, 0:2]] + route[:, 2:4]).reshape(2 * m).astype(jnp.int32)
    xs = _moe_dispatch(slots, x, mod, n_ctx=n_ctx, t_s=t_s)
    ys = _moe_experts(items, xs, w_gu, w_down, layer)
    comb = functools.partial(_moe_combine, slots, ys, gates, x, mod, lng, lnb, n_ctx=n_ctx, t_s=t_s)
    if split:
        return comb(row0=0, n_rows=n_ctx), comb(row0=n_ctx, n_rows=m - n_ctx)
    return comb()


def _rope_tables(n_tokens):
    n_rows = n_tokens // GRID_W
    row = jnp.repeat(jnp.arange(n_rows, dtype=F32), GRID_W)
    col = jnp.tile(jnp.arange(GRID_W, dtype=F32), n_rows)
    n_axis = HEAD_DIM // 4
    inv = ROPE_THETA ** (-jnp.arange(n_axis, dtype=F32) / n_axis)
    ang = jnp.concatenate([row[:, None] * inv, col[:, None] * inv], axis=-1)
    cos, sin = jnp.cos(ang), jnp.sin(ang)
    reps = LANES // HEAD_DIM
    cos_t = jnp.tile(jnp.concatenate([cos, cos], axis=-1), (1, reps))
    sin_t = jnp.tile(jnp.concatenate([-sin, sin], axis=-1), (1, reps))
    return cos_t, sin_t


def _gqa_head_perm():
    cols = []
    per_pair = 2 * (N_Q_HEADS // N_KV_HEADS)
    for p in range(N_KV_HEADS // 2):
        for j in range(N_Q_HEADS // N_KV_HEADS):
            for hd in (per_pair * p + j, per_pair * p + per_pair // 2 + j):
                cols.append(np.arange(hd * HEAD_DIM, (hd + 1) * HEAD_DIM))
    return np.concatenate(cols)


def _block_diag_windows(w):
    eye = jnp.eye(N_RNN_BLOCKS, dtype=bool)[:, None, :, None]
    dense = jnp.where(eye, w[:, :, None, :], 0.0).reshape(D_RNN, D_RNN)
    for ct, k0 in enumerate(GATE_K0):
        lo, hi = ct * GATE_N, (ct + 1) * GATE_N
        assert k0 <= RNN_BLOCK * (lo // RNN_BLOCK) and RNN_BLOCK * -(-hi // RNN_BLOCK) <= k0 + GATE_K <= D_RNN
    return jnp.stack([dense[k0:k0 + GATE_K, ct * GATE_N:(ct + 1) * GATE_N] for ct, k0 in enumerate(GATE_K0)])


def kernel(x_prompt, x_sample, cache_attn_k, cache_attn_v, state_rglru, cache_diff_k, cache_diff_v, c, c_ctx, w_ada, b_ada, ln_g, ln_b, attn_w_qkv, attn_g_q, attn_g_k, attn_w_o, rnn_w_in, rnn_conv_w, rnn_conv_b, rnn_w_a, rnn_b_a, rnn_w_x, rnn_b_x, rnn_lambda, rnn_w_out, diff_w_qkv, diff_lambda, diff_g_sub, diff_w_o, ffn_w_gu, ffn_w_down, moe_w_router, moe_w_gu, moe_w_down):
    b_c, t_c, d = x_prompt.shape
    b_s, t_s, _ = x_sample.shape
    n_ctx = b_c * t_c
    n_s = b_s * t_s
    past = cache_attn_k.shape[2]
    assert d == D_MODEL and n_ctx % t_s == 0 and t_c == SCAN_CHUNK and t_s % SCAN_CHUNK == 0
    assert 1 + b_s <= COND_ROWS

    assert DEPTH % 2 == 0
    x = (x_prompt.reshape(n_ctx, d), x_sample.reshape(n_s, d))
    cond =jnp.zeros((COND_ROWS, d), F32).at[0].set(c_ctx).at[1:1 + b_s].set(c)
    mods = _ada_table(cond, w_ada, b_ada)[:, :1 + b_s].reshape(DEPTH, 1 + b_s, 1, 6 * d)

    cos_t, sin_t = _rope_tables(t_s)
    rope_t = (cos_t.T, sin_t.T)
    perm = _gqa_head_perm()
    nq = N_Q_HEADS * HEAD_DIM
    tile2 = lambda g: jnp.tile(g, LANES // g.shape[-1]).reshape(1, LANES)
    kw = dict(n_ctx=n_ctx, t_s=t_s)
    moe_gu_b = moe_w_gu.astype(BF16)
    moe_down_b = moe_w_down.astype(BF16)
    ffn_gu_b = ffn_w_gu.astype(BF16)
    ffn_down_b = ffn_w_down.astype(BF16)

    attn_k, attn_v, rnn_s, diff_k, diff_v = [], [], [], [], []
    for li in range(DEPTH):
        mod = mods[li]
        lng = ln_g[li].reshape(2, 1, d)
        lnb = ln_b[li].reshape(2, 1, d)
        j = li // N_MIXERS
        kind = li % N_MIXERS
        if kind == 0:
            w = attn_w_qkv[j]
            w = jnp.concatenate([w[:, :nq][:, perm], w[:, nq:]], axis=1).astype(BF16)
            q, kb, vb, kf, vf = _gqa_qkv(x, mod, w, tile2(attn_g_k[j]), cos_t, sin_t, **kw)
            gq = tile2(attn_g_q[j]).reshape(LANES, 1)
            ck = cache_attn_k[:, j].reshape(b_s, past, N_KV_HEADS * HEAD_DIM).astype(BF16)
            cv = cache_attn_v[:, j].reshape(b_s, past, N_KV_HEADS * HEAD_DIM).astype(BF16)
            o_c = _attention(q, kb, vb, None, None, row0=0, n_b=b_c, t=t_c, tq=t_c, gq=gq)
            o_s = _attention(q, kb, vb, ck, cv, row0=n_ctx, n_b=b_s, t=t_s, tq=512, gq=gq, rope=rope_t)
            x = _oproj(o_c, o_s, attn_w_o[j][perm, :].astype(BF16), x, mod, lng[0], lnb[0], **kw)
            attn_k.append(kf)
            attn_v.append(vf)
        elif kind == 1:
            bx = _rnn_in(x, mod, rnn_w_in[j].astype(BF16), **kw)
            wg = jnp.stack([_block_diag_windows(rnn_w_a[j, 0]), _block_diag_windows(rnn_w_x[j, 0]),
                            _block_diag_windows(rnn_w_a[j, 1]), _block_diag_windows(rnn_w_x[j, 1])]).astype(BF16)
            bg = jnp.concatenate([rnn_b_a[j, 0], rnn_b_x[j, 0], rnn_b_a[j, 1], rnn_b_x[j, 1]]).reshape(1, -1)
            a_f, u_f, a_b, u_b = _rnn_gates(bx, rnn_conv_w[j], rnn_conv_b[j].reshape(1, -1), wg, bg,
                                            rnn_lambda[j], n_ctx=n_ctx, t_c=t_c, t_s=t_s)
            h0 = jnp.concatenate([jnp.zeros((b_c, 2, D_RNN), F32), state_rglru[:, j]], axis=0)
            hs_f, hs_b, fin = _rnn_scan(a_f, u_f, a_b, u_b, h0, seq_lens=[t_c] * b_c + [t_s] * b_s)
            x = _rnn_oproj(hs_f, hs_b, bx, rnn_w_out[j].astype(BF16), x, mod, lng[0], lnb[0], **kw)
            rnn_s.append(fin[:b_c])
        else:
            lam_init = 0.8 - 0.6 * math.exp(-0.3 * li)
            q, kb, vb, kf, vf = _diff_qkv(x, mod, diff_w_qkv[j].astype(BF16), cos_t, sin_t, **kw)
            ck = cache_diff_k[:, j].reshape(b_s, past, D_MODEL).astype(BF16)
            cv = cache_diff_v[:, j].reshape(b_s, past, D_MODEL).astype(BF16)
            g_sub = diff_g_sub[j].reshape(1, LANES)
            dargs = (diff_lambda[j], g_sub, lam_init)
            o_c = _attention(q, kb, vb, None, None, row0=0, n_b=b_c, t=t_c, tq=t_c, diff=dargs)
            o_s = _attention(q, kb, vb, ck, cv, row0=n_ctx, n_b=b_s, t=t_s, tq=256, rope=rope_t, diff=dargs)
            x = _oproj(o_c, o_s, diff_w_o[j].astype(BF16), x, mod, lng[0], lnb[0], **kw)
            diff_k.append(kf)
            diff_v.append(vf)
        if li % 2 == 0:
            x = _ffn(x, mod, ffn_gu_b, ffn_down_b, li // 2, lng[1], lnb[1], **kw)
        else:
            w_r = jnp.zeros((d, LANES), F32).at[:, :N_EXPERTS].set(moe_w_router[li // 2]).astype(BF16)
            x = _moe(x, mod, w_r, moe_gu_b, moe_down_b, li // 2, lng[1], lnb[1], split=li == DEPTH - 1, **kw)

    y_prompt = x[0].reshape(b_c, t_c, d)
    y_sample = x[1].reshape(b_s, t_s, d)
    new_attn_k = jnp.stack([k.reshape(b_c, t_c, N_KV_HEADS, HEAD_DIM) for k in attn_k], axis=1)
    new_attn_v = jnp.stack([v.reshape(b_c, t_c, N_KV_HEADS, HEAD_DIM) for v in attn_v], axis=1)
    new_state = jnp.stack(rnn_s, axis=1)
    new_diff_k = jnp.stack([k.reshape(b_c, t_c, N_DIFF_HEADS, 2, HEAD_DIM) for k in diff_k], axis=1)
    new_diff_v = jnp.stack([v.reshape(b_c, t_c, N_DIFF_HEADS, 2 * HEAD_DIM) for v in diff_v], axis=1)
    return (y_prompt, y_sample, new_attn_k, new_attn_v, new_state, new_diff_k, new_diff_v)
```

```python
import functools
import math

import jax
import jax.numpy as jnp
import numpy as np
from jax import lax
from jax.experimental import pallas as pl
from jax.experimental.pallas import tpu as pltpu

F32 = jnp.float32
BF16 = jnp.bfloat16

D_MODEL = 1024
DEPTH = 4
GRID_W = 64
HEAD_DIM = 64
N_Q_HEADS = 16
N_KV_HEADS = 4
ROPE_THETA = 10000.0
N_DIFF_HEADS = 8
D_RNN = 1280
N_RNN_BLOCKS = 16
RNN_BLOCK = 80
CONV_W = 4
CONV_LEFT = 2
RGLRU_C = 8.0
D_FF = 2816
N_EXPERTS = 8
D_FF_EXPERT = 1408
N_MIXERS = 3
DEEPNORM_ALPHA = (2.0 * DEPTH) ** 0.25
LN_EPS = 1e-6
RMS_EPS = 1e-6

LANES = 128
SUBLANES = 8
VMEM_LIMIT = 56 * 1024 * 1024
COND_ROWS = 8
SCAN_CHUNK = 256
MOE_TILE = 512
ROUTE_COLS = 8
GATE_N = 256
GATE_K = 512
GATE_K0 = (0, 128, 384, 640, 768)
Q_SCALE = HEAD_DIM ** -0.5 * math.log2(math.e)


def _cparams(*sem):
    return pltpu.CompilerParams(dimension_semantics=sem, vmem_limit_bytes=VMEM_LIMIT)


def _sigmoid(x):
    return 0.5 * jnp.tanh(0.5 * x) + 0.5


def _layer_norm_rows(z, g, b):
    mu = jnp.mean(z, axis=-1, keepdims=True)
    zc = z - mu
    var = jnp.mean(zc * zc, axis=-1, keepdims=True)
    return zc * lax.rsqrt(var + LN_EPS) * g + b


def _seg_of_block(i, tm, n_ctx, t_s):
    r0 = i * tm
    return jnp.where(r0 < n_ctx, 0, 1 + (r0 - n_ctx) // t_s)


def _mod_spec(col, tm, n_ctx, t_s):
    return pl.BlockSpec((None, 1, D_MODEL), lambda i, *_: (_seg_of_block(i, tm, n_ctx, t_s), 0, col))


def _ada_kernel(c_ref, w_ref, b_ref, o_ref):
    c = c_ref[...]
    a = (c * _sigmoid(c)).astype(BF16)
    o_ref[...] = jnp.dot(a, w_ref[...].astype(BF16), preferred_element_type=F32) + b_ref[...]


def _ada_table(cond, w_ada, b_ada):
    n_l, d, n = w_ada.shape
    tn = 1536
    return pl.pallas_call(
        _ada_kernel,
        out_shape=jax.ShapeDtypeStruct((n_l, COND_ROWS, n), F32),
        grid=(n_l, n // tn),
        in_specs=[pl.BlockSpec((COND_ROWS, d), lambda l, j: (0, 0)),
                  pl.BlockSpec((None, d, tn), lambda l, j: (l, 0, j)),
                  pl.BlockSpec((None, 1, tn), lambda l, j: (l, 0, j))],
        out_specs=pl.BlockSpec((None, COND_ROWS, tn), lambda l, j: (l, 0, j)),
        compiler_params=_cparams("parallel", "parallel"),
        name="ada_table",
    )(cond, w_ada, b_ada.reshape(n_l, 1, n))


def _head_masks(tm):
    lane = lax.broadcasted_iota(jnp.int32, (tm, LANES), 1)
    return lane < HEAD_DIM, (lane & (HEAD_DIM - 1)) < HEAD_DIM // 2


def _rope_tile(t, first_half, cos, sin):
    partner = jnp.where(first_half, pltpu.roll(t, LANES - HEAD_DIM // 2, 1),
                        pltpu.roll(t, HEAD_DIM // 2, 1))
    return t * cos + partner * sin


def _x_specs(x, tm, n_ctx):
    if not isinstance(x, tuple):
        return [pl.BlockSpec((tm, D_MODEL), lambda i, *_: (i, 0))], [x]
    ncb = n_ctx // tm
    return ([pl.BlockSpec((tm, D_MODEL), lambda i, *_: (jnp.minimum(i, ncb - 1), 0)),
             pl.BlockSpec((tm, D_MODEL), lambda i, *_: (jnp.maximum(i - ncb, 0), 0))], list(x))


def _x_value(x_refs, tm, n_ctx):
    if len(x_refs) == 1:
        return x_refs[0][...]
    return jnp.where(pl.program_id(0) * tm < n_ctx, x_refs[0][...], x_refs[1][...])


def _gqa_qkv_kernel(*refs, tm, n_ctx, n_x):
    x_refs = refs[:n_x]
    sc_ref, sh_ref, w_ref, gk_ref, cos_ref, sin_ref, q_ref, kb_ref, vb_ref, kf_ref, vf_ref = refs[n_x:]
    i = pl.program_id(0)
    h = (_x_value(x_refs, tm, n_ctx) * (1.0 + sc_ref[...]) + sh_ref[...]).astype(BF16)
    y = jnp.dot(h, w_ref[...], preferred_element_type=F32)
    use_rope = i * tm >= n_ctx
    cos = jnp.where(use_rope, cos_ref[...], 1.0)
    sin = jnp.where(use_rope, sin_ref[...], 0.0)
    low_head, first_half = _head_masks(tm)

    def norm_rope(t, g):
        t2 = t * t
        s_lo = jnp.sum(jnp.where(low_head, t2, 0.0), axis=-1, keepdims=True)
        s_hi = jnp.sum(jnp.where(low_head, 0.0, t2), axis=-1, keepdims=True)
        inv = jnp.where(low_head, lax.rsqrt(s_lo * (1.0 / HEAD_DIM) + RMS_EPS),
                        lax.rsqrt(s_hi * (1.0 / HEAD_DIM) + RMS_EPS))
        return _rope_tile(t * inv * g, first_half, cos, sin)

    nq = N_Q_HEADS * HEAD_DIM
    nkv = N_KV_HEADS * HEAD_DIM
    gk = gk_ref[...]
    q_ref[...] = y[:, :nq].astype(BF16)
    k = jnp.concatenate([norm_rope(y[:, nq + t * LANES: nq + (t + 1) * LANES], gk) for t in range(nkv // LANES)],
                        axis=1)
    v = y[:, nq + nkv:]
    kb_ref[...] = k.astype(BF16)
    vb_ref[...] = v.astype(BF16)

    @pl.when(i * tm < n_ctx)
    def _():
        kf_ref[...] = k
        vf_ref[...] = v


def _gqa_qkv(x, mod, w, gk, cos, sin, *, n_ctx, t_s):
    m = sum(a.shape[0] for a in x) if isinstance(x, tuple) else x.shape[0]
    tm = 512
    x_specs, x_args = _x_specs(x, tm, n_ctx)
    nq = N_Q_HEADS * HEAD_DIM
    nkv = N_KV_HEADS * HEAD_DIM
    n_rope_blocks = t_s // tm
    rope_spec = pl.BlockSpec(
        (tm, LANES), lambda i: (jnp.where(i * tm >= n_ctx, ((i * tm - n_ctx) % t_s) // tm, 0) % n_rope_blocks, 0))
    row = lambda n: pl.BlockSpec((tm, n), lambda i: (i, 0))
    ctx_row = lambda n: pl.BlockSpec((tm, n), lambda i: (jnp.minimum(i, n_ctx // tm - 1), 0))
    const = lambda a: pl.BlockSpec(a.shape, lambda i: (0,) * a.ndim)
    return pl.pallas_call(
        functools.partial(_gqa_qkv_kernel, tm=tm, n_ctx=n_ctx, n_x=len(x_args)),
        out_shape=(jax.ShapeDtypeStruct((m, nq), BF16), jax.ShapeDtypeStruct((m, nkv), BF16),
                   jax.ShapeDtypeStruct((m, nkv), BF16), jax.ShapeDtypeStruct((n_ctx, nkv), F32),
                   jax.ShapeDtypeStruct((n_ctx, nkv), F32)),
        grid=(m // tm,),
        in_specs=x_specs + [_mod_spec(1, tm, n_ctx, t_s), _mod_spec(0, tm, n_ctx, t_s),
                            const(w), const(gk), rope_spec, rope_spec],
        out_specs=(row(nq), row(nkv), row(nkv), ctx_row(nkv), ctx_row(nkv)),
        compiler_params=_cparams("arbitrary"),
        name="gqa_qkv",
    )(*x_args, mod, mod, w, gk, cos, sin)


def _diff_qkv_kernel(x_ref, sc_ref, sh_ref, w_ref, cos_ref, sin_ref,
                     q_ref, kb_ref, vb_ref, kf_ref, vf_ref, *, tm, n_ctx):
    i = pl.program_id(0)
    h = (x_ref[...] * (1.0 + sc_ref[...]) + sh_ref[...]).astype(BF16)
    y = jnp.dot(h, w_ref[...], preferred_element_type=F32)
    use_rope = i * tm >= n_ctx
    cos = jnp.where(use_rope, cos_ref[...], 1.0)
    sin = jnp.where(use_rope, sin_ref[...], 0.0)
    _, first_half = _head_masks(tm)
    q_ref[...] = y[:, :D_MODEL].astype(BF16)
    k = jnp.concatenate([_rope_tile(y[:, D_MODEL + t * LANES: D_MODEL + (t + 1) * LANES], first_half, cos, sin)
                         for t in range(D_MODEL // LANES)], axis=1)
    v = y[:, 2 * D_MODEL:]
    kb_ref[...] = k.astype(BF16)
    vb_ref[...] = v.astype(BF16)

    @pl.when(i * tm < n_ctx)
    def _():
        kf_ref[...] = k
        vf_ref[...] = v


def _diff_qkv(x, mod, w, cos, sin, *, n_ctx, t_s):
    m = x.shape[0]
    tm = 256
    n_rope_blocks = t_s // tm
    rope_spec = pl.BlockSpec(
        (tm, LANES), lambda i: (jnp.where(i * tm >= n_ctx, ((i * tm - n_ctx) % t_s) // tm, 0) % n_rope_blocks, 0))
    row = pl.BlockSpec((tm, D_MODEL), lambda i: (i, 0))
    ctx_row = pl.BlockSpec((tm, D_MODEL), lambda i: (jnp.minimum(i, n_ctx // tm - 1), 0))
    return pl.pallas_call(
        functools.partial(_diff_qkv_kernel, tm=tm, n_ctx=n_ctx),
        out_shape=(jax.ShapeDtypeStruct((m, D_MODEL), BF16), jax.ShapeDtypeStruct((m, D_MODEL), BF16),
                   jax.ShapeDtypeStruct((m, D_MODEL), BF16), jax.ShapeDtypeStruct((n_ctx, D_MODEL), F32),
                   jax.ShapeDtypeStruct((n_ctx, D_MODEL), F32)),
        grid=(m // tm,),
        in_specs=[row, _mod_spec(1, tm, n_ctx, t_s), _mod_spec(0, tm, n_ctx, t_s),
                  pl.BlockSpec(w.shape, lambda i: (0, 0)), rope_spec, rope_spec],
        out_specs=(row, row, row, ctx_row, ctx_row),
        compiler_params=_cparams("arbitrary"),
        name="diff_qkv",
    )(x, mod, mod, w, cos, sin)


ATT_GROUP = 4
ATT_KCH = 512
ATT_UNROLL = 3


def _attn_kernel(*refs, diff, rope, has_cache, tq, kch, n_own, n_cache, lam_init):
    refs = list(refs)
    if diff:
        lam_ref, g_ref = refs[:2]
        refs = refs[2:]
    else:
        gq_ref = refs[0]
        refs = refs[1:]
    if rope:
        cos_ref, sin_ref = refs[:2]
        refs = refs[2:]
    if has_cache:
        q_ref, k_ref, v_ref, ck_ref, cv_ref, o_ref, k_scr, vt_scr, s_scr = refs
    else:
        q_ref, k_ref, v_ref, o_ref, k_scr, vt_scr, s_scr = refs
    nch = n_own + n_cache
    n_kv = k_ref.shape[1] // LANES
    v_rows = LANES if diff else HEAD_DIM
    ones = jnp.where(lax.broadcasted_iota(jnp.int32, (SUBLANES, kch), 0) == 0, 1.0, 0.0).astype(BF16)

    @pl.when(pl.program_id(2) == 0)
    def _():
        for t in range(n_kv):
            lanes = slice(t * LANES, (t + 1) * LANES)
            for c in range(nch):
                if c < n_own:
                    rows = slice(c * kch, (c + 1) * kch)
                    kc, vc = k_ref[rows, lanes], v_ref[rows, lanes]
                else:
                    rows = slice((c - n_own) * kch, (c - n_own + 1) * kch)
                    kc, vc = ck_ref[rows, lanes], cv_ref[rows, lanes]
                k_scr[t, c] = kc
                vt = vc.T
                if diff:
                    vt_scr[t, c, 0] = jnp.concatenate([vt, ones], axis=0)
                else:
                    vt_scr[t, c, 0] = jnp.concatenate([vt[:HEAD_DIM], ones], axis=0)
                    vt_scr[t, c, 1] = jnp.concatenate([vt[HEAD_DIM:], ones], axis=0)

    low_row = lax.broadcasted_iota(jnp.int32, (LANES, tq), 0) < HEAD_DIM
    n_tiles = q_ref.shape[1] // LANES
    kv_of = (lambda j: j) if diff else (lambda j: j // (N_Q_HEADS // N_KV_HEADS))
    half, quarter = HEAD_DIM, HEAD_DIM // 2

    def weights(j):
        t = q_ref[:, j * LANES:(j + 1) * LANES].astype(F32).T
        if not diff:
            inv = [lax.rsqrt(jnp.mean(h * h, axis=0, keepdims=True) + RMS_EPS) for h in (t[:half], t[half:])]
            t = jnp.concatenate([t[:half] * inv[0], t[half:] * inv[1]], axis=0) * gq_ref[...]
        if rope:
            partner = jnp.concatenate([t[quarter:half], t[:quarter], t[half + quarter:], t[half:half + quarter]], axis=0)
            t = t * cos_ref[...] + partner * sin_ref[...]
        tb = (t * Q_SCALE).astype(BF16)
        zero = jnp.zeros_like(tb)
        return jnp.concatenate([jnp.where(low_row, tb, zero), jnp.where(low_row, zero, tb)], axis=1)

    def stage(j_a, j_b, mx_b):
        w = weights(j_a) if j_a is not None else None

        def body(c, carry):
            m8, accs = carry
            if j_a is not None:
                s = jnp.dot(k_scr[kv_of(j_a), c], w, preferred_element_type=F32)
                s_scr[j_a % 2, c] = s
                m8 = jnp.maximum(m8, jnp.max(s.reshape(kch // SUBLANES, SUBLANES, 2 * tq), axis=0))
            if j_b is not None:
                e = jnp.exp2(s_scr[j_b % 2, c] - mx_b).astype(BF16)
                if diff:
                    accs = (accs[0] + jnp.dot(vt_scr[kv_of(j_b), c, 0], e, preferred_element_type=F32),)
                else:
                    accs = (accs[0] + jnp.dot(vt_scr[kv_of(j_b), c, 0], e[:, :tq], preferred_element_type=F32),
                            accs[1] + jnp.dot(vt_scr[kv_of(j_b), c, 1], e[:, tq:], preferred_element_type=F32))
            return m8, accs

        acc_shape = (v_rows + SUBLANES, 2 * tq if diff else tq)
        init = (jnp.full((SUBLANES, 2 * tq), -jnp.inf, F32),
                tuple(jnp.zeros(acc_shape, F32) for _ in range(1 if diff else 2)))
        m8, accs = lax.fori_loop(0, nch, body, init, unroll=min(nch, ATT_UNROLL))
        return jnp.max(m8, axis=0, keepdims=True), accs

    if diff:
        lv = lam_ref[...]
        lam = (jnp.exp(jnp.sum(lv[0:1] * lv[1:2], axis=-1, keepdims=True))
               - jnp.exp(jnp.sum(lv[2:3] * lv[3:4], axis=-1, keepdims=True)) + lam_init)

    mx_prev = None
    for st in range(n_tiles + 1):
        j_a = st if st < n_tiles else None
        j_b = st - 1 if st >= 1 else None
        mx_new, accs = stage(j_a, j_b, mx_prev)
        if j_b is not None:
            if diff:
                a = accs[0]
                o_t = (a[:v_rows, :tq] / a[v_rows:v_rows + 1, :tq]
                       - lam * (a[:v_rows, tq:] / a[v_rows:v_rows + 1, tq:]))
                o = o_t.T
                inv = lax.rsqrt(jnp.mean(o * o, axis=-1, keepdims=True) + RMS_EPS)
                o = (o * inv * g_ref[...]) * (1.0 - lam_init)
            else:
                o_t = jnp.concatenate([a[:v_rows] / a[v_rows:v_rows + 1] for a in accs], axis=0)
                o = o_t.T
            o_ref[:, j_b * LANES:(j_b + 1) * LANES] = o.astype(BF16)
        mx_prev = mx_new


def _attention(q, k, v, cache_k, cache_v, *, row0, n_b, t, tq, gq=None, rope=None, diff=None):
    is_diff = diff is not None
    qw = ATT_GROUP * LANES if is_diff else D_MODEL
    n_groups = D_MODEL // qw
    kvw = qw if is_diff else N_KV_HEADS * HEAD_DIM
    nq_blocks = t // tq
    has_cache = cache_k is not None
    past = cache_k.shape[1] if has_cache else 0
    kch = min(ATT_KCH, t)
    assert t % kch == 0 and past % kch == 0
    n_own, n_cache = t // kch, past // kch
    nch = n_own + n_cache
    q_spec = pl.BlockSpec((tq, qw), lambda b, p, iq: (row0 // tq + b * nq_blocks + iq, p))
    kv_spec = pl.BlockSpec((t, kvw), lambda b, p, iq: (row0 // t + b, p),
                           pipeline_mode=pl.Buffered(2 if is_diff else 1))
    in_specs, args = [], []
    const = lambda a: pl.BlockSpec(a.shape, lambda b, p, iq: (0,) * a.ndim)
    if is_diff:
        lam_vec, g_sub, lam_init = diff
        in_specs += [const(lam_vec), const(g_sub)]
        args += [lam_vec, g_sub]
    else:
        lam_init = 0.0
        in_specs.append(const(gq))
        args.append(gq)
    if rope is not None:
        rope_spec = pl.BlockSpec((LANES, tq), lambda b, p, iq: (0, iq))
        in_specs += [rope_spec, rope_spec]
        args += list(rope)
    in_specs += [q_spec, kv_spec, kv_spec]
    args += [q, k, v]
    if has_cache:
        c_spec = pl.BlockSpec((None, past, kvw), lambda b, p, iq: (b, 0, p))
        in_specs += [c_spec, c_spec]
        args += [cache_k, cache_v]
    n_kv = kvw // LANES
    v_rows = (LANES if is_diff else HEAD_DIM) + SUBLANES
    return pl.pallas_call(
        functools.partial(_attn_kernel, diff=is_diff, rope=rope is not None, has_cache=has_cache, tq=tq, kch=kch,
                          n_own=n_own, n_cache=n_cache, lam_init=lam_init),
        out_shape=jax.ShapeDtypeStruct((n_b * t, D_MODEL), BF16),
        grid=(n_b, n_groups, nq_blocks),
        in_specs=in_specs,
        out_specs=pl.BlockSpec((tq, qw), lambda b, p, iq: (b * nq_blocks + iq, p)),
        scratch_shapes=[pltpu.VMEM((n_kv, nch, kch, LANES), BF16),
                        pltpu.VMEM((n_kv, nch, 1 if is_diff else 2, v_rows, kch), BF16),
                        pltpu.VMEM((2, nch, kch, 2 * tq), F32)],
        compiler_params=_cparams("arbitrary", "arbitrary", "arbitrary"),
        name=("diff_attn" if is_diff else "gqa_attn") + ("_cache" if has_cache else ""),
    )(*args)


def _oproj_kernel(*refs, tm, n_ctx, n_x):
    x_refs = refs[:n_x]
    ac_ref, as_ref, w_ref, gate_ref, lng_ref, lnb_ref, o_ref = refs[n_x:]
    a = jnp.where(pl.program_id(0) * tm < n_ctx, ac_ref[...], as_ref[...])
    out = jnp.dot(a, w_ref[...], preferred_element_type=F32)
    z = DEEPNORM_ALPHA * _x_value(x_refs, tm, n_ctx) + gate_ref[...] * out
    o_ref[...] = _layer_norm_rows(z, lng_ref[...], lnb_ref[...])


def _oproj(a_ctx, a_smp, w, x, mod, lng, lnb, *, n_ctx, t_s):
    m = a_ctx.shape[0] + a_smp.shape[0]
    kdim = a_ctx.shape[1]
    tm = 512
    ncb = n_ctx // tm
    const = lambda arr: pl.BlockSpec(arr.shape, lambda i: (0,) * arr.ndim)
    x_specs, x_args = _x_specs(x, tm, n_ctx)
    return pl.pallas_call(
        functools.partial(_oproj_kernel, tm=tm, n_ctx=n_ctx, n_x=len(x_args)),
        out_shape=jax.ShapeDtypeStruct((m, D_MODEL), F32),
        grid=(m // tm,),
        in_specs=x_specs + [pl.BlockSpec((tm, kdim), lambda i: (jnp.minimum(i, ncb - 1), 0)),
                            pl.BlockSpec((tm, kdim), lambda i: (jnp.maximum(i - ncb, 0), 0)),
                            const(w), _mod_spec(2, tm, n_ctx, t_s), const(lng), const(lnb)],
        out_specs=pl.BlockSpec((tm, D_MODEL), lambda i: (i, 0)),
        compiler_params=_cparams("parallel"),
        name="oproj_ln",
    )(*x_args, a_ctx, a_smp, w, mod, lng, lnb)


def _gelu_tanh(x):
    return 0.5 * x * (1.0 + jnp.tanh(math.sqrt(2.0 / math.pi) * (x + 0.044715 * (x * x * x))))


def _rnn_oproj_kernel(hf_ref, hb_ref, br_ref, w_ref, x_ref, gate_ref, lng_ref, lnb_ref, o_ref):
    y = (hf_ref[...] + hb_ref[...]) * _gelu_tanh(br_ref[...])
    out = jnp.dot(y.astype(BF16), w_ref[...], preferred_element_type=F32)
    z = DEEPNORM_ALPHA * x_ref[...] + gate_ref[...] * out
    o_ref[...] = _layer_norm_rows(z, lng_ref[...], lnb_ref[...])


def _rnn_oproj(hs_f, hs_b, bx, w, x, mod, lng, lnb, *, n_ctx, t_s):
    m = x.shape[0]
    tm = 512
    const = lambda arr: pl.BlockSpec(arr.shape, lambda i: (0,) * arr.ndim)
    row = pl.BlockSpec((tm, D_MODEL), lambda i: (i, 0))
    rnn_row = pl.BlockSpec((tm, D_RNN), lambda i: (i, 0))
    return pl.pallas_call(
        _rnn_oproj_kernel,
        out_shape=jax.ShapeDtypeStruct((m, D_MODEL), F32),
        grid=(m // tm,),
        in_specs=[rnn_row, rnn_row, rnn_row, const(w), row,
                  _mod_spec(2, tm, n_ctx, t_s), const(lng), const(lnb)],
        out_specs=row,
        compiler_params=_cparams("parallel"),
        name="rnn_oproj_ln",
    )(hs_f, hs_b, bx, w, x, mod, lng, lnb)


def _mod_mm_kernel(x_ref, sc_ref, sh_ref, w_ref, o_ref):
    h = (x_ref[...] * (1.0 + sc_ref[...]) + sh_ref[...]).astype(BF16)
    o_ref[...] = jnp.dot(h, w_ref[...], preferred_element_type=F32)


def _rnn_in(x, mod, w, *, n_ctx, t_s):
    m = x.shape[0]
    n = w.shape[1]
    tm = 512
    return pl.pallas_call(
        _mod_mm_kernel,
        out_shape=jax.ShapeDtypeStruct((m, n), F32),
        grid=(m // tm,),
        in_specs=[pl.BlockSpec((tm, D_MODEL), lambda i: (i, 0)),
                  _mod_spec(1, tm, n_ctx, t_s), _mod_spec(0, tm, n_ctx, t_s),
                  pl.BlockSpec(w.shape, lambda i: (0, 0))],
        out_specs=pl.BlockSpec((tm, n), lambda i: (i, 0)),
        compiler_params=_cparams("parallel"),
        name="rnn_in",
    )(x, mod, mod, w)


def _rnn_gate_kernel(xb_ref, prev_ref, next_ref, cw_ref, cb_ref, wg_ref, bg_ref, lam_ref,
                     af_ref, uf_ref, ab_ref, ub_ref, ext_ref, *, tm, n_ctx, t_c, t_s):
    i = pl.program_id(0)
    r0 = i * tm
    pos = jnp.where(r0 < n_ctx, r0 % t_c, (r0 - n_ctx) % t_s)
    t_seq = jnp.where(r0 < n_ctx, t_c, t_s)
    at_start = pos == 0
    at_end = pos + tm == t_seq
    ext_ref[0:SUBLANES, :] = jnp.where(at_start, 0.0, prev_ref[...])
    ext_ref[SUBLANES:SUBLANES + tm, :] = xb_ref[...]
    ext_ref[SUBLANES + tm:, :] = jnp.where(at_end, 0.0, next_ref[...])
    xc = 0.0
    for j in range(CONV_W):
        off = SUBLANES - CONV_LEFT + j
        xc = xc + ext_ref[off:off + tm, :] * cw_ref[j:j + 1, :]
    xc = xc + cb_ref[...]
    xcb = xc.astype(BF16)
    neg_lam = -lam_ref[...]
    sp = jnp.maximum(neg_lam, 0.0) + jnp.log1p(jnp.exp(-jnp.abs(neg_lam)))

    def gate(g, ct, cols):
        k0 = GATE_K0[ct]
        pre = jnp.dot(xcb[:, k0:k0 + GATE_K], wg_ref[g, ct], preferred_element_type=F32)
        return _sigmoid(pre + bg_ref[:, g * D_RNN + cols.start:g * D_RNN + cols.stop])

    for z, (a_ref, u_ref) in enumerate(((af_ref, uf_ref), (ab_ref, ub_ref))):
        for ct in range(D_RNN // GATE_N):
            cols = slice(ct * GATE_N, (ct + 1) * GATE_N)
            r = gate(2 * z, ct, cols)
            g_in = gate(2 * z + 1, ct, cols)
            log_a = -RGLRU_C * r * sp[z:z + 1, cols]
            a = jnp.exp(log_a)
            a_ref[:, cols] = a
            u_ref[:, cols] = jnp.sqrt(-jnp.tanh(log_a) * (a * a + 1.0)) * (g_in * xc[:, cols])


def _rnn_gates(bx, conv_w, conv_b, wg, bg, lam, *, n_ctx, t_c, t_s):
    m = bx.shape[0]
    tm = 256
    n8 = tm // SUBLANES
    last8 = m // SUBLANES - 1
    const = lambda arr: pl.BlockSpec(arr.shape, lambda i: (0,) * arr.ndim)
    row = pl.BlockSpec((tm, D_RNN), lambda i: (i, 0))
    out = jax.ShapeDtypeStruct((m, D_RNN), F32)
    return pl.pallas_call(
        functools.partial(_rnn_gate_kernel, tm=tm, n_ctx=n_ctx, t_c=t_c, t_s=t_s),
        out_shape=(out, out, out, out),
        grid=(m // tm,),
        in_specs=[pl.BlockSpec((tm, D_RNN), lambda i: (i, 1)),
                  pl.BlockSpec((SUBLANES, D_RNN), lambda i: (jnp.maximum(i * n8 - 1, 0), 1)),
                  pl.BlockSpec((SUBLANES, D_RNN), lambda i: (jnp.minimum((i + 1) * n8, last8), 1)),
                  const(conv_w), const(conv_b), const(wg), const(bg), const(lam)],
        out_specs=(row, row, row, row),
        scratch_shapes=[pltpu.VMEM((tm + 2 * SUBLANES, D_RNN), F32)],
        compiler_params=_cparams("parallel"),
        name="rnn_gates",
    )(bx, bx, bx, conv_w, conv_b, wg, bg, lam)


def _rnn_scan_kernel(fblk_ref, bblk_ref, seq_ref, first_ref,
                     af_ref, uf_ref, ab_ref, ub_ref, h0_ref, hf_ref, hb_ref, fin_ref, carry_ref, *, tc):
    s = pl.program_id(0)

    @pl.when(first_ref[s] == 1)
    def _():
        carry_ref[...] = h0_ref[...]

    def step(t, carry):
        h_f, h_b = carry
        tb = tc - 1 - t
        h_f = af_ref[pl.ds(t, 1), :] * h_f + uf_ref[pl.ds(t, 1), :]
        h_b = ab_ref[pl.ds(tb, 1), :] * h_b + ub_ref[pl.ds(tb, 1), :]
        hf_ref[pl.ds(t, 1), :] = h_f
        hb_ref[pl.ds(tb, 1), :] = h_b
        return h_f, h_b

    h_f, h_b = lax.fori_loop(0, tc, step, (carry_ref[0:1, :], carry_ref[1:2, :]), unroll=8)
    carry_ref[0:1, :] = h_f
    carry_ref[1:2, :] = h_b
    fin_ref[...] = carry_ref[...]


def _rnn_scan(a_f, u_f, a_b, u_b, h0, *, seq_lens):
    m = a_f.shape[0]
    tc = SCAN_CHUNK
    fblk, bblk, seq, first = [], [], [], []
    blk0 = 0
    for si, t in enumerate(seq_lens):
        nch = t // tc
        for c in range(nch):
            fblk.append(blk0 + c)
            bblk.append(blk0 + nch - 1 - c)
            seq.append(si)
            first.append(1 if c == 0 else 0)
        blk0 += nch
    n_steps = len(fblk)
    tables = [jnp.asarray(np.array(v, np.int32)) for v in (fblk, bblk, seq, first)]
    f_spec = pl.BlockSpec((tc, D_RNN), lambda s, fb, bb, sq, fr: (fb[s], 0))
    b_spec = pl.BlockSpec((tc, D_RNN), lambda s, fb, bb, sq, fr: (bb[s], 0))
    st_spec = pl.BlockSpec((None, 2, D_RNN), lambda s, fb, bb, sq, fr: (sq[s], 0, 0))
    out = jax.ShapeDtypeStruct((m, D_RNN), F32)
    return pl.pallas_call(
        functools.partial(_rnn_scan_kernel, tc=tc),
        out_shape=(out, out, jax.ShapeDtypeStruct(h0.shape, F32)),
        grid_spec=pltpu.PrefetchScalarGridSpec(
            num_scalar_prefetch=4, grid=(n_steps,),
            in_specs=[f_spec, f_spec, b_spec, b_spec, st_spec],
            out_specs=(f_spec, b_spec, st_spec),
            scratch_shapes=[pltpu.VMEM((2, D_RNN), F32)]),
        compiler_params=_cparams("arbitrary"),
        name="rnn_scan",
    )(*tables, a_f, u_f, a_b, u_b, h0)


def _ffn_kernel(x_ref, sc_ref, sh_ref, gate_ref, wg_ref, wu_ref, wd_ref, lng_ref, lnb_ref,
                o_ref, h_ref, acc_ref):
    j = pl.program_id(1)

    def half(h):
        g = jnp.dot(h, wg_ref[...], preferred_element_type=F32)
        u = jnp.dot(h, wu_ref[...], preferred_element_type=F32)
        act = ((g * _sigmoid(g)) * u).astype(BF16)
        return jnp.dot(act, wd_ref[...], preferred_element_type=F32)

    @pl.when(j == 0)
    def _():
        h = (x_ref[...] * (1.0 + sc_ref[...]) + sh_ref[...]).astype(BF16)
        h_ref[...] = h
        acc_ref[...] = half(h)

    @pl.when(j == 1)
    def _():
        z = DEEPNORM_ALPHA * x_ref[...] + gate_ref[...] * (acc_ref[...] + half(h_ref[...]))
        o_ref[...] = _layer_norm_rows(z, lng_ref[...], lnb_ref[...])


def _ffn(x, mod, w_gu, w_down, layer, lng, lnb, *, n_ctx, t_s):
    m = x.shape[0]
    tm = 512
    tf = D_FF // 2
    nf = D_FF // tf
    assert nf == 2
    const = lambda arr: pl.BlockSpec(arr.shape, lambda i, j: (0,) * arr.ndim)
    row = pl.BlockSpec((tm, D_MODEL), lambda i, j: (i, 0))
    return pl.pallas_call(
        _ffn_kernel,
        out_shape=jax.ShapeDtypeStruct((m, D_MODEL), F32),
        grid=(m // tm, nf),
        in_specs=[row, _mod_spec(4, tm, n_ctx, t_s), _mod_spec(3, tm, n_ctx, t_s), _mod_spec(5, tm, n_ctx, t_s),
                  pl.BlockSpec((None, D_MODEL, tf), lambda i, j: (layer, 0, j)),
                  pl.BlockSpec((None, D_MODEL, tf), lambda i, j: (layer, 0, nf + j)),
                  pl.BlockSpec((None, tf, D_MODEL), lambda i, j: (layer, j, 0)),
                  const(lng), const(lnb)],
        out_specs=row,
        scratch_shapes=[pltpu.VMEM((tm, D_MODEL), BF16), pltpu.VMEM((tm, D_MODEL), F32)],
        compiler_params=_cparams("parallel", "arbitrary"),
        name="ffn",
    )(x, mod, mod, mod, w_gu, w_gu, w_down, lng, lnb)


ROW_TILES = D_MODEL // LANES
DMA_GROUP = 8


def _to_token_rows(ref, base, value):
    n = value.shape[0]
    for j in range(ROW_TILES):
        ref[pl.ds(base + j, n, stride=ROW_TILES), :] = value[:, j * LANES:(j + 1) * LANES]


def _from_token_rows(ref, base, n):
    return jnp.concatenate([ref[pl.ds(base + j, n, stride=ROW_TILES), :] for j in range(ROW_TILES)], axis=1)


def _token_row(ref, r):
    start = r * ROW_TILES if isinstance(r, int) else pl.multiple_of(r * ROW_TILES, ROW_TILES)
    return ref.at[pl.ds(start, ROW_TILES)]


def _route_kernel(x_ref, sc_ref, sh_ref, wr_ref, route_ref, gates_ref, cnt_ref, carry_ref, *, tm):
    @pl.when(pl.program_id(0) == 0)
    def _():
        carry_ref[...] = jnp.zeros_like(carry_ref)

    lane = lax.broadcasted_iota(jnp.int32, (tm, LANES), 1)
    h = x_ref[...] * (1.0 + sc_ref[...]) + sh_ref[...]
    logits = jnp.dot(h.astype(BF16), wr_ref[...], preferred_element_type=F32)
    s1 = jnp.where(lane < N_EXPERTS, logits, -jnp.inf)
    m1 = jnp.max(s1, axis=-1, keepdims=True)
    i1 = jnp.min(jnp.where(s1 == m1, lane, LANES), axis=-1, keepdims=True)
    s2 = jnp.where(lane == i1, -jnp.inf, s1)
    m2 = jnp.max(s2, axis=-1, keepdims=True)
    i2 = jnp.min(jnp.where(s2 == m2, lane, LANES), axis=-1, keepdims=True)
    e2 = jnp.exp(m2 - m1)
    den = 1.0 + e2
    hit = jnp.where((lane == i1) | (lane == i2), 1.0, 0.0)
    earlier = lax.broadcasted_iota(jnp.int32, (tm, tm), 1) < lax.broadcasted_iota(jnp.int32, (tm, tm), 0)
    before = carry_ref[...] + jnp.dot(jnp.where(earlier, 1.0, 0.0).astype(BF16), hit.astype(BF16),
                                      preferred_element_type=F32)
    r1 = jnp.sum(jnp.where(lane == i1, before, 0.0), axis=-1, keepdims=True).astype(jnp.int32)
    r2 = jnp.sum(jnp.where(lane == i2, before, 0.0), axis=-1, keepdims=True).astype(jnp.int32)
    carry_ref[...] += jnp.sum(hit, axis=0, keepdims=True)
    route = jnp.where(lane == 0, i1, jnp.where(lane == 1, i2, jnp.where(lane == 2, r1, jnp.where(lane == 3, r2, 0))))
    route_ref[...] = route[:, :ROUTE_COLS]
    gates_ref[...] = jnp.where(lane == 0, 1.0 / den, jnp.where(lane == 1, e2 / den, 0.0))[:, :ROUTE_COLS]
    cnt_ref[...] = jnp.broadcast_to(carry_ref[...], cnt_ref.shape)


def _moe_route(x, mod, w_router, *, n_ctx, t_s):
    m = x.shape[0]
    tm = 512
    small = pl.BlockSpec((tm, ROUTE_COLS), lambda i: (i, 0))
    return pl.pallas_call(
        functools.partial(_route_kernel, tm=tm),
        out_shape=(jax.ShapeDtypeStruct((m, ROUTE_COLS), jnp.int32), jax.ShapeDtypeStruct((m, ROUTE_COLS), F32),
                   jax.ShapeDtypeStruct((SUBLANES, LANES), F32)),
        grid=(m // tm,),
        in_specs=[pl.BlockSpec((tm, D_MODEL), lambda i: (i, 0)), _mod_spec(4, tm, n_ctx, t_s),
                  _mod_spec(3, tm, n_ctx, t_s), pl.BlockSpec(w_router.shape, lambda i: (0, 0))],
        out_specs=(small, small, pl.BlockSpec((SUBLANES, LANES), lambda i: (0, 0))),
        scratch_shapes=[pltpu.VMEM((1, LANES), F32)],
        compiler_params=_cparams("arbitrary"),
        name="moe_route",
    )(x, mod, mod, w_router)


def _dispatch_kernel(slot_ref, x_ref, sc_ref, sh_ref, xs_hbm, buf, sem, *, tok):
    i = pl.program_id(0)
    par = i % 2
    cur = buf.at[par]
    _to_token_rows(cur, 0, x_ref[...] * (1.0 + sc_ref[...]) + sh_ref[...])

    def issue(g, c):
        t0 = g * DMA_GROUP
        slots = [slot_ref[2 * t0 + k] for k in range(2 * DMA_GROUP)]
        for k in range(DMA_GROUP):
            src = _token_row(cur, t0 + k)
            pltpu.make_async_copy(src, _token_row(xs_hbm, slots[2 * k]), sem.at[par]).start(priority=0)
            pltpu.make_async_copy(src, _token_row(xs_hbm, slots[2 * k + 1]), sem.at[par]).start(priority=1)
        return c
    lax.fori_loop(0, tok // DMA_GROUP, issue, 0)

    def drain(which):
        def body(t, c):
            pltpu.make_async_copy(_token_row(buf.at[which], 0), _token_row(xs_hbm, 0), sem.at[which]).wait()
            return c
        lax.fori_loop(0, 2 * tok, body, 0, unroll=8)

    @pl.when(i > 0)
    def _():
        drain(1 - par)

    @pl.when(i == pl.num_programs(0) - 1)
    def _():
        drain(par)


def _moe_dispatch(slots, x, mod, *, n_ctx, t_s):
    m = x.shape[0]
    tok = 1024
    return pl.pallas_call(
        functools.partial(_dispatch_kernel, tok=tok),
        out_shape=jax.ShapeDtypeStruct((2 * m * ROW_TILES, LANES), F32),
        grid=(m // tok,),
        in_specs=[pl.BlockSpec((2 * tok,), lambda i: (i,), memory_space=pltpu.SMEM),
                  pl.BlockSpec((tok, D_MODEL), lambda i: (i, 0)),
                  _mod_spec(4, tok, n_ctx, t_s), _mod_spec(3, tok, n_ctx, t_s)],
        out_specs=pl.BlockSpec(memory_space=pl.ANY),
        scratch_shapes=[pltpu.VMEM((2, tok * ROW_TILES, LANES), F32), pltpu.SemaphoreType.DMA((2,))],
        compiler_params=_cparams("arbitrary"),
        name="moe_dispatch",
    )(slots, x, mod, mod)


def _experts_kernel(wt_ref, we_ref, lo_ref, hi_ref, nw_ref, xs_ref, wg_ref, wu_ref, wd_ref, o_ref, *, rows):
    del we_ref
    w = pl.program_id(0)

    def expert_rows():
        xs = _from_token_rows(xs_ref, 0, rows).astype(BF16)
        g = jnp.dot(xs, wg_ref[...], preferred_element_type=F32)
        u = jnp.dot(xs, wu_ref[...], preferred_element_type=F32)
        act = ((g * _sigmoid(g)) * u).astype(BF16)
        return jnp.dot(act, wd_ref[...], preferred_element_type=F32)

    active = w < nw_ref[0]
    first = jnp.logical_or(w == 0, wt_ref[jnp.maximum(w - 1, 0)] != wt_ref[w])

    @pl.when(jnp.logical_and(active, first))
    def _():
        _to_token_rows(o_ref, 0, expert_rows())

    @pl.when(jnp.logical_and(active, jnp.logical_not(first)))
    def _():
        row = lax.broadcasted_iota(jnp.int32, (rows, D_MODEL), 0)
        mine = (row >= lo_ref[w]) & (row < hi_ref[w])
        _to_token_rows(o_ref, 0, jnp.where(mine, expert_rows(), _from_token_rows(o_ref, 0, rows)))


def _moe_experts(items, xs, w_gu, w_down, layer):
    rows = MOE_TILE
    n_items = items[0].shape[0]
    tile = lambda w, wt, we, lo, hi, nw: (wt[w], 0)
    return pl.pallas_call(
        functools.partial(_experts_kernel, rows=rows),
        out_shape=jax.ShapeDtypeStruct(xs.shape, F32),
        grid_spec=pltpu.PrefetchScalarGridSpec(
            num_scalar_prefetch=5, grid=(n_items,),
            in_specs=[pl.BlockSpec((rows * ROW_TILES, LANES), tile),
                      pl.BlockSpec((None, None, D_MODEL, D_FF_EXPERT),
                                   lambda w, wt, we, lo, hi, nw: (layer, we[w], 0, 0)),
                      pl.BlockSpec((None, None, D_MODEL, D_FF_EXPERT),
                                   lambda w, wt, we, lo, hi, nw: (layer, we[w], 0, 1)),
                      pl.BlockSpec((None, None, D_FF_EXPERT, D_MODEL),
                                   lambda w, wt, we, lo, hi, nw: (layer, we[w], 0, 0))],
            out_specs=pl.BlockSpec((rows * ROW_TILES, LANES), tile)),
        compiler_params=_cparams("arbitrary"),
        name="moe_experts",
    )(*items, xs, w_gu, w_gu, w_down)


def _combine_kernel(idx_ref, nxt_ref, ys_hbm, gates_ref, x_ref, gate_ref, lng_ref, lnb_ref, o_ref, buf, sem, *, tm):
    i = pl.program_id(0)
    slot = i % 2

    def issue(ref, s):
        def body(g, c):
            t0 = g * DMA_GROUP
            slots = [ref[2 * t0 + k] for k in range(2 * DMA_GROUP)]
            for k in range(DMA_GROUP):
                pltpu.make_async_copy(_token_row(ys_hbm, slots[2 * k]), _token_row(buf.at[s], t0 + k),
                                      sem.at[s]).start(priority=0)
                pltpu.make_async_copy(_token_row(ys_hbm, slots[2 * k + 1]), _token_row(buf.at[s], tm + t0 + k),
                                      sem.at[s]).start(priority=1)
            return c
        lax.fori_loop(0, tm // DMA_GROUP, body, 0)

    @pl.when(i == 0)
    def _():
        issue(idx_ref, 0)

    @pl.when(i + 1 < pl.num_programs(0))
    def _():
        issue(nxt_ref, 1 - slot)

    def drain(r, c):
        pltpu.make_async_copy(_token_row(ys_hbm, 0), _token_row(buf.at[slot], 0), sem.at[slot]).wait()
        return c
    lax.fori_loop(0, 2 * tm, drain, 0, unroll=8)
    cur = buf.at[slot]
    y = (gates_ref[:, 0:1] * _from_token_rows(cur, 0, tm)
         + gates_ref[:, 1:2] * _from_token_rows(cur, tm * ROW_TILES, tm))
    z = DEEPNORM_ALPHA * x_ref[...] + gate_ref[...] * y
    o_ref[...] = _layer_norm_rows(z, lng_ref[...], lnb_ref[...])


def _moe_combine(slots, ys, gates, x, mod, lng, lnb, *, n_ctx, t_s, row0=0, n_rows=None):
    tm = 512
    n_blocks = (x.shape[0] if n_rows is None else n_rows) // tm
    b0 = row0 // tm
    idx = lambda f: pl.BlockSpec((2 * tm,), f, memory_space=pltpu.SMEM)
    const = lambda arr: pl.BlockSpec(arr.shape, lambda i: (0,) * arr.ndim)
    return pl.pallas_call(
        functools.partial(_combine_kernel, tm=tm),
        out_shape=jax.ShapeDtypeStruct((n_blocks * tm, D_MODEL), F32),
        grid=(n_blocks,),
        in_specs=[idx(lambda i: (b0 + i,)), idx(lambda i: (b0 + jnp.minimum(i + 1, n_blocks - 1),)),
                  pl.BlockSpec(memory_space=pl.ANY), pl.BlockSpec((tm, ROUTE_COLS), lambda i: (b0 + i, 0)),
                  pl.BlockSpec((tm, D_MODEL), lambda i: (b0 + i, 0)),
                  pl.BlockSpec((None, 1, D_MODEL), lambda i: (_seg_of_block(b0 + i, tm, n_ctx, t_s), 0, 5)),
                  const(lng), const(lnb)],
        out_specs=pl.BlockSpec((tm, D_MODEL), lambda i: (i, 0)),
        scratch_shapes=[pltpu.VMEM((2, 2 * tm * ROW_TILES, LANES), F32), pltpu.SemaphoreType.DMA((2,))],
        compiler_params=_cparams("arbitrary"),
        name="moe_combine",
    )(slots, slots, ys, gates, x, mod, lng, lnb)


def _moe_work_items(counts, n_rows):
    n_tiles = n_rows // MOE_TILE
    n_items = n_tiles + N_EXPERTS - 1
    g_end = jnp.cumsum(counts)
    g_start = g_end - counts
    row0 = jnp.arange(n_tiles, dtype=jnp.int32) * MOE_TILE
    first_e = jnp.sum(row0[:, None] >= g_end[None, :], axis=1)
    last_e = jnp.sum((row0 + MOE_TILE - 1)[:, None] >= g_end[None, :], axis=1)
    per_tile = last_e - first_e + 1
    item0 = jnp.cumsum(per_tile) - per_tile
    w = jnp.arange(n_items, dtype=jnp.int32)
    tile = jnp.clip(jnp.sum(w[:, None] >= item0[None, :], axis=1) - 1, 0, n_tiles - 1)
    expert = jnp.clip(first_e[tile] + w - item0[tile], 0, N_EXPERTS - 1)
    lo = jnp.clip(g_start[expert] - row0[tile], 0, MOE_TILE)
    hi = jnp.clip(g_end[expert] - row0[tile], 0, MOE_TILE)
    n_used = jnp.sum(per_tile).reshape(1)
    i32 = lambda a: a.astype(jnp.int32)
    return i32(tile), i32(expert), i32(lo), i32(hi), i32(n_used), g_start


def _moe(x, mod, w_router, w_gu, w_down, layer, lng, lnb, *, n_ctx, t_s, split=False):
    m = x.shape[0]
    assert (2 * m) % MOE_TILE == 0
    route, gates, cnt = _moe_route(x, mod, w_router, n_ctx=n_ctx, t_s=t_s)
    counts = cnt[0, :N_EXPERTS].astype(jnp.int32)
    *items, g_start = _moe_work_items(counts, 2 * m)
    slots = (g_start[route[:, 0:2]] + route[:, 2:4]).reshape(2 * m).astype(jnp.int32)
    xs = _moe_dispatch(slots, x, mod, n_ctx=n_ctx, t_s=t_s)
    ys = _moe_experts(items, xs, w_gu, w_down, layer)
    comb = functools.partial(_moe_combine, slots, ys, gates, x, mod, lng, lnb, n_ctx=n_ctx, t_s=t_s)
    if split:
        return comb(row0=0, n_rows=n_ctx), comb(row0=n_ctx, n_rows=m - n_ctx)
    return comb()


def _rope_tables(n_tokens):
    n_rows = n_tokens // GRID_W
    row = jnp.repeat(jnp.arange(n_rows, dtype=F32), GRID_W)
    col = jnp.tile(jnp.arange(GRID_W, dtype=F32), n_rows)
    n_axis = HEAD_DIM // 4
    inv = ROPE_THETA ** (-jnp.arange(n_axis, dtype=F32) / n_axis)
    ang = jnp.concatenate([row[:, None] * inv, col[:, None] * inv], axis=-1)
    cos, sin = jnp.cos(ang), jnp.sin(ang)
    reps = LANES // HEAD_DIM
    cos_t = jnp.tile(jnp.concatenate([cos, cos], axis=-1), (1, reps))
    sin_t = jnp.tile(jnp.concatenate([-sin, sin], axis=-1), (1, reps))
    return cos_t, sin_t


def _gqa_head_perm():
    cols = []
    per_pair = 2 * (N_Q_HEADS // N_KV_HEADS)
    for p in range(N_KV_HEADS // 2):
        for j in range(N_Q_HEADS // N_KV_HEADS):
            for hd in (per_pair * p + j, per_pair * p + per_pair // 2 + j):
                cols.append(np.arange(hd * HEAD_DIM, (hd + 1) * HEAD_DIM))
    return np.concatenate(cols)


def _block_diag_windows(w):
    eye = jnp.eye(N_RNN_BLOCKS, dtype=bool)[:, None, :, None]
    dense = jnp.where(eye, w[:, :, None, :], 0.0).reshape(D_RNN, D_RNN)
    for ct, k0 in enumerate(GATE_K0):
        lo, hi = ct * GATE_N, (ct + 1) * GATE_N
        assert k0 <= RNN_BLOCK * (lo // RNN_BLOCK) and RNN_BLOCK * -(-hi // RNN_BLOCK) <= k0 + GATE_K <= D_RNN
    return jnp.stack([dense[k0:k0 + GATE_K, ct * GATE_N:(ct + 1) * GATE_N] for ct, k0 in enumerate(GATE_K0)])


def kernel(x_prompt, x_sample, cache_attn_k, cache_attn_v, state_rglru, cache_diff_k, cache_diff_v, c, c_ctx, w_ada, b_ada, ln_g, ln_b, attn_w_qkv, attn_g_q, attn_g_k, attn_w_o, rnn_w_in, rnn_conv_w, rnn_conv_b, rnn_w_a, rnn_b_a, rnn_w_x, rnn_b_x, rnn_lambda, rnn_w_out, diff_w_qkv, diff_lambda, diff_g_sub, diff_w_o, ffn_w_gu, ffn_w_down, moe_w_router, moe_w_gu, moe_w_down):
    b_c, t_c, d = x_prompt.shape
    b_s, t_s, _ = x_sample.shape
    n_ctx = b_c * t_c
    n_s = b_s * t_s
    past = cache_attn_k.shape[2]
    assert d == D_MODEL and n_ctx % t_s == 0 and t_c == SCAN_CHUNK and t_s % SCAN_CHUNK == 0
    assert 1 + b_s <= COND_ROWS

    assert DEPTH % 2 == 0
    x = (x_prompt.reshape(n_ctx, d), x_sample.reshape(n_s, d))
    cond =jnp.zeros((COND_ROWS, d), F32).at[0].set(c_ctx).at[1:1 + b_s].set(c)
    mods = _ada_table(cond, w_ada, b_ada)[:, :1 + b_s].reshape(DEPTH, 1 + b_s, 1, 6 * d)

    cos_t, sin_t = _rope_tables(t_s)
    rope_t = (cos_t.T, sin_t.T)
    perm = _gqa_head_perm()
    nq = N_Q_HEADS * HEAD_DIM
    tile2 = lambda g: jnp.tile(g, LANES // g.shape[-1]).reshape(1, LANES)
    kw = dict(n_ctx=n_ctx, t_s=t_s)
    moe_gu_b = moe_w_gu.astype(BF16)
    moe_down_b = moe_w_down.astype(BF16)
    ffn_gu_b = ffn_w_gu.astype(BF16)
    ffn_down_b = ffn_w_down.astype(BF16)

    attn_k, attn_v, rnn_s, diff_k, diff_v = [], [], [], [], []
    for li in range(DEPTH):
        mod = mods[li]
        lng = ln_g[li].reshape(2, 1, d)
        lnb = ln_b[li].reshape(2, 1, d)
        j = li // N_MIXERS
        kind = li % N_MIXERS
        if kind == 0:
            w = attn_w_qkv[j]
            w = jnp.concatenate([w[:, :nq][:, perm], w[:, nq:]], axis=1).astype(BF16)
            q, kb, vb, kf, vf = _gqa_qkv(x, mod, w, tile2(attn_g_k[j]), cos_t, sin_t, **kw)
            gq = tile2(attn_g_q[j]).reshape(LANES, 1)
            ck = cache_attn_k[:, j].reshape(b_s, past, N_KV_HEADS * HEAD_DIM).astype(BF16)
            cv = cache_attn_v[:, j].reshape(b_s, past, N_KV_HEADS * HEAD_DIM).astype(BF16)
            o_c = _attention(q, kb, vb, None, None, row0=0, n_b=b_c, t=t_c, tq=t_c, gq=gq)
            o_s = _attention(q, kb, vb, ck, cv, row0=n_ctx, n_b=b_s, t=t_s, tq=512, gq=gq, rope=rope_t)
            x = _oproj(o_c, o_s, attn_w_o[j][perm, :].astype(BF16), x, mod, lng[0], lnb[0], **kw)
            attn_k.append(kf)
            attn_v.append(vf)
        elif kind == 1:
            bx = _rnn_in(x, mod, rnn_w_in[j].astype(BF16), **kw)
            wg = jnp.stack([_block_diag_windows(rnn_w_a[j, 0]), _block_diag_windows(rnn_w_x[j, 0]),
                            _block_diag_windows(rnn_w_a[j, 1]), _block_diag_windows(rnn_w_x[j, 1])]).astype(BF16)
            bg = jnp.concatenate([rnn_b_a[j, 0], rnn_b_x[j, 0], rnn_b_a[j, 1], rnn_b_x[j, 1]]).reshape(1, -1)
            a_f, u_f, a_b, u_b = _rnn_gates(bx, rnn_conv_w[j], rnn_conv_b[j].reshape(1, -1), wg, bg,
                                            rnn_lambda[j], n_ctx=n_ctx, t_c=t_c, t_s=t_s)
            h0 = jnp.concatenate([jnp.zeros((b_c, 2, D_RNN), F32), state_rglru[:, j]], axis=0)
            hs_f, hs_b, fin = _rnn_scan(a_f, u_f, a_b, u_b, h0, seq_lens=[t_c] * b_c + [t_s] * b_s)
            x = _rnn_oproj(hs_f, hs_b, bx, rnn_w_out[j].astype(BF16), x, mod, lng[0], lnb[0], **kw)
            rnn_s.append(fin[:b_c])
        else:
            lam_init = 0.8 - 0.6 * math.exp(-0.3 * li)
            q, kb, vb, kf, vf = _diff_qkv(x, mod, diff_w_qkv[j].astype(BF16), cos_t, sin_t, **kw)
            ck = cache_diff_k[:, j].reshape(b_s, past, D_MODEL).astype(BF16)
            cv = cache_diff_v[:, j].reshape(b_s, past, D_MODEL).astype(BF16)
            g_sub = diff_g_sub[j].reshape(1, LANES)
            dargs = (diff_lambda[j], g_sub, lam_init)
            o_c = _attention(q, kb, vb, None, None, row0=0, n_b=b_c, t=t_c, tq=t_c, diff=dargs)
            o_s = _attention(q, kb, vb, ck, cv, row0=n_ctx, n_b=b_s, t=t_s, tq=256, rope=rope_t, diff=dargs)
            x = _oproj(o_c, o_s, diff_w_o[j].astype(BF16), x, mod, lng[0], lnb[0], **kw)
            diff_k.append(kf)
            diff_v.append(vf)
        if li % 2 == 0:
            x = _ffn(x, mod, ffn_gu_b, ffn_down_b, li // 2, lng[1], lnb[1], **kw)
        else:
            w_r = jnp.zeros((d, LANES), F32).at[:, :N_EXPERTS].set(moe_w_router[li // 2]).astype(BF16)
            x = _moe(x, mod, w_r, moe_gu_b, moe_down_b, li // 2, lng[1], lnb[1], split=li == DEPTH - 1, **kw)

    y_prompt = x[0].reshape(b_c, t_c, d)
    y_sample = x[1].reshape(b_s, t_s, d)
    new_attn_k = jnp.stack([k.reshape(b_c, t_c, N_KV_HEADS, HEAD_DIM) for k in attn_k], axis=1)
    new_attn_v = jnp.stack([v.reshape(b_c, t_c, N_KV_HEADS, HEAD_DIM) for v in attn_v], axis=1)
    new_state = jnp.stack(rnn_s, axis=1)
    new_diff_k = jnp.stack([k.reshape(b_c, t_c, N_DIFF_HEADS, 2, HEAD_DIM) for k in diff_k], axis=1)
    new_diff_v = jnp.stack([v.reshape(b_c, t_c, N_DIFF_HEADS, 2 * HEAD_DIM) for v in diff_v], axis=1)
    return (y_prompt, y_sample, new_attn_k, new_attn_v, new_state, new_diff_k, new_diff_v)
```
